```python
import math
import jax
import jax.numpy as jnp
from jax import lax
import numpy as np

D_MODEL = 1024
BATCH = 2
SEQ = 8192
DEPTH = 2

CTX_LEN = 256
GRID_W = 64
N_MOD = 6
NA_HEADS = 8
NA_HEAD_DIM = 64
NA_WIDTH = NA_HEADS * NA_HEAD_DIM
WIN_R_MAX = 8
WIN_C = 16
QC_BLK = 16
KC_BLK = QC_BLK + WIN_C
DN_HEADS = 4
DN_HEAD_DIM = 128
DN_WIDTH = DN_HEADS * DN_HEAD_DIM
DN_CONV = 5
DN_CHUNK = 64
ROPE_BASE = 10000.0
SC_WIDTH = 512
SC_CONV = 3
N_BRANCH = 3
N_EXPERTS = 16
N_GROUPS = 4
EXPERTS_PER_GROUP = N_EXPERTS // N_GROUPS
TOP_K = 2
D_EXPERT = 512
EPS = 1e-6
NEG_INF = -1e30
P_SIZES = (3 * NA_WIDTH, 3 * DN_WIDTH, DN_WIDTH, 4 * DN_HEADS, 3 * SC_WIDTH, N_BRANCH * D_MODEL)
P_TOTAL = 3 * NA_WIDTH + 3 * DN_WIDTH + DN_WIDTH + 4 * DN_HEADS + 3 * SC_WIDTH + N_BRANCH * D_MODEL

kernel_name = "hybrid_natten_deltanet_shortconv_moe_dit"


def rms_norm(x, g):
    xf = x.astype(jnp.float32)
    y = xf * lax.rsqrt(jnp.mean(xf * xf, axis=-1, keepdims=True) + EPS)
    return (y * g.astype(jnp.float32)).astype(x.dtype)


def l2_norm(x):
    xf = x.astype(jnp.float32)
    return xf * lax.rsqrt(jnp.sum(xf * xf, axis=-1, keepdims=True) + EPS)


def modulate(h, shift, scale):
    return h * (1.0 + scale) + shift


def centred_dwconv(x, w):
    k = w.shape[0]
    return lax.conv_general_dilated(
        x, w[:, None, :].astype(x.dtype), window_strides=(1,), padding=[(k // 2, k // 2)],
        dimension_numbers=('NWC', 'WIO', 'NWC'), feature_group_count=x.shape[-1])


def axial_rope_tables(n_tokens):
    t = jnp.arange(n_tokens)
    n_freq = DN_HEAD_DIM // 4
    inv_freq = ROPE_BASE ** (-jnp.arange(n_freq, dtype=jnp.float32) / n_freq)
    pos = jnp.stack([t // GRID_W, t % GRID_W], axis=-1).astype(jnp.float32)
    ang = pos[..., None] * inv_freq
    return jnp.cos(ang), jnp.sin(ang)


def apply_axial_rope(x, cos, sin):
    b, t, h, dk = x.shape
    xr = x.reshape(b, t, h, 2, 2, dk // 4)
    x1, x2 = xr[..., 0, :], xr[..., 1, :]
    c, s = cos[None, :, None], sin[None, :, None]
    out = jnp.stack([x1 * c - x2 * s, x2 * c + x1 * s], axis=-2)
    return out.reshape(b, t, h, dk)


def na_heads(p, q_g, k_g):
    shape = p.shape[:2] + (NA_HEADS, NA_HEAD_DIM)
    q, k, v = jnp.split(p, 3, axis=-1)
    return rms_norm(q.reshape(shape), q_g), rms_norm(k.reshape(shape), k_g), v.reshape(shape)


def neighbourhood_attention(q, k, v, k_ctx, v_ctx, rpb):
    b, s, h, d = q.shape
    rows = s // GRID_W
    win_r = min(WIN_R_MAX, rows)
    n_cb = GRID_W // QC_BLK
    r = jnp.arange(rows)
    row_idx = jnp.clip(r - win_r // 2, 0, rows - win_r)[:, None] + jnp.arange(win_r)[None, :]
    q_col = jnp.arange(GRID_W).reshape(n_cb, QC_BLK)
    key_col0 = jnp.clip(q_col[:, 0] - WIN_C // 2, 0, GRID_W - KC_BLK)
    col_idx = key_col0[:, None] + jnp.arange(KC_BLK)[None, :]
    win_c0 = jnp.clip(q_col - WIN_C // 2, 0, GRID_W - WIN_C)
    kcol = col_idx[:, None, :]
    in_win = (kcol >= win_c0[..., None]) & (kcol < win_c0[..., None] + WIN_C)
    rel_r = row_idx - r[:, None] + WIN_R_MAX - 1
    rel_c = jnp.clip(kcol - q_col[..., None] + WIN_C - 1, 0, 2 * WIN_C - 2)
    bias = rpb.astype(jnp.float32)[:, rel_r[:, None, None, :, None], rel_c[None, :, :, None, :]]
    bias = jnp.where(in_win[None, None, :, :, None, :], bias, NEG_INF)
    scale = d ** -0.5
    qg = q.reshape(b, rows, n_cb, QC_BLK, h, d)
    ri = row_idx[:, None, :, None]
    ci = col_idx[None, :, None, :]
    kg = k.reshape(b, rows, GRID_W, h, d)[:, ri, ci]
    vg = v.reshape(b, rows, GRID_W, h, d)[:, ri, ci]
    s_win = jnp.einsum('brnqhd,brnwkhd->bhrnqwk', qg, kg).astype(jnp.float32) * scale + bias[None]
    s_ctx = jnp.einsum('brnqhd,bchd->bhrnqc', qg, k_ctx).astype(jnp.float32) * scale
    n_win = win_r * KC_BLK
    logits = jnp.concatenate([s_win.reshape(s_win.shape[:5] + (n_win,)), s_ctx], axis=-1)
    p = jax.nn.softmax(logits, axis=-1).astype(v.dtype)
    p_win = p[..., :n_win].reshape(s_win.shape)
    p_ctx = p[..., n_win:]
    o = (jnp.einsum('bhrnqwk,brnwkhd->brnqhd', p_win, vg)
         + jnp.einsum('bhrnqc,bchd->brnqhd', p_ctx, v_ctx))
    return o.reshape(b, s, h * d)


def context_attention(q, k, v):
    b, n, h, d = q.shape
    logits = jnp.einsum('bqhd,bkhd->bhqk', q, k).astype(jnp.float32) * d ** -0.5
    p = jax.nn.softmax(logits, axis=-1).astype(v.dtype)
    return jnp.einsum('bhqk,bkhd->bqhd', p, v).reshape(b, n, h * d)


def deltanet_inputs(qkv, ab, conv_w, a_log, dt_bias, rope):
    b, t, _ = qkv.shape
    qkv = jax.nn.silu(centred_dwconv(qkv, conv_w)).astype(jnp.float32)
    q, k, v = [u.reshape(b, t, DN_HEADS, DN_HEAD_DIM) for u in jnp.split(qkv, 3, axis=-1)]
    q, k = l2_norm(q), l2_norm(k)
    if rope is not None:
        q = apply_axial_rope(q, rope[0], rope[1])
        k = apply_axial_rope(k, rope[0], rope[1])
    ab = ab.astype(jnp.float32).reshape(b, t, 2, 2, DN_HEADS)
    g = -jnp.exp(a_log.astype(jnp.float32)) * jax.nn.softplus(ab[:, :, 0] + dt_bias.astype(jnp.float32))
    beta = jax.nn.sigmoid(ab[:, :, 1])
    to_bht = lambda u: jnp.transpose(u, (0, 2, 1, 3))
    return (to_bht(q), to_bht(k), to_bht(v),
            jnp.transpose(g, (2, 0, 3, 1)), jnp.transpose(beta, (2, 0, 3, 1)))


def gated_delta_chunked(q, k, v, g, beta, s0):
    b, h, t, dk = q.shape
    n = t // DN_CHUNK
    q = q * dk ** -0.5
    q, k, v = [u.reshape(b, h, n, DN_CHUNK, u.shape[-1]) for u in (q, k, v)]
    g = jnp.cumsum(g.reshape(b, h, n, DN_CHUNK), axis=-1)
    beta = beta.reshape(b, h, n, DN_CHUNK)
    idx = jnp.arange(DN_CHUNK)
    lower = idx[:, None] >= idx[None, :]
    strict = idx[:, None] > idx[None, :]
    decay = jnp.exp(jnp.where(lower, g[..., :, None] - g[..., None, :], -jnp.inf))
    kb = k * beta[..., None]
    lmat = jnp.where(strict, jnp.einsum('bhncd,bhnsd->bhncs', kb, k) * decay, 0.0)
    eye = jnp.eye(DN_CHUNK, dtype=q.dtype)
    tinv = lax.linalg.triangular_solve(eye + lmat, jnp.broadcast_to(eye, lmat.shape),
                                       left_side=True, lower=True, unit_diagonal=True)
    u = tinv @ (v * beta[..., None])
    w = tinv @ (kb * jnp.exp(g)[..., None])
    a_intra = jnp.einsum('bhncd,bhnsd->bhncs', q, k) * decay
    qg = q * jnp.exp(g)[..., None]
    kg = k * jnp.exp(g[..., -1:] - g)[..., None]
    glast = jnp.exp(g[..., -1])

    def step(state, xs):
        qg_i, kg_i, u_i, w_i, a_i, gl_i = xs
        v_new = u_i - jnp.einsum('bhcd,bhde->bhce', w_i, state)
        o = jnp.einsum('bhcd,bhde->bhce', qg_i, state) + jnp.einsum('bhcs,bhse->bhce', a_i, v_new)
        state = state * gl_i[..., None, None] + jnp.einsum('bhcd,bhce->bhde', kg_i, v_new)
        return state, o

    xs = tuple(jnp.moveaxis(u_, 2, 0) for u_ in (qg, kg, u, w, a_intra, glast))
    s_final, o = lax.scan(step, s0, xs)
    o = jnp.moveaxis(o, 0, 2).reshape(b, h, t, -1)
    return o, s_final


def bidirectional_deltanet(ctx_in, lat_in):
    qc, kc, vc, gc, bc = ctx_in
    ql, kl, vl, gl, bl = lat_in
    s0 = jnp.zeros((qc.shape[0], DN_HEADS, DN_HEAD_DIM, DN_HEAD_DIM), jnp.float32)
    fl = lambda u: jnp.flip(u, axis=2)
    oc_f, sc_f = gated_delta_chunked(qc, kc, vc, gc[0], bc[0], s0)
    ol_f, _ = gated_delta_chunked(ql, kl, vl, gl[0], bl[0], sc_f)
    oc_b, sc_b = gated_delta_chunked(fl(qc), fl(kc), fl(vc), fl(gc[1]), fl(bc[1]), s0)
    ol_b, _ = gated_delta_chunked(fl(ql), fl(kl), fl(vl), fl(gl[1]), fl(bl[1]), sc_b)
    return oc_f + fl(oc_b), ol_f + fl(ol_b)


def deltanet_output(o, z, out_g):
    b, t, _ = z.shape
    o = rms_norm(jnp.transpose(o, (0, 2, 1, 3)), out_g)
    o = o * jax.nn.silu(z.astype(jnp.float32)).reshape(b, t, DN_HEADS, DN_HEAD_DIM)
    return o.reshape(b, t, DN_WIDTH).astype(z.dtype)


def short_conv_mixer(p, conv_w):
    u, b_gate, c_gate = jnp.split(p, 3, axis=-1)
    return b_gate * centred_dwconv(c_gate * u, conv_w)


def merge_branches(o_na, o_dn, o_sc, gate_p, w_br_na, w_br_dn, w_br_sc, w_out):
    g_na, g_dn, g_sc = jnp.split(jax.nn.sigmoid(gate_p), N_BRANCH, axis=-1)
    merged = g_na * (o_na @ w_br_na) + g_dn * (o_dn @ w_br_dn) + g_sc * (o_sc @ w_br_sc)
    return merged @ w_out


def hybrid_mixer(h, hc, w_in, na_q_g, na_k_g, na_rpb, dn_conv_w, dn_a_log, dn_dt_bias, dn_out_g,
                 sc_conv_w, w_br_na, w_br_dn, w_br_sc, w_out, rope, with_ctx):
    cuts = np.cumsum(P_SIZES)[:-1]
    na_p, dn_qkv, dn_z, dn_ab, sc_p, gate_p = jnp.split(h @ w_in, cuts, axis=-1)
    na_pc, dn_qkvc, dn_zc, dn_abc, sc_pc, gate_pc = jnp.split(hc @ w_in, cuts, axis=-1)
    q, k, v = na_heads(na_p, na_q_g, na_k_g)
    qc, kc, vc = na_heads(na_pc, na_q_g, na_k_g)
    o_na = neighbourhood_attention(q, k, v, kc, vc, na_rpb)
    dn_lat = deltanet_inputs(dn_qkv, dn_ab, dn_conv_w, dn_a_log, dn_dt_bias, rope)
    dn_ctx = deltanet_inputs(dn_qkvc, dn_abc, dn_conv_w, dn_a_log, dn_dt_bias, None)
    oc_dn, ol_dn = bidirectional_deltanet(dn_ctx, dn_lat)
    o_dn = deltanet_output(ol_dn, dn_z, dn_out_g)
    o_sc = short_conv_mixer(sc_p, sc_conv_w)
    y = merge_branches(o_na, o_dn, o_sc, gate_p, w_br_na, w_br_dn, w_br_sc, w_out)
    if not with_ctx:
        return y, None
    o_na_c = context_attention(qc, kc, vc)
    o_dn_c = deltanet_output(oc_dn, dn_zc, dn_out_g)
    o_sc_c = short_conv_mixer(sc_pc, sc_conv_w)
    y_c = merge_branches(o_na_c, o_dn_c, o_sc_c, gate_pc, w_br_na, w_br_dn, w_br_sc, w_out)
    return y, y_c


def grouped_moe(h, w_router, router_bias, w_gate, w_up, w_down):
    b, t, d = h.shape
    tok = h.reshape(b * t, d)
    scores = jax.nn.sigmoid((tok @ w_router).astype(jnp.float32))
    biased = scores + router_bias.astype(jnp.float32)
    group_score = lax.top_k(biased.reshape(-1, N_GROUPS, EXPERTS_PER_GROUP), 2)[0].sum(-1)
    group = jnp.argmax(group_score, axis=-1)
    in_group = (jnp.arange(N_EXPERTS) // EXPERTS_PER_GROUP)[None, :] == group[:, None]
    _, top_idx = lax.top_k(jnp.where(in_group, biased, -jnp.inf), TOP_K)
    top_w = jnp.take_along_axis(scores, top_idx, axis=-1)
    top_w = top_w / jnp.sum(top_w, axis=-1, keepdims=True)
    combine = jnp.sum(jax.nn.one_hot(top_idx, N_EXPERTS, dtype=jnp.float32) * top_w[..., None], axis=1)
    hidden = jax.nn.silu(jnp.einsum('nd,edf->nef', tok, w_gate)) * jnp.einsum('nd,edf->nef', tok, w_up)
    hidden = hidden * combine.astype(h.dtype)[..., None]
    return jnp.einsum('nef,efd->nd', hidden, w_down).reshape(b, t, d)


def setup_inputs(seed: int = 0) -> dict:
    key = jax.random.key(seed)
    ks = jax.random.split(key, 32)
    f32 = jnp.float32
    nrm = lambda k, shape, scale: jax.random.normal(k, shape, f32) * scale
    dt = jnp.exp(jax.random.uniform(ks[14], (DEPTH, 2, DN_HEADS), f32, math.log(1e-3), math.log(1e-1)))
    return {
        'x': nrm(ks[0], (BATCH, SEQ, D_MODEL), 1.0),
        'c': nrm(ks[1], (BATCH, D_MODEL), 1.0),
        'ctx': nrm(ks[2], (BATCH, CTX_LEN, D_MODEL), 1.0),
        'c_ctx': nrm(ks[3], (D_MODEL,), 1.0),
        'w_ada': nrm(ks[4], (DEPTH, D_MODEL, N_MOD * D_MODEL), 0.5 * D_MODEL ** -0.5),
        'b_ada': nrm(ks[5], (DEPTH, N_MOD * D_MODEL), 0.02),
        'norm_mix_g': 1.0 + nrm(ks[6], (DEPTH, D_MODEL), 0.02),
        'norm_ffn_g': 1.0 + nrm(ks[7], (DEPTH, D_MODEL), 0.02),
        'w_in': nrm(ks[8], (DEPTH, D_MODEL, P_TOTAL), D_MODEL ** -0.5),
        'na_q_norm_g': 1.0 + nrm(ks[9], (DEPTH, NA_HEAD_DIM), 0.02),
        'na_k_norm_g': 1.0 + nrm(ks[10], (DEPTH, NA_HEAD_DIM), 0.02),
        'na_rpb': nrm(ks[11], (DEPTH, NA_HEADS, 2 * WIN_R_MAX - 1, 2 * WIN_C - 1), 0.1),
        'dn_conv_w': nrm(ks[12], (DEPTH, DN_CONV, 3 * DN_WIDTH), DN_CONV ** -0.5),
        'dn_a_log': jnp.log(jax.random.uniform(ks[13], (DEPTH, 2, DN_HEADS), f32, 1.0, 16.0)),
        'dn_dt_bias': dt + jnp.log(-jnp.expm1(-dt)),
        'dn_out_norm_g': 1.0 + nrm(ks[15], (DEPTH, DN_HEAD_DIM), 0.02),
        'sc_conv_w': nrm(ks[16], (DEPTH, SC_CONV, SC_WIDTH), SC_CONV ** -0.5),
        'w_branch_na': nrm(ks[17], (DEPTH, NA_WIDTH, D_MODEL), NA_WIDTH ** -0.5),
        'w_branch_dn': nrm(ks[18], (DEPTH, DN_WIDTH, D_MODEL), DN_WIDTH ** -0.5),
        'w_branch_sc': nrm(ks[19], (DEPTH, SC_WIDTH, D_MODEL), SC_WIDTH ** -0.5),
        'w_out': nrm(ks[20], (DEPTH, D_MODEL, D_MODEL), D_MODEL ** -0.5),
        'w_router': nrm(ks[21], (D_MODEL, N_EXPERTS), D_MODEL ** -0.5),
        'router_bias': nrm(ks[22], (N_EXPERTS,), 0.01),
        'moe_w_gate': nrm(ks[23], (DEPTH, N_EXPERTS, D_MODEL, D_EXPERT), D_MODEL ** -0.5),
        'moe_w_up': nrm(ks[24], (DEPTH, N_EXPERTS, D_MODEL, D_EXPERT), D_MODEL ** -0.5),
        'moe_w_down': nrm(ks[25], (DEPTH, N_EXPERTS, D_EXPERT, D_MODEL), D_EXPERT ** -0.5),
    }


def reference(x, c, ctx, c_ctx, w_ada, b_ada, norm_mix_g, norm_ffn_g, w_in, na_q_norm_g, na_k_norm_g,
              na_rpb, dn_conv_w, dn_a_log, dn_dt_bias, dn_out_norm_g, sc_conv_w, w_branch_na, w_branch_dn,
              w_branch_sc, w_out, w_router, router_bias, moe_w_gate, moe_w_up, moe_w_down):
    rope = axial_rope_tables(x.shape[1])
    for l in range(DEPTH):
        with_ctx = l < DEPTH - 1
        mod = (jax.nn.silu(c) @ w_ada[l] + b_ada[l])[:, None, :]
        mod_c = (jax.nn.silu(c_ctx) @ w_ada[l] + b_ada[l])[None, None, :]
        sh1, sc1, g1, sh2, sc2, g2 = jnp.split(mod, N_MOD, axis=-1)
        csh1, csc1, cg1, csh2, csc2, cg2 = jnp.split(mod_c, N_MOD, axis=-1)
        h = modulate(rms_norm(x, norm_mix_g[l]), sh1, sc1)
        hc = modulate(rms_norm(ctx, norm_mix_g[l]), csh1, csc1)
        y, y_c = hybrid_mixer(h, hc, w_in[l], na_q_norm_g[l], na_k_norm_g[l], na_rpb[l], dn_conv_w[l],
                              dn_a_log[l], dn_dt_bias[l], dn_out_norm_g[l], sc_conv_w[l], w_branch_na[l],
                              w_branch_dn[l], w_branch_sc[l], w_out[l], rope, with_ctx)
        x = x + g1 * y
        h = modulate(rms_norm(x, norm_ffn_g[l]), sh2, sc2)
        x = x + g2 * grouped_moe(h, w_router, router_bias, moe_w_gate[l], moe_w_up[l], moe_w_down[l])
        if with_ctx:
            ctx = ctx + cg1 * y_c
            hc = modulate(rms_norm(ctx, norm_ffn_g[l]), csh2, csc2)
            ctx = ctx + cg2 * grouped_moe(hc, w_router, router_bias, moe_w_gate[l], moe_w_up[l], moe_w_down[l])
    return x
```

```python
import functools
import math

import numpy as np
import jax
import jax.numpy as jnp
from jax import lax
from jax.experimental import pallas as pl
from jax.experimental.pallas import tpu as pltpu

F32 = jnp.float32
BF16 = jnp.bfloat16
HIGHEST = lax.Precision.HIGHEST

GRID_W = 64
NA_HEADS = 8
NA_HEAD_DIM = 64
NA_WIDTH = NA_HEADS * NA_HEAD_DIM
WIN_R = 8
WIN_C = 16
DN_HEADS = 4
DN_HEAD_DIM = 128
DN_WIDTH = DN_HEADS * DN_HEAD_DIM
DN_CONV = 5
DN_CHUNK = 64
ROPE_BASE = 10000.0
SC_WIDTH = 512
SC_CONV = 3
N_EXPERTS = 16
N_GROUPS = 4
EXPERTS_PER_GROUP = N_EXPERTS // N_GROUPS
D_EXPERT = 512
N_MOD = 6
EPS = 1e-6
NEG_INF = -1e30

LANES = 128
HALO = 16
ROW_BLK = WIN_R * GRID_W
PREP_TILE = 256
VMEM_LIMIT = 56 * 1024 * 1024

COL_GATE = 0
N_PROJ_TILE = 512


def _cparams(sem):
    return pltpu.CompilerParams(dimension_semantics=sem, vmem_limit_bytes=VMEM_LIMIT)


def _dot(a, b):
    return jnp.dot(a, b, preferred_element_type=F32)


def _dot_nt(a, b):
    return lax.dot_general(a, b, (((1,), (1,)), ((), ())), preferred_element_type=F32)


def _silu(x):
    return x * jax.nn.sigmoid(x)


def _row_select(i, tm, s_len, lat, ctx):
    row = i * tm + lax.broadcasted_iota(jnp.int32, (tm, 1), 0)
    return jnp.where(row >= s_len, ctx, lat)


def _mod_kernel(c_ref, w_ref, b_ref, o_ref):
    o_ref[0] = jnp.dot(_silu(c_ref[...]), w_ref[0], precision=HIGHEST,
                       preferred_element_type=F32) + b_ref[0]


def _modulation(c_rows, w_ada, b_ada):
    depth, d, n = w_ada.shape
    tn = 512
    return pl.pallas_call(
        _mod_kernel,
        grid=(depth, n // tn),
        in_specs=[pl.BlockSpec((8, d), lambda l, j: (0, 0)),
                  pl.BlockSpec((1, d, tn), lambda l, j: (l, 0, j)),
                  pl.BlockSpec((1, 1, tn), lambda l, j: (l, 0, j))],
        out_specs=pl.BlockSpec((1, 8, tn), lambda l, j: (l, 0, j)),
        out_shape=jax.ShapeDtypeStruct((depth, 8, n), F32),
        compiler_params=_cparams(("parallel", "parallel")),
        name="adaln_mod",
    )(c_rows, w_ada, b_ada.reshape(depth, 1, n))


def _proj_kernel(x_ref, shl_ref, scl_ref, shc_ref, scc_ref, g_ref, w_ref, wab_ref, gq_ref, gk_ref,
                 bd_ref, p_ref, ab_ref, h_scr, *, tm, s_len):
    i = pl.program_id(1)
    j = pl.program_id(2)

    @pl.when(j == 0)
    def _():
        x = x_ref[0]
        y = x * lax.rsqrt(jnp.mean(x * x, axis=-1, keepdims=True) + EPS) * g_ref[...]
        scale = _row_select(i, tm, s_len, scl_ref[0], scc_ref[...])
        shift = _row_select(i, tm, s_len, shl_ref[0], shc_ref[...])
        h = y * (1.0 + scale) + shift
        h_scr[...] = h.astype(BF16)
        ab_ref[0] = jnp.dot(h, wab_ref[...], precision=HIGHEST, preferred_element_type=F32)

    acc = _dot(h_scr[...], w_ref[...])

    def head_norm(gain_ref):
        ss = _dot((acc * acc).astype(BF16), bd_ref[...])
        return acc * lax.rsqrt(ss * (1.0 / NA_HEAD_DIM) + EPS) * gain_ref[...]

    n_gate = 3 * x_ref.shape[-1] // N_PROJ_TILE

    @pl.when(j < n_gate)
    def _():
        p_ref[0] = jax.nn.sigmoid(acc).astype(BF16)

    @pl.when(j == n_gate)
    def _():
        p_ref[0] = head_norm(gq_ref).astype(BF16)

    @pl.when(j == n_gate + 1)
    def _():
        p_ref[0] = head_norm(gk_ref).astype(BF16)

    @pl.when(j > n_gate + 1)
    def _():
        p_ref[0] = acc.astype(BF16)


def _projection(xs, mod_l, mod_c, norm_g, w, w_ab, gq, gk, bd, *, s_len, tm):
    b, ta, d = xs.shape
    n = w.shape[1]
    shl, scl = mod_l
    shc, scc = mod_c
    vec = lambda: pl.BlockSpec((1, d), lambda bb, i, j: (0, 0))
    bvec = lambda: pl.BlockSpec((1, 1, d), lambda bb, i, j: (bb, 0, 0))
    return pl.pallas_call(
        functools.partial(_proj_kernel, tm=tm, s_len=s_len),
        grid=(b, ta // tm, n // N_PROJ_TILE),
        in_specs=[pl.BlockSpec((1, tm, d), lambda bb, i, j: (bb, i, 0)),
                  bvec(), bvec(), vec(), vec(), vec(),
                  pl.BlockSpec((d, N_PROJ_TILE), lambda bb, i, j: (0, j)),
                  pl.BlockSpec((d, LANES), lambda bb, i, j: (0, 0)),
                  pl.BlockSpec((1, N_PROJ_TILE), lambda bb, i, j: (0, 0)),
                  pl.BlockSpec((1, N_PROJ_TILE), lambda bb, i, j: (0, 0)),
                  pl.BlockSpec((N_PROJ_TILE, N_PROJ_TILE), lambda bb, i, j: (0, 0))],
        out_specs=[pl.BlockSpec((1, tm, N_PROJ_TILE), lambda bb, i, j: (bb, i, j)),
                   pl.BlockSpec((1, tm, LANES), lambda bb, i, j: (bb, i, 0))],
        out_shape=[jax.ShapeDtypeStruct((b, ta, n), BF16),
                   jax.ShapeDtypeStruct((b, ta, LANES), F32)],
        scratch_shapes=[pltpu.VMEM((tm, d), BF16)],
        compiler_params=_cparams(("parallel", "parallel", "arbitrary")),
        name="in_proj",
    )(xs, shl, scl, shc, scc, norm_g, w, w_ab, gq, gk, bd)


def _softmax_pv(q2, keys, vals, biases):
    lane = lax.broadcasted_iota(jnp.int32, q2.shape, 1)
    outs = []
    for hh in range(2):
        in_head = (lane >= hh * NA_HEAD_DIM) & (lane < (hh + 1) * NA_HEAD_DIM)
        qm = jnp.where(in_head, q2, jnp.zeros_like(q2))
        logits = []
        for k2, bias in zip(keys, biases):
            s = _dot_nt(qm, k2)
            if bias is not None:
                s = s + bias[hh]
            logits.append(s)
        m = logits[0].max(axis=-1, keepdims=True)
        for s in logits[1:]:
            m = jnp.maximum(m, s.max(axis=-1, keepdims=True))
        ps = [jnp.exp(s - m) for s in logits]
        denom = ps[0].sum(axis=-1, keepdims=True)
        for p in ps[1:]:
            denom = denom + p.sum(axis=-1, keepdims=True)
        o = _dot(ps[0].astype(BF16), vals[0])
        for p, v2 in zip(ps[1:], vals[1:]):
            o = o + _dot(p.astype(BF16), v2)
        outs.append(o / denom)
    lane_o = lax.broadcasted_iota(jnp.int32, outs[0].shape, 1)
    return jnp.where(lane_o < NA_HEAD_DIM, outs[0], outs[1])


def _na_kernel(q_ref, kp_ref, kc_ref, kn_ref, vp_ref, vc_ref, vn_ref, kx_ref, vx_ref, bias_ref, o_ref,
               kbuf, vbuf, *, n_rows):
    i = pl.program_id(1)
    for slot, (kr, vr) in enumerate(((kp_ref, vp_ref), (kc_ref, vc_ref), (kn_ref, vn_ref))):
        kbuf[slot * ROW_BLK:(slot + 1) * ROW_BLK, :] = kr[0]
        vbuf[slot * ROW_BLK:(slot + 1) * ROW_BLK, :] = vr[0]

    def row_body(j, carry):
        r = i * WIN_R + j
        start = jnp.clip(r - WIN_R // 2, 0, n_rows - WIN_R)
        var = r - start
        tok0 = pl.multiple_of((start - (i - 1) * WIN_R) * GRID_W, GRID_W)
        q0 = pl.multiple_of(j * GRID_W, GRID_W)
        for hp in range(NA_HEADS // 2):
            lanes = slice(hp * LANES, (hp + 1) * LANES)
            q2 = q_ref[0, pl.ds(q0, GRID_W), lanes]
            kw = kbuf[pl.ds(tok0, ROW_BLK), lanes]
            vw = vbuf[pl.ds(tok0, ROW_BLK), lanes]
            bias = [bias_ref[var, 2 * hp + hh] for hh in range(2)]
            o2 = _softmax_pv(q2, [kw, kx_ref[0, :, lanes]], [vw, vx_ref[0, :, lanes]], [bias, None])
            o_ref[0, pl.ds(q0, GRID_W), lanes] = o2.astype(BF16)
        return carry

    lax.fori_loop(0, WIN_R, row_body, 0)


def _ctx_attn_kernel(q_ref, k_ref, v_ref, prev_ref, o_ref):
    del prev_ref
    for hp in range(NA_HEADS // 2):
        lanes = slice(hp * LANES, (hp + 1) * LANES)
        o2 = _softmax_pv(q_ref[0, :, lanes], [k_ref[0, :, lanes]], [v_ref[0, :, lanes]], [None])
        o_ref[0, :, lanes] = o2.astype(BF16)


def _neighbourhood_attention(p, bias_tab, *, s_len, ctx_len, col_q):
    b, ta, _ = p.shape
    n_rows = s_len // GRID_W
    n_blk = n_rows // WIN_R
    cq, ck, cv = col_q, col_q + 1, col_q + 2
    ctx_blk = s_len // ctx_len
    rb = lambda off: (lambda bb, i: (bb, jnp.clip(i + off, 0, n_blk - 1), 0))
    kv = lambda col, off: pl.BlockSpec(
        (1, ROW_BLK, NA_WIDTH), lambda bb, i: (bb, jnp.clip(i + off, 0, n_blk - 1), col))
    cx = lambda col: pl.BlockSpec((1, ctx_len, NA_WIDTH), lambda bb, i: (bb, ctx_blk, col))
    o_lat = pl.pallas_call(
        functools.partial(_na_kernel, n_rows=n_rows),
        grid=(b, n_blk),
        in_specs=[pl.BlockSpec((1, ROW_BLK, NA_WIDTH), lambda bb, i: (bb, i, cq)),
                  kv(ck, -1), kv(ck, 0), kv(ck, 1), kv(cv, -1), kv(cv, 0), kv(cv, 1),
                  cx(ck), cx(cv),
                  pl.BlockSpec(bias_tab.shape, lambda bb, i: (0, 0, 0, 0))],
        out_specs=pl.BlockSpec((1, ROW_BLK, NA_WIDTH), lambda bb, i: (bb, i, 0)),
        out_shape=jax.ShapeDtypeStruct((b, ta, NA_WIDTH), BF16),
        scratch_shapes=[pltpu.VMEM((3 * ROW_BLK, NA_WIDTH), BF16),
                        pltpu.VMEM((3 * ROW_BLK, NA_WIDTH), BF16)],
        compiler_params=_cparams(("parallel", "arbitrary")),
        name="na_attention",
    )(p, p, p, p, p, p, p, p, p, bias_tab)
    del rb
    cblk = lambda col: pl.BlockSpec((1, ctx_len, NA_WIDTH), lambda bb: (bb, ctx_blk, col))
    return pl.pallas_call(
        _ctx_attn_kernel,
        grid=(b,),
        in_specs=[cblk(cq), cblk(ck), cblk(cv), pl.BlockSpec(memory_space=pl.ANY)],
        out_specs=pl.BlockSpec((1, ctx_len, NA_WIDTH), lambda bb: (bb, ctx_blk, 0)),
        out_shape=jax.ShapeDtypeStruct((b, ta, NA_WIDTH), BF16),
        input_output_aliases={3: 0},
        compiler_params=_cparams(("parallel",)),
        name="ctx_attention",
    )(p, p, p, o_lat)


def _na_bias_table(rpb):
    qc = np.arange(GRID_W)[:, None]
    kc = np.arange(GRID_W)[None, :]
    win_c0 = np.clip(qc - WIN_C // 2, 0, GRID_W - WIN_C)
    in_win = (kc >= win_c0) & (kc < win_c0 + WIN_C)
    rel_c = np.clip(kc - qc + WIN_C - 1, 0, 2 * WIN_C - 2)
    v = np.arange(WIN_R)[:, None]
    wr = np.arange(WIN_R)[None, :]
    rel_r = wr - v + WIN_R - 1
    tab = rpb.astype(F32)[:, rel_r[:, None, :, None], rel_c[None, :, None, :]]
    tab = jnp.where(in_win[None, None, :, None, :], tab, NEG_INF)
    tab = jnp.transpose(tab, (1, 0, 2, 3, 4))
    return tab.reshape(WIN_R, NA_HEADS, GRID_W, WIN_R * GRID_W)


def _dn_prep_kernel(x_ref, hp_ref, hn_ref, w_ref, rc_ref, rs_ref, ab_ref, alog_ref, dtb_ref,
                    q_ref, k_ref, v_ref, g_ref, beta_ref, xpad, *, s_len, ta):
    i = pl.program_id(1)
    t0 = i * PREP_TILE
    has_prev = jnp.logical_and(t0 != 0, t0 != s_len).astype(F32)
    has_next = jnp.logical_and(t0 + PREP_TILE != s_len, t0 + PREP_TILE != ta).astype(F32)
    half = HALO // 2
    xpad[0:half, :] = hp_ref[0].astype(F32)[half:, :] * has_prev
    xpad[half:half + PREP_TILE, :] = x_ref[0].astype(F32)
    xpad[half + PREP_TILE:, :] = hn_ref[0].astype(F32)[:half, :] * has_next
    y = jnp.zeros((PREP_TILE, x_ref.shape[-1]), F32)
    for tap in range(DN_CONV):
        y = y + w_ref[tap:tap + 1, :] * xpad[pl.ds(half - DN_CONV // 2 + tap, PREP_TILE), :]
    y = _silu(y)

    lane = lax.broadcasted_iota(jnp.int32, (PREP_TILE, LANES), 1)
    first_half = (lane % (LANES // 2)) < (LANES // 4)
    rc = rc_ref[...]
    rs = rs_ref[...]

    def norm_rope(u, scale):
        u = u * (lax.rsqrt(jnp.sum(u * u, axis=-1, keepdims=True) + EPS) * scale)
        partner = jnp.where(first_half, pltpu.roll(u, LANES - LANES // 4, 1), pltpu.roll(u, LANES // 4, 1))
        return u * rc + partner * rs

    for h in range(DN_HEADS):
        lanes = slice(h * LANES, (h + 1) * LANES)
        qh = norm_rope(y[:, h * LANES:(h + 1) * LANES], DN_HEAD_DIM ** -0.5)
        kh = norm_rope(y[:, DN_WIDTH + h * LANES:DN_WIDTH + (h + 1) * LANES], 1.0)
        q_ref[0, :, lanes] = qh.astype(BF16)
        k_ref[0, :, lanes] = kh.astype(BF16)
    v_ref[0] = y[:, 2 * DN_WIDTH:].astype(BF16)

    ab = ab_ref[0]
    z = ab + dtb_ref[...]
    softplus = jnp.maximum(z, 0.0) + jnp.log1p(jnp.exp(-jnp.abs(z)))
    g = -jnp.exp(alog_ref[...]) * softplus
    g = jnp.where(lane < 2 * DN_HEADS, g, 0.0)
    beta = jax.nn.sigmoid(pltpu.roll(ab, LANES - 2 * DN_HEADS, 1))
    beta_ref[0] = jnp.where(lane < 2 * DN_HEADS, beta, 0.0)

    r = lax.broadcasted_iota(jnp.int32, (PREP_TILE, PREP_TILE), 0)
    c = lax.broadcasted_iota(jnp.int32, (PREP_TILE, PREP_TILE), 1)
    same = (r // DN_CHUNK) == (c // DN_CHUNK)
    tri_f = jnp.where(same & (c <= r), 1.0, 0.0).astype(BF16)
    tri_b = jnp.where(same & (c >= r), 1.0, 0.0).astype(BF16)
    g1 = g.astype(BF16)
    rem = g - g1.astype(F32)
    g2 = rem.astype(BF16)
    g3 = (rem - g2.astype(F32)).astype(BF16)
    cs_f = _dot(tri_f, g1) + _dot(tri_f, g2) + _dot(tri_f, g3)
    cs_b = _dot(tri_b, g1) + _dot(tri_b, g2) + _dot(tri_b, g3)
    g_ref[0] = jnp.where(lane < DN_HEADS, cs_f, cs_b)


def _dn_prep(p, ab, conv_w, rope_c, rope_s, alog_row, dtb_row, *, s_len, col_dn):
    b, ta, _ = p.shape
    w3 = 3 * DN_WIDTH
    n_t = ta // PREP_TILE
    hb = PREP_TILE // HALO
    n_hb = ta // HALO
    tok = lambda: pl.BlockSpec((1, PREP_TILE, DN_WIDTH), lambda bb, i: (bb, i, 0))
    sca = lambda: pl.BlockSpec((1, PREP_TILE, LANES), lambda bb, i: (bb, i, 0))
    row = lambda: pl.BlockSpec((1, LANES), lambda bb, i: (0, 0))
    return pl.pallas_call(
        functools.partial(_dn_prep_kernel, s_len=s_len, ta=ta),
        grid=(b, n_t),
        in_specs=[pl.BlockSpec((1, PREP_TILE, w3), lambda bb, i: (bb, i, col_dn)),
                  pl.BlockSpec((1, HALO, w3), lambda bb, i: (bb, jnp.maximum(i * hb - 1, 0), col_dn)),
                  pl.BlockSpec((1, HALO, w3), lambda bb, i: (bb, jnp.minimum((i + 1) * hb, n_hb - 1), col_dn)),
                  pl.BlockSpec((DN_CONV, w3), lambda bb, i: (0, 0)),
                  pl.BlockSpec((PREP_TILE, LANES), lambda bb, i: (i, 0)),
                  pl.BlockSpec((PREP_TILE, LANES), lambda bb, i: (i, 0)),
                  sca(), row(), row()],
        out_specs=[tok(), tok(), tok(), sca(), sca()],
        out_shape=[jax.ShapeDtypeStruct((b, ta, DN_WIDTH), BF16)] * 3
        + [jax.ShapeDtypeStruct((b, ta, LANES), F32)] * 2,
        scratch_shapes=[pltpu.VMEM((PREP_TILE + HALO, w3), F32)],
        compiler_params=_cparams(("parallel", "parallel")),
        name="dn_prep",
    )(p, p, p, conv_w, rope_c, rope_s, ab, alog_row, dtb_row)


CH4 = DN_HEADS * DN_CHUNK


def _stack_heads(x):
    return jnp.concatenate([x[:, h * LANES:(h + 1) * LANES] for h in range(DN_HEADS)], axis=0)


def _dn_local_kernel(q_ref, k_ref, v_ref, gc_ref, bc_ref, gr_ref, br_ref, gl_ref,
                     wq_ref, u_ref, a_ref, kgt_ref):
    kst = _stack_heads(k_ref[0])
    qst = _stack_heads(q_ref[0])
    vst = _stack_heads(v_ref[0])
    kq = _dot_nt(jnp.concatenate([kst, qst], axis=0), kst)
    kk = kq[:CH4]
    qk = kq[CH4:]
    r = lax.broadcasted_iota(jnp.int32, (CH4, CH4), 0)
    c = lax.broadcasted_iota(jnp.int32, (CH4, CH4), 1)
    same = (r // DN_CHUNK) == (c // DN_CHUNK)
    eye = (r == c).astype(F32)
    gc = gc_ref[0]
    bc = bc_ref[0]
    for d in range(2):
        later = (r >= c) if d == 0 else (r <= c)
        incl = same & later
        strict = incl & (r != c)
        col = lambda a: jnp.concatenate(
            [a[:, d * DN_HEADS + h:d * DN_HEADS + h + 1] for h in range(DN_HEADS)], axis=0)
        g_col = col(gc)
        b_col = col(bc)
        g_row = gr_ref[0, 0, d:d + 1, :]
        b_row = br_ref[0, 0, d:d + 1, :]
        gl_row = gl_ref[0, 0, d:d + 1, :]
        decay = jnp.exp(jnp.where(incl, g_col - g_row, NEG_INF))
        lmat = jnp.where(strict, kk * decay * b_col, 0.0)
        pw = -lmat
        tinv = eye + pw
        for _ in range(5):
            pwb = pw.astype(BF16)
            pw = _dot(pwb, pwb)
            tinv = tinv + _dot(tinv.astype(BF16), pw.astype(BF16))
        u = _dot((tinv * b_row).astype(BF16), vst)
        w = _dot((tinv * (b_row * jnp.exp(g_row))).astype(BF16), kst)
        qg = qst.astype(F32) * jnp.exp(g_col)
        kg = kst.astype(F32) * jnp.exp(col_bcast_last(gl_row, g_col))
        wq_ref[0, 0, d, :CH4, :] = w.astype(BF16)
        wq_ref[0, 0, d, CH4:, :] = qg.astype(BF16)
        u_ref[0, 0, d] = u
        a_ref[0, 0, d] = (qk * decay).astype(BF16)
        kgt_ref[0, 0, d] = kg.T.astype(BF16)


def col_bcast_last(gl_row, g_col):
    r = lax.broadcasted_iota(jnp.int32, (CH4, CH4), 0)
    c = lax.broadcasted_iota(jnp.int32, (CH4, CH4), 1)
    gl_col = jnp.sum(jnp.where(r == c, jnp.broadcast_to(gl_row, (CH4, CH4)), 0.0), axis=1, keepdims=True)
    return gl_col - g_col


def _dn_local(qd, kd, vd, gc, bc, g_rows, b_rows, gl_rows):
    b, ta, _ = qd.shape
    nc = ta // DN_CHUNK
    tok = lambda w: pl.BlockSpec((1, DN_CHUNK, w), lambda bb, n: (bb, n, 0))
    rows = lambda: pl.BlockSpec((1, 1, 2, CH4), lambda bb, n: (bb, n, 0, 0))
    out5 = lambda r_, c_: pl.BlockSpec((1, 1, 2, r_, c_), lambda bb, n: (bb, n, 0, 0, 0))
    return pl.pallas_call(
        _dn_local_kernel,
        grid=(b, nc),
        in_specs=[tok(DN_WIDTH), tok(DN_WIDTH), tok(DN_WIDTH), tok(LANES), tok(LANES),
                  rows(), rows(), rows()],
        out_specs=[out5(2 * CH4, LANES), out5(CH4, LANES), out5(CH4, CH4), out5(LANES, CH4)],
        out_shape=[jax.ShapeDtypeStruct((b, nc, 2, 2 * CH4, LANES), BF16),
                   jax.ShapeDtypeStruct((b, nc, 2, CH4, LANES), F32),
                   jax.ShapeDtypeStruct((b, nc, 2, CH4, CH4), BF16),
                   jax.ShapeDtypeStruct((b, nc, 2, LANES, CH4), BF16)],
        compiler_params=_cparams(("parallel", "parallel")),
        name="dn_local",
    )(qd, kd, vd, gc, bc, g_rows, b_rows, gl_rows)


def _dn_scan_kernel(wq_f, u_f, a_f, kgt_f, gl_f, wq_b, u_b, a_b, kgt_b, gl_b, of_ref, ob_ref, state):
    n = pl.program_id(1)

    @pl.when(n == 0)
    def _():
        state[...] = jnp.zeros_like(state)

    r = lax.broadcasted_iota(jnp.int32, (CH4, DN_WIDTH), 0)
    c = lax.broadcasted_iota(jnp.int32, (CH4, DN_WIDTH), 1)
    own = (r // DN_CHUNK) == (c // LANES)
    for d, (wq, u, a, kgt, gl, o_ref) in enumerate(((wq_f, u_f, a_f, kgt_f, gl_f, of_ref),
                                                     (wq_b, u_b, a_b, kgt_b, gl_b, ob_ref))):
        s_all = state[d]
        ws = _dot(wq[0, 0, 0], s_all.astype(BF16))
        pick = lambda m: jnp.concatenate(
            [m[h * DN_CHUNK:(h + 1) * DN_CHUNK, h * LANES:(h + 1) * LANES] for h in range(DN_HEADS)], axis=0)
        v_new = u[0, 0, 0] - pick(ws[:CH4])
        v_nb = v_new.astype(BF16)
        o = pick(ws[CH4:]) + _dot(a[0, 0, 0], v_nb)
        for h in range(DN_HEADS):
            o_ref[0, :, h * LANES:(h + 1) * LANES] = o[h * DN_CHUNK:(h + 1) * DN_CHUNK]
        v_bd = jnp.where(own, jnp.concatenate([v_nb] * DN_HEADS, axis=1), jnp.zeros((), BF16))
        state[d] = s_all * jnp.exp(gl[0, 0, 0]) + _dot(kgt[0, 0, 0], v_bd)


def _dn_scan(wq, u, a, kgt, gl_wide, *, s_len):
    b, nc = wq.shape[:2]
    n_lat = s_len // DN_CHUNK
    n_ctx = nc - n_lat
    fwd = lambda n: jnp.where(n < n_ctx, n_lat + n, n - n_ctx)
    bwd = lambda n: nc - 1 - n
    specs = []
    for d, order in ((0, fwd), (1, bwd)):
        blk = lambda arr, d=d, order=order: pl.BlockSpec(
            (1, 1, 1) + arr.shape[3:], lambda bb, n: (bb, order(n), d, 0, 0))
        specs += [blk(wq), blk(u), blk(a), blk(kgt), blk(gl_wide)]
    o_spec = lambda order: pl.BlockSpec((1, DN_CHUNK, DN_WIDTH), lambda bb, n: (bb, order(n), 0))
    return pl.pallas_call(
        _dn_scan_kernel,
        grid=(b, nc),
        in_specs=specs,
        out_specs=[o_spec(fwd), o_spec(bwd)],
        out_shape=[jax.ShapeDtypeStruct((b, nc * DN_CHUNK, DN_WIDTH), F32)] * 2,
        scratch_shapes=[pltpu.VMEM((2, DN_HEAD_DIM, DN_WIDTH), F32)],
        compiler_params=_cparams(("parallel", "arbitrary")),
        name="dn_scan",
    )(wq, u, a, kgt, gl_wide, wq, u, a, kgt, gl_wide)


def _merge_kernel(ona_ref, of_ref, ob_ref, z_ref, sc_ref, schp_ref, schn_ref, gate_ref, x_ref,
                  g1l_ref, g1c_ref, sh2l_ref, sh2c_ref, sc2l_ref, sc2c_ref, ng_ref, og_ref, scw_ref,
                  wna_ref, wdn_ref, wsc_ref, wout_ref, wr_ref, rb_ref,
                  x1_ref, h2_ref, comb_ref, cpad, *, tm, s_len, ta):
    i = pl.program_id(1)
    d = x_ref.shape[-1]
    o = of_ref[0] + ob_ref[0]
    z = z_ref[0].astype(F32)
    parts = []
    for h in range(DN_HEADS):
        oh = o[:, h * LANES:(h + 1) * LANES]
        parts.append(oh * lax.rsqrt(jnp.mean(oh * oh, axis=-1, keepdims=True) + EPS))
    o_dn = jnp.concatenate(parts, axis=1) * og_ref[...] * _silu(z)

    def cu(ref_val):
        v = ref_val.astype(F32)
        return v[:, 2 * SC_WIDTH:] * v[:, :SC_WIDTH]
    scp = sc_ref[0].astype(F32)
    half = HALO // 2
    cpad[0:half, :] = cu(schp_ref[0])[half:, :]
    cpad[half:half + tm, :] = scp[:, 2 * SC_WIDTH:] * scp[:, :SC_WIDTH]
    cpad[half + tm:, :] = cu(schn_ref[0])[:half, :]
    gt = i * tm + lax.broadcasted_iota(jnp.int32, (tm, 1), 0)
    has_prev = (gt != 0) & (gt != s_len)
    has_next = (gt != s_len - 1) & (gt != ta - 1)
    conv = (scw_ref[0:1, :] * jnp.where(has_prev, cpad[pl.ds(half - 1, tm), :], 0.0)
            + scw_ref[1:2, :] * cpad[pl.ds(half, tm), :]
            + scw_ref[2:3, :] * jnp.where(has_next, cpad[pl.ds(half + 1, tm), :], 0.0))
    o_sc = scp[:, SC_WIDTH:2 * SC_WIDTH] * conv

    merged = (gate_ref[0, :, :d].astype(F32) * _dot(ona_ref[0], wna_ref[...])
              + gate_ref[0, :, d:2 * d].astype(F32) * _dot(o_dn.astype(BF16), wdn_ref[...])
              + gate_ref[0, :, 2 * d:].astype(F32) * _dot(o_sc.astype(BF16), wsc_ref[...]))
    y = _dot(merged.astype(BF16), wout_ref[...])
    x1 = x_ref[0] + _row_select(i, tm, s_len, g1l_ref[0], g1c_ref[...]) * y
    x1_ref[0] = x1

    hn = x1 * lax.rsqrt(jnp.mean(x1 * x1, axis=-1, keepdims=True) + EPS) * ng_ref[...]
    h2 = (hn * (1.0 + _row_select(i, tm, s_len, sc2l_ref[0], sc2c_ref[...]))
          + _row_select(i, tm, s_len, sh2l_ref[0], sh2c_ref[...]))
    h2_ref[0] = h2.astype(BF16)

    logits = jnp.dot(h2, wr_ref[...], precision=HIGHEST, preferred_element_type=F32)
    scores = jax.nn.sigmoid(logits.T)
    biased = scores + rb_ref[...]
    rows = [biased[e:e + 1, :] for e in range(N_EXPERTS)]
    gsc = []
    for g in range(N_GROUPS):
        a, b_, c_, d_ = rows[4 * g:4 * g + 4]
        gsc.append(jnp.maximum(jnp.maximum(jnp.maximum(a + b_, a + c_), jnp.maximum(a + d_, b_ + c_)),
                               jnp.maximum(b_ + d_, c_ + d_)))
    best = gsc[0]
    grp = jnp.zeros_like(best, dtype=jnp.int32)
    for g in range(1, N_GROUPS):
        upd = gsc[g] > best
        best = jnp.where(upd, gsc[g], best)
        grp = jnp.where(upd, g, grp)
    eidx = lax.broadcasted_iota(jnp.int32, biased.shape, 0)
    masked = jnp.where((eidx // EXPERTS_PER_GROUP) == grp, biased, -jnp.inf)
    big = jnp.int32(LANES)
    m1 = masked.max(axis=0, keepdims=True)
    i1 = jnp.where(masked == m1, eidx, big).min(axis=0, keepdims=True)
    masked2 = jnp.where(eidx == i1, -jnp.inf, masked)
    m2 = masked2.max(axis=0, keepdims=True)
    i2 = jnp.where(masked2 == m2, eidx, big).min(axis=0, keepdims=True)
    sel1 = eidx == i1
    sel2 = eidx == i2
    s1 = jnp.sum(jnp.where(sel1, scores, 0.0), axis=0, keepdims=True)
    s2 = jnp.sum(jnp.where(sel2, scores, 0.0), axis=0, keepdims=True)
    tot = s1 + s2
    comb_t = jnp.where(sel1, s1 / tot, 0.0) + jnp.where(sel2, s2 / tot, 0.0)
    comb_ref[0] = comb_t.T


def _merge(o_na, o_f, o_b, p, xs, mods_l, mods_c, norm_g, out_g, sc_w, wna, wdn, wsc, wout, wr, rb,
           *, s_len, tm, col_z, col_sc):
    b, ta, d = xs.shape
    hb = tm // HALO
    n_hb = ta // HALO
    tok = lambda w, col=0: pl.BlockSpec((1, tm, w), lambda bb, i: (bb, i, col))
    vec = lambda w: pl.BlockSpec((1, w), lambda bb, i: (0, 0))
    bvec = lambda: pl.BlockSpec((1, 1, d), lambda bb, i: (bb, 0, 0))
    full = lambda a: pl.BlockSpec(a.shape, lambda bb, i: (0,) * a.ndim)
    g1l, sh2l, sc2l = mods_l
    g1c, sh2c, sc2c = mods_c
    w3 = 3 * SC_WIDTH
    return pl.pallas_call(
        functools.partial(_merge_kernel, tm=tm, s_len=s_len, ta=ta),
        grid=(b, ta // tm),
        in_specs=[tok(NA_WIDTH), tok(DN_WIDTH), tok(DN_WIDTH), tok(DN_WIDTH, col_z), tok(w3, col_sc),
                  pl.BlockSpec((1, HALO, w3), lambda bb, i: (bb, jnp.maximum(i * hb - 1, 0), col_sc)),
                  pl.BlockSpec((1, HALO, w3), lambda bb, i: (bb, jnp.minimum((i + 1) * hb, n_hb - 1), col_sc)),
                  tok(3 * d, 0), tok(d),
                  bvec(), vec(d), bvec(), vec(d), bvec(), vec(d), vec(d), vec(DN_WIDTH),
                  full(sc_w), full(wna), full(wdn), full(wsc), full(wout), full(wr), full(rb)],
        out_specs=[tok(d), tok(d), tok(LANES)],
        out_shape=[jax.ShapeDtypeStruct((b, ta, d), F32),
                   jax.ShapeDtypeStruct((b, ta, d), BF16),
                   jax.ShapeDtypeStruct((b, ta, LANES), F32)],
        scratch_shapes=[pltpu.VMEM((tm + HALO, SC_WIDTH), F32)],
        compiler_params=_cparams(("parallel", "parallel")),
        name="merge_router",
    )(o_na, o_f, o_b, p, p, p, p, p, xs, g1l, g1c, sh2l, sh2c, sc2l, sc2c, norm_g, out_g,
      sc_w, wna, wdn, wsc, wout, wr, rb)


def _moe_kernel(h_ref, comb_ref, wgu_ref, wd_ref, x_ref, g2l_ref, g2c_ref, o_ref, acc, *, tm, s_len):
    i = pl.program_id(1)
    e = pl.program_id(2)

    @pl.when(e == 0)
    def _():
        acc[...] = jnp.zeros_like(acc)

    gu = _dot(h_ref[0], wgu_ref[0])
    hidden = _silu(gu[:, :D_EXPERT]) * gu[:, D_EXPERT:]
    lane = lax.broadcasted_iota(jnp.int32, (tm, LANES), 1)
    cw = jnp.sum(jnp.where(lane == e, comb_ref[0], 0.0), axis=-1, keepdims=True)
    acc[...] += _dot((hidden * cw).astype(BF16), wd_ref[0])

    @pl.when(e == N_EXPERTS - 1)
    def _():
        o_ref[0] = x_ref[0] + _row_select(i, tm, s_len, g2l_ref[0], g2c_ref[...]) * acc[...]


def _moe(h2, comb, wgu, wd, x1, g2l, g2c, *, s_len, tm):
    b, ta, d = x1.shape
    tok = lambda w: pl.BlockSpec((1, tm, w), lambda bb, i, e: (bb, i, 0))
    return pl.pallas_call(
        functools.partial(_moe_kernel, tm=tm, s_len=s_len),
        grid=(b, ta // tm, N_EXPERTS),
        in_specs=[tok(d), tok(LANES),
                  pl.BlockSpec((1, d, 2 * D_EXPERT), lambda bb, i, e: (e, 0, 0)),
                  pl.BlockSpec((1, D_EXPERT, d), lambda bb, i, e: (e, 0, 0)),
                  tok(d),
                  pl.BlockSpec((1, 1, d), lambda bb, i, e: (bb, 0, 0)),
                  pl.BlockSpec((1, d), lambda bb, i, e: (0, 0))],
        out_specs=tok(d),
        out_shape=jax.ShapeDtypeStruct((b, ta, d), F32),
        scratch_shapes=[pltpu.VMEM((tm, d), F32)],
        compiler_params=_cparams(("parallel", "parallel", "arbitrary")),
        name="moe_dense",
    )(h2, comb, wgu, wd, x1, g2l, g2c)


def _rope_tables(s_len, ctx_len):
    t = jnp.arange(s_len)
    n_freq = DN_HEAD_DIM // 4
    inv_freq = ROPE_BASE ** (-jnp.arange(n_freq, dtype=F32) / n_freq)
    pos = jnp.stack([t // GRID_W, t % GRID_W], axis=-1).astype(F32)
    ang = pos[..., None] * inv_freq
    cos, sin = jnp.cos(ang), jnp.sin(ang)
    rc = jnp.concatenate([cos[:, 0], cos[:, 0], cos[:, 1], cos[:, 1]], axis=-1)
    rs = jnp.concatenate([-sin[:, 0], sin[:, 0], -sin[:, 1], sin[:, 1]], axis=-1)
    rc = jnp.concatenate([rc, jnp.ones((ctx_len, LANES), F32)], axis=0)
    rs = jnp.concatenate([rs, jnp.zeros((ctx_len, LANES), F32)], axis=0)
    return rc, rs


def _pick_tile(ta, candidates):
    for t in candidates:
        if ta % t == 0:
            return t
    raise ValueError(f"no row tile for {ta} tokens")


def kernel(x, c, ctx, c_ctx, w_ada, b_ada, norm_mix_g, norm_ffn_g, w_in, na_q_norm_g, na_k_norm_g, na_rpb, dn_conv_w, dn_a_log, dn_dt_bias, dn_out_norm_g, sc_conv_w, w_branch_na, w_branch_dn, w_branch_sc, w_out, w_router, router_bias, moe_w_gate, moe_w_up, moe_w_down):
    b, s_len, d = x.shape
    ctx_len = ctx.shape[1]
    depth = w_ada.shape[0]
    ta = s_len + ctx_len
    assert s_len % ROW_BLK == 0 and s_len // ROW_BLK >= 2 and s_len % ctx_len == 0
    assert ctx_len % PREP_TILE == 0 and b + 1 <= 8
    tm_proj = _pick_tile(ta, (768, 512, 256))
    tm_merge = _pick_tile(ta, (384, 256))
    tm_moe = _pick_tile(ta, (768, 512, 256))

    xs = jnp.concatenate([x, ctx], axis=1)
    c_rows = jnp.zeros((8, d), F32).at[:b].set(c).at[b].set(c_ctx)
    mod = _modulation(c_rows, w_ada, b_ada)

    sizes = (3 * NA_WIDTH, 3 * DN_WIDTH, DN_WIDTH, 4 * DN_HEADS, 3 * SC_WIDTH, 3 * d)
    cuts = np.cumsum((0,) + sizes)
    seg = lambda w, k: w[:, cuts[k]:cuts[k + 1]]
    col_q = 3 * d // NA_WIDTH
    col_dn = (3 * d + 3 * NA_WIDTH) // (3 * DN_WIDTH)
    col_sc = (3 * d + 3 * NA_WIDTH + 3 * DN_WIDTH) // (3 * SC_WIDTH)
    col_z = (3 * d + 3 * NA_WIDTH + 3 * DN_WIDTH + 3 * SC_WIDTH) // DN_WIDTH

    rope_c, rope_s = _rope_tables(s_len, ctx_len)
    bd = jnp.asarray(np.kron(np.eye(NA_HEADS), np.ones((NA_HEAD_DIM, NA_HEAD_DIM))), BF16)
    nc = ta // DN_CHUNK
    wr = jnp.zeros((d, LANES), F32).at[:, :N_EXPERTS].set(w_router)
    rb = jnp.zeros((LANES, 1), F32).at[:N_EXPERTS, 0].set(router_bias)

    for l in range(depth):
        m = mod[l].reshape(8, N_MOD, d)
        lat = lambda k: m[:b, k][:, None, :]
        cx = lambda k: m[b:b + 1, k]
        wl = w_in[l]
        w_main = jnp.concatenate([seg(wl, 5), seg(wl, 0), seg(wl, 1), seg(wl, 4), seg(wl, 2)],
                                 axis=1).astype(BF16)
        w_ab = jnp.zeros((d, LANES), F32).at[:, :4 * DN_HEADS].set(seg(wl, 3))
        gq = jnp.tile(na_q_norm_g[l], NA_HEADS)[None, :] * (NA_HEAD_DIM ** -0.5)
        gk = jnp.tile(na_k_norm_g[l], NA_HEADS)[None, :]
        p, ab = _projection(xs, (lat(0), lat(1)), (cx(0), cx(1)), norm_mix_g[l][None, :],
                            w_main, w_ab, gq, gk, bd, s_len=s_len, tm=tm_proj)

        o_na = _neighbourhood_attention(p, _na_bias_table(na_rpb[l]), s_len=s_len, ctx_len=ctx_len,
                                        col_q=col_q)

        alog_row = jnp.zeros((1, LANES), F32).at[0, :2 * DN_HEADS].set(dn_a_log[l].reshape(-1))
        dtb_row = jnp.zeros((1, LANES), F32).at[0, :2 * DN_HEADS].set(dn_dt_bias[l].reshape(-1))
        qd, kd, vd, gc, bc = _dn_prep(p, ab, dn_conv_w[l], rope_c, rope_s, alog_row, dtb_row,
                                         s_len=s_len, col_dn=col_dn)
        to_rows = lambda a: jnp.transpose(
            a[..., :2 * DN_HEADS].reshape(b, nc, DN_CHUNK, 2, DN_HEADS), (0, 1, 3, 4, 2)
        ).reshape(b, nc, 2, CH4)
        g8 = gc[..., :2 * DN_HEADS].reshape(b, nc, DN_CHUNK, 2, DN_HEADS)
        g_last = jnp.stack([g8[:, :, DN_CHUNK - 1, 0], g8[:, :, 0, 1]], axis=2)
        gl_rows = jnp.repeat(g_last, DN_CHUNK, axis=-1)
        gl_wide = jnp.repeat(g_last, LANES, axis=-1)[:, :, :, None, :]
        wq, u, a, kgt = _dn_local(qd, kd, vd, gc, bc, to_rows(gc), to_rows(bc), gl_rows)
        o_f, o_b = _dn_scan(wq, u, a, kgt, gl_wide, s_len=s_len)

        x1, h2, comb = _merge(
            o_na, o_f, o_b, p, xs, (lat(2), lat(3), lat(4)), (cx(2), cx(3), cx(4)),
            norm_ffn_g[l][None, :], jnp.tile(dn_out_norm_g[l], DN_HEADS)[None, :], sc_conv_w[l],
            w_branch_na[l].astype(BF16), w_branch_dn[l].astype(BF16), w_branch_sc[l].astype(BF16),
            w_out[l].astype(BF16), wr, rb, s_len=s_len, tm=tm_merge, col_z=col_z, col_sc=col_sc)

        wgu = jnp.concatenate([moe_w_gate[l], moe_w_up[l]], axis=-1).astype(BF16)
        xs = _moe(h2, comb, wgu, moe_w_down[l].astype(BF16), x1, lat(5), cx(5), s_len=s_len, tm=tm_moe)

    return xs[:, :s_len]
```

```python
import functools
import math

import numpy as np
import jax
import jax.numpy as jnp
from jax import lax
from jax.experimental import pallas as pl
from jax.experimental.pallas import tpu as pltpu

F32 = jnp.float32
BF16 = jnp.bfloat16
HIGHEST = lax.Precision.HIGHEST

GRID_W = 64
NA_HEADS = 8
NA_HEAD_DIM = 64
NA_WIDTH = NA_HEADS * NA_HEAD_DIM
WIN_R = 8
WIN_C = 16
DN_HEADS = 4
DN_HEAD_DIM = 128
DN_WIDTH = DN_HEADS * DN_HEAD_DIM
DN_CONV = 5
DN_CHUNK = 64
ROPE_BASE = 10000.0
SC_WIDTH = 512
SC_CONV = 3
N_EXPERTS = 16
N_GROUPS = 4
EXPERTS_PER_GROUP = N_EXPERTS // N_GROUPS
D_EXPERT = 512
N_MOD = 6
EPS = 1e-6
NEG_INF = -1e30

LANES = 128
HALO = 16
ROW_BLK = WIN_R * GRID_W
PREP_TILE = 256
VMEM_LIMIT = 56 * 1024 * 1024

COL_GATE = 0
N_PROJ_TILE = 1024


def _cparams(sem):
    return pltpu.CompilerParams(dimension_semantics=sem, vmem_limit_bytes=VMEM_LIMIT)


def _dot(a, b):
    return jnp.dot(a, b, preferred_element_type=F32)


def _dot_nt(a, b):
    return lax.dot_general(a, b, (((1,), (1,)), ((), ())), preferred_element_type=F32)


def _silu(x):
    return x * jax.nn.sigmoid(x)


def _row_select(i, tm, s_len, lat, ctx):
    row = i * tm + lax.broadcasted_iota(jnp.int32, (tm, 1), 0)
    return jnp.where(row >= s_len, ctx, lat)


def _mod_kernel(c_ref, w_ref, b_ref, o_ref):
    o_ref[0] = jnp.dot(_silu(c_ref[...]), w_ref[0], precision=HIGHEST,
                       preferred_element_type=F32) + b_ref[0]


def _modulation(c_rows, w_ada, b_ada):
    depth, d, n = w_ada.shape
    tn = 512
    return pl.pallas_call(
        _mod_kernel,
        grid=(depth, n // tn),
        in_specs=[pl.BlockSpec((8, d), lambda l, j: (0, 0)),
                  pl.BlockSpec((1, d, tn), lambda l, j: (l, 0, j)),
                  pl.BlockSpec((1, 1, tn), lambda l, j: (l, 0, j))],
        out_specs=pl.BlockSpec((1, 8, tn), lambda l, j: (l, 0, j)),
        out_shape=jax.ShapeDtypeStruct((depth, 8, n), F32),
        compiler_params=_cparams(("parallel", "parallel")),
        name="adaln_mod",
    )(c_rows, w_ada, b_ada.reshape(depth, 1, n))


def _split_bf16(a):
    hi = a.astype(BF16)
    return hi, (a - hi.astype(F32)).astype(BF16)


def _dot3(a_hi, a_lo, b_hi, b_lo):
    return _dot(a_hi, b_hi) + _dot(a_lo, b_hi) + _dot(a_hi, b_lo)


def _proj_kernel(x_ref, shl_ref, scl_ref, shc_ref, scc_ref, g_ref, w_ref, wabh_ref, wabl_ref, gqk_ref,
                 bd_ref, p_ref, ab_ref, h_scr, *, tm, s_len):
    i = pl.program_id(1)
    j = pl.program_id(2)

    @pl.when(j == 0)
    def _():
        x = x_ref[0]
        y = x * lax.rsqrt(jnp.mean(x * x, axis=-1, keepdims=True) + EPS) * g_ref[...]
        scale = _row_select(i, tm, s_len, scl_ref[0], scc_ref[...])
        shift = _row_select(i, tm, s_len, shl_ref[0], shc_ref[...])
        h = y * (1.0 + scale) + shift
        h_hi, h_lo = _split_bf16(h)
        h_scr[...] = h_hi
        ab_ref[0] = _dot3(h_hi, h_lo, wabh_ref[...], wabl_ref[...])

    acc = _dot(h_scr[...], w_ref[...])
    n_gate = 3 * x_ref.shape[-1] // N_PROJ_TILE

    @pl.when(j < n_gate)
    def _():
        p_ref[0] = jax.nn.sigmoid(acc).astype(BF16)

    @pl.when(j == n_gate)
    def _():
        for half in range(N_PROJ_TILE // NA_WIDTH):
            cols = slice(half * NA_WIDTH, (half + 1) * NA_WIDTH)
            a = acc[:, cols]
            ss = _dot((a * a).astype(BF16), bd_ref[...])
            p_ref[0, :, cols] = (a * lax.rsqrt(ss * (1.0 / NA_HEAD_DIM) + EPS) * gqk_ref[:, cols]).astype(BF16)

    @pl.when(j > n_gate)
    def _():
        p_ref[0] = acc.astype(BF16)


def _projection(xs, mod_l, mod_c, norm_g, w, w_ab, gqk, bd, *, s_len, tm):
    b, ta, d = xs.shape
    n = w.shape[1]
    shl, scl = mod_l
    shc, scc = mod_c
    wab_hi = w_ab.astype(BF16)
    wab_lo = (w_ab - wab_hi.astype(F32)).astype(BF16)
    vec = lambda: pl.BlockSpec((1, d), lambda bb, i, j: (0, 0))
    bvec = lambda: pl.BlockSpec((1, 1, d), lambda bb, i, j: (bb, 0, 0))
    return pl.pallas_call(
        functools.partial(_proj_kernel, tm=tm, s_len=s_len),
        grid=(b, ta // tm, n // N_PROJ_TILE),
        in_specs=[pl.BlockSpec((1, tm, d), lambda bb, i, j: (bb, i, 0)),
                  bvec(), bvec(), vec(), vec(), vec(),
                  pl.BlockSpec((d, N_PROJ_TILE), lambda bb, i, j: (0, j)),
                  pl.BlockSpec((d, LANES), lambda bb, i, j: (0, 0)),
                  pl.BlockSpec((d, LANES), lambda bb, i, j: (0, 0)),
                  pl.BlockSpec((1, N_PROJ_TILE), lambda bb, i, j: (0, 0)),
                  pl.BlockSpec((NA_WIDTH, NA_WIDTH), lambda bb, i, j: (0, 0))],
        out_specs=[pl.BlockSpec((1, tm, N_PROJ_TILE), lambda bb, i, j: (bb, i, j)),
                   pl.BlockSpec((1, tm, LANES), lambda bb, i, j: (bb, i, 0))],
        out_shape=[jax.ShapeDtypeStruct((b, ta, n), BF16),
                   jax.ShapeDtypeStruct((b, ta, LANES), F32)],
        scratch_shapes=[pltpu.VMEM((tm, d), BF16)],
        compiler_params=_cparams(("parallel", "parallel", "arbitrary")),
        name="in_proj",
    )(xs, shl, scl, shc, scc, norm_g, w, wab_hi, wab_lo, gqk, bd)


def _softmax_pv(q2, keys, vals, biases):
    lane = lax.broadcasted_iota(jnp.int32, q2.shape, 1)
    outs = []
    for hh in range(2):
        in_head = (lane >= hh * NA_HEAD_DIM) & (lane < (hh + 1) * NA_HEAD_DIM)
        qm = jnp.where(in_head, q2, jnp.zeros_like(q2))
        logits = []
        for k2, bias in zip(keys, biases):
            s = _dot_nt(qm, k2)
            if bias is not None:
                s = s + bias[hh]
            logits.append(s)
        m = logits[0].max(axis=-1, keepdims=True)
        for s in logits[1:]:
            m = jnp.maximum(m, s.max(axis=-1, keepdims=True))
        ps = [jnp.exp(s - m) for s in logits]
        denom = ps[0].sum(axis=-1, keepdims=True)
        for p in ps[1:]:
            denom = denom + p.sum(axis=-1, keepdims=True)
        o = _dot(ps[0].astype(BF16), vals[0])
        for p, v2 in zip(ps[1:], vals[1:]):
            o = o + _dot(p.astype(BF16), v2)
        outs.append(o / denom)
    lane_o = lax.broadcasted_iota(jnp.int32, outs[0].shape, 1)
    return jnp.where(lane_o < NA_HEAD_DIM, outs[0], outs[1])


def _na_kernel(q_ref, kp_ref, kc_ref, kn_ref, vp_ref, vc_ref, vn_ref, kx_ref, vx_ref, bias_ref, o_ref,
               kbuf, vbuf, *, n_rows):
    i = pl.program_id(1)
    for slot, (kr, vr) in enumerate(((kp_ref, vp_ref), (kc_ref, vc_ref), (kn_ref, vn_ref))):
        kbuf[slot * ROW_BLK:(slot + 1) * ROW_BLK, :] = kr[0]
        vbuf[slot * ROW_BLK:(slot + 1) * ROW_BLK, :] = vr[0]

    win = []
    for j in range(WIN_R):
        r = i * WIN_R + j
        start = jnp.clip(r - WIN_R // 2, 0, n_rows - WIN_R)
        win.append((pl.multiple_of((start - (i - 1) * WIN_R) * GRID_W, GRID_W), r - start))

    for hp in range(NA_HEADS // 2):
        lanes = slice(hp * LANES, (hp + 1) * LANES)
        q2 = q_ref[0, :, lanes]
        kx, vx = kx_ref[0, :, lanes], vx_ref[0, :, lanes]
        lane = lax.broadcasted_iota(jnp.int32, q2.shape, 1)
        outs = []
        for hh in range(2):
            in_head = (lane >= hh * NA_HEAD_DIM) & (lane < (hh + 1) * NA_HEAD_DIM)
            qm = jnp.where(in_head, q2, jnp.zeros_like(q2))
            s_c = _dot_nt(qm, kx)
            s_w = jnp.concatenate(
                [_dot_nt(qm[j * GRID_W:(j + 1) * GRID_W], kbuf[pl.ds(win[j][0], ROW_BLK), lanes])
                 + bias_ref[win[j][1], 2 * hp + hh] for j in range(WIN_R)], axis=0)
            m = jnp.maximum(s_w.max(axis=-1, keepdims=True), s_c.max(axis=-1, keepdims=True))
            p_w = jnp.exp(s_w - m)
            p_c = jnp.exp(s_c - m)
            denom = p_w.sum(axis=-1, keepdims=True) + p_c.sum(axis=-1, keepdims=True)
            p_wb = p_w.astype(BF16)
            o = _dot(p_c.astype(BF16), vx) + jnp.concatenate(
                [_dot(p_wb[j * GRID_W:(j + 1) * GRID_W], vbuf[pl.ds(win[j][0], ROW_BLK), lanes])
                 for j in range(WIN_R)], axis=0)
            outs.append(o / denom)
        lane_o = lax.broadcasted_iota(jnp.int32, outs[0].shape, 1)
        o_ref[0, :, lanes] = jnp.where(lane_o < NA_HEAD_DIM, outs[0], outs[1]).astype(BF16)


def _ctx_attn_kernel(q_ref, k_ref, v_ref, prev_ref, o_ref):
    del prev_ref
    for hp in range(NA_HEADS // 2):
        lanes = slice(hp * LANES, (hp + 1) * LANES)
        o2 = _softmax_pv(q_ref[0, :, lanes], [k_ref[0, :, lanes]], [v_ref[0, :, lanes]], [None])
        o_ref[0, :, lanes] = o2.astype(BF16)


def _neighbourhood_attention(p, bias_tab, *, s_len, ctx_len, col_q):
    b, ta, _ = p.shape
    n_rows = s_len // GRID_W
    n_blk = n_rows // WIN_R
    cq, ck, cv = col_q, col_q + 1, col_q + 2
    ctx_blk = s_len // ctx_len
    rb = lambda off: (lambda bb, i: (bb, jnp.clip(i + off, 0, n_blk - 1), 0))
    kv = lambda col, off: pl.BlockSpec(
        (1, ROW_BLK, NA_WIDTH), lambda bb, i: (bb, jnp.clip(i + off, 0, n_blk - 1), col))
    cx = lambda col: pl.BlockSpec((1, ctx_len, NA_WIDTH), lambda bb, i: (bb, ctx_blk, col))
    o_lat = pl.pallas_call(
        functools.partial(_na_kernel, n_rows=n_rows),
        grid=(b, n_blk),
        in_specs=[pl.BlockSpec((1, ROW_BLK, NA_WIDTH), lambda bb, i: (bb, i, cq)),
                  kv(ck, -1), kv(ck, 0), kv(ck, 1), kv(cv, -1), kv(cv, 0), kv(cv, 1),
                  cx(ck), cx(cv),
                  pl.BlockSpec(bias_tab.shape, lambda bb, i: (0, 0, 0, 0))],
        out_specs=pl.BlockSpec((1, ROW_BLK, NA_WIDTH), lambda bb, i: (bb, i, 0)),
        out_shape=jax.ShapeDtypeStruct((b, ta, NA_WIDTH), BF16),
        scratch_shapes=[pltpu.VMEM((3 * ROW_BLK, NA_WIDTH), BF16),
                        pltpu.VMEM((3 * ROW_BLK, NA_WIDTH), BF16)],
        compiler_params=_cparams(("parallel", "arbitrary")),
        name="na_attention",
    )(p, p, p, p, p, p, p, p, p, bias_tab)
    del rb
    cblk = lambda col: pl.BlockSpec((1, ctx_len, NA_WIDTH), lambda bb: (bb, ctx_blk, col))
    return pl.pallas_call(
        _ctx_attn_kernel,
        grid=(b,),
        in_specs=[cblk(cq), cblk(ck), cblk(cv), pl.BlockSpec(memory_space=pl.ANY)],
        out_specs=pl.BlockSpec((1, ctx_len, NA_WIDTH), lambda bb: (bb, ctx_blk, 0)),
        out_shape=jax.ShapeDtypeStruct((b, ta, NA_WIDTH), BF16),
        input_output_aliases={3: 0},
        compiler_params=_cparams(("parallel",)),
        name="ctx_attention",
    )(p, p, p, o_lat)


def _na_bias_table(rpb):
    qc = np.arange(GRID_W)[:, None]
    kc = np.arange(GRID_W)[None, :]
    win_c0 = np.clip(qc - WIN_C // 2, 0, GRID_W - WIN_C)
    in_win = (kc >= win_c0) & (kc < win_c0 + WIN_C)
    rel_c = np.clip(kc - qc + WIN_C - 1, 0, 2 * WIN_C - 2)
    pick_c = (rel_c[..., None] == np.arange(2 * WIN_C - 1)).astype(np.float32)
    by_col = jnp.einsum('hab,qkb->haqk', rpb.astype(F32), pick_c, precision=HIGHEST)
    by_col = jnp.where(in_win[None, None], by_col, NEG_INF)
    tab = jnp.stack([by_col[:, WIN_R - 1 - v:2 * WIN_R - 1 - v] for v in range(WIN_R)], axis=0)
    tab = jnp.transpose(tab, (0, 1, 3, 2, 4))
    return tab.reshape(WIN_R, NA_HEADS, GRID_W, WIN_R * GRID_W)


def _dn_prep_kernel(x_ref, hp_ref, hn_ref, w_ref, rc_ref, rs_ref, ab_ref, alog_ref, dtb_ref,
                    q_ref, k_ref, v_ref, g_ref, beta_ref, xpad, *, s_len, ta):
    i = pl.program_id(1)
    t0 = i * PREP_TILE
    has_prev = jnp.logical_and(t0 != 0, t0 != s_len).astype(F32)
    has_next = jnp.logical_and(t0 + PREP_TILE != s_len, t0 + PREP_TILE != ta).astype(F32)
    half = HALO // 2
    xpad[0:half, :] = hp_ref[0].astype(F32)[half:, :] * has_prev
    xpad[half:half + PREP_TILE, :] = x_ref[0].astype(F32)
    xpad[half + PREP_TILE:, :] = hn_ref[0].astype(F32)[:half, :] * has_next
    y = jnp.zeros((PREP_TILE, x_ref.shape[-1]), F32)
    for tap in range(DN_CONV):
        y = y + w_ref[tap:tap + 1, :] * xpad[pl.ds(half - DN_CONV // 2 + tap, PREP_TILE), :]
    y = _silu(y)

    lane = lax.broadcasted_iota(jnp.int32, (PREP_TILE, LANES), 1)
    first_half = (lane % (LANES // 2)) < (LANES // 4)
    rc = rc_ref[...]
    rs = rs_ref[...]

    def norm_rope(u, scale):
        u = u * (lax.rsqrt(jnp.sum(u * u, axis=-1, keepdims=True) + EPS) * scale)
        partner = jnp.where(first_half, pltpu.roll(u, LANES - LANES // 4, 1), pltpu.roll(u, LANES // 4, 1))
        return u * rc + partner * rs

    for h in range(DN_HEADS):
        lanes = slice(h * LANES, (h + 1) * LANES)
        qh = norm_rope(y[:, h * LANES:(h + 1) * LANES], DN_HEAD_DIM ** -0.5)
        kh = norm_rope(y[:, DN_WIDTH + h * LANES:DN_WIDTH + (h + 1) * LANES], 1.0)
        q_ref[0, :, lanes] = qh.astype(BF16)
        k_ref[0, :, lanes] = kh.astype(BF16)
    v_ref[0] = y[:, 2 * DN_WIDTH:].astype(BF16)

    ab = ab_ref[0]
    z = ab + dtb_ref[...]
    softplus = jnp.maximum(z, 0.0) + jnp.log1p(jnp.exp(-jnp.abs(z)))
    g = -jnp.exp(alog_ref[...]) * softplus
    g = jnp.where(lane < 2 * DN_HEADS, g, 0.0)
    beta = jax.nn.sigmoid(pltpu.roll(ab, LANES - 2 * DN_HEADS, 1))
    beta_ref[0] = jnp.where(lane < 2 * DN_HEADS, beta, 0.0)

    r = lax.broadcasted_iota(jnp.int32, (PREP_TILE, PREP_TILE), 0)
    c = lax.broadcasted_iota(jnp.int32, (PREP_TILE, PREP_TILE), 1)
    same = (r // DN_CHUNK) == (c // DN_CHUNK)
    tri_f = jnp.where(same & (c <= r), 1.0, 0.0).astype(BF16)
    tri_b = jnp.where(same & (c >= r), 1.0, 0.0).astype(BF16)
    g1 = g.astype(BF16)
    rem = g - g1.astype(F32)
    g2 = rem.astype(BF16)
    g3 = (rem - g2.astype(F32)).astype(BF16)
    cs_f = _dot(tri_f, g1) + _dot(tri_f, g2) + _dot(tri_f, g3)
    cs_b = _dot(tri_b, g1) + _dot(tri_b, g2) + _dot(tri_b, g3)
    g_ref[0] = jnp.where(lane < DN_HEADS, cs_f, cs_b)


def _dn_prep(p, ab, conv_w, rope_c, rope_s, alog_row, dtb_row, *, s_len, col_dn):
    b, ta, _ = p.shape
    w3 = 3 * DN_WIDTH
    n_t = ta // PREP_TILE
    hb = PREP_TILE // HALO
    n_hb = ta // HALO
    tok = lambda: pl.BlockSpec((1, PREP_TILE, DN_WIDTH), lambda bb, i: (bb, i, 0))
    sca = lambda: pl.BlockSpec((1, PREP_TILE, LANES), lambda bb, i: (bb, i, 0))
    row = lambda: pl.BlockSpec((1, LANES), lambda bb, i: (0, 0))
    return pl.pallas_call(
        functools.partial(_dn_prep_kernel, s_len=s_len, ta=ta),
        grid=(b, n_t),
        in_specs=[pl.BlockSpec((1, PREP_TILE, w3), lambda bb, i: (bb, i, col_dn)),
                  pl.BlockSpec((1, HALO, w3), lambda bb, i: (bb, jnp.maximum(i * hb - 1, 0), col_dn)),
                  pl.BlockSpec((1, HALO, w3), lambda bb, i: (bb, jnp.minimum((i + 1) * hb, n_hb - 1), col_dn)),
                  pl.BlockSpec((DN_CONV, w3), lambda bb, i: (0, 0)),
                  pl.BlockSpec((PREP_TILE, LANES), lambda bb, i: (i, 0)),
                  pl.BlockSpec((PREP_TILE, LANES), lambda bb, i: (i, 0)),
                  sca(), row(), row()],
        out_specs=[tok(), tok(), tok(), sca(), sca()],
        out_shape=[jax.ShapeDtypeStruct((b, ta, DN_WIDTH), BF16)] * 3
        + [jax.ShapeDtypeStruct((b, ta, LANES), F32)] * 2,
        scratch_shapes=[pltpu.VMEM((PREP_TILE + HALO, w3), F32)],
        compiler_params=_cparams(("parallel", "parallel")),
        name="dn_prep",
    )(p, p, p, conv_w, rope_c, rope_s, ab, alog_row, dtb_row)


CH4 = DN_HEADS * DN_CHUNK


def _stack_heads(x):
    return jnp.concatenate([x[:, h * LANES:(h + 1) * LANES] for h in range(DN_HEADS)], axis=0)


DN_CPS = 2


def _dn_local_kernel(q_ref, k_ref, v_ref, gc_ref, bc_ref, gr_ref, wq_ref, u_ref, a_ref, kgt_ref):
    r = lax.broadcasted_iota(jnp.int32, (CH4, CH4), 0)
    c = lax.broadcasted_iota(jnp.int32, (CH4, CH4), 1)
    same = (r // DN_CHUNK) == (c // DN_CHUNK)
    for cc in range(DN_CPS):
        toks = slice(cc * DN_CHUNK, (cc + 1) * DN_CHUNK)
        kst = _stack_heads(k_ref[0, toks, :])
        qst = _stack_heads(q_ref[0, toks, :])
        vst = _stack_heads(v_ref[0, toks, :])
        kq = _dot_nt(jnp.concatenate([kst, qst], axis=0), kst)
        kk = kq[:CH4]
        qk = kq[CH4:]
        gc = gc_ref[0, toks, :]
        bc = bc_ref[0, toks, :]
        kf = kst.astype(F32)
        for d in range(2):
            later = (r >= c) if d == 0 else (r <= c)
            incl = same & later
            strict = incl & (r != c)
            col = lambda a: jnp.concatenate(
                [a[:, d * DN_HEADS + h:d * DN_HEADS + h + 1] for h in range(DN_HEADS)], axis=0)
            g_col = col(gc)
            b_col = col(bc)
            last = DN_CHUNK - 1 if d == 0 else 0
            gl_col = jnp.concatenate(
                [jnp.broadcast_to(gc[last:last + 1, d * DN_HEADS + h:d * DN_HEADS + h + 1], (DN_CHUNK, 1))
                 for h in range(DN_HEADS)], axis=0)
            g_row = gr_ref[0, cc, d:d + 1, :]
            decay = jnp.exp(jnp.where(incl, g_col - g_row, NEG_INF))
            pw = jnp.where(strict, -(kk * decay * b_col), 0.0)
            z = jnp.concatenate([vst.astype(F32) * b_col, kf * (b_col * jnp.exp(g_col))], axis=1)
            z = z + _dot(pw.astype(BF16), z.astype(BF16))
            for _ in range(5):
                pwb = pw.astype(BF16)
                pw = _dot(pwb, pwb)
                z = z + _dot(pw.astype(BF16), z.astype(BF16))
            wq_ref[0, cc, d, :CH4, :] = z[:, LANES:].astype(BF16)
            wq_ref[0, cc, d, CH4:, :] = (qst.astype(F32) * jnp.exp(g_col)).astype(BF16)
            u_ref[0, cc, d] = z[:, :LANES]
            a_ref[0, cc, d] = (qk * decay).astype(BF16)
            kgt_ref[0, cc, d] = (kf * jnp.exp(gl_col - g_col)).T.astype(BF16)


def _dn_local(qd, kd, vd, gc, bc, g_rows):
    b, ta, _ = qd.shape
    nc = ta // DN_CHUNK
    tok = lambda w: pl.BlockSpec((1, DN_CPS * DN_CHUNK, w), lambda bb, n: (bb, n, 0))
    out5 = lambda r_, c_: pl.BlockSpec((1, DN_CPS, 2, r_, c_), lambda bb, n: (bb, n, 0, 0, 0))
    return pl.pallas_call(
        _dn_local_kernel,
        grid=(b, nc // DN_CPS),
        in_specs=[tok(DN_WIDTH), tok(DN_WIDTH), tok(DN_WIDTH), tok(LANES), tok(LANES),
                  pl.BlockSpec((1, DN_CPS, 2, CH4), lambda bb, n: (bb, n, 0, 0))],
        out_specs=[out5(2 * CH4, LANES), out5(CH4, LANES), out5(CH4, CH4), out5(LANES, CH4)],
        out_shape=[jax.ShapeDtypeStruct((b, nc, 2, 2 * CH4, LANES), BF16),
                   jax.ShapeDtypeStruct((b, nc, 2, CH4, LANES), F32),
                   jax.ShapeDtypeStruct((b, nc, 2, CH4, CH4), BF16),
                   jax.ShapeDtypeStruct((b, nc, 2, LANES, CH4), BF16)],
        compiler_params=_cparams(("parallel", "parallel")),
        name="dn_local",
    )(qd, kd, vd, gc, bc, g_rows)


def _dn_scan_kernel(wq_f, u_f, a_f, kgt_f, gl_f, wq_b, u_b, a_b, kgt_b, gl_b, of_ref, ob_ref, state):
    n = pl.program_id(1)

    @pl.when(n == 0)
    def _():
        state[...] = jnp.zeros_like(state)

    r = lax.broadcasted_iota(jnp.int32, (CH4, DN_WIDTH), 0)
    c = lax.broadcasted_iota(jnp.int32, (CH4, DN_WIDTH), 1)
    own = (r // DN_CHUNK) == (c // LANES)
    for d, (wq, u, a, kgt, gl, o_ref) in enumerate(((wq_f, u_f, a_f, kgt_f, gl_f, of_ref),
                                                     (wq_b, u_b, a_b, kgt_b, gl_b, ob_ref))):
        s_all = state[d]
        ws = _dot(wq[0, 0, 0], s_all.astype(BF16))
        pick = lambda m: jnp.concatenate(
            [m[h * DN_CHUNK:(h + 1) * DN_CHUNK, h * LANES:(h + 1) * LANES] for h in range(DN_HEADS)], axis=0)
        v_new = u[0, 0, 0] - pick(ws[:CH4])
        v_nb = v_new.astype(BF16)
        o = pick(ws[CH4:]) + _dot(a[0, 0, 0], v_nb)
        for h in range(DN_HEADS):
            o_ref[0, :, h * LANES:(h + 1) * LANES] = o[h * DN_CHUNK:(h + 1) * DN_CHUNK]
        v_bd = jnp.where(own, jnp.concatenate([v_nb] * DN_HEADS, axis=1), jnp.zeros((), BF16))
        state[d] = s_all * jnp.exp(gl[0, 0, 0]) + _dot(kgt[0, 0, 0], v_bd)


def _dn_scan(wq, u, a, kgt, gl_wide, *, s_len):
    b, nc = wq.shape[:2]
    n_lat = s_len // DN_CHUNK
    n_ctx = nc - n_lat
    fwd = lambda n: jnp.where(n < n_ctx, n_lat + n, n - n_ctx)
    bwd = lambda n: nc - 1 - n
    specs = []
    for d, order in ((0, fwd), (1, bwd)):
        blk = lambda arr, d=d, order=order: pl.BlockSpec(
            (1, 1, 1) + arr.shape[3:], lambda bb, n: (bb, order(n), d, 0, 0))
        specs += [blk(wq), blk(u), blk(a), blk(kgt), blk(gl_wide)]
    o_spec = lambda order: pl.BlockSpec((1, DN_CHUNK, DN_WIDTH), lambda bb, n: (bb, order(n), 0))
    return pl.pallas_call(
        _dn_scan_kernel,
        grid=(b, nc),
        in_specs=specs,
        out_specs=[o_spec(fwd), o_spec(bwd)],
        out_shape=[jax.ShapeDtypeStruct((b, nc * DN_CHUNK, DN_WIDTH), F32)] * 2,
        scratch_shapes=[pltpu.VMEM((2, DN_HEAD_DIM, DN_WIDTH), F32)],
        compiler_params=_cparams(("parallel", "arbitrary")),
        name="dn_scan",
    )(wq, u, a, kgt, gl_wide, wq, u, a, kgt, gl_wide)


def _merge_kernel(ona_ref, of_ref, ob_ref, z_ref, sc_ref, schp_ref, schn_ref, gate_ref, x_ref,
                  g1l_ref, g1c_ref, sh2l_ref, sh2c_ref, sc2l_ref, sc2c_ref, ng_ref, og_ref, scw_ref,
                  wna_ref, wdn_ref, wsc_ref, wout_ref, wrh_ref, wrl_ref, rb_ref,
                  x1_ref, h2_ref, comb_ref, combt_ref, cpad, *, tm, s_len, ta):
    i = pl.program_id(1)
    d = x_ref.shape[-1]
    o = of_ref[0] + ob_ref[0]
    z = z_ref[0].astype(F32)
    parts = []
    for h in range(DN_HEADS):
        oh = o[:, h * LANES:(h + 1) * LANES]
        parts.append(oh * lax.rsqrt(jnp.mean(oh * oh, axis=-1, keepdims=True) + EPS))
    o_dn = jnp.concatenate(parts, axis=1) * og_ref[...] * _silu(z)

    def cu(ref_val):
        v = ref_val.astype(F32)
        return v[:, 2 * SC_WIDTH:] * v[:, :SC_WIDTH]
    scp = sc_ref[0].astype(F32)
    half = HALO // 2
    cpad[0:half, :] = cu(schp_ref[0])[half:, :]
    cpad[half:half + tm, :] = scp[:, 2 * SC_WIDTH:] * scp[:, :SC_WIDTH]
    cpad[half + tm:, :] = cu(schn_ref[0])[:half, :]
    gt = i * tm + lax.broadcasted_iota(jnp.int32, (tm, 1), 0)
    has_prev = (gt != 0) & (gt != s_len)
    has_next = (gt != s_len - 1) & (gt != ta - 1)
    conv = (scw_ref[0:1, :] * jnp.where(has_prev, cpad[pl.ds(half - 1, tm), :], 0.0)
            + scw_ref[1:2, :] * cpad[pl.ds(half, tm), :]
            + scw_ref[2:3, :] * jnp.where(has_next, cpad[pl.ds(half + 1, tm), :], 0.0))
    o_sc = scp[:, SC_WIDTH:2 * SC_WIDTH] * conv

    merged = (gate_ref[0, :, :d].astype(F32) * _dot(ona_ref[0], wna_ref[...])
              + gate_ref[0, :, d:2 * d].astype(F32) * _dot(o_dn.astype(BF16), wdn_ref[...])
              + gate_ref[0, :, 2 * d:].astype(F32) * _dot(o_sc.astype(BF16), wsc_ref[...]))
    y = _dot(merged.astype(BF16), wout_ref[...])
    x1 = x_ref[0] + _row_select(i, tm, s_len, g1l_ref[0], g1c_ref[...]) * y
    x1_ref[0] = x1

    hn = x1 * lax.rsqrt(jnp.mean(x1 * x1, axis=-1, keepdims=True) + EPS) * ng_ref[...]
    h2 = (hn * (1.0 + _row_select(i, tm, s_len, sc2l_ref[0], sc2c_ref[...]))
          + _row_select(i, tm, s_len, sh2l_ref[0], sh2c_ref[...]))
    h2_ref[0] = h2.astype(BF16)

    h2_hi, h2_lo = _split_bf16(h2)
    logits = _dot3(h2_hi, h2_lo, wrh_ref[...], wrl_ref[...])
    scores = jax.nn.sigmoid(logits.T)
    biased = scores + rb_ref[...]
    rows = [biased[e:e + 1, :] for e in range(N_EXPERTS)]
    gsc = []
    for g in range(N_GROUPS):
        a, b_, c_, d_ = rows[4 * g:4 * g + 4]
        gsc.append(jnp.maximum(jnp.maximum(jnp.maximum(a + b_, a + c_), jnp.maximum(a + d_, b_ + c_)),
                               jnp.maximum(b_ + d_, c_ + d_)))
    best = gsc[0]
    grp = jnp.zeros_like(best, dtype=jnp.int32)
    for g in range(1, N_GROUPS):
        upd = gsc[g] > best
        best = jnp.where(upd, gsc[g], best)
        grp = jnp.where(upd, g, grp)
    eidx = lax.broadcasted_iota(jnp.int32, biased.shape, 0)
    masked = jnp.where((eidx // EXPERTS_PER_GROUP) == grp, biased, -jnp.inf)
    big = jnp.int32(LANES)
    m1 = masked.max(axis=0, keepdims=True)
    i1 = jnp.where(masked == m1, eidx, big).min(axis=0, keepdims=True)
    masked2 = jnp.where(eidx == i1, -jnp.inf, masked)
    m2 = masked2.max(axis=0, keepdims=True)
    i2 = jnp.where(masked2 == m2, eidx, big).min(axis=0, keepdims=True)
    sel1 = eidx == i1
    sel2 = eidx == i2
    s1 = jnp.sum(jnp.where(sel1, scores, 0.0), axis=0, keepdims=True)
    s2 = jnp.sum(jnp.where(sel2, scores, 0.0), axis=0, keepdims=True)
    tot = s1 + s2
    comb_t = jnp.where(sel1, s1 / tot, 0.0) + jnp.where(sel2, s2 / tot, 0.0)
    meta_t = jnp.where(eidx == N_EXPERTS, grp.astype(F32), comb_t)
    combt_ref[0] = meta_t
    comb_ref[0] = meta_t.T


def _merge(o_na, o_f, o_b, p, xs, mods_l, mods_c, norm_g, out_g, sc_w, wna, wdn, wsc, wout, wr, rb,
           *, s_len, tm, col_z, col_sc):
    b, ta, d = xs.shape
    hb = tm // HALO
    n_hb = ta // HALO
    tok = lambda w, col=0: pl.BlockSpec((1, tm, w), lambda bb, i: (bb, i, col))
    vec = lambda w: pl.BlockSpec((1, w), lambda bb, i: (0, 0))
    bvec = lambda: pl.BlockSpec((1, 1, d), lambda bb, i: (bb, 0, 0))
    full = lambda a: pl.BlockSpec(a.shape, lambda bb, i: (0,) * a.ndim)
    g1l, sh2l, sc2l = mods_l
    g1c, sh2c, sc2c = mods_c
    w3 = 3 * SC_WIDTH
    wr_hi = wr.astype(BF16)
    wr_lo = (wr - wr_hi.astype(F32)).astype(BF16)
    return pl.pallas_call(
        functools.partial(_merge_kernel, tm=tm, s_len=s_len, ta=ta),
        grid=(b, ta // tm),
        in_specs=[tok(NA_WIDTH), tok(DN_WIDTH), tok(DN_WIDTH), tok(DN_WIDTH, col_z), tok(w3, col_sc),
                  pl.BlockSpec((1, HALO, w3), lambda bb, i: (bb, jnp.maximum(i * hb - 1, 0), col_sc)),
                  pl.BlockSpec((1, HALO, w3), lambda bb, i: (bb, jnp.minimum((i + 1) * hb, n_hb - 1), col_sc)),
                  tok(3 * d, 0), tok(d),
                  bvec(), vec(d), bvec(), vec(d), bvec(), vec(d), vec(d), vec(DN_WIDTH),
                  full(sc_w), full(wna), full(wdn), full(wsc), full(wout), full(wr_hi), full(wr_lo), full(rb)],
        out_specs=[tok(d), tok(d), tok(LANES), pl.BlockSpec((1, LANES, tm), lambda bb, i: (bb, 0, i))],
        out_shape=[jax.ShapeDtypeStruct((b, ta, d), F32),
                   jax.ShapeDtypeStruct((b, ta, d), BF16),
                   jax.ShapeDtypeStruct((b, ta, LANES), F32),
                   jax.ShapeDtypeStruct((b, LANES, ta), F32)],
        scratch_shapes=[pltpu.VMEM((tm + HALO, SC_WIDTH), F32)],
        compiler_params=_cparams(("parallel", "parallel")),
        name="merge_router",
    )(o_na, o_f, o_b, p, p, p, p, p, xs, g1l, g1c, sh2l, sh2c, sc2l, sc2c, norm_g, out_g,
      sc_w, wna, wdn, wsc, wout, wr_hi, wr_lo, rb)


MOE_CAP = 256


def _moe_kernel(h_ref, comb_ref, combt_ref, wg_ref, wu_ref, wd_ref, x_ref, g2l_ref, g2c_ref, o_ref, acc,
                *, tm, s_len):
    i = pl.program_id(1)
    g = pl.program_id(2)

    @pl.when(g == 0)
    def _():
        acc[...] = jnp.zeros_like(acc)

    gf = g.astype(F32)
    comb = comb_ref[0]
    lane = lax.broadcasted_iota(jnp.int32, (tm, LANES), 1)
    member_row = combt_ref[0, N_EXPERTS:N_EXPERTS + 1, :] == gf
    member_col = jnp.sum(jnp.where(lane == N_EXPERTS, comb, 0.0), axis=-1, keepdims=True) == gf
    r = lax.broadcasted_iota(jnp.int32, (tm, tm), 0)
    c = lax.broadcasted_iota(jnp.int32, (tm, tm), 1)
    ones_r = jnp.where(member_row, 1.0, 0.0)
    ones_c = jnp.where(member_col, 1.0, 0.0)
    rank_row = _dot(jnp.broadcast_to(ones_r, (8, tm)).astype(BF16),
                    jnp.where(r < c, 1.0, 0.0).astype(BF16))[0:1]
    rank_col = _dot(jnp.where(c < r, 1.0, 0.0).astype(BF16),
                    jnp.broadcast_to(ones_c, (tm, LANES)).astype(BF16))[:, 0:1]
    count = jnp.sum(ones_r)
    comb_hi, comb_lo = _split_bf16(comb)
    d = x_ref.shape[-1]

    for k in range(tm // MOE_CAP):
        @pl.when(count > k * MOE_CAP)
        def _(k=k):
            slot_c = (lax.broadcasted_iota(jnp.int32, (MOE_CAP, 1), 0) + k * MOE_CAP).astype(F32)
            gather = jnp.where(member_row & (rank_row == slot_c), 1.0, 0.0).astype(BF16)
            hs = _dot(gather, h_ref[0]).astype(BF16)
            cws = _dot(gather, comb_hi) + _dot(gather, comb_lo)
            lane_c = lax.broadcasted_iota(jnp.int32, (MOE_CAP, LANES), 1)
            ys = jnp.zeros((MOE_CAP, d), F32)
            for e4 in range(EXPERTS_PER_GROUP):
                cw = jnp.sum(jnp.where(lane_c == g * EXPERTS_PER_GROUP + e4, cws, 0.0), axis=-1, keepdims=True)
                hidden = _silu(_dot(hs, wg_ref[e4])) * _dot(hs, wu_ref[e4]) * cw
                ys = ys + _dot(hidden.astype(BF16), wd_ref[e4])
            slot_r = (lax.broadcasted_iota(jnp.int32, (1, MOE_CAP), 1) + k * MOE_CAP).astype(F32)
            scatter = jnp.where(member_col & (rank_col == slot_r), 1.0, 0.0).astype(BF16)
            ys_hi, ys_lo = _split_bf16(ys)
            acc[...] += _dot(scatter, ys_hi) + _dot(scatter, ys_lo)

    @pl.when(g == N_GROUPS - 1)
    def _():
        o_ref[0] = x_ref[0] + _row_select(i, tm, s_len, g2l_ref[0], g2c_ref[...]) * acc[...]


def _moe(h2, comb, comb_t, wg, wu, wd, x1, g2l, g2c, *, s_len, tm):
    b, ta, d = x1.shape
    tok = lambda w: pl.BlockSpec((1, tm, w), lambda bb, i, g: (bb, i, 0))
    grp = lambda a: pl.BlockSpec((EXPERTS_PER_GROUP,) + a.shape[1:], lambda bb, i, g: (g, 0, 0))
    return pl.pallas_call(
        functools.partial(_moe_kernel, tm=tm, s_len=s_len),
        grid=(b, ta // tm, N_GROUPS),
        in_specs=[tok(d), tok(LANES),
                  pl.BlockSpec((1, LANES, tm), lambda bb, i, g: (bb, 0, i)),
                  grp(wg), grp(wu), grp(wd),
                  tok(d),
                  pl.BlockSpec((1, 1, d), lambda bb, i, g: (bb, 0, 0)),
                  pl.BlockSpec((1, d), lambda bb, i, g: (0, 0))],
        out_specs=tok(d),
        out_shape=jax.ShapeDtypeStruct((b, ta, d), F32),
        scratch_shapes=[pltpu.VMEM((tm, d), F32)],
        compiler_params=_cparams(("parallel", "parallel", "arbitrary")),
        name="moe_grouped",
    )(h2, comb, comb_t, wg, wu, wd, x1, g2l, g2c)


def _rope_tables(s_len, ctx_len):
    t = jnp.arange(s_len)
    n_freq = DN_HEAD_DIM // 4
    inv_freq = ROPE_BASE ** (-jnp.arange(n_freq, dtype=F32) / n_freq)
    pos = jnp.stack([t // GRID_W, t % GRID_W], axis=-1).astype(F32)
    ang = pos[..., None] * inv_freq
    cos, sin = jnp.cos(ang), jnp.sin(ang)
    rc = jnp.concatenate([cos[:, 0], cos[:, 0], cos[:, 1], cos[:, 1]], axis=-1)
    rs = jnp.concatenate([-sin[:, 0], sin[:, 0], -sin[:, 1], sin[:, 1]], axis=-1)
    rc = jnp.concatenate([rc, jnp.ones((ctx_len, LANES), F32)], axis=0)
    rs = jnp.concatenate([rs, jnp.zeros((ctx_len, LANES), F32)], axis=0)
    return rc, rs


def _pick_tile(ta, candidates):
    for t in candidates:
        if ta % t == 0:
            return t
    raise ValueError(f"no row tile for {ta} tokens")


def kernel(x, c, ctx, c_ctx, w_ada, b_ada, norm_mix_g, norm_ffn_g, w_in, na_q_norm_g, na_k_norm_g, na_rpb, dn_conv_w, dn_a_log, dn_dt_bias, dn_out_norm_g, sc_conv_w, w_branch_na, w_branch_dn, w_branch_sc, w_out, w_router, router_bias, moe_w_gate, moe_w_up, moe_w_down):
    b, s_len, d = x.shape
    ctx_len = ctx.shape[1]
    depth = w_ada.shape[0]
    ta = s_len + ctx_len
    assert s_len % ROW_BLK == 0 and s_len // ROW_BLK >= 2 and s_len % ctx_len == 0
    assert ctx_len % PREP_TILE == 0 and b + 1 <= 8
    tm_proj = _pick_tile(ta, (768, 512, 256))
    tm_merge = _pick_tile(ta, (384, 256))
    tm_moe = _pick_tile(ta, (768, 512, 256))

    xs = jnp.concatenate([x, ctx], axis=1)
    c_rows = jnp.zeros((8, d), F32).at[:b].set(c).at[b].set(c_ctx)
    mod = _modulation(c_rows, w_ada, b_ada)

    sizes = (3 * NA_WIDTH, 3 * DN_WIDTH, DN_WIDTH, 4 * DN_HEADS, 3 * SC_WIDTH, 3 * d)
    cuts = np.cumsum((0,) + sizes)
    seg = lambda w, k: w[:, cuts[k]:cuts[k + 1]]
    col_q = 3 * d // NA_WIDTH
    col_dn = (3 * d + 3 * NA_WIDTH) // (3 * DN_WIDTH)
    col_sc = (3 * d + 3 * NA_WIDTH + 3 * DN_WIDTH) // (3 * SC_WIDTH)
    col_z = (3 * d + 3 * NA_WIDTH + 3 * DN_WIDTH + 3 * SC_WIDTH) // DN_WIDTH

    rope_c, rope_s = _rope_tables(s_len, ctx_len)
    bd = jnp.asarray(np.kron(np.eye(NA_HEADS), np.ones((NA_HEAD_DIM, NA_HEAD_DIM))), BF16)
    nc = ta // DN_CHUNK
    wr = jnp.zeros((d, LANES), F32).at[:, :N_EXPERTS].set(w_router)
    rb = jnp.zeros((LANES, 1), F32).at[:N_EXPERTS, 0].set(router_bias)

    for l in range(depth):
        m = mod[l].reshape(8, N_MOD, d)
        lat = lambda k: m[:b, k][:, None, :]
        cx = lambda k: m[b:b + 1, k]
        wl = w_in[l]
        w_main = jnp.concatenate([seg(wl, 5), seg(wl, 0), seg(wl, 1), seg(wl, 4), seg(wl, 2)],
                                 axis=1).astype(BF16)
        w_ab = jnp.zeros((d, LANES), F32).at[:, :4 * DN_HEADS].set(seg(wl, 3))
        gqk = jnp.concatenate([jnp.tile(na_q_norm_g[l], NA_HEADS) * (NA_HEAD_DIM ** -0.5),
                               jnp.tile(na_k_norm_g[l], NA_HEADS)])[None, :]
        p, ab = _projection(xs, (lat(0), lat(1)), (cx(0), cx(1)), norm_mix_g[l][None, :],
                            w_main, w_ab, gqk, bd, s_len=s_len, tm=tm_proj)

        o_na = _neighbourhood_attention(p, _na_bias_table(na_rpb[l]), s_len=s_len, ctx_len=ctx_len,
                                        col_q=col_q)

        alog_row = jnp.zeros((1, LANES), F32).at[0, :2 * DN_HEADS].set(dn_a_log[l].reshape(-1))
        dtb_row = jnp.zeros((1, LANES), F32).at[0, :2 * DN_HEADS].set(dn_dt_bias[l].reshape(-1))
        qd, kd, vd, gc, bc = _dn_prep(p, ab, dn_conv_w[l], rope_c, rope_s, alog_row, dtb_row,
                                         s_len=s_len, col_dn=col_dn)
        g8 = gc[..., :2 * DN_HEADS].reshape(b, nc, DN_CHUNK, 2, DN_HEADS)
        g_rows = jnp.transpose(g8, (0, 1, 3, 4, 2)).reshape(b, nc, 2, CH4)
        g_last = jnp.stack([g8[:, :, DN_CHUNK - 1, 0], g8[:, :, 0, 1]], axis=2)
        gl_wide = jnp.repeat(g_last, LANES, axis=-1)[:, :, :, None, :]
        wq, u, a, kgt = _dn_local(qd, kd, vd, gc, bc, g_rows)
        o_f, o_b = _dn_scan(wq, u, a, kgt, gl_wide, s_len=s_len)

        x1, h2, comb, comb_t = _merge(
            o_na, o_f, o_b, p, xs, (lat(2), lat(3), lat(4)), (cx(2), cx(3), cx(4)),
            norm_ffn_g[l][None, :], jnp.tile(dn_out_norm_g[l], DN_HEADS)[None, :], sc_conv_w[l],
            w_branch_na[l].astype(BF16), w_branch_dn[l].astype(BF16), w_branch_sc[l].astype(BF16),
            w_out[l].astype(BF16), wr, rb, s_len=s_len, tm=tm_merge, col_z=col_z, col_sc=col_sc)

        xs = _moe(h2, comb, comb_t, moe_w_gate[l].astype(BF16), moe_w_up[l].astype(BF16),
                  moe_w_down[l].astype(BF16), x1, lat(5), cx(5), s_len=s_len, tm=tm_moe)

    return xs[:, :s_len]
```

```python
import functools
import math

import numpy as np
import jax
import jax.numpy as jnp
from jax import lax
from jax.experimental import pallas as pl
from jax.experimental.pallas import tpu as pltpu

F32 = jnp.float32
BF16 = jnp.bfloat16
HIGHEST = lax.Precision.HIGHEST

GRID_W = 64
NA_HEADS = 8
NA_HEAD_DIM = 64
NA_WIDTH = NA_HEADS * NA_HEAD_DIM
WIN_R = 8
WIN_C = 16
DN_HEADS = 4
DN_HEAD_DIM = 128
DN_WIDTH = DN_HEADS * DN_HEAD_DIM
DN_CONV = 5
DN_CHUNK = 64
ROPE_BASE = 10000.0
SC_WIDTH = 512
SC_CONV = 3
N_EXPERTS = 16
N_GROUPS = 4
EXPERTS_PER_GROUP = N_EXPERTS // N_GROUPS
D_EXPERT = 512
N_MOD = 6
EPS = 1e-6
NEG_INF = -1e30

LANES = 128
HALO = 16
PREP_TILE = 256
VMEM_LIMIT = 56 * 1024 * 1024

N_PROJ_TILE = 512


def _cparams(sem):
    return pltpu.CompilerParams(dimension_semantics=sem, vmem_limit_bytes=VMEM_LIMIT)


def _dot(a, b):
    return jnp.dot(a, b, preferred_element_type=F32)


def _dot_nt(a, b):
    return lax.dot_general(a, b, (((1,), (1,)), ((), ())), preferred_element_type=F32)


def _silu(x):
    return x * jax.nn.sigmoid(x)


def _row_select(i, tm, s_len, lat, ctx):
    row = i * tm + lax.broadcasted_iota(jnp.int32, (tm, 1), 0)
    return jnp.where(row >= s_len, ctx, lat)


def _mod_kernel(c_ref, w_ref, b_ref, o_ref):
    c_hi, c_lo = _split_bf16(_silu(c_ref[...]))
    w_hi, w_lo = _split_bf16(w_ref[0])
    o_ref[0] = _dot3(c_hi, c_lo, w_hi, w_lo) + b_ref[0]


def _modulation(c_rows, w_ada, b_ada):
    depth, d, n = w_ada.shape
    tn = 512
    return pl.pallas_call(
        _mod_kernel,
        grid=(depth, n // tn),
        in_specs=[pl.BlockSpec((8, d), lambda l, j: (0, 0)),
                  pl.BlockSpec((1, d, tn), lambda l, j: (l, 0, j)),
                  pl.BlockSpec((1, 1, tn), lambda l, j: (l, 0, j))],
        out_specs=pl.BlockSpec((1, 8, tn), lambda l, j: (l, 0, j)),
        out_shape=jax.ShapeDtypeStruct((depth, 8, n), F32),
        compiler_params=_cparams(("parallel", "parallel")),
        name="adaln_mod",
    )(c_rows, w_ada, b_ada.reshape(depth, 1, n))


def _split_bf16(a):
    hi = a.astype(BF16)
    return hi, (a - hi.astype(F32)).astype(BF16)


def _dot3(a_hi, a_lo, b_hi, b_lo):
    return _dot(a_hi, b_hi) + _dot(a_lo, b_hi) + _dot(a_hi, b_lo)


def _modulated_norm(x_ref, shl_ref, scl_ref, shc_ref, scc_ref, g_ref, *, tm, s_len):
    i = pl.program_id(1)
    x = x_ref[0]
    y = x * lax.rsqrt(jnp.mean(x * x, axis=-1, keepdims=True) + EPS) * g_ref[...]
    scale = _row_select(i, tm, s_len, scl_ref[0], scc_ref[...])
    shift = _row_select(i, tm, s_len, shl_ref[0], shc_ref[...])
    return y * (1.0 + scale) + shift


def _proj_gate_kernel(x_ref, shl_ref, scl_ref, shc_ref, scc_ref, g_ref, w_ref, gate_ref, *, tm, s_len):
    h = _modulated_norm(x_ref, shl_ref, scl_ref, shc_ref, scc_ref, g_ref, tm=tm, s_len=s_len).astype(BF16)
    for c0 in range(0, w_ref.shape[1], N_PROJ_TILE):
        cols = slice(c0, c0 + N_PROJ_TILE)
        gate_ref[0, :, cols] = jax.nn.sigmoid(_dot(h, w_ref[:, cols])).astype(BF16)


def _proj_mix_kernel(x_ref, shl_ref, scl_ref, shc_ref, scc_ref, g_ref, wna_ref, wdn_ref, wsc_ref, wz_ref,
                     wabh_ref, wabl_ref, gqk_ref, bd_ref,
                     q_ref, k_ref, v_ref, dn_ref, sc_ref, z_ref, ab_ref, *, tm, s_len):
    h = _modulated_norm(x_ref, shl_ref, scl_ref, shc_ref, scc_ref, g_ref, tm=tm, s_len=s_len)
    h_hi, h_lo = _split_bf16(h)
    ab_ref[0] = _dot3(h_hi, h_lo, wabh_ref[...], wabl_ref[...])
    for part, out in enumerate((q_ref, k_ref)):
        cols = slice(part * NA_WIDTH, (part + 1) * NA_WIDTH)
        a = _dot(h_hi, wna_ref[:, cols])
        ss = _dot((a * a).astype(BF16), bd_ref[...])
        out[0] = (a * lax.rsqrt(ss * (1.0 / NA_HEAD_DIM) + EPS) * gqk_ref[:, cols]).astype(BF16)
    v_ref[0] = _dot(h_hi, wna_ref[:, 2 * NA_WIDTH:]).astype(BF16)
    for w_ref, out in ((wdn_ref, dn_ref), (wsc_ref, sc_ref), (wz_ref, z_ref)):
        for c0 in range(0, w_ref.shape[1], N_PROJ_TILE):
            cols = slice(c0, c0 + N_PROJ_TILE)
            out[0, :, cols] = _dot(h_hi, w_ref[:, cols]).astype(BF16)


def _projection(xs, mod_l, mod_c, norm_g, w_gate, w_na, w_dn, w_sc, w_z, w_ab, gqk, bd, *, s_len, tm):
    b, ta, d = xs.shape
    shl, scl = mod_l
    shc, scc = mod_c
    wab_hi, wab_lo = _split_bf16(w_ab)
    tok = lambda w: pl.BlockSpec((1, tm, w), lambda bb, i: (bb, i, 0))
    bvec = lambda: pl.BlockSpec((1, 1, d), lambda bb, i: (bb, 0, 0))
    full = lambda a: pl.BlockSpec(a.shape, lambda bb, i: (0,) * a.ndim)
    head = [tok(d), bvec(), bvec(), full(shc), full(scc), full(norm_g)]
    gates = pl.pallas_call(
        functools.partial(_proj_gate_kernel, tm=tm, s_len=s_len),
        grid=(b, ta // tm),
        in_specs=head + [full(w_gate)],
        out_specs=tok(w_gate.shape[1]),
        out_shape=jax.ShapeDtypeStruct((b, ta, w_gate.shape[1]), BF16),
        compiler_params=_cparams(("parallel", "parallel")),
        name="proj_gates",
    )(xs, shl, scl, shc, scc, norm_g, w_gate)
    consts = (w_na, w_dn, w_sc, w_z, wab_hi, wab_lo, gqk, bd)
    widths = (NA_WIDTH, NA_WIDTH, NA_WIDTH, w_dn.shape[1], w_sc.shape[1], w_z.shape[1])
    outs = pl.pallas_call(
        functools.partial(_proj_mix_kernel, tm=tm, s_len=s_len),
        grid=(b, ta // tm),
        in_specs=head + [full(a) for a in consts],
        out_specs=[tok(w) for w in widths] + [tok(LANES)],
        out_shape=[jax.ShapeDtypeStruct((b, ta, w), BF16) for w in widths]
        + [jax.ShapeDtypeStruct((b, ta, LANES), F32)],
        compiler_params=_cparams(("parallel", "parallel")),
        name="proj_mixers",
    )(xs, shl, scl, shc, scc, norm_g, *consts)
    return (gates,) + tuple(outs)


NA_QROWS = 4
NA_BLK = NA_QROWS * GRID_W
WIN_TOK = WIN_R * GRID_W


def _stack_two_heads(q2):
    lane = lax.broadcasted_iota(jnp.int32, q2.shape, 1)
    zero = jnp.zeros_like(q2)
    return jnp.concatenate([jnp.where(lane < NA_HEAD_DIM, q2, zero),
                            jnp.where(lane >= NA_HEAD_DIM, q2, zero)], axis=0)


def _unstack_two_heads(o2):
    m = o2.shape[0] // 2
    lane = lax.broadcasted_iota(jnp.int32, (m, o2.shape[1]), 1)
    return jnp.where(lane < NA_HEAD_DIM, o2[:m], o2[m:])


def _na_kernel(q_ref, kp_ref, kc_ref, kn_ref, vp_ref, vc_ref, vn_ref, kx_ref, vx_ref, bias_ref, o_ref,
               kbuf, vbuf, *, n_rows):
    i = pl.program_id(1)
    n_lat = n_rows // NA_QROWS

    @pl.when(i < n_lat)
    def _():
        for slot, (kr, vr) in enumerate(((kp_ref, vp_ref), (kc_ref, vc_ref), (kn_ref, vn_ref))):
            kbuf[slot * NA_BLK:(slot + 1) * NA_BLK, :] = kr[0]
            vbuf[slot * NA_BLK:(slot + 1) * NA_BLK, :] = vr[0]
        win = []
        for j in range(NA_QROWS):
            r = i * NA_QROWS + j
            start = jnp.clip(r - WIN_R // 2, 0, n_rows - WIN_R)
            win.append((pl.multiple_of((start - (i - 1) * NA_QROWS) * GRID_W, GRID_W), r - start))

        rows2 = 2 * GRID_W
        for hp in range(NA_HEADS // 2):
            lanes = slice(hp * LANES, (hp + 1) * LANES)
            kx, vx = kx_ref[0, :, lanes], vx_ref[0, :, lanes]
            qs = jnp.concatenate([_stack_two_heads(q_ref[0, j * GRID_W:(j + 1) * GRID_W, lanes])
                                  for j in range(NA_QROWS)], axis=0)
            s_c = _dot_nt(qs, kx)
            s_w = jnp.concatenate(
                [_dot_nt(qs[j * rows2:(j + 1) * rows2], kbuf[pl.ds(win[j][0], WIN_TOK), lanes])
                 + bias_ref[win[j][1], hp] for j in range(NA_QROWS)], axis=0)
            m = jnp.maximum(s_w.max(axis=-1, keepdims=True), s_c.max(axis=-1, keepdims=True))
            p_w = jnp.exp(s_w - m)
            p_c = jnp.exp(s_c - m)
            p_wb = p_w.astype(BF16)
            o = _dot(p_c.astype(BF16), vx) + jnp.concatenate(
                [_dot(p_wb[j * rows2:(j + 1) * rows2], vbuf[pl.ds(win[j][0], WIN_TOK), lanes])
                 for j in range(NA_QROWS)], axis=0)
            o = o / (p_w.sum(axis=-1, keepdims=True) + p_c.sum(axis=-1, keepdims=True))
            for j in range(NA_QROWS):
                o_ref[0, j * GRID_W:(j + 1) * GRID_W, lanes] = _unstack_two_heads(
                    o[j * rows2:(j + 1) * rows2]).astype(BF16)

    @pl.when(i == n_lat)
    def _():
        for hp in range(NA_HEADS // 2):
            lanes = slice(hp * LANES, (hp + 1) * LANES)
            s = _dot_nt(_stack_two_heads(q_ref[0, :, lanes]), kx_ref[0, :, lanes])
            p = jnp.exp(s - s.max(axis=-1, keepdims=True))
            o = _dot(p.astype(BF16), vx_ref[0, :, lanes]) / p.sum(axis=-1, keepdims=True)
            o_ref[0, :, lanes] = _unstack_two_heads(o).astype(BF16)


def _neighbourhood_attention(q, k, v, bias_tab, *, s_len):
    b, ta, _ = q.shape
    n_rows = s_len // GRID_W
    n_lat = n_rows // NA_QROWS
    blk = lambda off: pl.BlockSpec((1, NA_BLK, NA_WIDTH), lambda bb, i: (bb, jnp.clip(i + off, 0, n_lat - 1), 0))
    cx = lambda: pl.BlockSpec((1, NA_BLK, NA_WIDTH), lambda bb, i: (bb, n_lat, 0))
    own = lambda: pl.BlockSpec((1, NA_BLK, NA_WIDTH), lambda bb, i: (bb, i, 0))
    return pl.pallas_call(
        functools.partial(_na_kernel, n_rows=n_rows),
        grid=(b, n_lat + 1),
        in_specs=[own(), blk(-1), blk(0), blk(1), blk(-1), blk(0), blk(1), cx(), cx(),
                  pl.BlockSpec(bias_tab.shape, lambda bb, i: (0, 0, 0, 0))],
        out_specs=own(),
        out_shape=jax.ShapeDtypeStruct((b, ta, NA_WIDTH), BF16),
        scratch_shapes=[pltpu.VMEM((3 * NA_BLK, NA_WIDTH), BF16),
                        pltpu.VMEM((3 * NA_BLK, NA_WIDTH), BF16)],
        compiler_params=_cparams(("parallel", "arbitrary")),
        name="na_attention",
    )(q, k, k, k, v, v, v, k, v, bias_tab)


def _na_bias_table(rpb):
    qc = np.arange(GRID_W)[:, None]
    kc = np.arange(GRID_W)[None, :]
    win_c0 = np.clip(qc - WIN_C // 2, 0, GRID_W - WIN_C)
    in_win = (kc >= win_c0) & (kc < win_c0 + WIN_C)
    rel_c = np.clip(kc - qc + WIN_C - 1, 0, 2 * WIN_C - 2)
    pick_c = (rel_c[..., None] == np.arange(2 * WIN_C - 1)).astype(np.float32)
    by_col = jnp.einsum('hab,qkb->haqk', rpb.astype(F32), pick_c, precision=HIGHEST)
    by_col = jnp.where(in_win[None, None], by_col, NEG_INF)
    tab = jnp.stack([by_col[:, WIN_R - 1 - v:2 * WIN_R - 1 - v] for v in range(WIN_R)], axis=0)
    tab = jnp.transpose(tab, (0, 1, 3, 2, 4))
    return tab.reshape(WIN_R, NA_HEADS // 2, 2 * GRID_W, WIN_R * GRID_W)


def _dn_prep_kernel(x_ref, hp_ref, hn_ref, w_ref, rc_ref, rs_ref, ab_ref, alog_ref, dtb_ref,
                    q_ref, k_ref, v_ref, g_ref, beta_ref, xpad, *, s_len, ta):
    i = pl.program_id(1)
    t0 = i * PREP_TILE
    has_prev = jnp.logical_and(t0 != 0, t0 != s_len).astype(F32)
    has_next = jnp.logical_and(t0 + PREP_TILE != s_len, t0 + PREP_TILE != ta).astype(F32)
    half = HALO // 2
    xpad[0:half, :] = hp_ref[0].astype(F32)[half:, :] * has_prev
    xpad[half:half + PREP_TILE, :] = x_ref[0].astype(F32)
    xpad[half + PREP_TILE:, :] = hn_ref[0].astype(F32)[:half, :] * has_next
    y = jnp.zeros((PREP_TILE, x_ref.shape[-1]), F32)
    for tap in range(DN_CONV):
        y = y + w_ref[tap:tap + 1, :] * xpad[pl.ds(half - DN_CONV // 2 + tap, PREP_TILE), :]
    y = _silu(y)

    lane = lax.broadcasted_iota(jnp.int32, (PREP_TILE, LANES), 1)
    first_half = (lane % (LANES // 2)) < (LANES // 4)
    rc = rc_ref[...]
    rs = rs_ref[...]

    def norm_rope(u, scale):
        u = u * (lax.rsqrt(jnp.sum(u * u, axis=-1, keepdims=True) + EPS) * scale)
        partner = jnp.where(first_half, pltpu.roll(u, LANES - LANES // 4, 1), pltpu.roll(u, LANES // 4, 1))
        return u * rc + partner * rs

    for h in range(DN_HEADS):
        lanes = slice(h * LANES, (h + 1) * LANES)
        qh = norm_rope(y[:, h * LANES:(h + 1) * LANES], DN_HEAD_DIM ** -0.5)
        kh = norm_rope(y[:, DN_WIDTH + h * LANES:DN_WIDTH + (h + 1) * LANES], 1.0)
        q_ref[0, :, lanes] = qh.astype(BF16)
        k_ref[0, :, lanes] = kh.astype(BF16)
    v_ref[0] = y[:, 2 * DN_WIDTH:].astype(BF16)

    ab = ab_ref[0]
    z = ab + dtb_ref[...]
    softplus = jnp.maximum(z, 0.0) + jnp.log1p(jnp.exp(-jnp.abs(z)))
    g = -jnp.exp(alog_ref[...]) * softplus
    g = jnp.where(lane < 2 * DN_HEADS, g, 0.0)
    beta = jax.nn.sigmoid(pltpu.roll(ab, LANES - 2 * DN_HEADS, 1))
    beta_ref[0] = jnp.where(lane < 2 * DN_HEADS, beta, 0.0)

    r = lax.broadcasted_iota(jnp.int32, (PREP_TILE, PREP_TILE), 0)
    c = lax.broadcasted_iota(jnp.int32, (PREP_TILE, PREP_TILE), 1)
    same = (r // DN_CHUNK) == (c // DN_CHUNK)
    tri_f = jnp.where(same & (c <= r), 1.0, 0.0).astype(BF16)
    tri_b = jnp.where(same & (c >= r), 1.0, 0.0).astype(BF16)
    g1 = g.astype(BF16)
    rem = g - g1.astype(F32)
    g2 = rem.astype(BF16)
    g3 = (rem - g2.astype(F32)).astype(BF16)
    cs_f = _dot(tri_f, g1) + _dot(tri_f, g2) + _dot(tri_f, g3)
    cs_b = _dot(tri_b, g1) + _dot(tri_b, g2) + _dot(tri_b, g3)
    g_ref[0] = jnp.where(lane < DN_HEADS, cs_f, cs_b)


def _dn_prep(p, ab, conv_w, rope_c, rope_s, alog_row, dtb_row, *, s_len):
    b, ta, w3 = p.shape
    col_dn = 0
    n_t = ta // PREP_TILE
    hb = PREP_TILE // HALO
    n_hb = ta // HALO
    tok = lambda: pl.BlockSpec((1, PREP_TILE, DN_WIDTH), lambda bb, i: (bb, i, 0))
    sca = lambda: pl.BlockSpec((1, PREP_TILE, LANES), lambda bb, i: (bb, i, 0))
    row = lambda: pl.BlockSpec((1, LANES), lambda bb, i: (0, 0))
    return pl.pallas_call(
        functools.partial(_dn_prep_kernel, s_len=s_len, ta=ta),
        grid=(b, n_t),
        in_specs=[pl.BlockSpec((1, PREP_TILE, w3), lambda bb, i: (bb, i, col_dn)),
                  pl.BlockSpec((1, HALO, w3), lambda bb, i: (bb, jnp.maximum(i * hb - 1, 0), col_dn)),
                  pl.BlockSpec((1, HALO, w3), lambda bb, i: (bb, jnp.minimum((i + 1) * hb, n_hb - 1), col_dn)),
                  pl.BlockSpec((DN_CONV, w3), lambda bb, i: (0, 0)),
                  pl.BlockSpec((PREP_TILE, LANES), lambda bb, i: (i, 0)),
                  pl.BlockSpec((PREP_TILE, LANES), lambda bb, i: (i, 0)),
                  sca(), row(), row()],
        out_specs=[tok(), tok(), tok(), sca(), sca()],
        out_shape=[jax.ShapeDtypeStruct((b, ta, DN_WIDTH), BF16)] * 3
        + [jax.ShapeDtypeStruct((b, ta, LANES), F32)] * 2,
        scratch_shapes=[pltpu.VMEM((PREP_TILE + HALO, w3), F32)],
        compiler_params=_cparams(("parallel", "parallel")),
        name="dn_prep",
    )(p, p, p, conv_w, rope_c, rope_s, ab, alog_row, dtb_row)


CH4 = DN_HEADS * DN_CHUNK


def _stack_heads(x):
    return jnp.concatenate([x[:, h * LANES:(h + 1) * LANES] for h in range(DN_HEADS)], axis=0)


DN_CPS = 2


def _dn_local_kernel(q_ref, k_ref, v_ref, gc_ref, bc_ref, gr_ref, wq_ref, u_ref, a_ref, kgt_ref):
    r = lax.broadcasted_iota(jnp.int32, (CH4, CH4), 0)
    c = lax.broadcasted_iota(jnp.int32, (CH4, CH4), 1)
    same = (r // DN_CHUNK) == (c // DN_CHUNK)
    for cc in range(DN_CPS):
        toks = slice(cc * DN_CHUNK, (cc + 1) * DN_CHUNK)
        kst = _stack_heads(k_ref[0, toks, :])
        qst = _stack_heads(q_ref[0, toks, :])
        vst = _stack_heads(v_ref[0, toks, :])
        kq = _dot_nt(jnp.concatenate([kst, qst], axis=0), kst)
        kk = kq[:CH4]
        qk = kq[CH4:]
        gc = gc_ref[0, toks, :]
        bc = bc_ref[0, toks, :]
        kf = kst.astype(F32)
        for d in range(2):
            later = (r >= c) if d == 0 else (r <= c)
            incl = same & later
            strict = incl & (r != c)
            col = lambda a: jnp.concatenate(
                [a[:, d * DN_HEADS + h:d * DN_HEADS + h + 1] for h in range(DN_HEADS)], axis=0)
            g_col = col(gc)
            b_col = col(bc)
            last = DN_CHUNK - 1 if d == 0 else 0
            gl_col = jnp.concatenate(
                [jnp.broadcast_to(gc[last:last + 1, d * DN_HEADS + h:d * DN_HEADS + h + 1], (DN_CHUNK, 1))
                 for h in range(DN_HEADS)], axis=0)
            g_row = gr_ref[0, cc, d:d + 1, :]
            decay = jnp.exp(jnp.where(incl, g_col - g_row, NEG_INF))
            pw = jnp.where(strict, -(kk * decay * b_col), 0.0)
            z = jnp.concatenate([vst.astype(F32) * b_col, kf * (b_col * jnp.exp(g_col))], axis=1)
            z = z + _dot(pw.astype(BF16), z.astype(BF16))
            for _ in range(5):
                pwb = pw.astype(BF16)
                pw = _dot(pwb, pwb)
                z = z + _dot(pw.astype(BF16), z.astype(BF16))
            wq_ref[0, cc, d, :CH4, :] = z[:, LANES:].astype(BF16)
            wq_ref[0, cc, d, CH4:, :] = (qst.astype(F32) * jnp.exp(g_col)).astype(BF16)
            u_ref[0, cc, d] = z[:, :LANES]
            a_ref[0, cc, d] = (qk * decay).astype(BF16)
            kgt_ref[0, cc, d] = (kf * jnp.exp(gl_col - g_col)).T.astype(BF16)


def _dn_local(qd, kd, vd, gc, bc, g_rows):
    b, ta, _ = qd.shape
    nc = ta // DN_CHUNK
    tok = lambda w: pl.BlockSpec((1, DN_CPS * DN_CHUNK, w), lambda bb, n: (bb, n, 0))
    out5 = lambda r_, c_: pl.BlockSpec((1, DN_CPS, 2, r_, c_), lambda bb, n: (bb, n, 0, 0, 0))
    return pl.pallas_call(
        _dn_local_kernel,
        grid=(b, nc // DN_CPS),
        in_specs=[tok(DN_WIDTH), tok(DN_WIDTH), tok(DN_WIDTH), tok(LANES), tok(LANES),
                  pl.BlockSpec((1, DN_CPS, 2, CH4), lambda bb, n: (bb, n, 0, 0))],
        out_specs=[out5(2 * CH4, LANES), out5(CH4, LANES), out5(CH4, CH4), out5(LANES, CH4)],
        out_shape=[jax.ShapeDtypeStruct((b, nc, 2, 2 * CH4, LANES), BF16),
                   jax.ShapeDtypeStruct((b, nc, 2, CH4, LANES), F32),
                   jax.ShapeDtypeStruct((b, nc, 2, CH4, CH4), BF16),
                   jax.ShapeDtypeStruct((b, nc, 2, LANES, CH4), BF16)],
        compiler_params=_cparams(("parallel", "parallel")),
        name="dn_local",
    )(qd, kd, vd, gc, bc, g_rows)


def _dn_scan_kernel(wq_f, u_f, a_f, kgt_f, gl_f, wq_b, u_b, a_b, kgt_b, gl_b, of_ref, ob_ref, state):
    n = pl.program_id(0)

    @pl.when(n == 0)
    def _():
        state[...] = jnp.zeros_like(state)

    r = lax.broadcasted_iota(jnp.int32, (CH4, DN_WIDTH), 0)
    c = lax.broadcasted_iota(jnp.int32, (CH4, DN_WIDTH), 1)
    own = (r // DN_CHUNK) == (c // LANES)
    pick = lambda m: jnp.concatenate(
        [m[h * DN_CHUNK:(h + 1) * DN_CHUNK, h * LANES:(h + 1) * LANES] for h in range(DN_HEADS)], axis=0)
    for bb in range(state.shape[0]):
        for d, (wq, u, a, kgt, gl, o_ref) in enumerate(((wq_f, u_f, a_f, kgt_f, gl_f, of_ref),
                                                         (wq_b, u_b, a_b, kgt_b, gl_b, ob_ref))):
            s_all = state[bb, d]
            ws = _dot(wq[bb, 0, 0], s_all.astype(BF16))
            v_new = u[bb, 0, 0] - pick(ws[:CH4])
            v_nb = v_new.astype(BF16)
            o = pick(ws[CH4:]) + _dot(a[bb, 0, 0], v_nb)
            for h in range(DN_HEADS):
                o_ref[bb, :, h * LANES:(h + 1) * LANES] = o[h * DN_CHUNK:(h + 1) * DN_CHUNK]
            v_bd = jnp.where(own, jnp.concatenate([v_nb] * DN_HEADS, axis=1), jnp.zeros((), BF16))
            state[bb, d] = s_all * jnp.exp(gl[bb, 0, 0]) + _dot(kgt[bb, 0, 0], v_bd)


def _dn_scan(wq, u, a, kgt, gl_wide, *, s_len):
    b, nc = wq.shape[:2]
    n_lat = s_len // DN_CHUNK
    n_ctx = nc - n_lat
    fwd = lambda n: jnp.where(n < n_ctx, n_lat + n, n - n_ctx)
    bwd = lambda n: nc - 1 - n
    specs = []
    for d, order in ((0, fwd), (1, bwd)):
        blk = lambda arr, d=d, order=order: pl.BlockSpec(
            (b, 1, 1) + arr.shape[3:], lambda n: (0, order(n), d, 0, 0))
        specs += [blk(wq), blk(u), blk(a), blk(kgt), blk(gl_wide)]
    o_spec = lambda order: pl.BlockSpec((b, DN_CHUNK, DN_WIDTH), lambda n: (0, order(n), 0))
    return pl.pallas_call(
        _dn_scan_kernel,
        grid=(nc,),
        in_specs=specs,
        out_specs=[o_spec(fwd), o_spec(bwd)],
        out_shape=[jax.ShapeDtypeStruct((b, nc * DN_CHUNK, DN_WIDTH), F32)] * 2,
        scratch_shapes=[pltpu.VMEM((b, 2, DN_HEAD_DIM, DN_WIDTH), F32)],
        compiler_params=_cparams(("arbitrary",)),
        name="dn_scan",
    )(wq, u, a, kgt, gl_wide, wq, u, a, kgt, gl_wide)


def _merge_kernel(ona_ref, of_ref, ob_ref, z_ref, sc_ref, schp_ref, schn_ref, gate_ref, x_ref,
                  g1l_ref, g1c_ref, sh2l_ref, sh2c_ref, sc2l_ref, sc2c_ref, ng_ref, og_ref, scw_ref,
                  wna_ref, wdn_ref, wsc_ref, wout_ref, wrh_ref, wrl_ref, rb_ref,
                  x1_ref, h2_ref, comb_ref, combt_ref, cpad, *, tm, s_len, ta):
    i = pl.program_id(1)
    d = x_ref.shape[-1]
    o = of_ref[0] + ob_ref[0]
    z = z_ref[0].astype(F32)
    parts = []
    for h in range(DN_HEADS):
        oh = o[:, h * LANES:(h + 1) * LANES]
        parts.append(oh * lax.rsqrt(jnp.mean(oh * oh, axis=-1, keepdims=True) + EPS))
    o_dn = jnp.concatenate(parts, axis=1) * og_ref[...] * _silu(z)

    def cu(ref_val):
        v = ref_val.astype(F32)
        return v[:, 2 * SC_WIDTH:] * v[:, :SC_WIDTH]
    scp = sc_ref[0].astype(F32)
    half = HALO // 2
    cpad[0:half, :] = cu(schp_ref[0])[half:, :]
    cpad[half:half + tm, :] = scp[:, 2 * SC_WIDTH:] * scp[:, :SC_WIDTH]
    cpad[half + tm:, :] = cu(schn_ref[0])[:half, :]
    gt = i * tm + lax.broadcasted_iota(jnp.int32, (tm, 1), 0)
    has_prev = (gt != 0) & (gt != s_len)
    has_next = (gt != s_len - 1) & (gt != ta - 1)
    conv = (scw_ref[0:1, :] * jnp.where(has_prev, cpad[pl.ds(half - 1, tm), :], 0.0)
            + scw_ref[1:2, :] * cpad[pl.ds(half, tm), :]
            + scw_ref[2:3, :] * jnp.where(has_next, cpad[pl.ds(half + 1, tm), :], 0.0))
    o_sc = scp[:, SC_WIDTH:2 * SC_WIDTH] * conv

    merged = (gate_ref[0, :, :d].astype(F32) * _dot(ona_ref[0], wna_ref[...])
              + gate_ref[0, :, d:2 * d].astype(F32) * _dot(o_dn.astype(BF16), wdn_ref[...])
              + gate_ref[0, :, 2 * d:].astype(F32) * _dot(o_sc.astype(BF16), wsc_ref[...]))
    y = _dot(merged.astype(BF16), wout_ref[...])
    x1 = x_ref[0] + _row_select(i, tm, s_len, g1l_ref[0], g1c_ref[...]) * y
    x1_ref[0] = x1

    hn = x1 * lax.rsqrt(jnp.mean(x1 * x1, axis=-1, keepdims=True) + EPS) * ng_ref[...]
    h2 = (hn * (1.0 + _row_select(i, tm, s_len, sc2l_ref[0], sc2c_ref[...]))
          + _row_select(i, tm, s_len, sh2l_ref[0], sh2c_ref[...]))
    h2_ref[0] = h2.astype(BF16)

    h2_hi, h2_lo = _split_bf16(h2)
    logits = _dot3(h2_hi, h2_lo, wrh_ref[...], wrl_ref[...])
    scores = jax.nn.sigmoid(logits.T)
    biased = scores + rb_ref[...]
    rows = [biased[e:e + 1, :] for e in range(N_EXPERTS)]
    gsc = []
    for g in range(N_GROUPS):
        a, b_, c_, d_ = rows[4 * g:4 * g + 4]
        gsc.append(jnp.maximum(jnp.maximum(jnp.maximum(a + b_, a + c_), jnp.maximum(a + d_, b_ + c_)),
                               jnp.maximum(b_ + d_, c_ + d_)))
    best = gsc[0]
    grp = jnp.zeros_like(best, dtype=jnp.int32)
    for g in range(1, N_GROUPS):
        upd = gsc[g] > best
        best = jnp.where(upd, gsc[g], best)
        grp = jnp.where(upd, g, grp)
    eidx = lax.broadcasted_iota(jnp.int32, biased.shape, 0)
    masked = jnp.where((eidx // EXPERTS_PER_GROUP) == grp, biased, -jnp.inf)
    big = jnp.int32(LANES)
    m1 = masked.max(axis=0, keepdims=True)
    i1 = jnp.where(masked == m1, eidx, big).min(axis=0, keepdims=True)
    masked2 = jnp.where(eidx == i1, -jnp.inf, masked)
    m2 = masked2.max(axis=0, keepdims=True)
    i2 = jnp.where(masked2 == m2, eidx, big).min(axis=0, keepdims=True)
    sel1 = eidx == i1
    sel2 = eidx == i2
    s1 = jnp.sum(jnp.where(sel1, scores, 0.0), axis=0, keepdims=True)
    s2 = jnp.sum(jnp.where(sel2, scores, 0.0), axis=0, keepdims=True)
    tot = s1 + s2
    comb_t = jnp.where(sel1, s1 / tot, 0.0) + jnp.where(sel2, s2 / tot, 0.0)
    meta_t = jnp.where(eidx == N_EXPERTS, grp.astype(F32), comb_t)
    combt_ref[0] = meta_t
    comb_ref[0] = meta_t.T


def _merge(o_na, o_f, o_b, z, sc, gates, xs, mods_l, mods_c, norm_g, out_g, sc_w, wna, wdn, wsc, wout, wr, rb,
           *, s_len, tm):
    b, ta, d = xs.shape
    col_z = col_sc = 0
    hb = tm // HALO
    n_hb = ta // HALO
    tok = lambda w, col=0: pl.BlockSpec((1, tm, w), lambda bb, i: (bb, i, col))
    vec = lambda w: pl.BlockSpec((1, w), lambda bb, i: (0, 0))
    bvec = lambda: pl.BlockSpec((1, 1, d), lambda bb, i: (bb, 0, 0))
    full = lambda a: pl.BlockSpec(a.shape, lambda bb, i: (0,) * a.ndim)
    g1l, sh2l, sc2l = mods_l
    g1c, sh2c, sc2c = mods_c
    w3 = 3 * SC_WIDTH
    wr_hi = wr.astype(BF16)
    wr_lo = (wr - wr_hi.astype(F32)).astype(BF16)
    return pl.pallas_call(
        functools.partial(_merge_kernel, tm=tm, s_len=s_len, ta=ta),
        grid=(b, ta // tm),
        in_specs=[tok(NA_WIDTH), tok(DN_WIDTH), tok(DN_WIDTH), tok(DN_WIDTH, col_z), tok(w3, col_sc),
                  pl.BlockSpec((1, HALO, w3), lambda bb, i: (bb, jnp.maximum(i * hb - 1, 0), col_sc)),
                  pl.BlockSpec((1, HALO, w3), lambda bb, i: (bb, jnp.minimum((i + 1) * hb, n_hb - 1), col_sc)),
                  tok(3 * d, 0), tok(d),
                  bvec(), vec(d), bvec(), vec(d), bvec(), vec(d), vec(d), vec(DN_WIDTH),
                  full(sc_w), full(wna), full(wdn), full(wsc), full(wout), full(wr_hi), full(wr_lo), full(rb)],
        out_specs=[tok(d), tok(d), tok(LANES), pl.BlockSpec((1, LANES, tm), lambda bb, i: (bb, 0, i))],
        out_shape=[jax.ShapeDtypeStruct((b, ta, d), F32),
                   jax.ShapeDtypeStruct((b, ta, d), BF16),
                   jax.ShapeDtypeStruct((b, ta, LANES), F32),
                   jax.ShapeDtypeStruct((b, LANES, ta), F32)],
        scratch_shapes=[pltpu.VMEM((tm + HALO, SC_WIDTH), F32)],
        compiler_params=_cparams(("parallel", "parallel")),
        name="merge_router",
    )(o_na, o_f, o_b, z, sc, sc, sc, gates, xs, g1l, g1c, sh2l, sh2c, sc2l, sc2c, norm_g, out_g,
      sc_w, wna, wdn, wsc, wout, wr_hi, wr_lo, rb)


MOE_CAP = 256


def _moe_kernel(h_ref, comb_ref, combt_ref, lt_ref, gt_ref, wg_ref, wu_ref, wd_ref, x_ref, g2l_ref, g2c_ref,
                o_ref, acc, rank_r, rank_c, *, tm, s_len):
    i = pl.program_id(1)
    g = pl.program_id(2)

    comb = comb_ref[0]
    lane = lax.broadcasted_iota(jnp.int32, (tm, LANES), 1)
    grp_row = combt_ref[0, N_EXPERTS:N_EXPERTS + 1, :]
    grp_col = jnp.sum(jnp.where(lane == N_EXPERTS, comb, 0.0), axis=-1, keepdims=True)

    @pl.when(g == 0)
    def _():
        acc[...] = jnp.zeros_like(acc)
        sub = lax.broadcasted_iota(jnp.int32, (8, tm), 0).astype(F32)
        before = _dot(jnp.where(sub == grp_row, 1.0, 0.0).astype(BF16), lt_ref[...])
        rank_r[...] = jnp.broadcast_to(
            jnp.sum(jnp.where(sub == grp_row, before, 0.0), axis=0, keepdims=True), (8, tm))
        lane_f = lane.astype(F32)
        before = _dot(gt_ref[...], jnp.where(lane_f == grp_col, 1.0, 0.0).astype(BF16))
        rank_c[...] = jnp.broadcast_to(
            jnp.sum(jnp.where(lane_f == grp_col, before, 0.0), axis=-1, keepdims=True), (tm, LANES))

    gf = g.astype(F32)
    member_row = grp_row == gf
    member_col = grp_col == gf
    rank_row = rank_r[0:1, :]
    rank_col = rank_c[:, 0:1]
    count = jnp.sum(jnp.where(member_row, 1.0, 0.0))
    comb_hi, comb_lo = _split_bf16(comb)
    d = x_ref.shape[-1]

    for k in range(tm // MOE_CAP):
        @pl.when(count > k * MOE_CAP)
        def _(k=k):
            slot_c = (lax.broadcasted_iota(jnp.int32, (MOE_CAP, 1), 0) + k * MOE_CAP).astype(F32)
            gather = jnp.where(member_row & (rank_row == slot_c), 1.0, 0.0).astype(BF16)
            hs = _dot(gather, h_ref[0]).astype(BF16)
            cws = _dot(gather, comb_hi) + _dot(gather, comb_lo)
            lane_c = lax.broadcasted_iota(jnp.int32, (MOE_CAP, LANES), 1)
            ys = jnp.zeros((MOE_CAP, d), F32)
            for e4 in range(EXPERTS_PER_GROUP):
                cw = jnp.sum(jnp.where(lane_c == g * EXPERTS_PER_GROUP + e4, cws, 0.0), axis=-1, keepdims=True)
                hidden = _silu(_dot(hs, wg_ref[e4])) * _dot(hs, wu_ref[e4]) * cw
                ys = ys + _dot(hidden.astype(BF16), wd_ref[e4])
            slot_r = (lax.broadcasted_iota(jnp.int32, (1, MOE_CAP), 1) + k * MOE_CAP).astype(F32)
            scatter = jnp.where(member_col & (rank_col == slot_r), 1.0, 0.0).astype(BF16)
            ys_hi, ys_lo = _split_bf16(ys)
            acc[...] += _dot(scatter, ys_hi) + _dot(scatter, ys_lo)

    @pl.when(g == N_GROUPS - 1)
    def _():
        o_ref[0] = x_ref[0] + _row_select(i, tm, s_len, g2l_ref[0], g2c_ref[...]) * acc[...]


def _moe(h2, comb, comb_t, wg, wu, wd, x1, g2l, g2c, *, s_len, tm):
    b, ta, d = x1.shape
    tok = lambda w: pl.BlockSpec((1, tm, w), lambda bb, i, g: (bb, i, 0))
    grp = lambda a: pl.BlockSpec((EXPERTS_PER_GROUP,) + a.shape[1:], lambda bb, i, g: (g, 0, 0))
    tri = lambda: pl.BlockSpec((tm, tm), lambda bb, i, g: (0, 0))
    earlier = np.arange(tm)[:, None] < np.arange(tm)[None, :]
    lt = jnp.asarray(earlier, BF16)
    gt = jnp.asarray(earlier.T, BF16)
    return pl.pallas_call(
        functools.partial(_moe_kernel, tm=tm, s_len=s_len),
        grid=(b, ta // tm, N_GROUPS),
        in_specs=[tok(d), tok(LANES),
                  pl.BlockSpec((1, LANES, tm), lambda bb, i, g: (bb, 0, i)),
                  tri(), tri(),
                  grp(wg), grp(wu), grp(wd),
                  tok(d),
                  pl.BlockSpec((1, 1, d), lambda bb, i, g: (bb, 0, 0)),
                  pl.BlockSpec((1, d), lambda bb, i, g: (0, 0))],
        out_specs=tok(d),
        out_shape=jax.ShapeDtypeStruct((b, ta, d), F32),
        scratch_shapes=[pltpu.VMEM((tm, d), F32), pltpu.VMEM((8, tm), F32), pltpu.VMEM((tm, LANES), F32)],
        compiler_params=_cparams(("parallel", "parallel", "arbitrary")),
        name="moe_grouped",
    )(h2, comb, comb_t, lt, gt, wg, wu, wd, x1, g2l, g2c)


def _rope_tables(s_len, ctx_len):
    t = np.arange(s_len)
    n_freq = DN_HEAD_DIM // 4
    inv_freq = (np.float32(ROPE_BASE) ** (-np.arange(n_freq, dtype=np.float32) / n_freq)).astype(np.float32)
    pos = np.stack([t // GRID_W, t % GRID_W], axis=-1).astype(np.float32)
    ang = (pos[..., None] * inv_freq).astype(np.float32)
    cos, sin = np.cos(ang.astype(np.float64)), np.sin(ang.astype(np.float64))
    rc = np.concatenate([cos[:, 0], cos[:, 0], cos[:, 1], cos[:, 1]], axis=-1)
    rs = np.concatenate([-sin[:, 0], sin[:, 0], -sin[:, 1], sin[:, 1]], axis=-1)
    rc = np.concatenate([rc, np.ones((ctx_len, LANES))], axis=0)
    rs = np.concatenate([rs, np.zeros((ctx_len, LANES))], axis=0)
    return jnp.asarray(rc, F32), jnp.asarray(rs, F32)


def _pick_tile(ta, candidates):
    for t in candidates:
        if ta % t == 0:
            return t
    raise ValueError(f"no row tile for {ta} tokens")


def kernel(x, c, ctx, c_ctx, w_ada, b_ada, norm_mix_g, norm_ffn_g, w_in, na_q_norm_g, na_k_norm_g, na_rpb, dn_conv_w, dn_a_log, dn_dt_bias, dn_out_norm_g, sc_conv_w, w_branch_na, w_branch_dn, w_branch_sc, w_out, w_router, router_bias, moe_w_gate, moe_w_up, moe_w_down):
    b, s_len, d = x.shape
    ctx_len = ctx.shape[1]
    depth = w_ada.shape[0]
    ta = s_len + ctx_len
    assert ctx_len == NA_BLK and s_len % (WIN_R * GRID_W) == 0 and s_len // NA_BLK >= 4
    assert ctx_len % PREP_TILE == 0 and b + 1 <= 8
    tm_proj = _pick_tile(ta, (768, 512, 256))
    tm_merge = _pick_tile(ta, (384, 256))
    tm_moe = _pick_tile(ta, (768, 512, 256))

    xs = jnp.concatenate([x, ctx], axis=1)
    c_rows = jnp.zeros((8, d), F32).at[:b].set(c).at[b].set(c_ctx)
    mod = _modulation(c_rows, w_ada, b_ada)

    sizes = (3 * NA_WIDTH, 3 * DN_WIDTH, DN_WIDTH, 4 * DN_HEADS, 3 * SC_WIDTH, 3 * d)
    cuts = np.cumsum((0,) + sizes)
    seg = lambda w, k: w[:, cuts[k]:cuts[k + 1]]

    rope_c, rope_s = _rope_tables(s_len, ctx_len)
    bd = jnp.asarray(np.kron(np.eye(NA_HEADS), np.ones((NA_HEAD_DIM, NA_HEAD_DIM))), BF16)
    nc = ta // DN_CHUNK
    wr = jnp.zeros((d, LANES), F32).at[:, :N_EXPERTS].set(w_router)
    rb = jnp.zeros((LANES, 1), F32).at[:N_EXPERTS, 0].set(router_bias)

    for l in range(depth):
        m = mod[l].reshape(8, N_MOD, d)
        lat = lambda k: m[:b, k][:, None, :]
        cx = lambda k: m[b:b + 1, k]
        wl = w_in[l]
        w_ab = jnp.zeros((d, LANES), F32).at[:, :4 * DN_HEADS].set(seg(wl, 3))
        gqk = jnp.concatenate([jnp.tile(na_q_norm_g[l], NA_HEADS) * (NA_HEAD_DIM ** -0.5),
                               jnp.tile(na_k_norm_g[l], NA_HEADS)])[None, :]
        gates, na_q, na_k, na_v, dn_qkv, sc_p, dn_z, ab = _projection(
            xs, (lat(0), lat(1)), (cx(0), cx(1)), norm_mix_g[l][None, :],
            seg(wl, 5).astype(BF16), seg(wl, 0).astype(BF16), seg(wl, 1).astype(BF16),
            seg(wl, 4).astype(BF16), seg(wl, 2).astype(BF16), w_ab, gqk, bd, s_len=s_len, tm=tm_proj)

        o_na = _neighbourhood_attention(na_q, na_k, na_v, _na_bias_table(na_rpb[l]), s_len=s_len)

        alog_row = jnp.zeros((1, LANES), F32).at[0, :2 * DN_HEADS].set(dn_a_log[l].reshape(-1))
        dtb_row = jnp.zeros((1, LANES), F32).at[0, :2 * DN_HEADS].set(dn_dt_bias[l].reshape(-1))
        qd, kd, vd, gc, bc = _dn_prep(dn_qkv, ab, dn_conv_w[l], rope_c, rope_s, alog_row, dtb_row,
                                      s_len=s_len)
        g8 = gc[..., :2 * DN_HEADS].reshape(b, nc, DN_CHUNK, 2, DN_HEADS)
        g_rows = jnp.transpose(g8, (0, 1, 3, 4, 2)).reshape(b, nc, 2, CH4)
        g_last = jnp.stack([g8[:, :, DN_CHUNK - 1, 0], g8[:, :, 0, 1]], axis=2)
        gl_wide = jnp.repeat(g_last, LANES, axis=-1)[:, :, :, None, :]
        wq, u, a, kgt = _dn_local(qd, kd, vd, gc, bc, g_rows)
        o_f, o_b = _dn_scan(wq, u, a, kgt, gl_wide, s_len=s_len)

        x1, h2, comb, comb_t = _merge(
            o_na, o_f, o_b, dn_z, sc_p, gates, xs, (lat(2), lat(3), lat(4)), (cx(2), cx(3), cx(4)),
            norm_ffn_g[l][None, :], jnp.tile(dn_out_norm_g[l], DN_HEADS)[None, :], sc_conv_w[l],
            w_branch_na[l].astype(BF16), w_branch_dn[l].astype(BF16), w_branch_sc[l].astype(BF16),
            w_out[l].astype(BF16), wr, rb, s_len=s_len, tm=tm_merge)

        xs = _moe(h2, comb, comb_t, moe_w_gate[l].astype(BF16), moe_w_up[l].astype(BF16),
                  moe_w_down[l].astype(BF16), x1, lat(5), cx(5), s_len=s_len, tm=tm_moe)

    return xs[:, :s_len]
```

```python
import functools
import math

import numpy as np
import jax
import jax.numpy as jnp
from jax import lax
from jax.experimental import pallas as pl
from jax.experimental.pallas import tpu as pltpu

F32 = jnp.float32
BF16 = jnp.bfloat16
HIGHEST = lax.Precision.HIGHEST

GRID_W = 64
NA_HEADS = 8
NA_HEAD_DIM = 64
NA_WIDTH = NA_HEADS * NA_HEAD_DIM
WIN_R = 8
WIN_C = 16
DN_HEADS = 4
DN_HEAD_DIM = 128
DN_WIDTH = DN_HEADS * DN_HEAD_DIM
DN_CONV = 5
DN_CHUNK = 64
ROPE_BASE = 10000.0
SC_WIDTH = 512
SC_CONV = 3
N_EXPERTS = 16
N_GROUPS = 4
EXPERTS_PER_GROUP = N_EXPERTS // N_GROUPS
D_EXPERT = 512
N_MOD = 6
EPS = 1e-6
NEG_INF = -1e30

LANES = 128
HALO = 16
PREP_TILE = 256
VMEM_LIMIT = 56 * 1024 * 1024

N_PROJ_TILE = 512


def _cparams(sem):
    return pltpu.CompilerParams(dimension_semantics=sem, vmem_limit_bytes=VMEM_LIMIT)


def _dot(a, b):
    return jnp.dot(a, b, preferred_element_type=F32)


def _dot_nt(a, b):
    return lax.dot_general(a, b, (((1,), (1,)), ((), ())), preferred_element_type=F32)


def _silu(x):
    return x * jax.nn.sigmoid(x)


def _row_select(i, tm, s_len, lat, ctx):
    row = i * tm + lax.broadcasted_iota(jnp.int32, (tm, 1), 0)
    return jnp.where(row >= s_len, ctx, lat)


def _mod_kernel(c_ref, w_ref, b_ref, o_ref):
    c_hi, c_lo = _split_bf16(_silu(c_ref[...]))
    w_hi, w_lo = _split_bf16(w_ref[0])
    o_ref[0] = _dot3(c_hi, c_lo, w_hi, w_lo) + b_ref[0]


def _modulation(c_rows, w_ada, b_ada):
    depth, d, n = w_ada.shape
    tn = 512
    return pl.pallas_call(
        _mod_kernel,
        grid=(depth, n // tn),
        in_specs=[pl.BlockSpec((8, d), lambda l, j: (0, 0)),
                  pl.BlockSpec((1, d, tn), lambda l, j: (l, 0, j)),
                  pl.BlockSpec((1, 1, tn), lambda l, j: (l, 0, j))],
        out_specs=pl.BlockSpec((1, 8, tn), lambda l, j: (l, 0, j)),
        out_shape=jax.ShapeDtypeStruct((depth, 8, n), F32),
        compiler_params=_cparams(("parallel", "parallel")),
        name="adaln_mod",
    )(c_rows, w_ada, b_ada.reshape(depth, 1, n))


def _split_bf16(a):
    hi = a.astype(BF16)
    return hi, (a - hi.astype(F32)).astype(BF16)


def _dot3(a_hi, a_lo, b_hi, b_lo):
    return _dot(a_hi, b_hi) + _dot(a_lo, b_hi) + _dot(a_hi, b_lo)


def _modulated_norm(x_ref, shl_ref, scl_ref, shc_ref, scc_ref, g_ref, *, tm, s_len):
    i = pl.program_id(1)
    x = x_ref[0]
    y = x * lax.rsqrt(jnp.mean(x * x, axis=-1, keepdims=True) + EPS) * g_ref[...]
    scale = _row_select(i, tm, s_len, scl_ref[0], scc_ref[...])
    shift = _row_select(i, tm, s_len, shl_ref[0], shc_ref[...])
    return y * (1.0 + scale) + shift


def _proj_gate_kernel(x_ref, shl_ref, scl_ref, shc_ref, scc_ref, g_ref, w_ref, gate_ref, *, tm, s_len):
    h = _modulated_norm(x_ref, shl_ref, scl_ref, shc_ref, scc_ref, g_ref, tm=tm, s_len=s_len).astype(BF16)
    for c0 in range(0, w_ref.shape[1], N_PROJ_TILE):
        cols = slice(c0, c0 + N_PROJ_TILE)
        gate_ref[0, :, cols] = jax.nn.sigmoid(_dot(h, w_ref[:, cols])).astype(BF16)


def _proj_mix_kernel(x_ref, shl_ref, scl_ref, shc_ref, scc_ref, g_ref, wna_ref, wdn_ref, wsc_ref, wz_ref,
                     wabh_ref, wabl_ref, gqk_ref, bd_ref,
                     q_ref, k_ref, v_ref, dn_ref, sc_ref, z_ref, ab_ref, *, tm, s_len):
    h = _modulated_norm(x_ref, shl_ref, scl_ref, shc_ref, scc_ref, g_ref, tm=tm, s_len=s_len)
    h_hi, h_lo = _split_bf16(h)
    ab_ref[0] = _dot3(h_hi, h_lo, wabh_ref[...], wabl_ref[...])
    for part, out in enumerate((q_ref, k_ref)):
        cols = slice(part * NA_WIDTH, (part + 1) * NA_WIDTH)
        a = _dot(h_hi, wna_ref[:, cols])
        ss = _dot((a * a).astype(BF16), bd_ref[...])
        out[0] = (a * lax.rsqrt(ss * (1.0 / NA_HEAD_DIM) + EPS) * gqk_ref[:, cols]).astype(BF16)
    v_ref[0] = _dot(h_hi, wna_ref[:, 2 * NA_WIDTH:]).astype(BF16)
    for w_ref, out in ((wdn_ref, dn_ref), (wsc_ref, sc_ref), (wz_ref, z_ref)):
        for c0 in range(0, w_ref.shape[1], N_PROJ_TILE):
            cols = slice(c0, c0 + N_PROJ_TILE)
            out[0, :, cols] = _dot(h_hi, w_ref[:, cols]).astype(BF16)


def _projection(xs, mod_l, mod_c, norm_g, w_gate, w_na, w_dn, w_sc, w_z, w_ab, gqk, bd, *, s_len, tm):
    b, ta, d = xs.shape
    shl, scl = mod_l
    shc, scc = mod_c
    wab_hi, wab_lo = _split_bf16(w_ab)
    tok = lambda w: pl.BlockSpec((1, tm, w), lambda bb, i: (bb, i, 0))
    bvec = lambda: pl.BlockSpec((1, 1, d), lambda bb, i: (bb, 0, 0))
    full = lambda a: pl.BlockSpec(a.shape, lambda bb, i: (0,) * a.ndim)
    head = [tok(d), bvec(), bvec(), full(shc), full(scc), full(norm_g)]
    gates = pl.pallas_call(
        functools.partial(_proj_gate_kernel, tm=tm, s_len=s_len),
        grid=(b, ta // tm),
        in_specs=head + [full(w_gate)],
        out_specs=tok(w_gate.shape[1]),
        out_shape=jax.ShapeDtypeStruct((b, ta, w_gate.shape[1]), BF16),
        compiler_params=_cparams(("parallel", "parallel")),
        name="proj_gates",
    )(xs, shl, scl, shc, scc, norm_g, w_gate)
    consts = (w_na, w_dn, w_sc, w_z, wab_hi, wab_lo, gqk, bd)
    widths = (NA_WIDTH, NA_WIDTH, NA_WIDTH, w_dn.shape[1], w_sc.shape[1], w_z.shape[1])
    outs = pl.pallas_call(
        functools.partial(_proj_mix_kernel, tm=tm, s_len=s_len),
        grid=(b, ta // tm),
        in_specs=head + [full(a) for a in consts],
        out_specs=[tok(w) for w in widths] + [tok(LANES)],
        out_shape=[jax.ShapeDtypeStruct((b, ta, w), BF16) for w in widths]
        + [jax.ShapeDtypeStruct((b, ta, LANES), F32)],
        compiler_params=_cparams(("parallel", "parallel")),
        name="proj_mixers",
    )(xs, shl, scl, shc, scc, norm_g, *consts)
    return (gates,) + tuple(outs)


NA_QROWS = 4
NA_BLK = NA_QROWS * GRID_W
WIN_TOK = WIN_R * GRID_W


def _stack_two_heads(q2):
    lane = lax.broadcasted_iota(jnp.int32, q2.shape, 1)
    zero = jnp.zeros_like(q2)
    return jnp.concatenate([jnp.where(lane < NA_HEAD_DIM, q2, zero),
                            jnp.where(lane >= NA_HEAD_DIM, q2, zero)], axis=0)


def _unstack_two_heads(o2):
    m = o2.shape[0] // 2
    lane = lax.broadcasted_iota(jnp.int32, (m, o2.shape[1]), 1)
    return jnp.where(lane < NA_HEAD_DIM, o2[:m], o2[m:])


def _na_kernel(q_ref, kp_ref, kc_ref, kn_ref, vp_ref, vc_ref, vn_ref, kx_ref, vx_ref, bias_ref, o_ref,
               kbuf, vbuf, *, n_rows):
    i = pl.program_id(1)
    n_lat = n_rows // NA_QROWS

    @pl.when(i < n_lat)
    def _():
        for slot, (kr, vr) in enumerate(((kp_ref, vp_ref), (kc_ref, vc_ref), (kn_ref, vn_ref))):
            kbuf[slot * NA_BLK:(slot + 1) * NA_BLK, :] = kr[0]
            vbuf[slot * NA_BLK:(slot + 1) * NA_BLK, :] = vr[0]
        win = []
        for j in range(NA_QROWS):
            r = i * NA_QROWS + j
            start = jnp.clip(r - WIN_R // 2, 0, n_rows - WIN_R)
            win.append((pl.multiple_of((start - (i - 1) * NA_QROWS) * GRID_W, GRID_W), r - start))

        rows2 = 2 * GRID_W
        for hp in range(NA_HEADS // 2):
            lanes = slice(hp * LANES, (hp + 1) * LANES)
            kx, vx = kx_ref[0, :, lanes], vx_ref[0, :, lanes]
            qs = jnp.concatenate([_stack_two_heads(q_ref[0, j * GRID_W:(j + 1) * GRID_W, lanes])
                                  for j in range(NA_QROWS)], axis=0)
            s_c = _dot_nt(qs, kx)
            s_w = jnp.concatenate(
                [_dot_nt(qs[j * rows2:(j + 1) * rows2], kbuf[pl.ds(win[j][0], WIN_TOK), lanes])
                 + bias_ref[win[j][1], hp] for j in range(NA_QROWS)], axis=0)
            m = jnp.maximum(s_w.max(axis=-1, keepdims=True), s_c.max(axis=-1, keepdims=True))
            p_w = jnp.exp(s_w - m)
            p_c = jnp.exp(s_c - m)
            p_wb = p_w.astype(BF16)
            o = _dot(p_c.astype(BF16), vx) + jnp.concatenate(
                [_dot(p_wb[j * rows2:(j + 1) * rows2], vbuf[pl.ds(win[j][0], WIN_TOK), lanes])
                 for j in range(NA_QROWS)], axis=0)
            o = o / (p_w.sum(axis=-1, keepdims=True) + p_c.sum(axis=-1, keepdims=True))
            for j in range(NA_QROWS):
                o_ref[0, j * GRID_W:(j + 1) * GRID_W, lanes] = _unstack_two_heads(
                    o[j * rows2:(j + 1) * rows2]).astype(BF16)

    @pl.when(i == n_lat)
    def _():
        for hp in range(NA_HEADS // 2):
            lanes = slice(hp * LANES, (hp + 1) * LANES)
            s = _dot_nt(_stack_two_heads(q_ref[0, :, lanes]), kx_ref[0, :, lanes])
            p = jnp.exp(s - s.max(axis=-1, keepdims=True))
            o = _dot(p.astype(BF16), vx_ref[0, :, lanes]) / p.sum(axis=-1, keepdims=True)
            o_ref[0, :, lanes] = _unstack_two_heads(o).astype(BF16)


def _neighbourhood_attention(q, k, v, bias_tab, *, s_len):
    b, ta, _ = q.shape
    n_rows = s_len // GRID_W
    n_lat = n_rows // NA_QROWS
    blk = lambda off: pl.BlockSpec((1, NA_BLK, NA_WIDTH), lambda bb, i: (bb, jnp.clip(i + off, 0, n_lat - 1), 0))
    cx = lambda: pl.BlockSpec((1, NA_BLK, NA_WIDTH), lambda bb, i: (bb, n_lat, 0))
    own = lambda: pl.BlockSpec((1, NA_BLK, NA_WIDTH), lambda bb, i: (bb, i, 0))
    return pl.pallas_call(
        functools.partial(_na_kernel, n_rows=n_rows),
        grid=(b, n_lat + 1),
        in_specs=[own(), blk(-1), blk(0), blk(1), blk(-1), blk(0), blk(1), cx(), cx(),
                  pl.BlockSpec(bias_tab.shape, lambda bb, i: (0, 0, 0, 0))],
        out_specs=own(),
        out_shape=jax.ShapeDtypeStruct((b, ta, NA_WIDTH), BF16),
        scratch_shapes=[pltpu.VMEM((3 * NA_BLK, NA_WIDTH), BF16),
                        pltpu.VMEM((3 * NA_BLK, NA_WIDTH), BF16)],
        compiler_params=_cparams(("parallel", "arbitrary")),
        name="na_attention",
    )(q, k, k, k, v, v, v, k, v, bias_tab)


def _na_bias_table(rpb):
    qc = np.arange(GRID_W)[:, None]
    kc = np.arange(GRID_W)[None, :]
    win_c0 = np.clip(qc - WIN_C // 2, 0, GRID_W - WIN_C)
    in_win = (kc >= win_c0) & (kc < win_c0 + WIN_C)
    rel_c = np.clip(kc - qc + WIN_C - 1, 0, 2 * WIN_C - 2)
    pick_c = (rel_c[..., None] == np.arange(2 * WIN_C - 1)).astype(np.float32)
    by_col = jnp.einsum('hab,qkb->haqk', rpb.astype(F32), pick_c, precision=HIGHEST)
    by_col = jnp.where(in_win[None, None], by_col, NEG_INF)
    tab = jnp.stack([by_col[:, WIN_R - 1 - v:2 * WIN_R - 1 - v] for v in range(WIN_R)], axis=0)
    tab = jnp.transpose(tab, (0, 1, 3, 2, 4))
    return tab.reshape(WIN_R, NA_HEADS // 2, 2 * GRID_W, WIN_R * GRID_W)


def _dn_prep_kernel(x_ref, hp_ref, hn_ref, w_ref, rc_ref, rs_ref, ab_ref, alog_ref, dtb_ref,
                    q_ref, k_ref, v_ref, g_ref, beta_ref, xpad, *, s_len, ta):
    i = pl.program_id(1)
    t0 = i * PREP_TILE
    has_prev = jnp.logical_and(t0 != 0, t0 != s_len).astype(F32)
    has_next = jnp.logical_and(t0 + PREP_TILE != s_len, t0 + PREP_TILE != ta).astype(F32)
    half = HALO // 2
    xpad[0:half, :] = hp_ref[0].astype(F32)[half:, :] * has_prev
    xpad[half:half + PREP_TILE, :] = x_ref[0].astype(F32)
    xpad[half + PREP_TILE:, :] = hn_ref[0].astype(F32)[:half, :] * has_next
    xp = xpad[...]
    n_pad = xp.shape[0]
    y = jnp.zeros((PREP_TILE, x_ref.shape[-1]), F32)
    for tap in range(DN_CONV):
        shifted = xp if tap == DN_CONV // 2 else pltpu.roll(xp, (DN_CONV // 2 - tap) % n_pad, 0)
        y = y + w_ref[tap:tap + 1, :] * shifted[half:half + PREP_TILE]
    y = _silu(y)

    lane = lax.broadcasted_iota(jnp.int32, (PREP_TILE, LANES), 1)
    first_half = (lane % (LANES // 2)) < (LANES // 4)
    rc = rc_ref[...]
    rs = rs_ref[...]

    def norm_rope(u, scale):
        u = u * (lax.rsqrt(jnp.sum(u * u, axis=-1, keepdims=True) + EPS) * scale)
        partner = jnp.where(first_half, pltpu.roll(u, LANES - LANES // 4, 1), pltpu.roll(u, LANES // 4, 1))
        return u * rc + partner * rs

    for h in range(DN_HEADS):
        lanes = slice(h * LANES, (h + 1) * LANES)
        qh = norm_rope(y[:, h * LANES:(h + 1) * LANES], DN_HEAD_DIM ** -0.5)
        kh = norm_rope(y[:, DN_WIDTH + h * LANES:DN_WIDTH + (h + 1) * LANES], 1.0)
        q_ref[0, :, lanes] = qh.astype(BF16)
        k_ref[0, :, lanes] = kh.astype(BF16)
    v_ref[0] = y[:, 2 * DN_WIDTH:].astype(BF16)

    ab = ab_ref[0]
    z = ab + dtb_ref[...]
    softplus = jnp.maximum(z, 0.0) + jnp.log1p(jnp.exp(-jnp.abs(z)))
    g = -jnp.exp(alog_ref[...]) * softplus
    g = jnp.where(lane < 2 * DN_HEADS, g, 0.0)
    beta = jax.nn.sigmoid(pltpu.roll(ab, LANES - 2 * DN_HEADS, 1))
    beta_ref[0] = jnp.where(lane < 2 * DN_HEADS, beta, 0.0)

    r = lax.broadcasted_iota(jnp.int32, (PREP_TILE, PREP_TILE), 0)
    c = lax.broadcasted_iota(jnp.int32, (PREP_TILE, PREP_TILE), 1)
    same = (r // DN_CHUNK) == (c // DN_CHUNK)
    tri_f = jnp.where(same & (c <= r), 1.0, 0.0).astype(BF16)
    tri_b = jnp.where(same & (c >= r), 1.0, 0.0).astype(BF16)
    g1 = g.astype(BF16)
    rem = g - g1.astype(F32)
    g2 = rem.astype(BF16)
    g3 = (rem - g2.astype(F32)).astype(BF16)
    cs_f = _dot(tri_f, g1) + _dot(tri_f, g2) + _dot(tri_f, g3)
    cs_b = _dot(tri_b, g1) + _dot(tri_b, g2) + _dot(tri_b, g3)
    g_ref[0] = jnp.where(lane < DN_HEADS, cs_f, cs_b)


def _dn_prep(p, ab, conv_w, rope_c, rope_s, alog_row, dtb_row, *, s_len):
    b, ta, w3 = p.shape
    col_dn = 0
    n_t = ta // PREP_TILE
    hb = PREP_TILE // HALO
    n_hb = ta // HALO
    tok = lambda: pl.BlockSpec((1, PREP_TILE, DN_WIDTH), lambda bb, i: (bb, i, 0))
    sca = lambda: pl.BlockSpec((1, PREP_TILE, LANES), lambda bb, i: (bb, i, 0))
    row = lambda: pl.BlockSpec((1, LANES), lambda bb, i: (0, 0))
    return pl.pallas_call(
        functools.partial(_dn_prep_kernel, s_len=s_len, ta=ta),
        grid=(b, n_t),
        in_specs=[pl.BlockSpec((1, PREP_TILE, w3), lambda bb, i: (bb, i, col_dn)),
                  pl.BlockSpec((1, HALO, w3), lambda bb, i: (bb, jnp.maximum(i * hb - 1, 0), col_dn)),
                  pl.BlockSpec((1, HALO, w3), lambda bb, i: (bb, jnp.minimum((i + 1) * hb, n_hb - 1), col_dn)),
                  pl.BlockSpec((DN_CONV, w3), lambda bb, i: (0, 0)),
                  pl.BlockSpec((PREP_TILE, LANES), lambda bb, i: (i, 0)),
                  pl.BlockSpec((PREP_TILE, LANES), lambda bb, i: (i, 0)),
                  sca(), row(), row()],
        out_specs=[tok(), tok(), tok(), sca(), sca()],
        out_shape=[jax.ShapeDtypeStruct((b, ta, DN_WIDTH), BF16)] * 3
        + [jax.ShapeDtypeStruct((b, ta, LANES), F32)] * 2,
        scratch_shapes=[pltpu.VMEM((PREP_TILE + HALO, w3), F32)],
        compiler_params=_cparams(("parallel", "parallel")),
        name="dn_prep",
    )(p, p, p, conv_w, rope_c, rope_s, ab, alog_row, dtb_row)


CH4 = DN_HEADS * DN_CHUNK


DN_CPS = 4
DN_PAIRS = DN_HEADS // 2
PAIR = 2 * DN_CHUNK


def _stack_pair(x, p):
    return jnp.concatenate([x[:, (2 * p + h) * LANES:(2 * p + h + 1) * LANES] for h in range(2)], axis=0)


def _dn_local_kernel(q_ref, k_ref, v_ref, gc_ref, bc_ref, gr_ref, wq_ref, u_ref, a_ref, kgt_ref):
    r = lax.broadcasted_iota(jnp.int32, (PAIR, PAIR), 0)
    c = lax.broadcasted_iota(jnp.int32, (PAIR, PAIR), 1)
    same = (r // DN_CHUNK) == (c // DN_CHUNK)
    chains = []
    for cc in range(DN_CPS):
        toks = slice(cc * DN_CHUNK, (cc + 1) * DN_CHUNK)
        gc = gc_ref[0, toks, :]
        bc = bc_ref[0, toks, :]
        for p in range(DN_PAIRS):
            rows = slice(p * PAIR, (p + 1) * PAIR)
            kst = _stack_pair(k_ref[0, toks, :], p)
            qst = _stack_pair(q_ref[0, toks, :], p)
            vst = _stack_pair(v_ref[0, toks, :], p)
            kq = _dot_nt(jnp.concatenate([kst, qst], axis=0), kst)
            kk = kq[:PAIR]
            qk = kq[PAIR:]
            kf = kst.astype(F32)
            for d in range(2):
                later = (r >= c) if d == 0 else (r <= c)
                incl = same & later
                strict = incl & (r != c)
                lane0 = d * DN_HEADS + 2 * p
                col = lambda a: jnp.concatenate([a[:, lane0 + h:lane0 + h + 1] for h in range(2)], axis=0)
                g_col = col(gc)
                b_col = col(bc)
                last = DN_CHUNK - 1 if d == 0 else 0
                gl_col = jnp.concatenate(
                    [jnp.broadcast_to(gc[last:last + 1, lane0 + h:lane0 + h + 1], (DN_CHUNK, 1))
                     for h in range(2)], axis=0)
                g_row = gr_ref[0, cc, d:d + 1, rows]
                decay = jnp.exp(jnp.where(incl, g_col - g_row, NEG_INF))
                pw = jnp.where(strict, -(kk * decay * b_col), 0.0)
                z = jnp.concatenate([vst.astype(F32) * b_col, kf * (b_col * jnp.exp(g_col))], axis=1)
                chains.append([pw, z, (cc, d, rows)])
                wq_ref[0, cc, d, CH4 + p * PAIR:CH4 + (p + 1) * PAIR, :] = (
                    qst.astype(F32) * jnp.exp(g_col)).astype(BF16)
                a_ref[0, cc, d, p] = (qk * decay).astype(BF16)
                kgt_ref[0, cc, d, :, rows] = (kf * jnp.exp(gl_col - g_col)).T.astype(BF16)

    for step in range(6):
        for chain in chains:
            pw, z, _ = chain
            if step:
                pwb = pw.astype(BF16)
                pw = _dot(pwb, pwb)
            chain[0] = pw
            chain[1] = z + _dot(pw.astype(BF16), z.astype(BF16))
    for _, z, (cc, d, rows) in chains:
        wq_ref[0, cc, d, rows, :] = z[:, LANES:].astype(BF16)
        u_ref[0, cc, d, rows, :] = z[:, :LANES]


def _dn_local(qd, kd, vd, gc, bc, g_rows):
    b, ta, _ = qd.shape
    nc = ta // DN_CHUNK
    tok = lambda w: pl.BlockSpec((1, DN_CPS * DN_CHUNK, w), lambda bb, n: (bb, n, 0))
    out = lambda *tail: pl.BlockSpec((1, DN_CPS, 2) + tail, lambda bb, n: (bb, n, 0) + (0,) * len(tail))
    return pl.pallas_call(
        _dn_local_kernel,
        grid=(b, nc // DN_CPS),
        in_specs=[tok(DN_WIDTH), tok(DN_WIDTH), tok(DN_WIDTH), tok(LANES), tok(LANES),
                  pl.BlockSpec((1, DN_CPS, 2, CH4), lambda bb, n: (bb, n, 0, 0))],
        out_specs=[out(2 * CH4, LANES), out(CH4, LANES), out(DN_PAIRS, PAIR, PAIR), out(LANES, CH4)],
        out_shape=[jax.ShapeDtypeStruct((b, nc, 2, 2 * CH4, LANES), BF16),
                   jax.ShapeDtypeStruct((b, nc, 2, CH4, LANES), F32),
                   jax.ShapeDtypeStruct((b, nc, 2, DN_PAIRS, PAIR, PAIR), BF16),
                   jax.ShapeDtypeStruct((b, nc, 2, LANES, CH4), BF16)],
        compiler_params=_cparams(("parallel", "parallel")),
        name="dn_local",
    )(qd, kd, vd, gc, bc, g_rows)


def _dn_scan_kernel(wq_f, u_f, a_f, kgt_f, gl_f, wq_b, u_b, a_b, kgt_b, gl_b, of_ref, ob_ref, state):
    n = pl.program_id(0)

    @pl.when(n == 0)
    def _():
        state[...] = jnp.zeros_like(state)

    r = lax.broadcasted_iota(jnp.int32, (CH4, DN_WIDTH), 0)
    c = lax.broadcasted_iota(jnp.int32, (CH4, DN_WIDTH), 1)
    own = (r // DN_CHUNK) == (c // LANES)
    pick = lambda m: jnp.concatenate(
        [m[h * DN_CHUNK:(h + 1) * DN_CHUNK, h * LANES:(h + 1) * LANES] for h in range(DN_HEADS)], axis=0)
    dirs = ((wq_f, u_f, a_f, kgt_f, gl_f, of_ref), (wq_b, u_b, a_b, kgt_b, gl_b, ob_ref))
    chains = [(bb, d) + refs for bb in range(state.shape[0]) for d, refs in enumerate(dirs)]
    s_alls = [state[bb, d] for bb, d, *_ in chains]
    wss = [_dot(wq[bb, 0, 0], s.astype(BF16)) for (bb, d, wq, *_), s in zip(chains, s_alls)]
    v_nbs = [(u[bb, 0, 0] - pick(ws[:CH4])).astype(BF16)
             for (bb, d, wq, u, *_), ws in zip(chains, wss)]
    for (bb, d, wq, u, a, kgt, gl, o_ref), ws, v_nb in zip(chains, wss, v_nbs):
        o = pick(ws[CH4:]) + jnp.concatenate(
            [_dot(a[bb, 0, 0, p], v_nb[p * PAIR:(p + 1) * PAIR]) for p in range(DN_PAIRS)], axis=0)
        for h in range(DN_HEADS):
            o_ref[bb, :, h * LANES:(h + 1) * LANES] = o[h * DN_CHUNK:(h + 1) * DN_CHUNK]
    for (bb, d, wq, u, a, kgt, gl, o_ref), s_all, v_nb in zip(chains, s_alls, v_nbs):
        v_bd = jnp.where(own, jnp.concatenate([v_nb] * DN_HEADS, axis=1), jnp.zeros((), BF16))
        state[bb, d] = s_all * jnp.exp(gl[bb, 0, 0]) + _dot(kgt[bb, 0, 0], v_bd)


def _dn_scan(wq, u, a, kgt, gl_wide, *, s_len):
    b, nc = wq.shape[:2]
    n_lat = s_len // DN_CHUNK
    n_ctx = nc - n_lat
    fwd = lambda n: jnp.where(n < n_ctx, n_lat + n, n - n_ctx)
    bwd = lambda n: nc - 1 - n
    specs = []
    for d, order in ((0, fwd), (1, bwd)):
        blk = lambda arr, d=d, order=order: pl.BlockSpec(
            (b, 1, 1) + arr.shape[3:], lambda n: (0, order(n), d) + (0,) * (arr.ndim - 3))
        specs += [blk(wq), blk(u), blk(a), blk(kgt), blk(gl_wide)]
    o_spec = lambda order: pl.BlockSpec((b, DN_CHUNK, DN_WIDTH), lambda n: (0, order(n), 0))
    return pl.pallas_call(
        _dn_scan_kernel,
        grid=(nc,),
        in_specs=specs,
        out_specs=[o_spec(fwd), o_spec(bwd)],
        out_shape=[jax.ShapeDtypeStruct((b, nc * DN_CHUNK, DN_WIDTH), F32)] * 2,
        scratch_shapes=[pltpu.VMEM((b, 2, DN_HEAD_DIM, DN_WIDTH), F32)],
        compiler_params=_cparams(("arbitrary",)),
        name="dn_scan",
    )(wq, u, a, kgt, gl_wide, wq, u, a, kgt, gl_wide)


def _merge_kernel(ona_ref, of_ref, ob_ref, z_ref, sc_ref, schp_ref, schn_ref, gate_ref, x_ref,
                  g1l_ref, g1c_ref, sh2l_ref, sh2c_ref, sc2l_ref, sc2c_ref, ng_ref, og_ref, scw_ref,
                  wna_ref, wdn_ref, wsc_ref, wout_ref, wrh_ref, wrl_ref, rb_ref,
                  x1_ref, h2_ref, comb_ref, combt_ref, cpad, *, tm, s_len, ta):
    i = pl.program_id(1)
    d = x_ref.shape[-1]
    o = of_ref[0] + ob_ref[0]
    z = z_ref[0].astype(F32)
    parts = []
    for h in range(DN_HEADS):
        oh = o[:, h * LANES:(h + 1) * LANES]
        parts.append(oh * lax.rsqrt(jnp.mean(oh * oh, axis=-1, keepdims=True) + EPS))
    o_dn = jnp.concatenate(parts, axis=1) * og_ref[...] * _silu(z)

    def cu(ref_val):
        v = ref_val.astype(F32)
        return v[:, 2 * SC_WIDTH:] * v[:, :SC_WIDTH]
    scp = sc_ref[0].astype(F32)
    half = HALO // 2
    cpad[0:half, :] = cu(schp_ref[0])[half:, :]
    cpad[half:half + tm, :] = scp[:, 2 * SC_WIDTH:] * scp[:, :SC_WIDTH]
    cpad[half + tm:, :] = cu(schn_ref[0])[:half, :]
    gt = i * tm + lax.broadcasted_iota(jnp.int32, (tm, 1), 0)
    has_prev = (gt != 0) & (gt != s_len)
    has_next = (gt != s_len - 1) & (gt != ta - 1)
    conv = (scw_ref[0:1, :] * jnp.where(has_prev, cpad[pl.ds(half - 1, tm), :], 0.0)
            + scw_ref[1:2, :] * cpad[pl.ds(half, tm), :]
            + scw_ref[2:3, :] * jnp.where(has_next, cpad[pl.ds(half + 1, tm), :], 0.0))
    o_sc = scp[:, SC_WIDTH:2 * SC_WIDTH] * conv

    merged = (gate_ref[0, :, :d].astype(F32) * _dot(ona_ref[0], wna_ref[...])
              + gate_ref[0, :, d:2 * d].astype(F32) * _dot(o_dn.astype(BF16), wdn_ref[...])
              + gate_ref[0, :, 2 * d:].astype(F32) * _dot(o_sc.astype(BF16), wsc_ref[...]))
    y = _dot(merged.astype(BF16), wout_ref[...])
    x1 = x_ref[0] + _row_select(i, tm, s_len, g1l_ref[0], g1c_ref[...]) * y
    x1_ref[0] = x1

    hn = x1 * lax.rsqrt(jnp.mean(x1 * x1, axis=-1, keepdims=True) + EPS) * ng_ref[...]
    h2 = (hn * (1.0 + _row_select(i, tm, s_len, sc2l_ref[0], sc2c_ref[...]))
          + _row_select(i, tm, s_len, sh2l_ref[0], sh2c_ref[...]))
    h2_ref[0] = h2.astype(BF16)

    h2_hi, h2_lo = _split_bf16(h2)
    logits = _dot3(h2_hi, h2_lo, wrh_ref[...], wrl_ref[...])
    scores = jax.nn.sigmoid(logits.T)
    biased = scores + rb_ref[...]
    rows = [biased[e:e + 1, :] for e in range(N_EXPERTS)]
    gsc = []
    for g in range(N_GROUPS):
        a, b_, c_, d_ = rows[4 * g:4 * g + 4]
        gsc.append(jnp.maximum(jnp.maximum(jnp.maximum(a + b_, a + c_), jnp.maximum(a + d_, b_ + c_)),
                               jnp.maximum(b_ + d_, c_ + d_)))
    best = gsc[0]
    grp = jnp.zeros_like(best, dtype=jnp.int32)
    for g in range(1, N_GROUPS):
        upd = gsc[g] > best
        best = jnp.where(upd, gsc[g], best)
        grp = jnp.where(upd, g, grp)
    eidx = lax.broadcasted_iota(jnp.int32, biased.shape, 0)
    masked = jnp.where((eidx // EXPERTS_PER_GROUP) == grp, biased, -jnp.inf)
    big = jnp.int32(LANES)
    m1 = masked.max(axis=0, keepdims=True)
    i1 = jnp.where(masked == m1, eidx, big).min(axis=0, keepdims=True)
    masked2 = jnp.where(eidx == i1, -jnp.inf, masked)
    m2 = masked2.max(axis=0, keepdims=True)
    i2 = jnp.where(masked2 == m2, eidx, big).min(axis=0, keepdims=True)
    sel1 = eidx == i1
    sel2 = eidx == i2
    s1 = jnp.sum(jnp.where(sel1, scores, 0.0), axis=0, keepdims=True)
    s2 = jnp.sum(jnp.where(sel2, scores, 0.0), axis=0, keepdims=True)
    tot = s1 + s2
    comb_t = jnp.where(sel1, s1 / tot, 0.0) + jnp.where(sel2, s2 / tot, 0.0)
    meta_t = jnp.where(eidx == N_EXPERTS, grp.astype(F32), comb_t)
    combt_ref[0] = meta_t
    comb_ref[0] = meta_t.T


def _merge(o_na, o_f, o_b, z, sc, gates, xs, mods_l, mods_c, norm_g, out_g, sc_w, wna, wdn, wsc, wout, wr, rb,
           *, s_len, tm):
    b, ta, d = xs.shape
    col_z = col_sc = 0
    hb = tm // HALO
    n_hb = ta // HALO
    tok = lambda w, col=0: pl.BlockSpec((1, tm, w), lambda bb, i: (bb, i, col))
    vec = lambda w: pl.BlockSpec((1, w), lambda bb, i: (0, 0))
    bvec = lambda: pl.BlockSpec((1, 1, d), lambda bb, i: (bb, 0, 0))
    full = lambda a: pl.BlockSpec(a.shape, lambda bb, i: (0,) * a.ndim)
    g1l, sh2l, sc2l = mods_l
    g1c, sh2c, sc2c = mods_c
    w3 = 3 * SC_WIDTH
    wr_hi = wr.astype(BF16)
    wr_lo = (wr - wr_hi.astype(F32)).astype(BF16)
    return pl.pallas_call(
        functools.partial(_merge_kernel, tm=tm, s_len=s_len, ta=ta),
        grid=(b, ta // tm),
        in_specs=[tok(NA_WIDTH), tok(DN_WIDTH), tok(DN_WIDTH), tok(DN_WIDTH, col_z), tok(w3, col_sc),
                  pl.BlockSpec((1, HALO, w3), lambda bb, i: (bb, jnp.maximum(i * hb - 1, 0), col_sc)),
                  pl.BlockSpec((1, HALO, w3), lambda bb, i: (bb, jnp.minimum((i + 1) * hb, n_hb - 1), col_sc)),
                  tok(3 * d, 0), tok(d),
                  bvec(), vec(d), bvec(), vec(d), bvec(), vec(d), vec(d), vec(DN_WIDTH),
                  full(sc_w), full(wna), full(wdn), full(wsc), full(wout), full(wr_hi), full(wr_lo), full(rb)],
        out_specs=[tok(d), tok(d), tok(LANES), pl.BlockSpec((1, LANES, tm), lambda bb, i: (bb, 0, i))],
        out_shape=[jax.ShapeDtypeStruct((b, ta, d), F32),
                   jax.ShapeDtypeStruct((b, ta, d), BF16),
                   jax.ShapeDtypeStruct((b, ta, LANES), F32),
                   jax.ShapeDtypeStruct((b, LANES, ta), F32)],
        scratch_shapes=[pltpu.VMEM((tm + HALO, SC_WIDTH), F32)],
        compiler_params=_cparams(("parallel", "parallel")),
        name="merge_router",
    )(o_na, o_f, o_b, z, sc, sc, sc, gates, xs, g1l, g1c, sh2l, sh2c, sc2l, sc2c, norm_g, out_g,
      sc_w, wna, wdn, wsc, wout, wr_hi, wr_lo, rb)


MOE_CAP = 256


def _moe_kernel(h_ref, comb_ref, combt_ref, lt_ref, gt_ref, wg_ref, wu_ref, wd_ref, x_ref, g2l_ref, g2c_ref,
                o_ref, acc, rank_r, rank_c, *, tm, s_len):
    i = pl.program_id(1)
    g = pl.program_id(2)

    comb = comb_ref[0]
    lane = lax.broadcasted_iota(jnp.int32, (tm, LANES), 1)
    grp_row = combt_ref[0, N_EXPERTS:N_EXPERTS + 1, :]
    grp_col = jnp.sum(jnp.where(lane == N_EXPERTS, comb, 0.0), axis=-1, keepdims=True)

    @pl.when(g == 0)
    def _():
        acc[...] = jnp.zeros_like(acc)
        sub = lax.broadcasted_iota(jnp.int32, (8, tm), 0).astype(F32)
        before = _dot(jnp.where(sub == grp_row, 1.0, 0.0).astype(BF16), lt_ref[...])
        rank_r[...] = jnp.broadcast_to(
            jnp.sum(jnp.where(sub == grp_row, before, 0.0), axis=0, keepdims=True), (8, tm))
        lane_f = lane.astype(F32)
        before = _dot(gt_ref[...], jnp.where(lane_f == grp_col, 1.0, 0.0).astype(BF16))
        rank_c[...] = jnp.broadcast_to(
            jnp.sum(jnp.where(lane_f == grp_col, before, 0.0), axis=-1, keepdims=True), (tm, LANES))

    gf = g.astype(F32)
    member_row = grp_row == gf
    member_col = grp_col == gf
    rank_row = rank_r[0:1, :]
    rank_col = rank_c[:, 0:1]
    count = jnp.sum(jnp.where(member_row, 1.0, 0.0))
    comb_hi, comb_lo = _split_bf16(comb)
    d = x_ref.shape[-1]

    for k in range(tm // MOE_CAP):
        @pl.when(count > k * MOE_CAP)
        def _(k=k):
            slot_c = (lax.broadcasted_iota(jnp.int32, (MOE_CAP, 1), 0) + k * MOE_CAP).astype(F32)
            gather = jnp.where(member_row & (rank_row == slot_c), 1.0, 0.0).astype(BF16)
            hs = _dot(gather, h_ref[0]).astype(BF16)
            cws = _dot(gather, comb_hi) + _dot(gather, comb_lo)
            lane_c = lax.broadcasted_iota(jnp.int32, (MOE_CAP, LANES), 1)
            ys = jnp.zeros((MOE_CAP, d), F32)
            for e4 in range(EXPERTS_PER_GROUP):
                cw = jnp.sum(jnp.where(lane_c == g * EXPERTS_PER_GROUP + e4, cws, 0.0), axis=-1, keepdims=True)
                hidden = _silu(_dot(hs, wg_ref[0, e4])) * _dot(hs, wu_ref[0, e4]) * cw
                ys = ys + _dot(hidden.astype(BF16), wd_ref[0, e4])
            slot_r = (lax.broadcasted_iota(jnp.int32, (1, MOE_CAP), 1) + k * MOE_CAP).astype(F32)
            scatter = jnp.where(member_col & (rank_col == slot_r), 1.0, 0.0).astype(BF16)
            ys_hi, ys_lo = _split_bf16(ys)
            acc[...] += _dot(scatter, ys_hi) + _dot(scatter, ys_lo)

    @pl.when(g == N_GROUPS - 1)
    def _():
        o_ref[0] = x_ref[0] + _row_select(i, tm, s_len, g2l_ref[0], g2c_ref[...]) * acc[...]


def _moe(h2, comb, comb_t, wg, wu, wd, x1, g2l, g2c, *, layer, s_len, tm):
    b, ta, d = x1.shape
    tok = lambda w: pl.BlockSpec((1, tm, w), lambda bb, i, g: (bb, i, 0))
    grp = lambda a: pl.BlockSpec((1, EXPERTS_PER_GROUP) + a.shape[2:], lambda bb, i, g: (layer, g, 0, 0))
    tri = lambda: pl.BlockSpec((tm, tm), lambda bb, i, g: (0, 0))
    earlier = np.arange(tm)[:, None] < np.arange(tm)[None, :]
    lt = jnp.asarray(earlier, BF16)
    gt = jnp.asarray(earlier.T, BF16)
    return pl.pallas_call(
        functools.partial(_moe_kernel, tm=tm, s_len=s_len),
        grid=(b, ta // tm, N_GROUPS),
        in_specs=[tok(d), tok(LANES),
                  pl.BlockSpec((1, LANES, tm), lambda bb, i, g: (bb, 0, i)),
                  tri(), tri(),
                  grp(wg), grp(wu), grp(wd),
                  tok(d),
                  pl.BlockSpec((1, 1, d), lambda bb, i, g: (bb, 0, 0)),
                  pl.BlockSpec((1, d), lambda bb, i, g: (0, 0))],
        out_specs=tok(d),
        out_shape=jax.ShapeDtypeStruct((b, ta, d), F32),
        scratch_shapes=[pltpu.VMEM((tm, d), F32), pltpu.VMEM((8, tm), F32), pltpu.VMEM((tm, LANES), F32)],
        compiler_params=_cparams(("parallel", "parallel", "arbitrary")),
        name="moe_grouped",
    )(h2, comb, comb_t, lt, gt, wg, wu, wd, x1, g2l, g2c)


def _rope_tables(s_len, ctx_len):
    t = np.arange(s_len)
    n_freq = DN_HEAD_DIM // 4
    inv_freq = (np.float32(ROPE_BASE) ** (-np.arange(n_freq, dtype=np.float32) / n_freq)).astype(np.float32)
    pos = np.stack([t // GRID_W, t % GRID_W], axis=-1).astype(np.float32)
    ang = (pos[..., None] * inv_freq).astype(np.float32)
    cos, sin = np.cos(ang.astype(np.float64)), np.sin(ang.astype(np.float64))
    rc = np.concatenate([cos[:, 0], cos[:, 0], cos[:, 1], cos[:, 1]], axis=-1)
    rs = np.concatenate([-sin[:, 0], sin[:, 0], -sin[:, 1], sin[:, 1]], axis=-1)
    rc = np.concatenate([rc, np.ones((ctx_len, LANES))], axis=0)
    rs = np.concatenate([rs, np.zeros((ctx_len, LANES))], axis=0)
    return jnp.asarray(rc, F32), jnp.asarray(rs, F32)


def _pick_tile(ta, candidates):
    for t in candidates:
        if ta % t == 0:
            return t
    raise ValueError(f"no row tile for {ta} tokens")


def kernel(x, c, ctx, c_ctx, w_ada, b_ada, norm_mix_g, norm_ffn_g, w_in, na_q_norm_g, na_k_norm_g, na_rpb, dn_conv_w, dn_a_log, dn_dt_bias, dn_out_norm_g, sc_conv_w, w_branch_na, w_branch_dn, w_branch_sc, w_out, w_router, router_bias, moe_w_gate, moe_w_up, moe_w_down):
    b, s_len, d = x.shape
    ctx_len = ctx.shape[1]
    depth = w_ada.shape[0]
    ta = s_len + ctx_len
    assert ctx_len == NA_BLK and s_len % (WIN_R * GRID_W) == 0 and s_len // NA_BLK >= 4
    assert ctx_len % PREP_TILE == 0 and b + 1 <= 8
    tm_proj = _pick_tile(ta, (768, 512, 256))
    tm_merge = _pick_tile(ta, (384, 256))
    tm_moe = _pick_tile(ta, (768, 512, 256))

    xs = jnp.concatenate([x, ctx], axis=1)
    c_rows = jnp.zeros((8, d), F32).at[:b].set(c).at[b].set(c_ctx)
    mod = _modulation(c_rows, w_ada, b_ada)

    sizes = (3 * NA_WIDTH, 3 * DN_WIDTH, DN_WIDTH, 4 * DN_HEADS, 3 * SC_WIDTH, 3 * d)
    cuts = [int(v) for v in np.cumsum((0,) + sizes)]
    seg = lambda l, k: lax.slice(w_in, (l, 0, cuts[k]), (l + 1, d, cuts[k + 1]))[0]
    wg_all, wu_all, wd_all = moe_w_gate.astype(BF16), moe_w_up.astype(BF16), moe_w_down.astype(BF16)

    rope_c, rope_s = _rope_tables(s_len, ctx_len)
    bd = jnp.asarray(np.kron(np.eye(NA_HEADS), np.ones((NA_HEAD_DIM, NA_HEAD_DIM))), BF16)
    nc = ta // DN_CHUNK
    wr = jnp.zeros((d, LANES), F32).at[:, :N_EXPERTS].set(w_router)
    rb = jnp.zeros((LANES, 1), F32).at[:N_EXPERTS, 0].set(router_bias)

    for l in range(depth):
        m = mod[l].reshape(8, N_MOD, d)
        lat = lambda k: m[:b, k][:, None, :]
        cx = lambda k: m[b:b + 1, k]
        w_ab = jnp.zeros((d, LANES), F32).at[:, :4 * DN_HEADS].set(seg(l, 3))
        gqk = jnp.concatenate([jnp.tile(na_q_norm_g[l], NA_HEADS) * (NA_HEAD_DIM ** -0.5),
                               jnp.tile(na_k_norm_g[l], NA_HEADS)])[None, :]
        gates, na_q, na_k, na_v, dn_qkv, sc_p, dn_z, ab = _projection(
            xs, (lat(0), lat(1)), (cx(0), cx(1)), norm_mix_g[l][None, :],
            seg(l, 5).astype(BF16), seg(l, 0).astype(BF16), seg(l, 1).astype(BF16),
            seg(l, 4).astype(BF16), seg(l, 2).astype(BF16), w_ab, gqk, bd, s_len=s_len, tm=tm_proj)

        o_na = _neighbourhood_attention(na_q, na_k, na_v, _na_bias_table(na_rpb[l]), s_len=s_len)

        alog_row = jnp.zeros((1, LANES), F32).at[0, :2 * DN_HEADS].set(dn_a_log[l].reshape(-1))
        dtb_row = jnp.zeros((1, LANES), F32).at[0, :2 * DN_HEADS].set(dn_dt_bias[l].reshape(-1))
        qd, kd, vd, gc, bc = _dn_prep(dn_qkv, ab, dn_conv_w[l], rope_c, rope_s, alog_row, dtb_row,
                                      s_len=s_len)
        g8 = gc[..., :2 * DN_HEADS].reshape(b, nc, DN_CHUNK, 2, DN_HEADS)
        g_rows = jnp.transpose(g8, (0, 1, 3, 4, 2)).reshape(b, nc, 2, CH4)
        g_last = jnp.stack([g8[:, :, DN_CHUNK - 1, 0], g8[:, :, 0, 1]], axis=2)
        gl_wide = jnp.repeat(g_last, LANES, axis=-1)[:, :, :, None, :]
        wq, u, a, kgt = _dn_local(qd, kd, vd, gc, bc, g_rows)
        o_f, o_b = _dn_scan(wq, u, a, kgt, gl_wide, s_len=s_len)

        x1, h2, comb, comb_t = _merge(
            o_na, o_f, o_b, dn_z, sc_p, gates, xs, (lat(2), lat(3), lat(4)), (cx(2), cx(3), cx(4)),
            norm_ffn_g[l][None, :], jnp.tile(dn_out_norm_g[l], DN_HEADS)[None, :], sc_conv_w[l],
            w_branch_na[l].astype(BF16), w_branch_dn[l].astype(BF16), w_branch_sc[l].astype(BF16),
            w_out[l].astype(BF16), wr, rb, s_len=s_len, tm=tm_merge)

        xs = _moe(h2, comb, comb_t, wg_all, wu_all, wd_all, x1, lat(5), cx(5), layer=l, s_len=s_len, tm=tm_moe)

    return xs[:, :s_len]
```

```python
import functools
import math

import numpy as np
import jax
import jax.numpy as jnp
from jax import lax
from jax.experimental import pallas as pl
from jax.experimental.pallas import tpu as pltpu

F32 = jnp.float32
BF16 = jnp.bfloat16
HIGHEST = lax.Precision.HIGHEST

GRID_W = 64
NA_HEADS = 8
NA_HEAD_DIM = 64
NA_WIDTH = NA_HEADS * NA_HEAD_DIM
WIN_R = 8
WIN_C = 16
DN_HEADS = 4
DN_HEAD_DIM = 128
DN_WIDTH = DN_HEADS * DN_HEAD_DIM
DN_CONV = 5
DN_CHUNK = 64
ROPE_BASE = 10000.0
SC_WIDTH = 512
SC_CONV = 3
N_EXPERTS = 16
N_GROUPS = 4
EXPERTS_PER_GROUP = N_EXPERTS // N_GROUPS
D_EXPERT = 512
N_MOD = 6
EPS = 1e-6
NEG_INF = -1e30

LANES = 128
HALO = 16
PREP_TILE = 256
VMEM_LIMIT = 56 * 1024 * 1024

N_PROJ_TILE = 512


def _cparams(sem):
    return pltpu.CompilerParams(dimension_semantics=sem, vmem_limit_bytes=VMEM_LIMIT)


def _dot(a, b):
    return jnp.dot(a, b, preferred_element_type=F32)


def _dot_nt(a, b):
    return lax.dot_general(a, b, (((1,), (1,)), ((), ())), preferred_element_type=F32)


def _silu(x):
    return x * jax.nn.sigmoid(x)


def _row_select(i, tm, s_len, lat, ctx):
    row = i * tm + lax.broadcasted_iota(jnp.int32, (tm, 1), 0)
    return jnp.where(row >= s_len, ctx, lat)


def _mod_kernel(c_ref, w_ref, b_ref, o_ref):
    c_hi, c_lo = _split_bf16(_silu(c_ref[...]))
    w_hi, w_lo = _split_bf16(w_ref[0])
    o_ref[0] = _dot3(c_hi, c_lo, w_hi, w_lo) + b_ref[0]


def _modulation(c_rows, w_ada, b_ada):
    depth, d, n = w_ada.shape
    tn = 512
    return pl.pallas_call(
        _mod_kernel,
        grid=(depth, n // tn),
        in_specs=[pl.BlockSpec((8, d), lambda l, j: (0, 0)),
                  pl.BlockSpec((1, d, tn), lambda l, j: (l, 0, j)),
                  pl.BlockSpec((1, 1, tn), lambda l, j: (l, 0, j))],
        out_specs=pl.BlockSpec((1, 8, tn), lambda l, j: (l, 0, j)),
        out_shape=jax.ShapeDtypeStruct((depth, 8, n), F32),
        compiler_params=_cparams(("parallel", "parallel")),
        name="adaln_mod",
    )(c_rows, w_ada, b_ada.reshape(depth, 1, n))


def _split_bf16(a):
    hi = a.astype(BF16)
    return hi, (a - hi.astype(F32)).astype(BF16)


def _dot3(a_hi, a_lo, b_hi, b_lo):
    return _dot(a_hi, b_hi) + _dot(a_lo, b_hi) + _dot(a_hi, b_lo)


def _modulated_norm(x_ref, shl_ref, scl_ref, shc_ref, scc_ref, g_ref, *, tm, s_len):
    i = pl.program_id(1)
    x = x_ref[0]
    y = x * lax.rsqrt(jnp.mean(x * x, axis=-1, keepdims=True) + EPS) * g_ref[...]
    scale = _row_select(i, tm, s_len, scl_ref[0], scc_ref[...])
    shift = _row_select(i, tm, s_len, shl_ref[0], shc_ref[...])
    return y * (1.0 + scale) + shift


def _proj_gate_kernel(x_ref, shl_ref, scl_ref, shc_ref, scc_ref, g_ref, w_ref, gate_ref, *, tm, s_len):
    h = _modulated_norm(x_ref, shl_ref, scl_ref, shc_ref, scc_ref, g_ref, tm=tm, s_len=s_len).astype(BF16)
    for c0 in range(0, w_ref.shape[1], N_PROJ_TILE):
        cols = slice(c0, c0 + N_PROJ_TILE)
        gate_ref[0, :, cols] = jax.nn.sigmoid(_dot(h, w_ref[:, cols])).astype(BF16)


def _proj_mix_kernel(x_ref, shl_ref, scl_ref, shc_ref, scc_ref, g_ref, wna_ref, wdn_ref, wsc_ref, wz_ref,
                     wabh_ref, wabl_ref, gqk_ref, bd_ref,
                     q_ref, k_ref, v_ref, dn_ref, sc_ref, z_ref, ab_ref, *, tm, s_len):
    h = _modulated_norm(x_ref, shl_ref, scl_ref, shc_ref, scc_ref, g_ref, tm=tm, s_len=s_len)
    h_hi, h_lo = _split_bf16(h)
    ab_ref[0] = _dot3(h_hi, h_lo, wabh_ref[...], wabl_ref[...])
    for part, out in enumerate((q_ref, k_ref)):
        cols = slice(part * NA_WIDTH, (part + 1) * NA_WIDTH)
        a = _dot(h_hi, wna_ref[:, cols])
        ss = _dot((a * a).astype(BF16), bd_ref[...])
        out[0] = (a * lax.rsqrt(ss * (1.0 / NA_HEAD_DIM) + EPS) * gqk_ref[:, cols]).astype(BF16)
    v_ref[0] = _dot(h_hi, wna_ref[:, 2 * NA_WIDTH:]).astype(BF16)
    for w_ref, out in ((wdn_ref, dn_ref), (wsc_ref, sc_ref), (wz_ref, z_ref)):
        for c0 in range(0, w_ref.shape[1], N_PROJ_TILE):
            cols = slice(c0, c0 + N_PROJ_TILE)
            out[0, :, cols] = _dot(h_hi, w_ref[:, cols]).astype(BF16)


def _projection(xs, mod_l, mod_c, norm_g, w_gate, w_na, w_dn, w_sc, w_z, w_ab, gqk, bd, *, s_len, tm):
    b, ta, d = xs.shape
    shl, scl = mod_l
    shc, scc = mod_c
    wab_hi, wab_lo = _split_bf16(w_ab)
    tok = lambda w: pl.BlockSpec((1, tm, w), lambda bb, i: (bb, i, 0))
    bvec = lambda: pl.BlockSpec((1, 1, d), lambda bb, i: (bb, 0, 0))
    full = lambda a: pl.BlockSpec(a.shape, lambda bb, i: (0,) * a.ndim)
    head = [tok(d), bvec(), bvec(), full(shc), full(scc), full(norm_g)]
    gates = pl.pallas_call(
        functools.partial(_proj_gate_kernel, tm=tm, s_len=s_len),
        grid=(b, ta // tm),
        in_specs=head + [full(w_gate)],
        out_specs=tok(w_gate.shape[1]),
        out_shape=jax.ShapeDtypeStruct((b, ta, w_gate.shape[1]), BF16),
        compiler_params=_cparams(("parallel", "parallel")),
        name="proj_gates",
    )(xs, shl, scl, shc, scc, norm_g, w_gate)
    consts = (w_na, w_dn, w_sc, w_z, wab_hi, wab_lo, gqk, bd)
    widths = (NA_WIDTH, NA_WIDTH, NA_WIDTH, w_dn.shape[1], w_sc.shape[1], w_z.shape[1])
    outs = pl.pallas_call(
        functools.partial(_proj_mix_kernel, tm=tm, s_len=s_len),
        grid=(b, ta // tm),
        in_specs=head + [full(a) for a in consts],
        out_specs=[tok(w) for w in widths] + [tok(LANES)],
        out_shape=[jax.ShapeDtypeStruct((b, ta, w), BF16) for w in widths]
        + [jax.ShapeDtypeStruct((b, ta, LANES), F32)],
        compiler_params=_cparams(("parallel", "parallel")),
        name="proj_mixers",
    )(xs, shl, scl, shc, scc, norm_g, *consts)
    return (gates,) + tuple(outs)


NA_QROWS = 4
NA_BLK = NA_QROWS * GRID_W
WIN_TOK = WIN_R * GRID_W


def _stack_two_heads(q2):
    lane = lax.broadcasted_iota(jnp.int32, q2.shape, 1)
    zero = jnp.zeros_like(q2)
    return jnp.concatenate([jnp.where(lane < NA_HEAD_DIM, q2, zero),
                            jnp.where(lane >= NA_HEAD_DIM, q2, zero)], axis=0)


def _unstack_two_heads(o2):
    m = o2.shape[0] // 2
    lane = lax.broadcasted_iota(jnp.int32, (m, o2.shape[1]), 1)
    return jnp.where(lane < NA_HEAD_DIM, o2[:m], o2[m:])


def _na_kernel(q_ref, kp_ref, kc_ref, kn_ref, vp_ref, vc_ref, vn_ref, kx_ref, vx_ref, bias_ref, o_ref,
               kbuf, vbuf, *, n_rows):
    i = pl.program_id(1)
    n_lat = n_rows // NA_QROWS

    @pl.when(i < n_lat)
    def _():
        for slot, (kr, vr) in enumerate(((kp_ref, vp_ref), (kc_ref, vc_ref), (kn_ref, vn_ref))):
            kbuf[slot * NA_BLK:(slot + 1) * NA_BLK, :] = kr[0]
            vbuf[slot * NA_BLK:(slot + 1) * NA_BLK, :] = vr[0]
        win = []
        for j in range(NA_QROWS):
            r = i * NA_QROWS + j
            start = jnp.clip(r - WIN_R // 2, 0, n_rows - WIN_R)
            win.append((pl.multiple_of((start - (i - 1) * NA_QROWS) * GRID_W, GRID_W), r - start))

        rows2 = 2 * GRID_W
        pairs = [slice(hp * LANES, (hp + 1) * LANES) for hp in range(NA_HEADS // 2)]
        logits = []
        for hp, lanes in enumerate(pairs):
            qs = jnp.concatenate([_stack_two_heads(q_ref[0, j * GRID_W:(j + 1) * GRID_W, lanes])
                                  for j in range(NA_QROWS)], axis=0)
            s_c = _dot_nt(qs, kx_ref[0, :, lanes])
            s_w = jnp.concatenate(
                [_dot_nt(qs[j * rows2:(j + 1) * rows2], kbuf[pl.ds(win[j][0], WIN_TOK), lanes])
                 + bias_ref[win[j][1], hp] for j in range(NA_QROWS)], axis=0)
            logits.append((s_w, s_c))
        probs = []
        for s_w, s_c in logits:
            m = jnp.maximum(s_w.max(axis=-1, keepdims=True), s_c.max(axis=-1, keepdims=True))
            p_w = jnp.exp(s_w - m)
            p_c = jnp.exp(s_c - m)
            denom = p_w.sum(axis=-1, keepdims=True) + p_c.sum(axis=-1, keepdims=True)
            probs.append((p_w.astype(BF16), p_c.astype(BF16), denom))
        for lanes, (p_wb, p_cb, denom) in zip(pairs, probs):
            o = _dot(p_cb, vx_ref[0, :, lanes]) + jnp.concatenate(
                [_dot(p_wb[j * rows2:(j + 1) * rows2], vbuf[pl.ds(win[j][0], WIN_TOK), lanes])
                 for j in range(NA_QROWS)], axis=0)
            o = o / denom
            for j in range(NA_QROWS):
                o_ref[0, j * GRID_W:(j + 1) * GRID_W, lanes] = _unstack_two_heads(
                    o[j * rows2:(j + 1) * rows2]).astype(BF16)

    @pl.when(i == n_lat)
    def _():
        for hp in range(NA_HEADS // 2):
            lanes = slice(hp * LANES, (hp + 1) * LANES)
            s = _dot_nt(_stack_two_heads(q_ref[0, :, lanes]), kx_ref[0, :, lanes])
            p = jnp.exp(s - s.max(axis=-1, keepdims=True))
            o = _dot(p.astype(BF16), vx_ref[0, :, lanes]) / p.sum(axis=-1, keepdims=True)
            o_ref[0, :, lanes] = _unstack_two_heads(o).astype(BF16)


def _neighbourhood_attention(q, k, v, bias_tab, *, s_len):
    b, ta, _ = q.shape
    n_rows = s_len // GRID_W
    n_lat = n_rows // NA_QROWS
    blk = lambda off: pl.BlockSpec((1, NA_BLK, NA_WIDTH), lambda bb, i: (bb, jnp.clip(i + off, 0, n_lat - 1), 0))
    cx = lambda: pl.BlockSpec((1, NA_BLK, NA_WIDTH), lambda bb, i: (bb, n_lat, 0))
    own = lambda: pl.BlockSpec((1, NA_BLK, NA_WIDTH), lambda bb, i: (bb, i, 0))
    return pl.pallas_call(
        functools.partial(_na_kernel, n_rows=n_rows),
        grid=(b, n_lat + 1),
        in_specs=[own(), blk(-1), blk(0), blk(1), blk(-1), blk(0), blk(1), cx(), cx(),
                  pl.BlockSpec(bias_tab.shape, lambda bb, i: (0, 0, 0, 0))],
        out_specs=own(),
        out_shape=jax.ShapeDtypeStruct((b, ta, NA_WIDTH), BF16),
        scratch_shapes=[pltpu.VMEM((3 * NA_BLK, NA_WIDTH), BF16),
                        pltpu.VMEM((3 * NA_BLK, NA_WIDTH), BF16)],
        compiler_params=_cparams(("parallel", "arbitrary")),
        name="na_attention",
    )(q, k, k, k, v, v, v, k, v, bias_tab)


def _na_bias_table(rpb):
    qc = np.arange(GRID_W)[:, None]
    kc = np.arange(GRID_W)[None, :]
    win_c0 = np.clip(qc - WIN_C // 2, 0, GRID_W - WIN_C)
    in_win = (kc >= win_c0) & (kc < win_c0 + WIN_C)
    rel_c = np.clip(kc - qc + WIN_C - 1, 0, 2 * WIN_C - 2)
    pick_c = (rel_c[..., None] == np.arange(2 * WIN_C - 1)).astype(np.float32)
    by_col = jnp.einsum('hab,qkb->haqk', rpb.astype(F32), pick_c, precision=HIGHEST)
    by_col = jnp.where(in_win[None, None], by_col, NEG_INF)
    tab = jnp.stack([by_col[:, WIN_R - 1 - v:2 * WIN_R - 1 - v] for v in range(WIN_R)], axis=0)
    tab = jnp.transpose(tab, (0, 1, 3, 2, 4))
    return tab.reshape(WIN_R, NA_HEADS // 2, 2 * GRID_W, WIN_R * GRID_W)


def _dn_prep_kernel(x_ref, hp_ref, hn_ref, w_ref, rc_ref, rs_ref, ab_ref, alog_ref, dtb_ref,
                    q_ref, k_ref, v_ref, g_ref, beta_ref, xpad, *, s_len, ta):
    i = pl.program_id(1)
    t0 = i * PREP_TILE
    has_prev = jnp.logical_and(t0 != 0, t0 != s_len).astype(F32)
    has_next = jnp.logical_and(t0 + PREP_TILE != s_len, t0 + PREP_TILE != ta).astype(F32)
    half = HALO // 2
    xpad[0:half, :] = hp_ref[0].astype(F32)[half:, :] * has_prev
    xpad[half:half + PREP_TILE, :] = x_ref[0].astype(F32)
    xpad[half + PREP_TILE:, :] = hn_ref[0].astype(F32)[:half, :] * has_next
    xp = xpad[...]
    n_pad = xp.shape[0]
    y = jnp.zeros((PREP_TILE, x_ref.shape[-1]), F32)
    for tap in range(DN_CONV):
        shifted = xp if tap == DN_CONV // 2 else pltpu.roll(xp, (DN_CONV // 2 - tap) % n_pad, 0)
        y = y + w_ref[tap:tap + 1, :] * shifted[half:half + PREP_TILE]
    y = _silu(y)

    lane = lax.broadcasted_iota(jnp.int32, (PREP_TILE, LANES), 1)
    first_half = (lane % (LANES // 2)) < (LANES // 4)
    rc = rc_ref[...]
    rs = rs_ref[...]

    def norm_rope(u, scale):
        u = u * (lax.rsqrt(jnp.sum(u * u, axis=-1, keepdims=True) + EPS) * scale)
        partner = jnp.where(first_half, pltpu.roll(u, LANES - LANES // 4, 1), pltpu.roll(u, LANES // 4, 1))
        return u * rc + partner * rs

    for h in range(DN_HEADS):
        lanes = slice(h * LANES, (h + 1) * LANES)
        qh = norm_rope(y[:, h * LANES:(h + 1) * LANES], DN_HEAD_DIM ** -0.5)
        kh = norm_rope(y[:, DN_WIDTH + h * LANES:DN_WIDTH + (h + 1) * LANES], 1.0)
        q_ref[0, :, lanes] = qh.astype(BF16)
        k_ref[0, :, lanes] = kh.astype(BF16)
    v_ref[0] = y[:, 2 * DN_WIDTH:].astype(BF16)

    ab = ab_ref[0]
    z = ab + dtb_ref[...]
    softplus = jnp.maximum(z, 0.0) + jnp.log1p(jnp.exp(-jnp.abs(z)))
    g = -jnp.exp(alog_ref[...]) * softplus
    g = jnp.where(lane < 2 * DN_HEADS, g, 0.0)
    beta = jax.nn.sigmoid(pltpu.roll(ab, LANES - 2 * DN_HEADS, 1))
    beta_ref[0] = jnp.where(lane < 2 * DN_HEADS, beta, 0.0)

    r = lax.broadcasted_iota(jnp.int32, (PREP_TILE, PREP_TILE), 0)
    c = lax.broadcasted_iota(jnp.int32, (PREP_TILE, PREP_TILE), 1)
    same = (r // DN_CHUNK) == (c // DN_CHUNK)
    tri_f = jnp.where(same & (c <= r), 1.0, 0.0).astype(BF16)
    tri_b = jnp.where(same & (c >= r), 1.0, 0.0).astype(BF16)
    g1 = g.astype(BF16)
    rem = g - g1.astype(F32)
    g2 = rem.astype(BF16)
    g3 = (rem - g2.astype(F32)).astype(BF16)
    cs_f = _dot(tri_f, g1) + _dot(tri_f, g2) + _dot(tri_f, g3)
    cs_b = _dot(tri_b, g1) + _dot(tri_b, g2) + _dot(tri_b, g3)
    g_ref[0] = jnp.where(lane < DN_HEADS, cs_f, cs_b)


def _dn_prep(p, ab, conv_w, rope_c, rope_s, alog_row, dtb_row, *, s_len):
    b, ta, w3 = p.shape
    col_dn = 0
    n_t = ta // PREP_TILE
    hb = PREP_TILE // HALO
    n_hb = ta // HALO
    tok = lambda: pl.BlockSpec((1, PREP_TILE, DN_WIDTH), lambda bb, i: (bb, i, 0))
    sca = lambda: pl.BlockSpec((1, PREP_TILE, LANES), lambda bb, i: (bb, i, 0))
    row = lambda: pl.BlockSpec((1, LANES), lambda bb, i: (0, 0))
    return pl.pallas_call(
        functools.partial(_dn_prep_kernel, s_len=s_len, ta=ta),
        grid=(b, n_t),
        in_specs=[pl.BlockSpec((1, PREP_TILE, w3), lambda bb, i: (bb, i, col_dn)),
                  pl.BlockSpec((1, HALO, w3), lambda bb, i: (bb, jnp.maximum(i * hb - 1, 0), col_dn)),
                  pl.BlockSpec((1, HALO, w3), lambda bb, i: (bb, jnp.minimum((i + 1) * hb, n_hb - 1), col_dn)),
                  pl.BlockSpec((DN_CONV, w3), lambda bb, i: (0, 0)),
                  pl.BlockSpec((PREP_TILE, LANES), lambda bb, i: (i, 0)),
                  pl.BlockSpec((PREP_TILE, LANES), lambda bb, i: (i, 0)),
                  sca(), row(), row()],
        out_specs=[tok(), tok(), tok(), sca(), sca()],
        out_shape=[jax.ShapeDtypeStruct((b, ta, DN_WIDTH), BF16)] * 3
        + [jax.ShapeDtypeStruct((b, ta, LANES), F32)] * 2,
        scratch_shapes=[pltpu.VMEM((PREP_TILE + HALO, w3), F32)],
        compiler_params=_cparams(("parallel", "parallel")),
        name="dn_prep",
    )(p, p, p, conv_w, rope_c, rope_s, ab, alog_row, dtb_row)


CH4 = DN_HEADS * DN_CHUNK


DN_CPS = 4
DN_PAIRS = DN_HEADS // 2
PAIR = 2 * DN_CHUNK


def _stack_pair(x, p):
    return jnp.concatenate([x[:, (2 * p + h) * LANES:(2 * p + h + 1) * LANES] for h in range(2)], axis=0)


def _dn_local_kernel(q_ref, k_ref, v_ref, gc_ref, bc_ref, gr_ref, wq_ref, u_ref, a_ref, kgt_ref):
    r = lax.broadcasted_iota(jnp.int32, (PAIR, PAIR), 0)
    c = lax.broadcasted_iota(jnp.int32, (PAIR, PAIR), 1)
    same = (r // DN_CHUNK) == (c // DN_CHUNK)
    chains = []
    for cc in range(DN_CPS):
        toks = slice(cc * DN_CHUNK, (cc + 1) * DN_CHUNK)
        gc = gc_ref[0, toks, :]
        bc = bc_ref[0, toks, :]
        for p in range(DN_PAIRS):
            rows = slice(p * PAIR, (p + 1) * PAIR)
            kst = _stack_pair(k_ref[0, toks, :], p)
            qst = _stack_pair(q_ref[0, toks, :], p)
            vst = _stack_pair(v_ref[0, toks, :], p)
            kq = _dot_nt(jnp.concatenate([kst, qst], axis=0), kst)
            kk = kq[:PAIR]
            qk = kq[PAIR:]
            kf = kst.astype(F32)
            for d in range(2):
                later = (r >= c) if d == 0 else (r <= c)
                incl = same & later
                strict = incl & (r != c)
                lane0 = d * DN_HEADS + 2 * p
                col = lambda a: jnp.concatenate([a[:, lane0 + h:lane0 + h + 1] for h in range(2)], axis=0)
                g_col = col(gc)
                b_col = col(bc)
                last = DN_CHUNK - 1 if d == 0 else 0
                gl_col = jnp.concatenate(
                    [jnp.broadcast_to(gc[last:last + 1, lane0 + h:lane0 + h + 1], (DN_CHUNK, 1))
                     for h in range(2)], axis=0)
                g_row = gr_ref[0, cc, d:d + 1, rows]
                decay = jnp.exp(jnp.where(incl, g_col - g_row, NEG_INF))
                pw = jnp.where(strict, -(kk * decay * b_col), 0.0)
                z = jnp.concatenate([vst.astype(F32) * b_col, kf * (b_col * jnp.exp(g_col))], axis=1)
                chains.append([pw, z, (cc, d, rows)])
                wq_ref[0, cc, d, CH4 + p * PAIR:CH4 + (p + 1) * PAIR, :] = (
                    qst.astype(F32) * jnp.exp(g_col)).astype(BF16)
                a_ref[0, cc, d, p] = (qk * decay).astype(BF16)
                kgt_ref[0, cc, d, :, rows] = (kf * jnp.exp(gl_col - g_col)).T.astype(BF16)

    for step in range(6):
        if step:
            for chain in chains:
                pwb = chain[0].astype(BF16)
                chain[0] = _dot(pwb, pwb)
        for chain in chains:
            pw, z, _ = chain
            chain[1] = z + _dot(pw.astype(BF16), z.astype(BF16))
    for _, z, (cc, d, rows) in chains:
        wq_ref[0, cc, d, rows, :] = z[:, LANES:].astype(BF16)
        u_ref[0, cc, d, rows, :] = z[:, :LANES]


def _dn_local(qd, kd, vd, gc, bc, g_rows):
    b, ta, _ = qd.shape
    nc = ta // DN_CHUNK
    tok = lambda w: pl.BlockSpec((1, DN_CPS * DN_CHUNK, w), lambda bb, n: (bb, n, 0))
    out = lambda *tail: pl.BlockSpec((1, DN_CPS, 2) + tail, lambda bb, n: (bb, n, 0) + (0,) * len(tail))
    return pl.pallas_call(
        _dn_local_kernel,
        grid=(b, nc // DN_CPS),
        in_specs=[tok(DN_WIDTH), tok(DN_WIDTH), tok(DN_WIDTH), tok(LANES), tok(LANES),
                  pl.BlockSpec((1, DN_CPS, 2, CH4), lambda bb, n: (bb, n, 0, 0))],
        out_specs=[out(2 * CH4, LANES), out(CH4, LANES), out(DN_PAIRS, PAIR, PAIR), out(LANES, CH4)],
        out_shape=[jax.ShapeDtypeStruct((b, nc, 2, 2 * CH4, LANES), BF16),
                   jax.ShapeDtypeStruct((b, nc, 2, CH4, LANES), F32),
                   jax.ShapeDtypeStruct((b, nc, 2, DN_PAIRS, PAIR, PAIR), BF16),
                   jax.ShapeDtypeStruct((b, nc, 2, LANES, CH4), BF16)],
        compiler_params=_cparams(("parallel", "parallel")),
        name="dn_local",
    )(qd, kd, vd, gc, bc, g_rows)


DN_SPS = 4


def _dn_scan_kernel(wq_f, u_f, a_f, kgt_f, gl_f, wq_b, u_b, a_b, kgt_b, gl_b, of_ref, ob_ref, state):
    n = pl.program_id(0)

    @pl.when(n == 0)
    def _():
        state[...] = jnp.zeros_like(state)

    r = lax.broadcasted_iota(jnp.int32, (CH4, DN_WIDTH), 0)
    c = lax.broadcasted_iota(jnp.int32, (CH4, DN_WIDTH), 1)
    own = (r // DN_CHUNK) == (c // LANES)
    pick = lambda m: jnp.concatenate(
        [m[h * DN_CHUNK:(h + 1) * DN_CHUNK, h * LANES:(h + 1) * LANES] for h in range(DN_HEADS)], axis=0)
    dirs = ((wq_f, u_f, a_f, kgt_f, gl_f, of_ref), (wq_b, u_b, a_b, kgt_b, gl_b, ob_ref))
    chains = [(bb, d) + refs for bb in range(state.shape[0]) for d, refs in enumerate(dirs)]
    s_alls = [state[bb, d] for bb, d, *_ in chains]
    for sub in range(DN_SPS):
        at = [sub if d == 0 else DN_SPS - 1 - sub for _, d, *_ in chains]
        wss = [_dot(wq[bb, j, 0], s.astype(BF16)) for (bb, d, wq, *_), s, j in zip(chains, s_alls, at)]
        v_nbs = [(u[bb, j, 0] - pick(ws[:CH4])).astype(BF16)
                 for (bb, d, wq, u, *_), ws, j in zip(chains, wss, at)]
        for (bb, d, wq, u, a, kgt, gl, o_ref), ws, v_nb, j in zip(chains, wss, v_nbs, at):
            o = pick(ws[CH4:]) + jnp.concatenate(
                [_dot(a[bb, j, 0, p], v_nb[p * PAIR:(p + 1) * PAIR]) for p in range(DN_PAIRS)], axis=0)
            for h in range(DN_HEADS):
                o_ref[bb, j * DN_CHUNK:(j + 1) * DN_CHUNK, h * LANES:(h + 1) * LANES] = (
                    o[h * DN_CHUNK:(h + 1) * DN_CHUNK])
        nxt = []
        for (bb, d, wq, u, a, kgt, gl, o_ref), s_all, v_nb, j in zip(chains, s_alls, v_nbs, at):
            v_bd = jnp.where(own, jnp.concatenate([v_nb] * DN_HEADS, axis=1), jnp.zeros((), BF16))
            nxt.append(s_all * jnp.exp(gl[bb, j, 0]) + _dot(kgt[bb, j, 0], v_bd))
        s_alls = nxt
    for (bb, d, *_), s_all in zip(chains, s_alls):
        state[bb, d] = s_all


def _dn_scan(wq, u, a, kgt, gl_wide, *, s_len):
    b, nc = wq.shape[:2]
    n_lat = s_len // DN_CHUNK
    n_ctx = nc - n_lat
    assert n_lat % DN_SPS == 0 and n_ctx % DN_SPS == 0
    nb, nb_lat, nb_ctx = nc // DN_SPS, n_lat // DN_SPS, n_ctx // DN_SPS
    fwd = lambda n: jnp.where(n < nb_ctx, nb_lat + n, n - nb_ctx)
    bwd = lambda n: nb - 1 - n
    specs = []
    for d, order in ((0, fwd), (1, bwd)):
        blk = lambda arr, d=d, order=order: pl.BlockSpec(
            (b, DN_SPS, 1) + arr.shape[3:], lambda n: (0, order(n), d) + (0,) * (arr.ndim - 3))
        specs += [blk(wq), blk(u), blk(a), blk(kgt), blk(gl_wide)]
    o_spec = lambda order: pl.BlockSpec((b, DN_SPS * DN_CHUNK, DN_WIDTH), lambda n: (0, order(n), 0))
    return pl.pallas_call(
        _dn_scan_kernel,
        grid=(nb,),
        in_specs=specs,
        out_specs=[o_spec(fwd), o_spec(bwd)],
        out_shape=[jax.ShapeDtypeStruct((b, nc * DN_CHUNK, DN_WIDTH), F32)] * 2,
        scratch_shapes=[pltpu.VMEM((b, 2, DN_HEAD_DIM, DN_WIDTH), F32)],
        compiler_params=_cparams(("arbitrary",)),
        name="dn_scan",
    )(wq, u, a, kgt, gl_wide, wq, u, a, kgt, gl_wide)


def _merge_kernel(ona_ref, of_ref, ob_ref, z_ref, sc_ref, schp_ref, schn_ref, gate_ref, x_ref,
                  g1l_ref, g1c_ref, sh2l_ref, sh2c_ref, sc2l_ref, sc2c_ref, ng_ref, og_ref, scw_ref,
                  wna_ref, wdn_ref, wsc_ref, wout_ref, wrh_ref, wrl_ref, rb_ref,
                  x1_ref, h2_ref, comb_ref, combt_ref, cpad, *, tm, s_len, ta):
    i = pl.program_id(1)
    d = x_ref.shape[-1]
    o = of_ref[0] + ob_ref[0]
    z = z_ref[0].astype(F32)
    parts = []
    for h in range(DN_HEADS):
        oh = o[:, h * LANES:(h + 1) * LANES]
        parts.append(oh * lax.rsqrt(jnp.mean(oh * oh, axis=-1, keepdims=True) + EPS))
    o_dn = jnp.concatenate(parts, axis=1) * og_ref[...] * _silu(z)

    def cu(ref_val):
        v = ref_val.astype(F32)
        return v[:, 2 * SC_WIDTH:] * v[:, :SC_WIDTH]
    scp = sc_ref[0].astype(F32)
    half = HALO // 2
    cpad[0:half, :] = cu(schp_ref[0])[half:, :]
    cpad[half:half + tm, :] = scp[:, 2 * SC_WIDTH:] * scp[:, :SC_WIDTH]
    cpad[half + tm:, :] = cu(schn_ref[0])[:half, :]
    gt = i * tm + lax.broadcasted_iota(jnp.int32, (tm, 1), 0)
    has_prev = (gt != 0) & (gt != s_len)
    has_next = (gt != s_len - 1) & (gt != ta - 1)
    conv = (scw_ref[0:1, :] * jnp.where(has_prev, cpad[pl.ds(half - 1, tm), :], 0.0)
            + scw_ref[1:2, :] * cpad[pl.ds(half, tm), :]
            + scw_ref[2:3, :] * jnp.where(has_next, cpad[pl.ds(half + 1, tm), :], 0.0))
    o_sc = scp[:, SC_WIDTH:2 * SC_WIDTH] * conv

    merged = (gate_ref[0, :, :d].astype(F32) * _dot(ona_ref[0], wna_ref[...])
              + gate_ref[0, :, d:2 * d].astype(F32) * _dot(o_dn.astype(BF16), wdn_ref[...])
              + gate_ref[0, :, 2 * d:].astype(F32) * _dot(o_sc.astype(BF16), wsc_ref[...]))
    y = _dot(merged.astype(BF16), wout_ref[...])
    x1 = x_ref[0] + _row_select(i, tm, s_len, g1l_ref[0], g1c_ref[...]) * y
    x1_ref[0] = x1

    hn = x1 * lax.rsqrt(jnp.mean(x1 * x1, axis=-1, keepdims=True) + EPS) * ng_ref[...]
    h2 = (hn * (1.0 + _row_select(i, tm, s_len, sc2l_ref[0], sc2c_ref[...]))
          + _row_select(i, tm, s_len, sh2l_ref[0], sh2c_ref[...]))
    h2_ref[0] = h2.astype(BF16)

    h2_hi, h2_lo = _split_bf16(h2)
    logits = _dot3(h2_hi, h2_lo, wrh_ref[...], wrl_ref[...])
    scores = jax.nn.sigmoid(logits.T)
    biased = scores + rb_ref[...]
    rows = [biased[e:e + 1, :] for e in range(N_EXPERTS)]
    gsc = []
    for g in range(N_GROUPS):
        a, b_, c_, d_ = rows[4 * g:4 * g + 4]
        gsc.append(jnp.maximum(jnp.maximum(jnp.maximum(a + b_, a + c_), jnp.maximum(a + d_, b_ + c_)),
                               jnp.maximum(b_ + d_, c_ + d_)))
    best = gsc[0]
    grp = jnp.zeros_like(best, dtype=jnp.int32)
    for g in range(1, N_GROUPS):
        upd = gsc[g] > best
        best = jnp.where(upd, gsc[g], best)
        grp = jnp.where(upd, g, grp)
    eidx = lax.broadcasted_iota(jnp.int32, biased.shape, 0)
    masked = jnp.where((eidx // EXPERTS_PER_GROUP) == grp, biased, -jnp.inf)
    big = jnp.int32(LANES)
    m1 = masked.max(axis=0, keepdims=True)
    i1 = jnp.where(masked == m1, eidx, big).min(axis=0, keepdims=True)
    masked2 = jnp.where(eidx == i1, -jnp.inf, masked)
    m2 = masked2.max(axis=0, keepdims=True)
    i2 = jnp.where(masked2 == m2, eidx, big).min(axis=0, keepdims=True)
    sel1 = eidx == i1
    sel2 = eidx == i2
    s1 = jnp.sum(jnp.where(sel1, scores, 0.0), axis=0, keepdims=True)
    s2 = jnp.sum(jnp.where(sel2, scores, 0.0), axis=0, keepdims=True)
    tot = s1 + s2
    comb_t = jnp.where(sel1, s1 / tot, 0.0) + jnp.where(sel2, s2 / tot, 0.0)
    meta_t = jnp.where(eidx == N_EXPERTS, grp.astype(F32), comb_t)
    combt_ref[0] = meta_t
    comb_ref[0] = meta_t.T


def _merge(o_na, o_f, o_b, z, sc, gates, xs, mods_l, mods_c, norm_g, out_g, sc_w, wna, wdn, wsc, wout, wr, rb,
           *, s_len, tm):
    b, ta, d = xs.shape
    col_z = col_sc = 0
    hb = tm // HALO
    n_hb = ta // HALO
    tok = lambda w, col=0: pl.BlockSpec((1, tm, w), lambda bb, i: (bb, i, col))
    vec = lambda w: pl.BlockSpec((1, w), lambda bb, i: (0, 0))
    bvec = lambda: pl.BlockSpec((1, 1, d), lambda bb, i: (bb, 0, 0))
    full = lambda a: pl.BlockSpec(a.shape, lambda bb, i: (0,) * a.ndim)
    g1l, sh2l, sc2l = mods_l
    g1c, sh2c, sc2c = mods_c
    w3 = 3 * SC_WIDTH
    wr_hi = wr.astype(BF16)
    wr_lo = (wr - wr_hi.astype(F32)).astype(BF16)
    return pl.pallas_call(
        functools.partial(_merge_kernel, tm=tm, s_len=s_len, ta=ta),
        grid=(b, ta // tm),
        in_specs=[tok(NA_WIDTH), tok(DN_WIDTH), tok(DN_WIDTH), tok(DN_WIDTH, col_z), tok(w3, col_sc),
                  pl.BlockSpec((1, HALO, w3), lambda bb, i: (bb, jnp.maximum(i * hb - 1, 0), col_sc)),
                  pl.BlockSpec((1, HALO, w3), lambda bb, i: (bb, jnp.minimum((i + 1) * hb, n_hb - 1), col_sc)),
                  tok(3 * d, 0), tok(d),
                  bvec(), vec(d), bvec(), vec(d), bvec(), vec(d), vec(d), vec(DN_WIDTH),
                  full(sc_w), full(wna), full(wdn), full(wsc), full(wout), full(wr_hi), full(wr_lo), full(rb)],
        out_specs=[tok(d), tok(d), tok(LANES), pl.BlockSpec((1, LANES, tm), lambda bb, i: (bb, 0, i))],
        out_shape=[jax.ShapeDtypeStruct((b, ta, d), F32),
                   jax.ShapeDtypeStruct((b, ta, d), BF16),
                   jax.ShapeDtypeStruct((b, ta, LANES), F32),
                   jax.ShapeDtypeStruct((b, LANES, ta), F32)],
        scratch_shapes=[pltpu.VMEM((tm + HALO, SC_WIDTH), F32)],
        compiler_params=_cparams(("parallel", "parallel")),
        name="merge_router",
    )(o_na, o_f, o_b, z, sc, sc, sc, gates, xs, g1l, g1c, sh2l, sh2c, sc2l, sc2c, norm_g, out_g,
      sc_w, wna, wdn, wsc, wout, wr_hi, wr_lo, rb)


MOE_CAP = 256


def _moe_kernel(h_ref, comb_ref, combt_ref, lt_ref, gt_ref, wg_ref, wu_ref, wd_ref, x_ref, g2l_ref, g2c_ref,
                o_ref, acc, rank_r, rank_c, *, tm, s_len):
    i = pl.program_id(1)
    g = pl.program_id(2)

    comb = comb_ref[0]
    lane = lax.broadcasted_iota(jnp.int32, (tm, LANES), 1)
    grp_row = combt_ref[0, N_EXPERTS:N_EXPERTS + 1, :]
    grp_col = jnp.sum(jnp.where(lane == N_EXPERTS, comb, 0.0), axis=-1, keepdims=True)

    @pl.when(g == 0)
    def _():
        acc[...] = jnp.zeros_like(acc)
        sub = lax.broadcasted_iota(jnp.int32, (8, tm), 0).astype(F32)
        before = _dot(jnp.where(sub == grp_row, 1.0, 0.0).astype(BF16), lt_ref[...])
        rank_r[...] = jnp.broadcast_to(
            jnp.sum(jnp.where(sub == grp_row, before, 0.0), axis=0, keepdims=True), (8, tm))
        lane_f = lane.astype(F32)
        before = _dot(gt_ref[...], jnp.where(lane_f == grp_col, 1.0, 0.0).astype(BF16))
        rank_c[...] = jnp.broadcast_to(
            jnp.sum(jnp.where(lane_f == grp_col, before, 0.0), axis=-1, keepdims=True), (tm, LANES))

    gf = g.astype(F32)
    member_row = grp_row == gf
    member_col = grp_col == gf
    rank_row = rank_r[0:1, :]
    rank_col = rank_c[:, 0:1]
    count = jnp.sum(jnp.where(member_row, 1.0, 0.0))
    comb_hi, comb_lo = _split_bf16(comb)
    d = x_ref.shape[-1]

    for k in range(tm // MOE_CAP):
        @pl.when(count > k * MOE_CAP)
        def _(k=k):
            slot_c = (lax.broadcasted_iota(jnp.int32, (MOE_CAP, 1), 0) + k * MOE_CAP).astype(F32)
            gather = jnp.where(member_row & (rank_row == slot_c), 1.0, 0.0).astype(BF16)
            hs = _dot(gather, h_ref[0]).astype(BF16)
            cws = _dot(gather, comb_hi) + _dot(gather, comb_lo)
            lane_c = lax.broadcasted_iota(jnp.int32, (MOE_CAP, LANES), 1)
            experts = range(EXPERTS_PER_GROUP)
            gates = [_dot(hs, wg_ref[0, e4]) for e4 in experts]
            ups = [_dot(hs, wu_ref[0, e4]) for e4 in experts]
            hiddens = []
            for e4 in experts:
                cw = jnp.sum(jnp.where(lane_c == g * EXPERTS_PER_GROUP + e4, cws, 0.0), axis=-1, keepdims=True)
                hiddens.append((_silu(gates[e4]) * ups[e4] * cw).astype(BF16))
            ys = _dot(hiddens[0], wd_ref[0, 0])
            for e4 in experts[1:]:
                ys = ys + _dot(hiddens[e4], wd_ref[0, e4])
            slot_r = (lax.broadcasted_iota(jnp.int32, (1, MOE_CAP), 1) + k * MOE_CAP).astype(F32)
            scatter = jnp.where(member_col & (rank_col == slot_r), 1.0, 0.0).astype(BF16)
            ys_hi, ys_lo = _split_bf16(ys)
            acc[...] += _dot(scatter, ys_hi) + _dot(scatter, ys_lo)

    @pl.when(g == N_GROUPS - 1)
    def _():
        o_ref[0] = x_ref[0] + _row_select(i, tm, s_len, g2l_ref[0], g2c_ref[...]) * acc[...]


def _moe(h2, comb, comb_t, wg, wu, wd, x1, g2l, g2c, *, layer, s_len, tm):
    b, ta, d = x1.shape
    tok = lambda w: pl.BlockSpec((1, tm, w), lambda bb, i, g: (bb, i, 0))
    grp = lambda a: pl.BlockSpec((1, EXPERTS_PER_GROUP) + a.shape[2:], lambda bb, i, g: (layer, g, 0, 0))
    tri = lambda: pl.BlockSpec((tm, tm), lambda bb, i, g: (0, 0))
    earlier = np.arange(tm)[:, None] < np.arange(tm)[None, :]
    lt = jnp.asarray(earlier, BF16)
    gt = jnp.asarray(earlier.T, BF16)
    return pl.pallas_call(
        functools.partial(_moe_kernel, tm=tm, s_len=s_len),
        grid=(b, ta // tm, N_GROUPS),
        in_specs=[tok(d), tok(LANES),
                  pl.BlockSpec((1, LANES, tm), lambda bb, i, g: (bb, 0, i)),
                  tri(), tri(),
                  grp(wg), grp(wu), grp(wd),
                  tok(d),
                  pl.BlockSpec((1, 1, d), lambda bb, i, g: (bb, 0, 0)),
                  pl.BlockSpec((1, d), lambda bb, i, g: (0, 0))],
        out_specs=tok(d),
        out_shape=jax.ShapeDtypeStruct((b, ta, d), F32),
        scratch_shapes=[pltpu.VMEM((tm, d), F32), pltpu.VMEM((8, tm), F32), pltpu.VMEM((tm, LANES), F32)],
        compiler_params=_cparams(("parallel", "parallel", "arbitrary")),
        name="moe_grouped",
    )(h2, comb, comb_t, lt, gt, wg, wu, wd, x1, g2l, g2c)


def _rope_tables(s_len, ctx_len):
    t = np.arange(s_len)
    n_freq = DN_HEAD_DIM // 4
    inv_freq = (np.float32(ROPE_BASE) ** (-np.arange(n_freq, dtype=np.float32) / n_freq)).astype(np.float32)
    pos = np.stack([t // GRID_W, t % GRID_W], axis=-1).astype(np.float32)
    ang = (pos[..., None] * inv_freq).astype(np.float32)
    cos, sin = np.cos(ang.astype(np.float64)), np.sin(ang.astype(np.float64))
    rc = np.concatenate([cos[:, 0], cos[:, 0], cos[:, 1], cos[:, 1]], axis=-1)
    rs = np.concatenate([-sin[:, 0], sin[:, 0], -sin[:, 1], sin[:, 1]], axis=-1)
    rc = np.concatenate([rc, np.ones((ctx_len, LANES))], axis=0)
    rs = np.concatenate([rs, np.zeros((ctx_len, LANES))], axis=0)
    return jnp.asarray(rc, F32), jnp.asarray(rs, F32)


def _pick_tile(ta, candidates):
    for t in candidates:
        if ta % t == 0:
            return t
    raise ValueError(f"no row tile for {ta} tokens")


def kernel(x, c, ctx, c_ctx, w_ada, b_ada, norm_mix_g, norm_ffn_g, w_in, na_q_norm_g, na_k_norm_g, na_rpb, dn_conv_w, dn_a_log, dn_dt_bias, dn_out_norm_g, sc_conv_w, w_branch_na, w_branch_dn, w_branch_sc, w_out, w_router, router_bias, moe_w_gate, moe_w_up, moe_w_down):
    b, s_len, d = x.shape
    ctx_len = ctx.shape[1]
    depth = w_ada.shape[0]
    ta = s_len + ctx_len
    assert ctx_len == NA_BLK and s_len % (WIN_R * GRID_W) == 0 and s_len // NA_BLK >= 4
    assert ctx_len % PREP_TILE == 0 and b + 1 <= 8
    tm_proj = _pick_tile(ta, (768, 512, 256))
    tm_merge = _pick_tile(ta, (384, 256))
    tm_moe = _pick_tile(ta, (768, 512, 256))

    xs = jnp.concatenate([x, ctx], axis=1)
    c_rows = jnp.zeros((8, d), F32).at[:b].set(c).at[b].set(c_ctx)
    mod = _modulation(c_rows, w_ada, b_ada)

    sizes = (3 * NA_WIDTH, 3 * DN_WIDTH, DN_WIDTH, 4 * DN_HEADS, 3 * SC_WIDTH, 3 * d)
    cuts = [int(v) for v in np.cumsum((0,) + sizes)]
    seg = lambda l, k: lax.slice(w_in, (l, 0, cuts[k]), (l + 1, d, cuts[k + 1]))[0]
    wg_all, wu_all, wd_all = moe_w_gate.astype(BF16), moe_w_up.astype(BF16), moe_w_down.astype(BF16)

    rope_c, rope_s = _rope_tables(s_len, ctx_len)
    bd = jnp.asarray(np.kron(np.eye(NA_HEADS), np.ones((NA_HEAD_DIM, NA_HEAD_DIM))), BF16)
    nc = ta // DN_CHUNK
    wr = jnp.zeros((d, LANES), F32).at[:, :N_EXPERTS].set(w_router)
    rb = jnp.zeros((LANES, 1), F32).at[:N_EXPERTS, 0].set(router_bias)

    for l in range(depth):
        m = mod[l].reshape(8, N_MOD, d)
        lat = lambda k: m[:b, k][:, None, :]
        cx = lambda k: m[b:b + 1, k]
        w_ab = jnp.zeros((d, LANES), F32).at[:, :4 * DN_HEADS].set(seg(l, 3))
        gqk = jnp.concatenate([jnp.tile(na_q_norm_g[l], NA_HEADS) * (NA_HEAD_DIM ** -0.5),
                               jnp.tile(na_k_norm_g[l], NA_HEADS)])[None, :]
        gates, na_q, na_k, na_v, dn_qkv, sc_p, dn_z, ab = _projection(
            xs, (lat(0), lat(1)), (cx(0), cx(1)), norm_mix_g[l][None, :],
            seg(l, 5).astype(BF16), seg(l, 0).astype(BF16), seg(l, 1).astype(BF16),
            seg(l, 4).astype(BF16), seg(l, 2).astype(BF16), w_ab, gqk, bd, s_len=s_len, tm=tm_proj)

        o_na = _neighbourhood_attention(na_q, na_k, na_v, _na_bias_table(na_rpb[l]), s_len=s_len)

        alog_row = jnp.zeros((1, LANES), F32).at[0, :2 * DN_HEADS].set(dn_a_log[l].reshape(-1))
        dtb_row = jnp.zeros((1, LANES), F32).at[0, :2 * DN_HEADS].set(dn_dt_bias[l].reshape(-1))
        qd, kd, vd, gc, bc = _dn_prep(dn_qkv, ab, dn_conv_w[l], rope_c, rope_s, alog_row, dtb_row,
                                      s_len=s_len)
        g8 = gc[..., :2 * DN_HEADS].reshape(b, nc, DN_CHUNK, 2, DN_HEADS)
        g_rows = jnp.transpose(g8, (0, 1, 3, 4, 2)).reshape(b, nc, 2, CH4)
        g_last = jnp.stack([g8[:, :, DN_CHUNK - 1, 0], g8[:, :, 0, 1]], axis=2)
        gl_wide = jnp.repeat(g_last, LANES, axis=-1)[:, :, :, None, :]
        wq, u, a, kgt = _dn_local(qd, kd, vd, gc, bc, g_rows)
        o_f, o_b = _dn_scan(wq, u, a, kgt, gl_wide, s_len=s_len)

        x1, h2, comb, comb_t = _merge(
            o_na, o_f, o_b, dn_z, sc_p, gates, xs, (lat(2), lat(3), lat(4)), (cx(2), cx(3), cx(4)),
            norm_ffn_g[l][None, :], jnp.tile(dn_out_norm_g[l], DN_HEADS)[None, :], sc_conv_w[l],
            w_branch_na[l].astype(BF16), w_branch_dn[l].astype(BF16), w_branch_sc[l].astype(BF16),
            w_out[l].astype(BF16), wr, rb, s_len=s_len, tm=tm_merge)

        xs = _moe(h2, comb, comb_t, wg_all, wu_all, wd_all, x1, lat(5), cx(5), layer=l, s_len=s_len, tm=tm_moe)

    return xs[:, :s_len]
```

```python
import functools
import math

import numpy as np
import jax
import jax.numpy as jnp
from jax import lax
from jax.experimental import pallas as pl
from jax.experimental.pallas import tpu as pltpu

F32 = jnp.float32
BF16 = jnp.bfloat16
HIGHEST = lax.Precision.HIGHEST

GRID_W = 64
NA_HEADS = 8
NA_HEAD_DIM = 64
NA_WIDTH = NA_HEADS * NA_HEAD_DIM
WIN_R = 8
WIN_C = 16
DN_HEADS = 4
DN_HEAD_DIM = 128
DN_WIDTH = DN_HEADS * DN_HEAD_DIM
DN_CONV = 5
DN_CHUNK = 64
ROPE_BASE = 10000.0
SC_WIDTH = 512
SC_CONV = 3
N_EXPERTS = 16
N_GROUPS = 4
EXPERTS_PER_GROUP = N_EXPERTS // N_GROUPS
D_EXPERT = 512
N_MOD = 6
EPS = 1e-6
NEG_INF = -1e30

LANES = 128
HALO = 16
PREP_TILE = 256
VMEM_LIMIT = 56 * 1024 * 1024

N_PROJ_TILE = 512


def _cparams(sem):
    return pltpu.CompilerParams(dimension_semantics=sem, vmem_limit_bytes=VMEM_LIMIT)


def _dot(a, b):
    return jnp.dot(a, b, preferred_element_type=F32)


def _dot_nt(a, b):
    return lax.dot_general(a, b, (((1,), (1,)), ((), ())), preferred_element_type=F32)


def _silu(x):
    return x * jax.nn.sigmoid(x)


def _row_select(i, tm, s_len, lat, ctx):
    row = i * tm + lax.broadcasted_iota(jnp.int32, (tm, 1), 0)
    return jnp.where(row >= s_len, ctx, lat)


def _mod_kernel(c_ref, w_ref, b_ref, o_ref):
    c_hi, c_lo = _split_bf16(_silu(c_ref[...]))
    w_hi, w_lo = _split_bf16(w_ref[0])
    o_ref[0] = _dot3(c_hi, c_lo, w_hi, w_lo) + b_ref[0]


def _modulation(c_rows, w_ada, b_ada):
    depth, d, n = w_ada.shape
    tn = 512
    return pl.pallas_call(
        _mod_kernel,
        grid=(depth, n // tn),
        in_specs=[pl.BlockSpec((8, d), lambda l, j: (0, 0)),
                  pl.BlockSpec((1, d, tn), lambda l, j: (l, 0, j)),
                  pl.BlockSpec((1, 1, tn), lambda l, j: (l, 0, j))],
        out_specs=pl.BlockSpec((1, 8, tn), lambda l, j: (l, 0, j)),
        out_shape=jax.ShapeDtypeStruct((depth, 8, n), F32),
        compiler_params=_cparams(("parallel", "parallel")),
        name="adaln_mod",
    )(c_rows, w_ada, b_ada.reshape(depth, 1, n))


def _split_bf16(a):
    hi = a.astype(BF16)
    return hi, (a - hi.astype(F32)).astype(BF16)


def _dot3(a_hi, a_lo, b_hi, b_lo):
    return _dot(a_hi, b_hi) + _dot(a_lo, b_hi) + _dot(a_hi, b_lo)


def _modulated_norm(x_ref, shl_ref, scl_ref, shc_ref, scc_ref, g_ref, *, tm, s_len):
    i = pl.program_id(1)
    x = x_ref[0]
    y = x * lax.rsqrt(jnp.mean(x * x, axis=-1, keepdims=True) + EPS) * g_ref[...]
    scale = _row_select(i, tm, s_len, scl_ref[0], scc_ref[...])
    shift = _row_select(i, tm, s_len, shl_ref[0], shc_ref[...])
    return y * (1.0 + scale) + shift


def _proj_gate_kernel(x_ref, shl_ref, scl_ref, shc_ref, scc_ref, g_ref, w_ref, gate_ref, *, tm, s_len):
    h = _modulated_norm(x_ref, shl_ref, scl_ref, shc_ref, scc_ref, g_ref, tm=tm, s_len=s_len).astype(BF16)
    for c0 in range(0, w_ref.shape[1], N_PROJ_TILE):
        cols = slice(c0, c0 + N_PROJ_TILE)
        gate_ref[0, :, cols] = jax.nn.sigmoid(_dot(h, w_ref[:, cols])).astype(BF16)


def _proj_mix_kernel(x_ref, shl_ref, scl_ref, shc_ref, scc_ref, g_ref, wna_ref, wdn_ref, wsc_ref, wz_ref,
                     wabh_ref, wabl_ref, gqk_ref, bd_ref,
                     q_ref, k_ref, v_ref, dn_ref, sc_ref, z_ref, ab_ref, *, tm, s_len):
    h = _modulated_norm(x_ref, shl_ref, scl_ref, shc_ref, scc_ref, g_ref, tm=tm, s_len=s_len)
    h_hi, h_lo = _split_bf16(h)
    ab_ref[0] = _dot3(h_hi, h_lo, wabh_ref[...], wabl_ref[...])
    for part, out in enumerate((q_ref, k_ref)):
        cols = slice(part * NA_WIDTH, (part + 1) * NA_WIDTH)
        a = _dot(h_hi, wna_ref[:, cols])
        ss = _dot((a * a).astype(BF16), bd_ref[...])
        out[0] = (a * lax.rsqrt(ss * (1.0 / NA_HEAD_DIM) + EPS) * gqk_ref[:, cols]).astype(BF16)
    v_ref[0] = _dot(h_hi, wna_ref[:, 2 * NA_WIDTH:]).astype(BF16)
    for w_ref, out in ((wdn_ref, dn_ref), (wsc_ref, sc_ref), (wz_ref, z_ref)):
        for c0 in range(0, w_ref.shape[1], N_PROJ_TILE):
            cols = slice(c0, c0 + N_PROJ_TILE)
            out[0, :, cols] = _dot(h_hi, w_ref[:, cols]).astype(BF16)


def _projection(xs, mod_l, mod_c, norm_g, w_gate, w_na, w_dn, w_sc, w_z, w_ab, gqk, bd, *, s_len, tm):
    b, ta, d = xs.shape
    shl, scl = mod_l
    shc, scc = mod_c
    wab_hi, wab_lo = _split_bf16(w_ab)
    tok = lambda w: pl.BlockSpec((1, tm, w), lambda bb, i: (bb, i, 0))
    bvec = lambda: pl.BlockSpec((1, 1, d), lambda bb, i: (bb, 0, 0))
    full = lambda a: pl.BlockSpec(a.shape, lambda bb, i: (0,) * a.ndim)
    head = [tok(d), bvec(), bvec(), full(shc), full(scc), full(norm_g)]
    gates = pl.pallas_call(
        functools.partial(_proj_gate_kernel, tm=tm, s_len=s_len),
        grid=(b, ta // tm),
        in_specs=head + [full(w_gate)],
        out_specs=tok(w_gate.shape[1]),
        out_shape=jax.ShapeDtypeStruct((b, ta, w_gate.shape[1]), BF16),
        compiler_params=_cparams(("parallel", "parallel")),
        name="proj_gates",
    )(xs, shl, scl, shc, scc, norm_g, w_gate)
    consts = (w_na, w_dn, w_sc, w_z, wab_hi, wab_lo, gqk, bd)
    widths = (NA_WIDTH, NA_WIDTH, NA_WIDTH, w_dn.shape[1], w_sc.shape[1], w_z.shape[1])
    outs = pl.pallas_call(
        functools.partial(_proj_mix_kernel, tm=tm, s_len=s_len),
        grid=(b, ta // tm),
        in_specs=head + [full(a) for a in consts],
        out_specs=[tok(w) for w in widths] + [tok(LANES)],
        out_shape=[jax.ShapeDtypeStruct((b, ta, w), BF16) for w in widths]
        + [jax.ShapeDtypeStruct((b, ta, LANES), F32)],
        compiler_params=_cparams(("parallel", "parallel")),
        name="proj_mixers",
    )(xs, shl, scl, shc, scc, norm_g, *consts)
    return (gates,) + tuple(outs)


NA_QROWS = 4
NA_BLK = NA_QROWS * GRID_W
WIN_TOK = WIN_R * GRID_W


def _stack_two_heads(q2):
    lane = lax.broadcasted_iota(jnp.int32, q2.shape, 1)
    zero = jnp.zeros_like(q2)
    return jnp.concatenate([jnp.where(lane < NA_HEAD_DIM, q2, zero),
                            jnp.where(lane >= NA_HEAD_DIM, q2, zero)], axis=0)


def _unstack_two_heads(o2):
    m = o2.shape[0] // 2
    lane = lax.broadcasted_iota(jnp.int32, (m, o2.shape[1]), 1)
    return jnp.where(lane < NA_HEAD_DIM, o2[:m], o2[m:])


def _na_kernel(q_ref, kp_ref, kc_ref, kn_ref, vp_ref, vc_ref, vn_ref, kx_ref, vx_ref, bias_ref, o_ref,
               kbuf, vbuf, *, n_rows):
    i = pl.program_id(1)
    n_lat = n_rows // NA_QROWS

    @pl.when(i < n_lat)
    def _():
        for slot, (kr, vr) in enumerate(((kp_ref, vp_ref), (kc_ref, vc_ref), (kn_ref, vn_ref))):
            kbuf[slot * NA_BLK:(slot + 1) * NA_BLK, :] = kr[0]
            vbuf[slot * NA_BLK:(slot + 1) * NA_BLK, :] = vr[0]
        win = []
        for j in range(NA_QROWS):
            r = i * NA_QROWS + j
            start = jnp.clip(r - WIN_R // 2, 0, n_rows - WIN_R)
            win.append((pl.multiple_of((start - (i - 1) * NA_QROWS) * GRID_W, GRID_W), r - start))

        rows2 = 2 * GRID_W
        pairs = [slice(hp * LANES, (hp + 1) * LANES) for hp in range(NA_HEADS // 2)]
        logits = []
        for hp, lanes in enumerate(pairs):
            qs = jnp.concatenate([_stack_two_heads(q_ref[0, j * GRID_W:(j + 1) * GRID_W, lanes])
                                  for j in range(NA_QROWS)], axis=0)
            s_c = _dot_nt(qs, kx_ref[0, :, lanes])
            s_w = jnp.concatenate(
                [_dot_nt(qs[j * rows2:(j + 1) * rows2], kbuf[pl.ds(win[j][0], WIN_TOK), lanes])
                 + bias_ref[win[j][1], hp] for j in range(NA_QROWS)], axis=0)
            logits.append((s_w, s_c))
        probs = []
        for s_w, s_c in logits:
            m = jnp.maximum(s_w.max(axis=-1, keepdims=True), s_c.max(axis=-1, keepdims=True))
            p_w = jnp.exp(s_w - m)
            p_c = jnp.exp(s_c - m)
            denom = p_w.sum(axis=-1, keepdims=True) + p_c.sum(axis=-1, keepdims=True)
            probs.append((p_w.astype(BF16), p_c.astype(BF16), denom))
        for lanes, (p_wb, p_cb, denom) in zip(pairs, probs):
            o = _dot(p_cb, vx_ref[0, :, lanes]) + jnp.concatenate(
                [_dot(p_wb[j * rows2:(j + 1) * rows2], vbuf[pl.ds(win[j][0], WIN_TOK), lanes])
                 for j in range(NA_QROWS)], axis=0)
            o = o / denom
            for j in range(NA_QROWS):
                o_ref[0, j * GRID_W:(j + 1) * GRID_W, lanes] = _unstack_two_heads(
                    o[j * rows2:(j + 1) * rows2]).astype(BF16)

    @pl.when(i == n_lat)
    def _():
        for hp in range(NA_HEADS // 2):
            lanes = slice(hp * LANES, (hp + 1) * LANES)
            s = _dot_nt(_stack_two_heads(q_ref[0, :, lanes]), kx_ref[0, :, lanes])
            p = jnp.exp(s - s.max(axis=-1, keepdims=True))
            o = _dot(p.astype(BF16), vx_ref[0, :, lanes]) / p.sum(axis=-1, keepdims=True)
            o_ref[0, :, lanes] = _unstack_two_heads(o).astype(BF16)


def _neighbourhood_attention(q, k, v, bias_tab, *, s_len):
    b, ta, _ = q.shape
    n_rows = s_len // GRID_W
    n_lat = n_rows // NA_QROWS
    blk = lambda off: pl.BlockSpec((1, NA_BLK, NA_WIDTH), lambda bb, i: (bb, jnp.clip(i + off, 0, n_lat - 1), 0))
    cx = lambda: pl.BlockSpec((1, NA_BLK, NA_WIDTH), lambda bb, i: (bb, n_lat, 0))
    own = lambda: pl.BlockSpec((1, NA_BLK, NA_WIDTH), lambda bb, i: (bb, i, 0))
    return pl.pallas_call(
        functools.partial(_na_kernel, n_rows=n_rows),
        grid=(b, n_lat + 1),
        in_specs=[own(), blk(-1), blk(0), blk(1), blk(-1), blk(0), blk(1), cx(), cx(),
                  pl.BlockSpec(bias_tab.shape, lambda bb, i: (0, 0, 0, 0))],
        out_specs=own(),
        out_shape=jax.ShapeDtypeStruct((b, ta, NA_WIDTH), BF16),
        scratch_shapes=[pltpu.VMEM((3 * NA_BLK, NA_WIDTH), BF16),
                        pltpu.VMEM((3 * NA_BLK, NA_WIDTH), BF16)],
        compiler_params=_cparams(("parallel", "arbitrary")),
        name="na_attention",
    )(q, k, k, k, v, v, v, k, v, bias_tab)


def _na_bias_table(rpb):
    qc = np.arange(GRID_W)[:, None]
    kc = np.arange(GRID_W)[None, :]
    win_c0 = np.clip(qc - WIN_C // 2, 0, GRID_W - WIN_C)
    in_win = (kc >= win_c0) & (kc < win_c0 + WIN_C)
    rel_c = np.clip(kc - qc + WIN_C - 1, 0, 2 * WIN_C - 2)
    pick_c = (rel_c[..., None] == np.arange(2 * WIN_C - 1)).astype(np.float32)
    by_col = jnp.einsum('hab,qkb->haqk', rpb.astype(F32), pick_c, precision=HIGHEST)
    by_col = jnp.where(in_win[None, None], by_col, NEG_INF)
    tab = jnp.stack([by_col[:, WIN_R - 1 - v:2 * WIN_R - 1 - v] for v in range(WIN_R)], axis=0)
    tab = jnp.transpose(tab, (0, 1, 3, 2, 4))
    return tab.reshape(WIN_R, NA_HEADS // 2, 2 * GRID_W, WIN_R * GRID_W)


def _dn_prep_kernel(x_ref, hp_ref, hn_ref, w_ref, rc_ref, rs_ref, ab_ref, alog_ref, dtb_ref,
                    q_ref, k_ref, v_ref, g_ref, beta_ref, xpad, *, s_len, ta):
    i = pl.program_id(1)
    t0 = i * PREP_TILE
    has_prev = jnp.logical_and(t0 != 0, t0 != s_len).astype(F32)
    has_next = jnp.logical_and(t0 + PREP_TILE != s_len, t0 + PREP_TILE != ta).astype(F32)
    half = HALO // 2
    xpad[0:half, :] = hp_ref[0].astype(F32)[half:, :] * has_prev
    xpad[half:half + PREP_TILE, :] = x_ref[0].astype(F32)
    xpad[half + PREP_TILE:, :] = hn_ref[0].astype(F32)[:half, :] * has_next
    xp = xpad[...]
    n_pad = xp.shape[0]
    y = jnp.zeros((PREP_TILE, x_ref.shape[-1]), F32)
    for tap in range(DN_CONV):
        shifted = xp if tap == DN_CONV // 2 else pltpu.roll(xp, (DN_CONV // 2 - tap) % n_pad, 0)
        y = y + w_ref[tap:tap + 1, :] * shifted[half:half + PREP_TILE]
    y = _silu(y)

    lane = lax.broadcasted_iota(jnp.int32, (PREP_TILE, LANES), 1)
    first_half = (lane % (LANES // 2)) < (LANES // 4)
    rc = rc_ref[...]
    rs = rs_ref[...]

    def norm_rope(u, scale):
        u = u * (lax.rsqrt(jnp.sum(u * u, axis=-1, keepdims=True) + EPS) * scale)
        partner = jnp.where(first_half, pltpu.roll(u, LANES - LANES // 4, 1), pltpu.roll(u, LANES // 4, 1))
        return u * rc + partner * rs

    for h in range(DN_HEADS):
        lanes = slice(h * LANES, (h + 1) * LANES)
        qh = norm_rope(y[:, h * LANES:(h + 1) * LANES], DN_HEAD_DIM ** -0.5)
        kh = norm_rope(y[:, DN_WIDTH + h * LANES:DN_WIDTH + (h + 1) * LANES], 1.0)
        q_ref[0, :, lanes] = qh.astype(BF16)
        k_ref[0, :, lanes] = kh.astype(BF16)
    v_ref[0] = y[:, 2 * DN_WIDTH:].astype(BF16)

    ab = ab_ref[0]
    z = ab + dtb_ref[...]
    softplus = jnp.maximum(z, 0.0) + jnp.log1p(jnp.exp(-jnp.abs(z)))
    g = -jnp.exp(alog_ref[...]) * softplus
    g = jnp.where(lane < 2 * DN_HEADS, g, 0.0)
    beta = jax.nn.sigmoid(pltpu.roll(ab, LANES - 2 * DN_HEADS, 1))
    beta_ref[0] = jnp.where(lane < 2 * DN_HEADS, beta, 0.0)

    r = lax.broadcasted_iota(jnp.int32, (PREP_TILE, PREP_TILE), 0)
    c = lax.broadcasted_iota(jnp.int32, (PREP_TILE, PREP_TILE), 1)
    same = (r // DN_CHUNK) == (c // DN_CHUNK)
    tri_f = jnp.where(same & (c <= r), 1.0, 0.0).astype(BF16)
    tri_b = jnp.where(same & (c >= r), 1.0, 0.0).astype(BF16)
    g1 = g.astype(BF16)
    rem = g - g1.astype(F32)
    g2 = rem.astype(BF16)
    g3 = (rem - g2.astype(F32)).astype(BF16)
    cs_f = _dot(tri_f, g1) + _dot(tri_f, g2) + _dot(tri_f, g3)
    cs_b = _dot(tri_b, g1) + _dot(tri_b, g2) + _dot(tri_b, g3)
    g_ref[0] = jnp.where(lane < DN_HEADS, cs_f, cs_b)


def _dn_prep(p, ab, conv_w, rope_c, rope_s, alog_row, dtb_row, *, s_len):
    b, ta, w3 = p.shape
    col_dn = 0
    n_t = ta // PREP_TILE
    hb = PREP_TILE // HALO
    n_hb = ta // HALO
    tok = lambda: pl.BlockSpec((1, PREP_TILE, DN_WIDTH), lambda bb, i: (bb, i, 0))
    sca = lambda: pl.BlockSpec((1, PREP_TILE, LANES), lambda bb, i: (bb, i, 0))
    row = lambda: pl.BlockSpec((1, LANES), lambda bb, i: (0, 0))
    return pl.pallas_call(
        functools.partial(_dn_prep_kernel, s_len=s_len, ta=ta),
        grid=(b, n_t),
        in_specs=[pl.BlockSpec((1, PREP_TILE, w3), lambda bb, i: (bb, i, col_dn)),
                  pl.BlockSpec((1, HALO, w3), lambda bb, i: (bb, jnp.maximum(i * hb - 1, 0), col_dn)),
                  pl.BlockSpec((1, HALO, w3), lambda bb, i: (bb, jnp.minimum((i + 1) * hb, n_hb - 1), col_dn)),
                  pl.BlockSpec((DN_CONV, w3), lambda bb, i: (0, 0)),
                  pl.BlockSpec((PREP_TILE, LANES), lambda bb, i: (i, 0)),
                  pl.BlockSpec((PREP_TILE, LANES), lambda bb, i: (i, 0)),
                  sca(), row(), row()],
        out_specs=[tok(), tok(), tok(), sca(), sca()],
        out_shape=[jax.ShapeDtypeStruct((b, ta, DN_WIDTH), BF16)] * 3
        + [jax.ShapeDtypeStruct((b, ta, LANES), F32)] * 2,
        scratch_shapes=[pltpu.VMEM((PREP_TILE + HALO, w3), F32)],
        compiler_params=_cparams(("parallel", "parallel")),
        name="dn_prep",
    )(p, p, p, conv_w, rope_c, rope_s, ab, alog_row, dtb_row)


CH4 = DN_HEADS * DN_CHUNK


DN_CPS = 4
DN_PAIRS = DN_HEADS // 2
PAIR = 2 * DN_CHUNK


def _stack_pair(x, p):
    return jnp.concatenate([x[:, (2 * p + h) * LANES:(2 * p + h + 1) * LANES] for h in range(2)], axis=0)


def _dn_local_kernel(q_ref, k_ref, v_ref, gc_ref, bc_ref, gr_ref, wq_ref, u_ref, a_ref, kgt_ref):
    r = lax.broadcasted_iota(jnp.int32, (PAIR, PAIR), 0)
    c = lax.broadcasted_iota(jnp.int32, (PAIR, PAIR), 1)
    same = (r // DN_CHUNK) == (c // DN_CHUNK)
    chains = []
    for cc in range(DN_CPS):
        toks = slice(cc * DN_CHUNK, (cc + 1) * DN_CHUNK)
        gc = gc_ref[0, toks, :]
        bc = bc_ref[0, toks, :]
        for p in range(DN_PAIRS):
            rows = slice(p * PAIR, (p + 1) * PAIR)
            kst = _stack_pair(k_ref[0, toks, :], p)
            qst = _stack_pair(q_ref[0, toks, :], p)
            vst = _stack_pair(v_ref[0, toks, :], p)
            kq = _dot_nt(jnp.concatenate([kst, qst], axis=0), kst)
            kk = kq[:PAIR]
            qk = kq[PAIR:]
            kf = kst.astype(F32)
            for d in range(2):
                later = (r >= c) if d == 0 else (r <= c)
                incl = same & later
                strict = incl & (r != c)
                lane0 = d * DN_HEADS + 2 * p
                col = lambda a: jnp.concatenate([a[:, lane0 + h:lane0 + h + 1] for h in range(2)], axis=0)
                g_col = col(gc)
                b_col = col(bc)
                last = DN_CHUNK - 1 if d == 0 else 0
                gl_col = jnp.concatenate(
                    [jnp.broadcast_to(gc[last:last + 1, lane0 + h:lane0 + h + 1], (DN_CHUNK, 1))
                     for h in range(2)], axis=0)
                g_row = gr_ref[0, cc, d:d + 1, rows]
                decay = jnp.exp(jnp.where(incl, g_col - g_row, NEG_INF))
                pw = jnp.where(strict, -(kk * decay * b_col), 0.0)
                z = jnp.concatenate([vst.astype(F32) * b_col, kf * (b_col * jnp.exp(g_col))], axis=1)
                chains.append([pw, z, (cc, d, rows)])
                wq_ref[0, cc, d, CH4 + p * PAIR:CH4 + (p + 1) * PAIR, :] = (
                    qst.astype(F32) * jnp.exp(g_col)).astype(BF16)
                a_ref[0, cc, d, p] = (qk * decay).astype(BF16)
                kgt_ref[0, cc, d, :, rows] = (kf * jnp.exp(gl_col - g_col)).T.astype(BF16)

    for step in range(6):
        if step:
            for chain in chains:
                pwb = chain[0].astype(BF16)
                chain[0] = _dot(pwb, pwb)
        for chain in chains:
            pw, z, _ = chain
            chain[1] = z + _dot(pw.astype(BF16), z.astype(BF16))
    for _, z, (cc, d, rows) in chains:
        wq_ref[0, cc, d, rows, :] = z[:, LANES:].astype(BF16)
        u_ref[0, cc, d, rows, :] = z[:, :LANES]


def _dn_local(qd, kd, vd, gc, bc, g_rows):
    b, ta, _ = qd.shape
    nc = ta // DN_CHUNK
    tok = lambda w: pl.BlockSpec((1, DN_CPS * DN_CHUNK, w), lambda bb, n: (bb, n, 0))
    out = lambda *tail: pl.BlockSpec((1, DN_CPS, 2) + tail, lambda bb, n: (bb, n, 0) + (0,) * len(tail))
    return pl.pallas_call(
        _dn_local_kernel,
        grid=(b, nc // DN_CPS),
        in_specs=[tok(DN_WIDTH), tok(DN_WIDTH), tok(DN_WIDTH), tok(LANES), tok(LANES),
                  pl.BlockSpec((1, DN_CPS, 2, CH4), lambda bb, n: (bb, n, 0, 0))],
        out_specs=[out(2 * CH4, LANES), out(CH4, LANES), out(DN_PAIRS, PAIR, PAIR), out(LANES, CH4)],
        out_shape=[jax.ShapeDtypeStruct((b, nc, 2, 2 * CH4, LANES), BF16),
                   jax.ShapeDtypeStruct((b, nc, 2, CH4, LANES), F32),
                   jax.ShapeDtypeStruct((b, nc, 2, DN_PAIRS, PAIR, PAIR), BF16),
                   jax.ShapeDtypeStruct((b, nc, 2, LANES, CH4), BF16)],
        compiler_params=_cparams(("parallel", "parallel")),
        name="dn_local",
    )(qd, kd, vd, gc, bc, g_rows)


DN_SPS = 4


def _dn_scan_kernel(wq_f, u_f, a_f, kgt_f, gl_f, wq_b, u_b, a_b, kgt_b, gl_b, of_ref, ob_ref, state):
    n = pl.program_id(0)

    @pl.when(n == 0)
    def _():
        state[...] = jnp.zeros_like(state)

    r = lax.broadcasted_iota(jnp.int32, (CH4, DN_WIDTH), 0)
    c = lax.broadcasted_iota(jnp.int32, (CH4, DN_WIDTH), 1)
    own = (r // DN_CHUNK) == (c // LANES)
    pick = lambda m: jnp.concatenate(
        [m[h * DN_CHUNK:(h + 1) * DN_CHUNK, h * LANES:(h + 1) * LANES] for h in range(DN_HEADS)], axis=0)
    dirs = ((wq_f, u_f, a_f, kgt_f, gl_f, of_ref), (wq_b, u_b, a_b, kgt_b, gl_b, ob_ref))
    chains = [(bb, d) + refs for bb in range(state.shape[0]) for d, refs in enumerate(dirs)]
    s_alls = [state[bb, d] for bb, d, *_ in chains]
    for sub in range(DN_SPS):
        at = [sub if d == 0 else DN_SPS - 1 - sub for _, d, *_ in chains]
        wss = [_dot(wq[bb, j, 0], s.astype(BF16)) for (bb, d, wq, *_), s, j in zip(chains, s_alls, at)]
        v_nbs = [(u[bb, j, 0] - pick(ws[:CH4])).astype(BF16)
                 for (bb, d, wq, u, *_), ws, j in zip(chains, wss, at)]
        for (bb, d, wq, u, a, kgt, gl, o_ref), ws, v_nb, j in zip(chains, wss, v_nbs, at):
            o = pick(ws[CH4:]) + jnp.concatenate(
                [_dot(a[bb, j, 0, p], v_nb[p * PAIR:(p + 1) * PAIR]) for p in range(DN_PAIRS)], axis=0)
            for h in range(DN_HEADS):
                o_ref[bb, j * DN_CHUNK:(j + 1) * DN_CHUNK, h * LANES:(h + 1) * LANES] = (
                    o[h * DN_CHUNK:(h + 1) * DN_CHUNK])
        nxt = []
        for (bb, d, wq, u, a, kgt, gl, o_ref), s_all, v_nb, j in zip(chains, s_alls, v_nbs, at):
            v_bd = jnp.where(own, jnp.concatenate([v_nb] * DN_HEADS, axis=1), jnp.zeros((), BF16))
            nxt.append(s_all * jnp.exp(gl[bb, j, 0]) + _dot(kgt[bb, j, 0], v_bd))
        s_alls = nxt
    for (bb, d, *_), s_all in zip(chains, s_alls):
        state[bb, d] = s_all


def _dn_scan(wq, u, a, kgt, gl_wide, *, s_len):
    b, nc = wq.shape[:2]
    n_lat = s_len // DN_CHUNK
    n_ctx = nc - n_lat
    assert n_lat % DN_SPS == 0 and n_ctx % DN_SPS == 0
    nb, nb_lat, nb_ctx = nc // DN_SPS, n_lat // DN_SPS, n_ctx // DN_SPS
    fwd = lambda n: jnp.where(n < nb_ctx, nb_lat + n, n - nb_ctx)
    bwd = lambda n: nb - 1 - n
    specs = []
    for d, order in ((0, fwd), (1, bwd)):
        blk = lambda arr, d=d, order=order: pl.BlockSpec(
            (b, DN_SPS, 1) + arr.shape[3:], lambda n: (0, order(n), d) + (0,) * (arr.ndim - 3))
        specs += [blk(wq), blk(u), blk(a), blk(kgt), blk(gl_wide)]
    o_spec = lambda order: pl.BlockSpec((b, DN_SPS * DN_CHUNK, DN_WIDTH), lambda n: (0, order(n), 0))
    return pl.pallas_call(
        _dn_scan_kernel,
        grid=(nb,),
        in_specs=specs,
        out_specs=[o_spec(fwd), o_spec(bwd)],
        out_shape=[jax.ShapeDtypeStruct((b, nc * DN_CHUNK, DN_WIDTH), F32)] * 2,
        scratch_shapes=[pltpu.VMEM((b, 2, DN_HEAD_DIM, DN_WIDTH), F32)],
        compiler_params=_cparams(("arbitrary",)),
        name="dn_scan",
    )(wq, u, a, kgt, gl_wide, wq, u, a, kgt, gl_wide)


def _merge_kernel(ona_ref, of_ref, ob_ref, z_ref, sc_ref, schp_ref, schn_ref, gate_ref, x_ref,
                  g1l_ref, g1c_ref, sh2l_ref, sh2c_ref, sc2l_ref, sc2c_ref, ng_ref, og_ref, scw_ref,
                  wna_ref, wdn_ref, wsc_ref, wout_ref, wrh_ref, wrl_ref, rb_ref,
                  x1_ref, h2_ref, comb_ref, combt_ref, cpad, *, tm, s_len, ta):
    i = pl.program_id(1)
    d = x_ref.shape[-1]
    o = of_ref[0] + ob_ref[0]
    z = z_ref[0].astype(F32)
    parts = []
    for h in range(DN_HEADS):
        oh = o[:, h * LANES:(h + 1) * LANES]
        parts.append(oh * lax.rsqrt(jnp.mean(oh * oh, axis=-1, keepdims=True) + EPS))
    o_dn = jnp.concatenate(parts, axis=1) * og_ref[...] * _silu(z)

    def cu(ref_val):
        v = ref_val.astype(F32)
        return v[:, 2 * SC_WIDTH:] * v[:, :SC_WIDTH]
    scp = sc_ref[0].astype(F32)
    half = HALO // 2
    cpad[0:half, :] = cu(schp_ref[0])[half:, :]
    cpad[half:half + tm, :] = scp[:, 2 * SC_WIDTH:] * scp[:, :SC_WIDTH]
    cpad[half + tm:, :] = cu(schn_ref[0])[:half, :]
    gt = i * tm + lax.broadcasted_iota(jnp.int32, (tm, 1), 0)
    has_prev = (gt != 0) & (gt != s_len)
    has_next = (gt != s_len - 1) & (gt != ta - 1)
    conv = (scw_ref[0:1, :] * jnp.where(has_prev, cpad[pl.ds(half - 1, tm), :], 0.0)
            + scw_ref[1:2, :] * cpad[pl.ds(half, tm), :]
            + scw_ref[2:3, :] * jnp.where(has_next, cpad[pl.ds(half + 1, tm), :], 0.0))
    o_sc = scp[:, SC_WIDTH:2 * SC_WIDTH] * conv

    merged = (gate_ref[0, :, :d].astype(F32) * _dot(ona_ref[0], wna_ref[...])
              + gate_ref[0, :, d:2 * d].astype(F32) * _dot(o_dn.astype(BF16), wdn_ref[...])
              + gate_ref[0, :, 2 * d:].astype(F32) * _dot(o_sc.astype(BF16), wsc_ref[...]))
    y = _dot(merged.astype(BF16), wout_ref[...])
    x1 = x_ref[0] + _row_select(i, tm, s_len, g1l_ref[0], g1c_ref[...]) * y
    x1_ref[0] = x1

    hn = x1 * lax.rsqrt(jnp.mean(x1 * x1, axis=-1, keepdims=True) + EPS) * ng_ref[...]
    h2 = (hn * (1.0 + _row_select(i, tm, s_len, sc2l_ref[0], sc2c_ref[...]))
          + _row_select(i, tm, s_len, sh2l_ref[0], sh2c_ref[...]))
    h2_ref[0] = h2.astype(BF16)

    h2_hi, h2_lo = _split_bf16(h2)
    logits = _dot3(h2_hi, h2_lo, wrh_ref[...], wrl_ref[...])
    scores = jax.nn.sigmoid(logits.T)
    biased = scores + rb_ref[...]
    rows = [biased[e:e + 1, :] for e in range(N_EXPERTS)]
    gsc = []
    for g in range(N_GROUPS):
        a, b_, c_, d_ = rows[4 * g:4 * g + 4]
        gsc.append(jnp.maximum(jnp.maximum(jnp.maximum(a + b_, a + c_), jnp.maximum(a + d_, b_ + c_)),
                               jnp.maximum(b_ + d_, c_ + d_)))
    best = gsc[0]
    grp = jnp.zeros_like(best, dtype=jnp.int32)
    for g in range(1, N_GROUPS):
        upd = gsc[g] > best
        best = jnp.where(upd, gsc[g], best)
        grp = jnp.where(upd, g, grp)
    eidx = lax.broadcasted_iota(jnp.int32, biased.shape, 0)
    masked = jnp.where((eidx // EXPERTS_PER_GROUP) == grp, biased, -jnp.inf)
    big = jnp.int32(LANES)
    m1 = masked.max(axis=0, keepdims=True)
    i1 = jnp.where(masked == m1, eidx, big).min(axis=0, keepdims=True)
    masked2 = jnp.where(eidx == i1, -jnp.inf, masked)
    m2 = masked2.max(axis=0, keepdims=True)
    i2 = jnp.where(masked2 == m2, eidx, big).min(axis=0, keepdims=True)
    sel1 = eidx == i1
    sel2 = eidx == i2
    s1 = jnp.sum(jnp.where(sel1, scores, 0.0), axis=0, keepdims=True)
    s2 = jnp.sum(jnp.where(sel2, scores, 0.0), axis=0, keepdims=True)
    tot = s1 + s2
    comb_t = jnp.where(sel1, s1 / tot, 0.0) + jnp.where(sel2, s2 / tot, 0.0)
    meta_t = jnp.where(eidx == N_EXPERTS, grp.astype(F32), comb_t)
    combt_ref[...] = meta_t
    comb_ref[0] = meta_t.T


def _merge(o_na, o_f, o_b, z, sc, gates, xs, mods_l, mods_c, norm_g, out_g, sc_w, wna, wdn, wsc, wout, wr, rb,
           *, s_len, tm):
    b, ta, d = xs.shape
    col_z = col_sc = 0
    hb = tm // HALO
    n_hb = ta // HALO
    tok = lambda w, col=0: pl.BlockSpec((1, tm, w), lambda bb, i: (bb, i, col))
    vec = lambda w: pl.BlockSpec((1, w), lambda bb, i: (0, 0))
    bvec = lambda: pl.BlockSpec((1, 1, d), lambda bb, i: (bb, 0, 0))
    full = lambda a: pl.BlockSpec(a.shape, lambda bb, i: (0,) * a.ndim)
    g1l, sh2l, sc2l = mods_l
    g1c, sh2c, sc2c = mods_c
    w3 = 3 * SC_WIDTH
    wr_hi = wr.astype(BF16)
    wr_lo = (wr - wr_hi.astype(F32)).astype(BF16)
    return pl.pallas_call(
        functools.partial(_merge_kernel, tm=tm, s_len=s_len, ta=ta),
        grid=(b, ta // tm),
        in_specs=[tok(NA_WIDTH), tok(DN_WIDTH), tok(DN_WIDTH), tok(DN_WIDTH, col_z), tok(w3, col_sc),
                  pl.BlockSpec((1, HALO, w3), lambda bb, i: (bb, jnp.maximum(i * hb - 1, 0), col_sc)),
                  pl.BlockSpec((1, HALO, w3), lambda bb, i: (bb, jnp.minimum((i + 1) * hb, n_hb - 1), col_sc)),
                  tok(3 * d, 0), tok(d),
                  bvec(), vec(d), bvec(), vec(d), bvec(), vec(d), vec(d), vec(DN_WIDTH),
                  full(sc_w), full(wna), full(wdn), full(wsc), full(wout), full(wr_hi), full(wr_lo), full(rb)],
        out_specs=[tok(d), tok(d), tok(LANES),
                   pl.BlockSpec((LANES, tm), lambda bb, i: (0, bb * (ta // tm) + i))],
        out_shape=[jax.ShapeDtypeStruct((b, ta, d), F32),
                   jax.ShapeDtypeStruct((b, ta, d), BF16),
                   jax.ShapeDtypeStruct((b, ta, LANES), F32),
                   jax.ShapeDtypeStruct((LANES, b * ta), F32)],
        scratch_shapes=[pltpu.VMEM((tm + HALO, SC_WIDTH), F32)],
        compiler_params=_cparams(("parallel", "parallel")),
        name="merge_router",
    )(o_na, o_f, o_b, z, sc, sc, sc, gates, xs, g1l, g1c, sh2l, sh2c, sc2l, sc2c, norm_g, out_g,
      sc_w, wna, wdn, wsc, wout, wr_hi, wr_lo, rb)


MOE_CAP = 256
MOE_HALVES = 2
MOE_EPS = 2
MOE_STEPS = N_EXPERTS // MOE_EPS


def _moe_kernel(h_ref, comb_ref, combt_ref, lt_ref, gt_ref, wg_ref, wu_ref, wd_ref, x_ref, g2l_ref, g2c_ref,
                o_ref, acc, rank_r, rank_c, hs_scr, cws_scr, ys_scr, *, half, ta, s_len):
    i = pl.program_id(0)
    s = pl.program_id(1)
    halves = [slice(t * half, (t + 1) * half) for t in range(MOE_HALVES)]
    lane = lax.broadcasted_iota(jnp.int32, (half, LANES), 1)
    combs = [comb_ref[0, hv, :] for hv in halves]
    grp_rows = [combt_ref[N_EXPERTS:N_EXPERTS + 1, hv] for hv in halves]
    grp_cols = [jnp.sum(jnp.where(lane == N_EXPERTS, cb, 0.0), axis=-1, keepdims=True) for cb in combs]

    @pl.when(s == 0)
    def _():
        acc[...] = jnp.zeros_like(acc)
        sub = lax.broadcasted_iota(jnp.int32, (8, half), 0).astype(F32)
        lane_f = lane.astype(F32)
        for t, hv in enumerate(halves):
            before = _dot(jnp.where(sub == grp_rows[t], 1.0, 0.0).astype(BF16), lt_ref[...])
            rank_r[:, hv] = jnp.broadcast_to(
                jnp.sum(jnp.where(sub == grp_rows[t], before, 0.0), axis=0, keepdims=True), (8, half))
            before = _dot(gt_ref[...], jnp.where(lane_f == grp_cols[t], 1.0, 0.0).astype(BF16))
            rank_c[hv, :] = jnp.broadcast_to(
                jnp.sum(jnp.where(lane_f == grp_cols[t], before, 0.0), axis=-1, keepdims=True),
                (half, LANES))

    @pl.when(s < MOE_STEPS)
    def _():
        steps_per_group = EXPERTS_PER_GROUP // MOE_EPS
        g = s // steps_per_group
        first = s % steps_per_group == 0
        gf = g.astype(F32)
        member_rows = [gr == gf for gr in grp_rows]
        member_cols = [gc == gf for gc in grp_cols]
        cnt = functools.reduce(jnp.maximum, [jnp.sum(jnp.where(mr, 1.0, 0.0)) for mr in member_rows])
        splits = [_split_bf16(cb) for cb in combs]
        lane_c = lax.broadcasted_iota(jnp.int32, (MOE_HALVES * MOE_CAP, LANES), 1)

        def gather_rows(k):
            slot_c = (lax.broadcasted_iota(jnp.int32, (MOE_CAP, 1), 0) + k * MOE_CAP).astype(F32)
            hs, cws = [], []
            for t, hv in enumerate(halves):
                pick = jnp.where(member_rows[t] & (rank_r[0:1, hv] == slot_c), 1.0, 0.0).astype(BF16)
                hs.append(_dot(pick, h_ref[0, hv, :]).astype(BF16))
                cws.append(_dot(pick, splits[t][0]) + _dot(pick, splits[t][1]))
            return jnp.concatenate(hs, axis=0), jnp.concatenate(cws, axis=0)

        def run_experts(hs, cws):
            gates = [_dot(hs, wg_ref[0, e]) for e in range(MOE_EPS)]
            ups = [_dot(hs, wu_ref[0, e]) for e in range(MOE_EPS)]
            ys = None
            for e in range(MOE_EPS):
                cw = jnp.sum(jnp.where(lane_c == s * MOE_EPS + e, cws, 0.0), axis=-1, keepdims=True)
                part = _dot((_silu(gates[e]) * ups[e] * cw).astype(BF16), wd_ref[0, e])
                ys = part if ys is None else ys + part
            return ys

        def scatter_rows(k, ys):
            slot_r = (lax.broadcasted_iota(jnp.int32, (1, MOE_CAP), 1) + k * MOE_CAP).astype(F32)
            ys_hi, ys_lo = _split_bf16(ys)
            for t, hv in enumerate(halves):
                place = jnp.where(member_cols[t] & (rank_c[hv, 0:1] == slot_r), 1.0, 0.0).astype(BF16)
                rows = slice(t * MOE_CAP, (t + 1) * MOE_CAP)
                acc[hv, :] += _dot(place, ys_hi[rows]) + _dot(place, ys_lo[rows])

        @pl.when(cnt > 0)
        def _():
            @pl.when(first)
            def _():
                hs, cws = gather_rows(0)
                hs_scr[...] = hs
                cws_scr[...] = cws

            ys = run_experts(hs_scr[...], cws_scr[...])

            @pl.when(first)
            def _():
                ys_scr[...] = ys

            @pl.when(jnp.logical_not(first))
            def _():
                scatter_rows(0, ys_scr[...] + ys)

        for k in range(1, -(-half // MOE_CAP)):
            @pl.when(cnt > k * MOE_CAP)
            def _(k=k):
                hs, cws = gather_rows(k)
                scatter_rows(k, run_experts(hs, cws))

    @pl.when(s >= MOE_STEPS - 1)
    def _():
        t = s - (MOE_STEPS - 1)
        row = i * (MOE_HALVES * half) + t * half + lax.broadcasted_iota(jnp.int32, (half, 1), 0)
        g2 = jnp.broadcast_to(g2c_ref[...], (half, g2c_ref.shape[-1]))
        for bb in range(g2l_ref.shape[0]):
            g2 = jnp.where((row // ta == bb) & (row % ta < s_len), g2l_ref[bb:bb + 1, :], g2)
        start = pl.multiple_of(t * half, half)
        o_ref[0] = x_ref[0] + g2 * acc[pl.ds(start, half), :]


def _moe(h2, comb, comb_t, wg, wu, wd, x1, g2l, g2c, *, layer, s_len, half):
    b, ta, d = x1.shape
    n = b * ta
    tile = MOE_HALVES * half
    flat = lambda a: a.reshape(1, n, a.shape[-1])
    tok = lambda w: pl.BlockSpec((1, tile, w), lambda i, s: (0, i, 0))
    half_blk = lambda: pl.BlockSpec(
        (1, half, d), lambda i, s: (0, MOE_HALVES * i + jnp.where(s >= MOE_STEPS, 1, 0), 0))
    pair = lambda a: pl.BlockSpec(
        (1, MOE_EPS) + a.shape[2:], lambda i, s: (layer, jnp.minimum(s, MOE_STEPS - 1), 0, 0))
    tri = lambda: pl.BlockSpec((half, half), lambda i, s: (0, 0))
    earlier = np.arange(half)[:, None] < np.arange(half)[None, :]
    lt = jnp.asarray(earlier, BF16)
    gt = jnp.asarray(earlier.T, BF16)
    rows = MOE_HALVES * MOE_CAP
    out = pl.pallas_call(
        functools.partial(_moe_kernel, half=half, ta=ta, s_len=s_len),
        grid=(n // tile, MOE_STEPS + 1),
        in_specs=[tok(d), tok(LANES),
                  pl.BlockSpec((LANES, tile), lambda i, s: (0, i)),
                  tri(), tri(),
                  pair(wg), pair(wu), pair(wd),
                  half_blk(),
                  pl.BlockSpec((b, d), lambda i, s: (0, 0)),
                  pl.BlockSpec((1, d), lambda i, s: (0, 0))],
        out_specs=half_blk(),
        out_shape=jax.ShapeDtypeStruct((1, n, d), F32),
        scratch_shapes=[pltpu.VMEM((tile, d), F32), pltpu.VMEM((8, tile), F32),
                        pltpu.VMEM((tile, LANES), F32), pltpu.VMEM((rows, d), BF16),
                        pltpu.VMEM((rows, LANES), F32), pltpu.VMEM((rows, d), F32)],
        compiler_params=_cparams(("parallel", "arbitrary")),
        name="moe_grouped",
    )(flat(h2), flat(comb), comb_t, lt, gt, wg, wu, wd, flat(x1), g2l.reshape(b, d), g2c)
    return out.reshape(b, ta, d)


def _rope_tables(s_len, ctx_len):
    t = np.arange(s_len)
    n_freq = DN_HEAD_DIM // 4
    inv_freq = (np.float32(ROPE_BASE) ** (-np.arange(n_freq, dtype=np.float32) / n_freq)).astype(np.float32)
    pos = np.stack([t // GRID_W, t % GRID_W], axis=-1).astype(np.float32)
    ang = (pos[..., None] * inv_freq).astype(np.float32)
    cos, sin = np.cos(ang.astype(np.float64)), np.sin(ang.astype(np.float64))
    rc = np.concatenate([cos[:, 0], cos[:, 0], cos[:, 1], cos[:, 1]], axis=-1)
    rs = np.concatenate([-sin[:, 0], sin[:, 0], -sin[:, 1], sin[:, 1]], axis=-1)
    rc = np.concatenate([rc, np.ones((ctx_len, LANES))], axis=0)
    rs = np.concatenate([rs, np.zeros((ctx_len, LANES))], axis=0)
    return jnp.asarray(rc, F32), jnp.asarray(rs, F32)


def _pick_tile(ta, candidates):
    for t in candidates:
        if ta % t == 0:
            return t
    raise ValueError(f"no row tile for {ta} tokens")


def kernel(x, c, ctx, c_ctx, w_ada, b_ada, norm_mix_g, norm_ffn_g, w_in, na_q_norm_g, na_k_norm_g, na_rpb, dn_conv_w, dn_a_log, dn_dt_bias, dn_out_norm_g, sc_conv_w, w_branch_na, w_branch_dn, w_branch_sc, w_out, w_router, router_bias, moe_w_gate, moe_w_up, moe_w_down):
    b, s_len, d = x.shape
    ctx_len = ctx.shape[1]
    depth = w_ada.shape[0]
    ta = s_len + ctx_len
    assert ctx_len == NA_BLK and s_len % (WIN_R * GRID_W) == 0 and s_len // NA_BLK >= 4
    assert ctx_len % PREP_TILE == 0 and b + 1 <= 8
    tm_proj = _pick_tile(ta, (768, 512, 256))
    tm_merge = _pick_tile(ta, (384, 256))
    moe_half = _pick_tile(b * ta // MOE_HALVES, (768, 640, 512, 384, 256))

    xs = jnp.concatenate([x, ctx], axis=1)
    c_rows = jnp.zeros((8, d), F32).at[:b].set(c).at[b].set(c_ctx)
    mod = _modulation(c_rows, w_ada, b_ada)

    sizes = (3 * NA_WIDTH, 3 * DN_WIDTH, DN_WIDTH, 4 * DN_HEADS, 3 * SC_WIDTH, 3 * d)
    cuts = [int(v) for v in np.cumsum((0,) + sizes)]
    seg = lambda l, k: lax.slice(w_in, (l, 0, cuts[k]), (l + 1, d, cuts[k + 1]))[0]
    wg_all, wu_all, wd_all = moe_w_gate.astype(BF16), moe_w_up.astype(BF16), moe_w_down.astype(BF16)

    rope_c, rope_s = _rope_tables(s_len, ctx_len)
    bd = jnp.asarray(np.kron(np.eye(NA_HEADS), np.ones((NA_HEAD_DIM, NA_HEAD_DIM))), BF16)
    nc = ta // DN_CHUNK
    wr = jnp.zeros((d, LANES), F32).at[:, :N_EXPERTS].set(w_router)
    rb = jnp.zeros((LANES, 1), F32).at[:N_EXPERTS, 0].set(router_bias)

    for l in range(depth):
        m = mod[l].reshape(8, N_MOD, d)
        lat = lambda k: m[:b, k][:, None, :]
        cx = lambda k: m[b:b + 1, k]
        w_ab = jnp.zeros((d, LANES), F32).at[:, :4 * DN_HEADS].set(seg(l, 3))
        gqk = jnp.concatenate([jnp.tile(na_q_norm_g[l], NA_HEADS) * (NA_HEAD_DIM ** -0.5),
                               jnp.tile(na_k_norm_g[l], NA_HEADS)])[None, :]
        gates, na_q, na_k, na_v, dn_qkv, sc_p, dn_z, ab = _projection(
            xs, (lat(0), lat(1)), (cx(0), cx(1)), norm_mix_g[l][None, :],
            seg(l, 5).astype(BF16), seg(l, 0).astype(BF16), seg(l, 1).astype(BF16),
            seg(l, 4).astype(BF16), seg(l, 2).astype(BF16), w_ab, gqk, bd, s_len=s_len, tm=tm_proj)

        o_na = _neighbourhood_attention(na_q, na_k, na_v, _na_bias_table(na_rpb[l]), s_len=s_len)

        alog_row = jnp.zeros((1, LANES), F32).at[0, :2 * DN_HEADS].set(dn_a_log[l].reshape(-1))
        dtb_row = jnp.zeros((1, LANES), F32).at[0, :2 * DN_HEADS].set(dn_dt_bias[l].reshape(-1))
        qd, kd, vd, gc, bc = _dn_prep(dn_qkv, ab, dn_conv_w[l], rope_c, rope_s, alog_row, dtb_row,
                                      s_len=s_len)
        g8 = gc[..., :2 * DN_HEADS].reshape(b, nc, DN_CHUNK, 2, DN_HEADS)
        g_rows = jnp.transpose(g8, (0, 1, 3, 4, 2)).reshape(b, nc, 2, CH4)
        g_last = jnp.stack([g8[:, :, DN_CHUNK - 1, 0], g8[:, :, 0, 1]], axis=2)
        gl_wide = jnp.repeat(g_last, LANES, axis=-1)[:, :, :, None, :]
        wq, u, a, kgt = _dn_local(qd, kd, vd, gc, bc, g_rows)
        o_f, o_b = _dn_scan(wq, u, a, kgt, gl_wide, s_len=s_len)

        x1, h2, comb, comb_t = _merge(
            o_na, o_f, o_b, dn_z, sc_p, gates, xs, (lat(2), lat(3), lat(4)), (cx(2), cx(3), cx(4)),
            norm_ffn_g[l][None, :], jnp.tile(dn_out_norm_g[l], DN_HEADS)[None, :], sc_conv_w[l],
            w_branch_na[l].astype(BF16), w_branch_dn[l].astype(BF16), w_branch_sc[l].astype(BF16),
            w_out[l].astype(BF16), wr, rb, s_len=s_len, tm=tm_merge)

        xs = _moe(h2, comb, comb_t, wg_all, wu_all, wd_all, x1, lat(5), cx(5), layer=l, s_len=s_len,
                  half=moe_half)

    return xs[:, :s_len]
```

```python
import functools
import math

import numpy as np
import jax
import jax.numpy as jnp
from jax import lax
from jax.experimental import pallas as pl
from jax.experimental.pallas import tpu as pltpu

F32 = jnp.float32
BF16 = jnp.bfloat16
HIGHEST = lax.Precision.HIGHEST

GRID_W = 64
NA_HEADS = 8
NA_HEAD_DIM = 64
NA_WIDTH = NA_HEADS * NA_HEAD_DIM
WIN_R = 8
WIN_C = 16
DN_HEADS = 4
DN_HEAD_DIM = 128
DN_WIDTH = DN_HEADS * DN_HEAD_DIM
DN_CONV = 5
DN_CHUNK = 64
ROPE_BASE = 10000.0
SC_WIDTH = 512
SC_CONV = 3
N_EXPERTS = 16
N_GROUPS = 4
EXPERTS_PER_GROUP = N_EXPERTS // N_GROUPS
D_EXPERT = 512
N_MOD = 6
EPS = 1e-6
NEG_INF = -1e30

LANES = 128
HALO = 16
PREP_TILE = 256
VMEM_LIMIT = 56 * 1024 * 1024

N_PROJ_TILE = 512


def _cparams(sem):
    return pltpu.CompilerParams(dimension_semantics=sem, vmem_limit_bytes=VMEM_LIMIT)


def _dot(a, b):
    return jnp.dot(a, b, preferred_element_type=F32)


def _dot_nt(a, b):
    return lax.dot_general(a, b, (((1,), (1,)), ((), ())), preferred_element_type=F32)


def _silu(x):
    return x * jax.nn.sigmoid(x)


def _row_select(i, tm, s_len, lat, ctx):
    row = i * tm + lax.broadcasted_iota(jnp.int32, (tm, 1), 0)
    return jnp.where(row >= s_len, ctx, lat)


def _mod_kernel(c_ref, w_ref, b_ref, o_ref):
    c_hi, c_lo = _split_bf16(_silu(c_ref[...]))
    w_hi, w_lo = _split_bf16(w_ref[0])
    o_ref[0] = _dot3(c_hi, c_lo, w_hi, w_lo) + b_ref[0]


def _modulation(c_rows, w_ada, b_ada):
    depth, d, n = w_ada.shape
    tn = 512
    return pl.pallas_call(
        _mod_kernel,
        grid=(depth, n // tn),
        in_specs=[pl.BlockSpec((8, d), lambda l, j: (0, 0)),
                  pl.BlockSpec((1, d, tn), lambda l, j: (l, 0, j)),
                  pl.BlockSpec((1, 1, tn), lambda l, j: (l, 0, j))],
        out_specs=pl.BlockSpec((1, 8, tn), lambda l, j: (l, 0, j)),
        out_shape=jax.ShapeDtypeStruct((depth, 8, n), F32),
        compiler_params=_cparams(("parallel", "parallel")),
        name="adaln_mod",
    )(c_rows, w_ada, b_ada.reshape(depth, 1, n))


def _split_bf16(a):
    hi = a.astype(BF16)
    return hi, (a - hi.astype(F32)).astype(BF16)


def _dot3(a_hi, a_lo, b_hi, b_lo):
    return _dot(a_hi, b_hi) + _dot(a_lo, b_hi) + _dot(a_hi, b_lo)


def _modulated_norm(x_ref, shl_ref, scl_ref, shc_ref, scc_ref, g_ref, *, tm, s_len):
    i = pl.program_id(1)
    x = x_ref[0]
    y = x * lax.rsqrt(jnp.mean(x * x, axis=-1, keepdims=True) + EPS) * g_ref[...]
    scale = _row_select(i, tm, s_len, scl_ref[0], scc_ref[...])
    shift = _row_select(i, tm, s_len, shl_ref[0], shc_ref[...])
    return y * (1.0 + scale) + shift


def _proj_gate_kernel(x_ref, shl_ref, scl_ref, shc_ref, scc_ref, g_ref, w_ref, gate_ref, *, tm, s_len):
    h = _modulated_norm(x_ref, shl_ref, scl_ref, shc_ref, scc_ref, g_ref, tm=tm, s_len=s_len).astype(BF16)
    for c0 in range(0, w_ref.shape[1], N_PROJ_TILE):
        cols = slice(c0, c0 + N_PROJ_TILE)
        gate_ref[0, :, cols] = jax.nn.sigmoid(_dot(h, w_ref[:, cols])).astype(BF16)


def _proj_mix_kernel(x_ref, shl_ref, scl_ref, shc_ref, scc_ref, g_ref, wna_ref, wdn_ref, wsc_ref, wz_ref,
                     wabh_ref, wabl_ref, gqk_ref, bd_ref,
                     q_ref, k_ref, v_ref, dn_ref, sc_ref, z_ref, ab_ref, *, tm, s_len):
    h = _modulated_norm(x_ref, shl_ref, scl_ref, shc_ref, scc_ref, g_ref, tm=tm, s_len=s_len)
    h_hi, h_lo = _split_bf16(h)
    ab_ref[0] = _dot3(h_hi, h_lo, wabh_ref[...], wabl_ref[...])
    for part, out in enumerate((q_ref, k_ref)):
        cols = slice(part * NA_WIDTH, (part + 1) * NA_WIDTH)
        a = _dot(h_hi, wna_ref[:, cols])
        ss = _dot((a * a).astype(BF16), bd_ref[...])
        out[0] = (a * lax.rsqrt(ss * (1.0 / NA_HEAD_DIM) + EPS) * gqk_ref[:, cols]).astype(BF16)
    v_ref[0] = _dot(h_hi, wna_ref[:, 2 * NA_WIDTH:]).astype(BF16)
    for w_ref, out in ((wdn_ref, dn_ref), (wsc_ref, sc_ref), (wz_ref, z_ref)):
        for c0 in range(0, w_ref.shape[1], N_PROJ_TILE):
            cols = slice(c0, c0 + N_PROJ_TILE)
            out[0, :, cols] = _dot(h_hi, w_ref[:, cols]).astype(BF16)


def _projection(xs, mod_l, mod_c, norm_g, w_gate, w_na, w_dn, w_sc, w_z, w_ab, gqk, bd, *, s_len, tm):
    b, ta, d = xs.shape
    shl, scl = mod_l
    shc, scc = mod_c
    wab_hi, wab_lo = _split_bf16(w_ab)
    tok = lambda w: pl.BlockSpec((1, tm, w), lambda bb, i: (bb, i, 0))
    bvec = lambda: pl.BlockSpec((1, 1, d), lambda bb, i: (bb, 0, 0))
    full = lambda a: pl.BlockSpec(a.shape, lambda bb, i: (0,) * a.ndim)
    head = [tok(d), bvec(), bvec(), full(shc), full(scc), full(norm_g)]
    gates = pl.pallas_call(
        functools.partial(_proj_gate_kernel, tm=tm, s_len=s_len),
        grid=(b, ta // tm),
        in_specs=head + [full(w_gate)],
        out_specs=tok(w_gate.shape[1]),
        out_shape=jax.ShapeDtypeStruct((b, ta, w_gate.shape[1]), BF16),
        compiler_params=_cparams(("parallel", "parallel")),
        name="proj_gates",
    )(xs, shl, scl, shc, scc, norm_g, w_gate)
    consts = (w_na, w_dn, w_sc, w_z, wab_hi, wab_lo, gqk, bd)
    widths = (NA_WIDTH, NA_WIDTH, NA_WIDTH, w_dn.shape[1], w_sc.shape[1], w_z.shape[1])
    outs = pl.pallas_call(
        functools.partial(_proj_mix_kernel, tm=tm, s_len=s_len),
        grid=(b, ta // tm),
        in_specs=head + [full(a) for a in consts],
        out_specs=[tok(w) for w in widths] + [tok(LANES)],
        out_shape=[jax.ShapeDtypeStruct((b, ta, w), BF16) for w in widths]
        + [jax.ShapeDtypeStruct((b, ta, LANES), F32)],
        compiler_params=_cparams(("parallel", "parallel")),
        name="proj_mixers",
    )(xs, shl, scl, shc, scc, norm_g, *consts)
    return (gates,) + tuple(outs)


NA_QROWS = 4
NA_BLK = NA_QROWS * GRID_W
WIN_TOK = WIN_R * GRID_W


def _stack_two_heads(q2):
    lane = lax.broadcasted_iota(jnp.int32, q2.shape, 1)
    zero = jnp.zeros_like(q2)
    return jnp.concatenate([jnp.where(lane < NA_HEAD_DIM, q2, zero),
                            jnp.where(lane >= NA_HEAD_DIM, q2, zero)], axis=0)


def _unstack_two_heads(o2):
    m = o2.shape[0] // 2
    lane = lax.broadcasted_iota(jnp.int32, (m, o2.shape[1]), 1)
    return jnp.where(lane < NA_HEAD_DIM, o2[:m], o2[m:])


def _na_kernel(q_ref, kp_ref, kc_ref, kn_ref, vp_ref, vc_ref, vn_ref, kx_ref, vx_ref, bias_ref, o_ref,
               kbuf, vbuf, *, n_rows):
    i = pl.program_id(1)
    n_lat = n_rows // NA_QROWS

    @pl.when(i < n_lat)
    def _():
        for slot, (kr, vr) in enumerate(((kp_ref, vp_ref), (kc_ref, vc_ref), (kn_ref, vn_ref))):
            kbuf[slot * NA_BLK:(slot + 1) * NA_BLK, :] = kr[0]
            vbuf[slot * NA_BLK:(slot + 1) * NA_BLK, :] = vr[0]
        win = []
        for j in range(NA_QROWS):
            r = i * NA_QROWS + j
            start = jnp.clip(r - WIN_R // 2, 0, n_rows - WIN_R)
            win.append((pl.multiple_of((start - (i - 1) * NA_QROWS) * GRID_W, GRID_W), r - start))

        rows2 = 2 * GRID_W
        pairs = [slice(hp * LANES, (hp + 1) * LANES) for hp in range(NA_HEADS // 2)]
        logits = []
        for hp, lanes in enumerate(pairs):
            qs = jnp.concatenate([_stack_two_heads(q_ref[0, j * GRID_W:(j + 1) * GRID_W, lanes])
                                  for j in range(NA_QROWS)], axis=0)
            s_c = _dot_nt(qs, kx_ref[0, :, lanes])
            s_w = jnp.concatenate(
                [_dot_nt(qs[j * rows2:(j + 1) * rows2], kbuf[pl.ds(win[j][0], WIN_TOK), lanes])
                 + bias_ref[win[j][1], hp] for j in range(NA_QROWS)], axis=0)
            logits.append((s_w, s_c))
        probs = []
        for s_w, s_c in logits:
            m = jnp.maximum(s_w.max(axis=-1, keepdims=True), s_c.max(axis=-1, keepdims=True))
            p_w = jnp.exp(s_w - m)
            p_c = jnp.exp(s_c - m)
            denom = p_w.sum(axis=-1, keepdims=True) + p_c.sum(axis=-1, keepdims=True)
            probs.append((p_w.astype(BF16), p_c.astype(BF16), denom))
        for lanes, (p_wb, p_cb, denom) in zip(pairs, probs):
            o = _dot(p_cb, vx_ref[0, :, lanes]) + jnp.concatenate(
                [_dot(p_wb[j * rows2:(j + 1) * rows2], vbuf[pl.ds(win[j][0], WIN_TOK), lanes])
                 for j in range(NA_QROWS)], axis=0)
            o = o / denom
            for j in range(NA_QROWS):
                o_ref[0, j * GRID_W:(j + 1) * GRID_W, lanes] = _unstack_two_heads(
                    o[j * rows2:(j + 1) * rows2]).astype(BF16)

    @pl.when(i == n_lat)
    def _():
        for hp in range(NA_HEADS // 2):
            lanes = slice(hp * LANES, (hp + 1) * LANES)
            s = _dot_nt(_stack_two_heads(q_ref[0, :, lanes]), kx_ref[0, :, lanes])
            p = jnp.exp(s - s.max(axis=-1, keepdims=True))
            o = _dot(p.astype(BF16), vx_ref[0, :, lanes]) / p.sum(axis=-1, keepdims=True)
            o_ref[0, :, lanes] = _unstack_two_heads(o).astype(BF16)


def _neighbourhood_attention(q, k, v, bias_tab, *, s_len):
    b, ta, _ = q.shape
    n_rows = s_len // GRID_W
    n_lat = n_rows // NA_QROWS
    blk = lambda off: pl.BlockSpec((1, NA_BLK, NA_WIDTH), lambda bb, i: (bb, jnp.clip(i + off, 0, n_lat - 1), 0))
    cx = lambda: pl.BlockSpec((1, NA_BLK, NA_WIDTH), lambda bb, i: (bb, n_lat, 0))
    own = lambda: pl.BlockSpec((1, NA_BLK, NA_WIDTH), lambda bb, i: (bb, i, 0))
    return pl.pallas_call(
        functools.partial(_na_kernel, n_rows=n_rows),
        grid=(b, n_lat + 1),
        in_specs=[own(), blk(-1), blk(0), blk(1), blk(-1), blk(0), blk(1), cx(), cx(),
                  pl.BlockSpec(bias_tab.shape, lambda bb, i: (0, 0, 0, 0))],
        out_specs=own(),
        out_shape=jax.ShapeDtypeStruct((b, ta, NA_WIDTH), BF16),
        scratch_shapes=[pltpu.VMEM((3 * NA_BLK, NA_WIDTH), BF16),
                        pltpu.VMEM((3 * NA_BLK, NA_WIDTH), BF16)],
        compiler_params=_cparams(("parallel", "arbitrary")),
        name="na_attention",
    )(q, k, k, k, v, v, v, k, v, bias_tab)


def _na_bias_table(rpb):
    qc = np.arange(GRID_W)[:, None]
    kc = np.arange(GRID_W)[None, :]
    win_c0 = np.clip(qc - WIN_C // 2, 0, GRID_W - WIN_C)
    in_win = (kc >= win_c0) & (kc < win_c0 + WIN_C)
    rel_c = np.clip(kc - qc + WIN_C - 1, 0, 2 * WIN_C - 2)
    pick_c = (rel_c[..., None] == np.arange(2 * WIN_C - 1)).astype(np.float32)
    by_col = jnp.einsum('hab,qkb->haqk', rpb.astype(F32), pick_c, precision=HIGHEST)
    by_col = jnp.where(in_win[None, None], by_col, NEG_INF)
    tab = jnp.stack([by_col[:, WIN_R - 1 - v:2 * WIN_R - 1 - v] for v in range(WIN_R)], axis=0)
    tab = jnp.transpose(tab, (0, 1, 3, 2, 4))
    return tab.reshape(WIN_R, NA_HEADS // 2, 2 * GRID_W, WIN_R * GRID_W)


def _dn_prep_kernel(x_ref, hp_ref, hn_ref, w_ref, rc_ref, rs_ref, ab_ref, alog_ref, dtb_ref,
                    q_ref, k_ref, v_ref, g_ref, beta_ref, xpad, *, s_len, ta):
    i = pl.program_id(1)
    t0 = i * PREP_TILE
    has_prev = jnp.logical_and(t0 != 0, t0 != s_len).astype(F32)
    has_next = jnp.logical_and(t0 + PREP_TILE != s_len, t0 + PREP_TILE != ta).astype(F32)
    half = HALO // 2
    xpad[0:half, :] = hp_ref[0].astype(F32)[half:, :] * has_prev
    xpad[half:half + PREP_TILE, :] = x_ref[0].astype(F32)
    xpad[half + PREP_TILE:, :] = hn_ref[0].astype(F32)[:half, :] * has_next
    xp = xpad[...]
    n_pad = xp.shape[0]
    y = jnp.zeros((PREP_TILE, x_ref.shape[-1]), F32)
    for tap in range(DN_CONV):
        shifted = xp if tap == DN_CONV // 2 else pltpu.roll(xp, (DN_CONV // 2 - tap) % n_pad, 0)
        y = y + w_ref[tap:tap + 1, :] * shifted[half:half + PREP_TILE]
    y = _silu(y)

    lane = lax.broadcasted_iota(jnp.int32, (PREP_TILE, LANES), 1)
    first_half = (lane % (LANES // 2)) < (LANES // 4)
    rc = rc_ref[...]
    rs = rs_ref[...]

    def norm_rope(u, scale):
        u = u * (lax.rsqrt(jnp.sum(u * u, axis=-1, keepdims=True) + EPS) * scale)
        partner = jnp.where(first_half, pltpu.roll(u, LANES - LANES // 4, 1), pltpu.roll(u, LANES // 4, 1))
        return u * rc + partner * rs

    for h in range(DN_HEADS):
        lanes = slice(h * LANES, (h + 1) * LANES)
        qh = norm_rope(y[:, h * LANES:(h + 1) * LANES], DN_HEAD_DIM ** -0.5)
        kh = norm_rope(y[:, DN_WIDTH + h * LANES:DN_WIDTH + (h + 1) * LANES], 1.0)
        q_ref[0, :, lanes] = qh.astype(BF16)
        k_ref[0, :, lanes] = kh.astype(BF16)
    v_ref[0] = y[:, 2 * DN_WIDTH:].astype(BF16)

    ab = ab_ref[0]
    z = ab + dtb_ref[...]
    softplus = jnp.maximum(z, 0.0) + jnp.log1p(jnp.exp(-jnp.abs(z)))
    g = -jnp.exp(alog_ref[...]) * softplus
    g = jnp.where(lane < 2 * DN_HEADS, g, 0.0)
    beta = jax.nn.sigmoid(pltpu.roll(ab, LANES - 2 * DN_HEADS, 1))
    beta_ref[0] = jnp.where(lane < 2 * DN_HEADS, beta, 0.0)

    r = lax.broadcasted_iota(jnp.int32, (PREP_TILE, PREP_TILE), 0)
    c = lax.broadcasted_iota(jnp.int32, (PREP_TILE, PREP_TILE), 1)
    same = (r // DN_CHUNK) == (c // DN_CHUNK)
    tri_f = jnp.where(same & (c <= r), 1.0, 0.0).astype(BF16)
    tri_b = jnp.where(same & (c >= r), 1.0, 0.0).astype(BF16)
    g1 = g.astype(BF16)
    rem = g - g1.astype(F32)
    g2 = rem.astype(BF16)
    g3 = (rem - g2.astype(F32)).astype(BF16)
    cs_f = _dot(tri_f, g1) + _dot(tri_f, g2) + _dot(tri_f, g3)
    cs_b = _dot(tri_b, g1) + _dot(tri_b, g2) + _dot(tri_b, g3)
    g_ref[0] = jnp.where(lane < DN_HEADS, cs_f, cs_b)


def _dn_prep(p, ab, conv_w, rope_c, rope_s, alog_row, dtb_row, *, s_len):
    b, ta, w3 = p.shape
    col_dn = 0
    n_t = ta // PREP_TILE
    hb = PREP_TILE // HALO
    n_hb = ta // HALO
    tok = lambda: pl.BlockSpec((1, PREP_TILE, DN_WIDTH), lambda bb, i: (bb, i, 0))
    sca = lambda: pl.BlockSpec((1, PREP_TILE, LANES), lambda bb, i: (bb, i, 0))
    row = lambda: pl.BlockSpec((1, LANES), lambda bb, i: (0, 0))
    return pl.pallas_call(
        functools.partial(_dn_prep_kernel, s_len=s_len, ta=ta),
        grid=(b, n_t),
        in_specs=[pl.BlockSpec((1, PREP_TILE, w3), lambda bb, i: (bb, i, col_dn)),
                  pl.BlockSpec((1, HALO, w3), lambda bb, i: (bb, jnp.maximum(i * hb - 1, 0), col_dn)),
                  pl.BlockSpec((1, HALO, w3), lambda bb, i: (bb, jnp.minimum((i + 1) * hb, n_hb - 1), col_dn)),
                  pl.BlockSpec((DN_CONV, w3), lambda bb, i: (0, 0)),
                  pl.BlockSpec((PREP_TILE, LANES), lambda bb, i: (i, 0)),
                  pl.BlockSpec((PREP_TILE, LANES), lambda bb, i: (i, 0)),
                  sca(), row(), row()],
        out_specs=[tok(), tok(), tok(), sca(), sca()],
        out_shape=[jax.ShapeDtypeStruct((b, ta, DN_WIDTH), BF16)] * 3
        + [jax.ShapeDtypeStruct((b, ta, LANES), F32)] * 2,
        scratch_shapes=[pltpu.VMEM((PREP_TILE + HALO, w3), F32)],
        compiler_params=_cparams(("parallel", "parallel")),
        name="dn_prep",
    )(p, p, p, conv_w, rope_c, rope_s, ab, alog_row, dtb_row)


CH4 = DN_HEADS * DN_CHUNK


DN_CPS = 4
DN_PAIRS = DN_HEADS // 2
PAIR = 2 * DN_CHUNK


def _stack_pair(x, p):
    return jnp.concatenate([x[:, (2 * p + h) * LANES:(2 * p + h + 1) * LANES] for h in range(2)], axis=0)


def _dn_local_kernel(q_ref, k_ref, v_ref, gc_ref, bc_ref, gr_ref, wq_ref, u_ref, a_ref, kgt_ref):
    r = lax.broadcasted_iota(jnp.int32, (PAIR, PAIR), 0)
    c = lax.broadcasted_iota(jnp.int32, (PAIR, PAIR), 1)
    same = (r // DN_CHUNK) == (c // DN_CHUNK)
    chains = []
    for cc in range(DN_CPS):
        toks = slice(cc * DN_CHUNK, (cc + 1) * DN_CHUNK)
        gc = gc_ref[0, toks, :]
        bc = bc_ref[0, toks, :]
        for p in range(DN_PAIRS):
            rows = slice(p * PAIR, (p + 1) * PAIR)
            kst = _stack_pair(k_ref[0, toks, :], p)
            qst = _stack_pair(q_ref[0, toks, :], p)
            vst = _stack_pair(v_ref[0, toks, :], p)
            kq = _dot_nt(jnp.concatenate([kst, qst], axis=0), kst)
            kk = kq[:PAIR]
            qk = kq[PAIR:]
            kf = kst.astype(F32)
            for d in range(2):
                later = (r >= c) if d == 0 else (r <= c)
                incl = same & later
                strict = incl & (r != c)
                lane0 = d * DN_HEADS + 2 * p
                col = lambda a: jnp.concatenate([a[:, lane0 + h:lane0 + h + 1] for h in range(2)], axis=0)
                g_col = col(gc)
                b_col = col(bc)
                last = DN_CHUNK - 1 if d == 0 else 0
                gl_col = jnp.concatenate(
                    [jnp.broadcast_to(gc[last:last + 1, lane0 + h:lane0 + h + 1], (DN_CHUNK, 1))
                     for h in range(2)], axis=0)
                g_row = gr_ref[0, cc, d:d + 1, rows]
                decay = jnp.exp(jnp.where(incl, g_col - g_row, NEG_INF))
                pw = jnp.where(strict, -(kk * decay * b_col), 0.0)
                z = jnp.concatenate([vst.astype(F32) * b_col, kf * (b_col * jnp.exp(g_col))], axis=1)
                chains.append([pw, z, (cc, d, rows)])
                wq_ref[0, cc, d, CH4 + p * PAIR:CH4 + (p + 1) * PAIR, :] = (
                    qst.astype(F32) * jnp.exp(g_col)).astype(BF16)
                a_ref[0, cc, d, p] = (qk * decay).astype(BF16)
                kgt_ref[0, cc, d, :, rows] = (kf * jnp.exp(gl_col - g_col)).T.astype(BF16)

    for step in range(6):
        if step:
            for chain in chains:
                pwb = chain[0].astype(BF16)
                chain[0] = _dot(pwb, pwb)
        for chain in chains:
            pw, z, _ = chain
            chain[1] = z + _dot(pw.astype(BF16), z.astype(BF16))
    for _, z, (cc, d, rows) in chains:
        wq_ref[0, cc, d, rows, :] = z[:, LANES:].astype(BF16)
        u_ref[0, cc, d, rows, :] = z[:, :LANES]


def _dn_local(qd, kd, vd, gc, bc, g_rows):
    b, ta, _ = qd.shape
    nc = ta // DN_CHUNK
    tok = lambda w: pl.BlockSpec((1, DN_CPS * DN_CHUNK, w), lambda bb, n: (bb, n, 0))
    out = lambda *tail: pl.BlockSpec((1, DN_CPS, 2) + tail, lambda bb, n: (bb, n, 0) + (0,) * len(tail))
    return pl.pallas_call(
        _dn_local_kernel,
        grid=(b, nc // DN_CPS),
        in_specs=[tok(DN_WIDTH), tok(DN_WIDTH), tok(DN_WIDTH), tok(LANES), tok(LANES),
                  pl.BlockSpec((1, DN_CPS, 2, CH4), lambda bb, n: (bb, n, 0, 0))],
        out_specs=[out(2 * CH4, LANES), out(CH4, LANES), out(DN_PAIRS, PAIR, PAIR), out(LANES, CH4)],
        out_shape=[jax.ShapeDtypeStruct((b, nc, 2, 2 * CH4, LANES), BF16),
                   jax.ShapeDtypeStruct((b, nc, 2, CH4, LANES), F32),
                   jax.ShapeDtypeStruct((b, nc, 2, DN_PAIRS, PAIR, PAIR), BF16),
                   jax.ShapeDtypeStruct((b, nc, 2, LANES, CH4), BF16)],
        compiler_params=_cparams(("parallel", "parallel")),
        name="dn_local",
    )(qd, kd, vd, gc, bc, g_rows)


DN_SPS = 4


def _dn_scan_kernel(wq_f, u_f, a_f, kgt_f, gl_f, wq_b, u_b, a_b, kgt_b, gl_b, of_ref, ob_ref, state):
    n = pl.program_id(0)

    @pl.when(n == 0)
    def _():
        state[...] = jnp.zeros_like(state)

    r = lax.broadcasted_iota(jnp.int32, (CH4, DN_WIDTH), 0)
    c = lax.broadcasted_iota(jnp.int32, (CH4, DN_WIDTH), 1)
    own = (r // DN_CHUNK) == (c // LANES)
    pick = lambda m: jnp.concatenate(
        [m[h * DN_CHUNK:(h + 1) * DN_CHUNK, h * LANES:(h + 1) * LANES] for h in range(DN_HEADS)], axis=0)
    dirs = ((wq_f, u_f, a_f, kgt_f, gl_f, of_ref), (wq_b, u_b, a_b, kgt_b, gl_b, ob_ref))
    chains = [(bb, d) + refs for bb in range(state.shape[0]) for d, refs in enumerate(dirs)]
    s_alls = [state[bb, d] for bb, d, *_ in chains]
    for sub in range(DN_SPS):
        at = [sub if d == 0 else DN_SPS - 1 - sub for _, d, *_ in chains]
        wss = [_dot(wq[bb, j, 0], s.astype(BF16)) for (bb, d, wq, *_), s, j in zip(chains, s_alls, at)]
        v_nbs = [(u[bb, j, 0] - pick(ws[:CH4])).astype(BF16)
                 for (bb, d, wq, u, *_), ws, j in zip(chains, wss, at)]
        for (bb, d, wq, u, a, kgt, gl, o_ref), ws, v_nb, j in zip(chains, wss, v_nbs, at):
            o = pick(ws[CH4:]) + jnp.concatenate(
                [_dot(a[bb, j, 0, p], v_nb[p * PAIR:(p + 1) * PAIR]) for p in range(DN_PAIRS)], axis=0)
            for h in range(DN_HEADS):
                o_ref[bb, j * DN_CHUNK:(j + 1) * DN_CHUNK, h * LANES:(h + 1) * LANES] = (
                    o[h * DN_CHUNK:(h + 1) * DN_CHUNK])
        nxt = []
        for (bb, d, wq, u, a, kgt, gl, o_ref), s_all, v_nb, j in zip(chains, s_alls, v_nbs, at):
            v_bd = jnp.where(own, jnp.concatenate([v_nb] * DN_HEADS, axis=1), jnp.zeros((), BF16))
            nxt.append(s_all * jnp.exp(gl[bb, j, 0]) + _dot(kgt[bb, j, 0], v_bd))
        s_alls = nxt
    for (bb, d, *_), s_all in zip(chains, s_alls):
        state[bb, d] = s_all


def _dn_scan(wq, u, a, kgt, gl_wide, *, s_len):
    b, nc = wq.shape[:2]
    n_lat = s_len // DN_CHUNK
    n_ctx = nc - n_lat
    assert n_lat % DN_SPS == 0 and n_ctx % DN_SPS == 0
    nb, nb_lat, nb_ctx = nc // DN_SPS, n_lat // DN_SPS, n_ctx // DN_SPS
    fwd = lambda n: jnp.where(n < nb_ctx, nb_lat + n, n - nb_ctx)
    bwd = lambda n: nb - 1 - n
    specs = []
    for d, order in ((0, fwd), (1, bwd)):
        blk = lambda arr, d=d, order=order: pl.BlockSpec(
            (b, DN_SPS, 1) + arr.shape[3:], lambda n: (0, order(n), d) + (0,) * (arr.ndim - 3))
        specs += [blk(wq), blk(u), blk(a), blk(kgt), blk(gl_wide)]
    o_spec = lambda order: pl.BlockSpec((b, DN_SPS * DN_CHUNK, DN_WIDTH), lambda n: (0, order(n), 0))
    return pl.pallas_call(
        _dn_scan_kernel,
        grid=(nb,),
        in_specs=specs,
        out_specs=[o_spec(fwd), o_spec(bwd)],
        out_shape=[jax.ShapeDtypeStruct((b, nc * DN_CHUNK, DN_WIDTH), F32)] * 2,
        scratch_shapes=[pltpu.VMEM((b, 2, DN_HEAD_DIM, DN_WIDTH), F32)],
        compiler_params=_cparams(("arbitrary",)),
        name="dn_scan",
    )(wq, u, a, kgt, gl_wide, wq, u, a, kgt, gl_wide)


def _merge_kernel(ona_ref, of_ref, ob_ref, z_ref, sc_ref, schp_ref, schn_ref, gate_ref, x_ref,
                  g1l_ref, g1c_ref, sh2l_ref, sh2c_ref, sc2l_ref, sc2c_ref, ng_ref, og_ref, scw_ref,
                  wna_ref, wdn_ref, wsc_ref, wout_ref, wrh_ref, wrl_ref, rb_ref,
                  x1_ref, h2_ref, comb_ref, combt_ref, cpad, *, tm, s_len, ta):
    i = pl.program_id(1)
    d = x_ref.shape[-1]
    o = of_ref[0] + ob_ref[0]
    z = z_ref[0].astype(F32)
    parts = []
    for h in range(DN_HEADS):
        oh = o[:, h * LANES:(h + 1) * LANES]
        parts.append(oh * lax.rsqrt(jnp.mean(oh * oh, axis=-1, keepdims=True) + EPS))
    o_dn = jnp.concatenate(parts, axis=1) * og_ref[...] * _silu(z)

    def cu(ref_val):
        v = ref_val.astype(F32)
        return v[:, 2 * SC_WIDTH:] * v[:, :SC_WIDTH]
    scp = sc_ref[0].astype(F32)
    half = HALO // 2
    cpad[0:half, :] = cu(schp_ref[0])[half:, :]
    cpad[half:half + tm, :] = scp[:, 2 * SC_WIDTH:] * scp[:, :SC_WIDTH]
    cpad[half + tm:, :] = cu(schn_ref[0])[:half, :]
    gt = i * tm + lax.broadcasted_iota(jnp.int32, (tm, 1), 0)
    has_prev = (gt != 0) & (gt != s_len)
    has_next = (gt != s_len - 1) & (gt != ta - 1)
    conv = (scw_ref[0:1, :] * jnp.where(has_prev, cpad[pl.ds(half - 1, tm), :], 0.0)
            + scw_ref[1:2, :] * cpad[pl.ds(half, tm), :]
            + scw_ref[2:3, :] * jnp.where(has_next, cpad[pl.ds(half + 1, tm), :], 0.0))
    o_sc = scp[:, SC_WIDTH:2 * SC_WIDTH] * conv

    merged = (gate_ref[0, :, :d].astype(F32) * _dot(ona_ref[0], wna_ref[...])
              + gate_ref[0, :, d:2 * d].astype(F32) * _dot(o_dn.astype(BF16), wdn_ref[...])
              + gate_ref[0, :, 2 * d:].astype(F32) * _dot(o_sc.astype(BF16), wsc_ref[...]))
    y = _dot(merged.astype(BF16), wout_ref[...])
    x1 = x_ref[0] + _row_select(i, tm, s_len, g1l_ref[0], g1c_ref[...]) * y
    x1_ref[0] = x1

    hn = x1 * lax.rsqrt(jnp.mean(x1 * x1, axis=-1, keepdims=True) + EPS) * ng_ref[...]
    h2 = (hn * (1.0 + _row_select(i, tm, s_len, sc2l_ref[0], sc2c_ref[...]))
          + _row_select(i, tm, s_len, sh2l_ref[0], sh2c_ref[...]))
    h2_ref[0] = h2.astype(BF16)

    h2_hi, h2_lo = _split_bf16(h2)
    logits = _dot3(h2_hi, h2_lo, wrh_ref[...], wrl_ref[...])
    scores = jax.nn.sigmoid(logits.T)
    biased = scores + rb_ref[...]
    rows = [biased[e:e + 1, :] for e in range(N_EXPERTS)]
    gsc = []
    for g in range(N_GROUPS):
        a, b_, c_, d_ = rows[4 * g:4 * g + 4]
        gsc.append(jnp.maximum(jnp.maximum(jnp.maximum(a + b_, a + c_), jnp.maximum(a + d_, b_ + c_)),
                               jnp.maximum(b_ + d_, c_ + d_)))
    best = gsc[0]
    grp = jnp.zeros_like(best, dtype=jnp.int32)
    for g in range(1, N_GROUPS):
        upd = gsc[g] > best
        best = jnp.where(upd, gsc[g], best)
        grp = jnp.where(upd, g, grp)
    eidx = lax.broadcasted_iota(jnp.int32, biased.shape, 0)
    masked = jnp.where((eidx // EXPERTS_PER_GROUP) == grp, biased, -jnp.inf)
    big = jnp.int32(LANES)
    m1 = masked.max(axis=0, keepdims=True)
    i1 = jnp.where(masked == m1, eidx, big).min(axis=0, keepdims=True)
    masked2 = jnp.where(eidx == i1, -jnp.inf, masked)
    m2 = masked2.max(axis=0, keepdims=True)
    i2 = jnp.where(masked2 == m2, eidx, big).min(axis=0, keepdims=True)
    sel1 = eidx == i1
    sel2 = eidx == i2
    s1 = jnp.sum(jnp.where(sel1, scores, 0.0), axis=0, keepdims=True)
    s2 = jnp.sum(jnp.where(sel2, scores, 0.0), axis=0, keepdims=True)
    tot = s1 + s2
    comb_t = jnp.where(sel1, s1 / tot, 0.0) + jnp.where(sel2, s2 / tot, 0.0)
    meta_t = jnp.where(eidx == N_EXPERTS, grp.astype(F32), comb_t)
    combt_ref[0] = meta_t
    comb_ref[0] = meta_t.T


def _merge(o_na, o_f, o_b, z, sc, gates, xs, mods_l, mods_c, norm_g, out_g, sc_w, wna, wdn, wsc, wout, wr, rb,
           *, s_len, tm):
    b, ta, d = xs.shape
    col_z = col_sc = 0
    hb = tm // HALO
    n_hb = ta // HALO
    tok = lambda w, col=0: pl.BlockSpec((1, tm, w), lambda bb, i: (bb, i, col))
    vec = lambda w: pl.BlockSpec((1, w), lambda bb, i: (0, 0))
    bvec = lambda: pl.BlockSpec((1, 1, d), lambda bb, i: (bb, 0, 0))
    full = lambda a: pl.BlockSpec(a.shape, lambda bb, i: (0,) * a.ndim)
    g1l, sh2l, sc2l = mods_l
    g1c, sh2c, sc2c = mods_c
    w3 = 3 * SC_WIDTH
    wr_hi = wr.astype(BF16)
    wr_lo = (wr - wr_hi.astype(F32)).astype(BF16)
    return pl.pallas_call(
        functools.partial(_merge_kernel, tm=tm, s_len=s_len, ta=ta),
        grid=(b, ta // tm),
        in_specs=[tok(NA_WIDTH), tok(DN_WIDTH), tok(DN_WIDTH), tok(DN_WIDTH, col_z), tok(w3, col_sc),
                  pl.BlockSpec((1, HALO, w3), lambda bb, i: (bb, jnp.maximum(i * hb - 1, 0), col_sc)),
                  pl.BlockSpec((1, HALO, w3), lambda bb, i: (bb, jnp.minimum((i + 1) * hb, n_hb - 1), col_sc)),
                  tok(3 * d, 0), tok(d),
                  bvec(), vec(d), bvec(), vec(d), bvec(), vec(d), vec(d), vec(DN_WIDTH),
                  full(sc_w), full(wna), full(wdn), full(wsc), full(wout), full(wr_hi), full(wr_lo), full(rb)],
        out_specs=[tok(d), tok(d), tok(LANES), pl.BlockSpec((1, LANES, tm), lambda bb, i: (bb, 0, i))],
        out_shape=[jax.ShapeDtypeStruct((b, ta, d), F32),
                   jax.ShapeDtypeStruct((b, ta, d), BF16),
                   jax.ShapeDtypeStruct((b, ta, LANES), F32),
                   jax.ShapeDtypeStruct((b, LANES, ta), F32)],
        scratch_shapes=[pltpu.VMEM((tm + HALO, SC_WIDTH), F32)],
        compiler_params=_cparams(("parallel", "parallel")),
        name="merge_router",
    )(o_na, o_f, o_b, z, sc, sc, sc, gates, xs, g1l, g1c, sh2l, sh2c, sc2l, sc2c, norm_g, out_g,
      sc_w, wna, wdn, wsc, wout, wr_hi, wr_lo, rb)


MOE_CAP = 256


def _moe_kernel(h_ref, comb_ref, combt_ref, lt_ref, gt_ref, wg_ref, wu_ref, wd_ref, x_ref, g2l_ref, g2c_ref,
                o_ref, acc, rank_r, rank_c, *, tm, s_len):
    i = pl.program_id(0)
    g = pl.program_id(1)
    bb = pl.program_id(2)

    comb = comb_ref[0]
    lane = lax.broadcasted_iota(jnp.int32, (tm, LANES), 1)
    grp_row = combt_ref[0, N_EXPERTS:N_EXPERTS + 1, :]
    grp_col = jnp.sum(jnp.where(lane == N_EXPERTS, comb, 0.0), axis=-1, keepdims=True)

    @pl.when(g == 0)
    def _():
        acc[bb] = jnp.zeros(acc.shape[1:], F32)
        sub = lax.broadcasted_iota(jnp.int32, (8, tm), 0).astype(F32)
        before = _dot(jnp.where(sub == grp_row, 1.0, 0.0).astype(BF16), lt_ref[...])
        rank_r[bb] = jnp.broadcast_to(
            jnp.sum(jnp.where(sub == grp_row, before, 0.0), axis=0, keepdims=True), (8, tm))
        lane_f = lane.astype(F32)
        before = _dot(gt_ref[...], jnp.where(lane_f == grp_col, 1.0, 0.0).astype(BF16))
        rank_c[bb] = jnp.broadcast_to(
            jnp.sum(jnp.where(lane_f == grp_col, before, 0.0), axis=-1, keepdims=True), (tm, LANES))

    gf = g.astype(F32)
    member_row = grp_row == gf
    member_col = grp_col == gf
    rank_row = rank_r[bb, 0:1, :]
    rank_col = rank_c[bb, :, 0:1]
    count = jnp.sum(jnp.where(member_row, 1.0, 0.0))
    comb_hi, comb_lo = _split_bf16(comb)
    d = x_ref.shape[-1]

    for k in range(tm // MOE_CAP):
        @pl.when(count > k * MOE_CAP)
        def _(k=k):
            slot_c = (lax.broadcasted_iota(jnp.int32, (MOE_CAP, 1), 0) + k * MOE_CAP).astype(F32)
            gather = jnp.where(member_row & (rank_row == slot_c), 1.0, 0.0).astype(BF16)
            hs = _dot(gather, h_ref[0]).astype(BF16)
            cws = _dot(gather, comb_hi) + _dot(gather, comb_lo)
            lane_c = lax.broadcasted_iota(jnp.int32, (MOE_CAP, LANES), 1)
            experts = range(EXPERTS_PER_GROUP)
            gates = [_dot(hs, wg_ref[0, e4]) for e4 in experts]
            ups = [_dot(hs, wu_ref[0, e4]) for e4 in experts]
            hiddens = []
            for e4 in experts:
                cw = jnp.sum(jnp.where(lane_c == g * EXPERTS_PER_GROUP + e4, cws, 0.0), axis=-1, keepdims=True)
                hiddens.append((_silu(gates[e4]) * ups[e4] * cw).astype(BF16))
            ys = _dot(hiddens[0], wd_ref[0, 0])
            for e4 in experts[1:]:
                ys = ys + _dot(hiddens[e4], wd_ref[0, e4])
            slot_r = (lax.broadcasted_iota(jnp.int32, (1, MOE_CAP), 1) + k * MOE_CAP).astype(F32)
            scatter = jnp.where(member_col & (rank_col == slot_r), 1.0, 0.0).astype(BF16)
            ys_hi, ys_lo = _split_bf16(ys)
            acc[bb] += _dot(scatter, ys_hi) + _dot(scatter, ys_lo)

    @pl.when(g == N_GROUPS - 1)
    def _():
        o_ref[0] = x_ref[0] + _row_select(i, tm, s_len, g2l_ref[0], g2c_ref[...]) * acc[bb]


def _moe(h2, comb, comb_t, wg, wu, wd, x1, g2l, g2c, *, layer, s_len, tm):
    b, ta, d = x1.shape
    tok = lambda w: pl.BlockSpec((1, tm, w), lambda i, g, bb: (bb, i, 0))
    res = lambda: pl.BlockSpec((1, tm, d), lambda i, g, bb: (jnp.where(g == N_GROUPS - 1, bb, 0), i, 0))
    grp = lambda a: pl.BlockSpec((1, EXPERTS_PER_GROUP) + a.shape[2:], lambda i, g, bb: (layer, g, 0, 0))
    tri = lambda: pl.BlockSpec((tm, tm), lambda i, g, bb: (0, 0))
    earlier = np.arange(tm)[:, None] < np.arange(tm)[None, :]
    lt = jnp.asarray(earlier, BF16)
    gt = jnp.asarray(earlier.T, BF16)
    return pl.pallas_call(
        functools.partial(_moe_kernel, tm=tm, s_len=s_len),
        grid=(ta // tm, N_GROUPS, b),
        in_specs=[tok(d), tok(LANES),
                  pl.BlockSpec((1, LANES, tm), lambda i, g, bb: (bb, 0, i)),
                  tri(), tri(),
                  grp(wg), grp(wu), grp(wd),
                  res(),
                  pl.BlockSpec((1, 1, d), lambda i, g, bb: (bb, 0, 0)),
                  pl.BlockSpec((1, d), lambda i, g, bb: (0, 0))],
        out_specs=res(),
        out_shape=jax.ShapeDtypeStruct((b, ta, d), F32),
        scratch_shapes=[pltpu.VMEM((b, tm, d), F32), pltpu.VMEM((b, 8, tm), F32),
                        pltpu.VMEM((b, tm, LANES), F32)],
        compiler_params=_cparams(("arbitrary", "arbitrary", "arbitrary")),
        name="moe_grouped",
    )(h2, comb, comb_t, lt, gt, wg, wu, wd, x1, g2l, g2c)


def _rope_tables(s_len, ctx_len):
    t = np.arange(s_len)
    n_freq = DN_HEAD_DIM // 4
    inv_freq = (np.float32(ROPE_BASE) ** (-np.arange(n_freq, dtype=np.float32) / n_freq)).astype(np.float32)
    pos = np.stack([t // GRID_W, t % GRID_W], axis=-1).astype(np.float32)
    ang = (pos[..., None] * inv_freq).astype(np.float32)
    cos, sin = np.cos(ang.astype(np.float64)), np.sin(ang.astype(np.float64))
    rc = np.concatenate([cos[:, 0], cos[:, 0], cos[:, 1], cos[:, 1]], axis=-1)
    rs = np.concatenate([-sin[:, 0], sin[:, 0], -sin[:, 1], sin[:, 1]], axis=-1)
    rc = np.concatenate([rc, np.ones((ctx_len, LANES))], axis=0)
    rs = np.concatenate([rs, np.zeros((ctx_len, LANES))], axis=0)
    return jnp.asarray(rc, F32), jnp.asarray(rs, F32)


def _pick_tile(ta, candidates):
    for t in candidates:
        if ta % t == 0:
            return t
    raise ValueError(f"no row tile for {ta} tokens")


def kernel(x, c, ctx, c_ctx, w_ada, b_ada, norm_mix_g, norm_ffn_g, w_in, na_q_norm_g, na_k_norm_g, na_rpb, dn_conv_w, dn_a_log, dn_dt_bias, dn_out_norm_g, sc_conv_w, w_branch_na, w_branch_dn, w_branch_sc, w_out, w_router, router_bias, moe_w_gate, moe_w_up, moe_w_down):
    b, s_len, d = x.shape
    ctx_len = ctx.shape[1]
    depth = w_ada.shape[0]
    ta = s_len + ctx_len
    assert ctx_len == NA_BLK and s_len % (WIN_R * GRID_W) == 0 and s_len // NA_BLK >= 4
    assert ctx_len % PREP_TILE == 0 and b + 1 <= 8
    tm_proj = _pick_tile(ta, (768, 512, 256))
    tm_merge = _pick_tile(ta, (384, 256))
    tm_moe = _pick_tile(ta, (768, 512, 256))

    xs = jnp.concatenate([x, ctx], axis=1)
    c_rows = jnp.zeros((8, d), F32).at[:b].set(c).at[b].set(c_ctx)
    mod = _modulation(c_rows, w_ada, b_ada)

    sizes = (3 * NA_WIDTH, 3 * DN_WIDTH, DN_WIDTH, 4 * DN_HEADS, 3 * SC_WIDTH, 3 * d)
    cuts = [int(v) for v in np.cumsum((0,) + sizes)]
    seg = lambda l, k: lax.slice(w_in, (l, 0, cuts[k]), (l + 1, d, cuts[k + 1]))[0]
    wg_all, wu_all, wd_all = moe_w_gate.astype(BF16), moe_w_up.astype(BF16), moe_w_down.astype(BF16)

    rope_c, rope_s = _rope_tables(s_len, ctx_len)
    bd = jnp.asarray(np.kron(np.eye(NA_HEADS), np.ones((NA_HEAD_DIM, NA_HEAD_DIM))), BF16)
    nc = ta // DN_CHUNK
    wr = jnp.zeros((d, LANES), F32).at[:, :N_EXPERTS].set(w_router)
    rb = jnp.zeros((LANES, 1), F32).at[:N_EXPERTS, 0].set(router_bias)

    for l in range(depth):
        m = mod[l].reshape(8, N_MOD, d)
        lat = lambda k: m[:b, k][:, None, :]
        cx = lambda k: m[b:b + 1, k]
        w_ab = jnp.zeros((d, LANES), F32).at[:, :4 * DN_HEADS].set(seg(l, 3))
        gqk = jnp.concatenate([jnp.tile(na_q_norm_g[l], NA_HEADS) * (NA_HEAD_DIM ** -0.5),
                               jnp.tile(na_k_norm_g[l], NA_HEADS)])[None, :]
        gates, na_q, na_k, na_v, dn_qkv, sc_p, dn_z, ab = _projection(
            xs, (lat(0), lat(1)), (cx(0), cx(1)), norm_mix_g[l][None, :],
            seg(l, 5).astype(BF16), seg(l, 0).astype(BF16), seg(l, 1).astype(BF16),
            seg(l, 4).astype(BF16), seg(l, 2).astype(BF16), w_ab, gqk, bd, s_len=s_len, tm=tm_proj)

        o_na = _neighbourhood_attention(na_q, na_k, na_v, _na_bias_table(na_rpb[l]), s_len=s_len)

        alog_row = jnp.zeros((1, LANES), F32).at[0, :2 * DN_HEADS].set(dn_a_log[l].reshape(-1))
        dtb_row = jnp.zeros((1, LANES), F32).at[0, :2 * DN_HEADS].set(dn_dt_bias[l].reshape(-1))
        qd, kd, vd, gc, bc = _dn_prep(dn_qkv, ab, dn_conv_w[l], rope_c, rope_s, alog_row, dtb_row,
                                      s_len=s_len)
        g8 = gc[..., :2 * DN_HEADS].reshape(b, nc, DN_CHUNK, 2, DN_HEADS)
        g_rows = jnp.transpose(g8, (0, 1, 3, 4, 2)).reshape(b, nc, 2, CH4)
        g_last = jnp.stack([g8[:, :, DN_CHUNK - 1, 0], g8[:, :, 0, 1]], axis=2)
        gl_wide = jnp.repeat(g_last, LANES, axis=-1)[:, :, :, None, :]
        wq, u, a, kgt = _dn_local(qd, kd, vd, gc, bc, g_rows)
        o_f, o_b = _dn_scan(wq, u, a, kgt, gl_wide, s_len=s_len)

        x1, h2, comb, comb_t = _merge(
            o_na, o_f, o_b, dn_z, sc_p, gates, xs, (lat(2), lat(3), lat(4)), (cx(2), cx(3), cx(4)),
            norm_ffn_g[l][None, :], jnp.tile(dn_out_norm_g[l], DN_HEADS)[None, :], sc_conv_w[l],
            w_branch_na[l].astype(BF16), w_branch_dn[l].astype(BF16), w_branch_sc[l].astype(BF16),
            w_out[l].astype(BF16), wr, rb, s_len=s_len, tm=tm_merge)

        xs = _moe(h2, comb, comb_t, wg_all, wu_all, wd_all, x1, lat(5), cx(5), layer=l, s_len=s_len, tm=tm_moe)

    return xs[:, :s_len]
```

```python
import functools
import math

import numpy as np
import jax
import jax.numpy as jnp
from jax import lax
from jax.experimental import pallas as pl
from jax.experimental.pallas import tpu as pltpu

F32 = jnp.float32
BF16 = jnp.bfloat16
HIGHEST = lax.Precision.HIGHEST

GRID_W = 64
NA_HEADS = 8
NA_HEAD_DIM = 64
NA_WIDTH = NA_HEADS * NA_HEAD_DIM
WIN_R = 8
WIN_C = 16
DN_HEADS = 4
DN_HEAD_DIM = 128
DN_WIDTH = DN_HEADS * DN_HEAD_DIM
DN_CONV = 5
DN_CHUNK = 64
ROPE_BASE = 10000.0
SC_WIDTH = 512
SC_CONV = 3
N_EXPERTS = 16
N_GROUPS = 4
EXPERTS_PER_GROUP = N_EXPERTS // N_GROUPS
D_EXPERT = 512
N_MOD = 6
EPS = 1e-6
NEG_INF = -1e30

LANES = 128
HALO = 16
PREP_TILE = 256
VMEM_LIMIT = 56 * 1024 * 1024

N_PROJ_TILE = 512


def _cparams(sem):
    return pltpu.CompilerParams(dimension_semantics=sem, vmem_limit_bytes=VMEM_LIMIT)


def _dot(a, b):
    return jnp.dot(a, b, preferred_element_type=F32)


def _dot_nt(a, b):
    return lax.dot_general(a, b, (((1,), (1,)), ((), ())), preferred_element_type=F32)


def _silu(x):
    return x * jax.nn.sigmoid(x)


def _row_select(i, tm, s_len, lat, ctx):
    row = i * tm + lax.broadcasted_iota(jnp.int32, (tm, 1), 0)
    return jnp.where(row >= s_len, ctx, lat)


def _mod_kernel(c_ref, w_ref, b_ref, o_ref):
    c_hi, c_lo = _split_bf16(_silu(c_ref[...]))
    w_hi, w_lo = _split_bf16(w_ref[0])
    o_ref[0] = _dot3(c_hi, c_lo, w_hi, w_lo) + b_ref[0]


def _modulation(c_rows, w_ada, b_ada):
    depth, d, n = w_ada.shape
    tn = 512
    return pl.pallas_call(
        _mod_kernel,
        grid=(depth, n // tn),
        in_specs=[pl.BlockSpec((8, d), lambda l, j: (0, 0)),
                  pl.BlockSpec((1, d, tn), lambda l, j: (l, 0, j)),
                  pl.BlockSpec((1, 1, tn), lambda l, j: (l, 0, j))],
        out_specs=pl.BlockSpec((1, 8, tn), lambda l, j: (l, 0, j)),
        out_shape=jax.ShapeDtypeStruct((depth, 8, n), F32),
        compiler_params=_cparams(("parallel", "parallel")),
        name="adaln_mod",
    )(c_rows, w_ada, b_ada.reshape(depth, 1, n))


def _split_bf16(a):
    hi = a.astype(BF16)
    return hi, (a - hi.astype(F32)).astype(BF16)


def _dot3(a_hi, a_lo, b_hi, b_lo):
    return _dot(a_hi, b_hi) + _dot(a_lo, b_hi) + _dot(a_hi, b_lo)


def _modulated_norm(x_ref, shl_ref, scl_ref, shc_ref, scc_ref, g_ref, *, tm, s_len):
    i = pl.program_id(1)
    x = x_ref[0]
    y = x * lax.rsqrt(jnp.mean(x * x, axis=-1, keepdims=True) + EPS) * g_ref[...]
    scale = _row_select(i, tm, s_len, scl_ref[0], scc_ref[...])
    shift = _row_select(i, tm, s_len, shl_ref[0], shc_ref[...])
    return y * (1.0 + scale) + shift


def _proj_gate_kernel(x_ref, shl_ref, scl_ref, shc_ref, scc_ref, g_ref, w_ref, gate_ref, *, tm, s_len):
    h = _modulated_norm(x_ref, shl_ref, scl_ref, shc_ref, scc_ref, g_ref, tm=tm, s_len=s_len).astype(BF16)
    for c0 in range(0, w_ref.shape[1], N_PROJ_TILE):
        cols = slice(c0, c0 + N_PROJ_TILE)
        gate_ref[0, :, cols] = jax.nn.sigmoid(_dot(h, w_ref[:, cols])).astype(BF16)


def _proj_mix_kernel(x_ref, shl_ref, scl_ref, shc_ref, scc_ref, g_ref, wna_ref, wdn_ref, wsc_ref, wz_ref,
                     wabh_ref, wabl_ref, gqk_ref, bd_ref,
                     q_ref, k_ref, v_ref, dn_ref, sc_ref, z_ref, ab_ref, *, tm, s_len):
    h = _modulated_norm(x_ref, shl_ref, scl_ref, shc_ref, scc_ref, g_ref, tm=tm, s_len=s_len)
    h_hi, h_lo = _split_bf16(h)
    ab_ref[0] = _dot3(h_hi, h_lo, wabh_ref[...], wabl_ref[...])
    for part, out in enumerate((q_ref, k_ref)):
        cols = slice(part * NA_WIDTH, (part + 1) * NA_WIDTH)
        a = _dot(h_hi, wna_ref[:, cols])
        ss = _dot((a * a).astype(BF16), bd_ref[...])
        out[0] = (a * lax.rsqrt(ss * (1.0 / NA_HEAD_DIM) + EPS) * gqk_ref[:, cols]).astype(BF16)
    v_ref[0] = _dot(h_hi, wna_ref[:, 2 * NA_WIDTH:]).astype(BF16)
    for w_ref, out in ((wdn_ref, dn_ref), (wsc_ref, sc_ref), (wz_ref, z_ref)):
        for c0 in range(0, w_ref.shape[1], N_PROJ_TILE):
            cols = slice(c0, c0 + N_PROJ_TILE)
            out[0, :, cols] = _dot(h_hi, w_ref[:, cols]).astype(BF16)


def _projection(xs, mod_l, mod_c, norm_g, w_gate, w_na, w_dn, w_sc, w_z, w_ab, gqk, bd, *, s_len, tm):
    b, ta, d = xs.shape
    shl, scl = mod_l
    shc, scc = mod_c
    wab_hi, wab_lo = _split_bf16(w_ab)
    tok = lambda w: pl.BlockSpec((1, tm, w), lambda bb, i: (bb, i, 0))
    bvec = lambda: pl.BlockSpec((1, 1, d), lambda bb, i: (bb, 0, 0))
    full = lambda a: pl.BlockSpec(a.shape, lambda bb, i: (0,) * a.ndim)
    head = [tok(d), bvec(), bvec(), full(shc), full(scc), full(norm_g)]
    gates = pl.pallas_call(
        functools.partial(_proj_gate_kernel, tm=tm, s_len=s_len),
        grid=(b, ta // tm),
        in_specs=head + [full(w_gate)],
        out_specs=tok(w_gate.shape[1]),
        out_shape=jax.ShapeDtypeStruct((b, ta, w_gate.shape[1]), BF16),
        compiler_params=_cparams(("parallel", "parallel")),
        name="proj_gates",
    )(xs, shl, scl, shc, scc, norm_g, w_gate)
    consts = (w_na, w_dn, w_sc, w_z, wab_hi, wab_lo, gqk, bd)
    widths = (NA_WIDTH, NA_WIDTH, NA_WIDTH, w_dn.shape[1], w_sc.shape[1], w_z.shape[1])
    outs = pl.pallas_call(
        functools.partial(_proj_mix_kernel, tm=tm, s_len=s_len),
        grid=(b, ta // tm),
        in_specs=head + [full(a) for a in consts],
        out_specs=[tok(w) for w in widths] + [tok(LANES)],
        out_shape=[jax.ShapeDtypeStruct((b, ta, w), BF16) for w in widths]
        + [jax.ShapeDtypeStruct((b, ta, LANES), F32)],
        compiler_params=_cparams(("parallel", "parallel")),
        name="proj_mixers",
    )(xs, shl, scl, shc, scc, norm_g, *consts)
    return (gates,) + tuple(outs)


NA_QROWS = 4
NA_BLK = NA_QROWS * GRID_W
WIN_TOK = WIN_R * GRID_W


def _stack_two_heads(q2):
    lane = lax.broadcasted_iota(jnp.int32, q2.shape, 1)
    zero = jnp.zeros_like(q2)
    return jnp.concatenate([jnp.where(lane < NA_HEAD_DIM, q2, zero),
                            jnp.where(lane >= NA_HEAD_DIM, q2, zero)], axis=0)


def _unstack_two_heads(o2):
    m = o2.shape[0] // 2
    lane = lax.broadcasted_iota(jnp.int32, (m, o2.shape[1]), 1)
    return jnp.where(lane < NA_HEAD_DIM, o2[:m], o2[m:])


def _na_kernel(q_ref, kp_ref, kc_ref, kn_ref, vp_ref, vc_ref, vn_ref, kx_ref, vx_ref, bias_ref, o_ref,
               kbuf, vbuf, *, n_rows):
    i = pl.program_id(1)
    n_lat = n_rows // NA_QROWS

    @pl.when(i < n_lat)
    def _():
        for slot, (kr, vr) in enumerate(((kp_ref, vp_ref), (kc_ref, vc_ref), (kn_ref, vn_ref))):
            kbuf[slot * NA_BLK:(slot + 1) * NA_BLK, :] = kr[0]
            vbuf[slot * NA_BLK:(slot + 1) * NA_BLK, :] = vr[0]
        win = []
        for j in range(NA_QROWS):
            r = i * NA_QROWS + j
            start = jnp.clip(r - WIN_R // 2, 0, n_rows - WIN_R)
            win.append((pl.multiple_of((start - (i - 1) * NA_QROWS) * GRID_W, GRID_W), r - start))

        rows2 = 2 * GRID_W
        pairs = [slice(hp * LANES, (hp + 1) * LANES) for hp in range(NA_HEADS // 2)]
        logits = []
        for hp, lanes in enumerate(pairs):
            qs = jnp.concatenate([_stack_two_heads(q_ref[0, j * GRID_W:(j + 1) * GRID_W, lanes])
                                  for j in range(NA_QROWS)], axis=0)
            s_c = _dot_nt(qs, kx_ref[0, :, lanes])
            s_w = jnp.concatenate(
                [_dot_nt(qs[j * rows2:(j + 1) * rows2], kbuf[pl.ds(win[j][0], WIN_TOK), lanes])
                 + bias_ref[win[j][1], hp] for j in range(NA_QROWS)], axis=0)
            logits.append((s_w, s_c))
        probs = []
        for s_w, s_c in logits:
            m = jnp.maximum(s_w.max(axis=-1, keepdims=True), s_c.max(axis=-1, keepdims=True))
            p_w = jnp.exp(s_w - m)
            p_c = jnp.exp(s_c - m)
            denom = p_w.sum(axis=-1, keepdims=True) + p_c.sum(axis=-1, keepdims=True)
            probs.append((p_w.astype(BF16), p_c.astype(BF16), denom))
        for lanes, (p_wb, p_cb, denom) in zip(pairs, probs):
            o = _dot(p_cb, vx_ref[0, :, lanes]) + jnp.concatenate(
                [_dot(p_wb[j * rows2:(j + 1) * rows2], vbuf[pl.ds(win[j][0], WIN_TOK), lanes])
                 for j in range(NA_QROWS)], axis=0)
            o = o / denom
            for j in range(NA_QROWS):
                o_ref[0, j * GRID_W:(j + 1) * GRID_W, lanes] = _unstack_two_heads(
                    o[j * rows2:(j + 1) * rows2]).astype(BF16)

    @pl.when(i == n_lat)
    def _():
        for hp in range(NA_HEADS // 2):
            lanes = slice(hp * LANES, (hp + 1) * LANES)
            s = _dot_nt(_stack_two_heads(q_ref[0, :, lanes]), kx_ref[0, :, lanes])
            p = jnp.exp(s - s.max(axis=-1, keepdims=True))
            o = _dot(p.astype(BF16), vx_ref[0, :, lanes]) / p.sum(axis=-1, keepdims=True)
            o_ref[0, :, lanes] = _unstack_two_heads(o).astype(BF16)


def _neighbourhood_attention(q, k, v, bias_tab, *, s_len):
    b, ta, _ = q.shape
    n_rows = s_len // GRID_W
    n_lat = n_rows // NA_QROWS
    blk = lambda off: pl.BlockSpec((1, NA_BLK, NA_WIDTH), lambda bb, i: (bb, jnp.clip(i + off, 0, n_lat - 1), 0))
    cx = lambda: pl.BlockSpec((1, NA_BLK, NA_WIDTH), lambda bb, i: (bb, n_lat, 0))
    own = lambda: pl.BlockSpec((1, NA_BLK, NA_WIDTH), lambda bb, i: (bb, i, 0))
    return pl.pallas_call(
        functools.partial(_na_kernel, n_rows=n_rows),
        grid=(b, n_lat + 1),
        in_specs=[own(), blk(-1), blk(0), blk(1), blk(-1), blk(0), blk(1), cx(), cx(),
                  pl.BlockSpec(bias_tab.shape, lambda bb, i: (0, 0, 0, 0))],
        out_specs=own(),
        out_shape=jax.ShapeDtypeStruct((b, ta, NA_WIDTH), BF16),
        scratch_shapes=[pltpu.VMEM((3 * NA_BLK, NA_WIDTH), BF16),
                        pltpu.VMEM((3 * NA_BLK, NA_WIDTH), BF16)],
        compiler_params=_cparams(("parallel", "arbitrary")),
        name="na_attention",
    )(q, k, k, k, v, v, v, k, v, bias_tab)


def _na_bias_table(rpb):
    qc = np.arange(GRID_W)[:, None]
    kc = np.arange(GRID_W)[None, :]
    win_c0 = np.clip(qc - WIN_C // 2, 0, GRID_W - WIN_C)
    in_win = (kc >= win_c0) & (kc < win_c0 + WIN_C)
    rel_c = np.clip(kc - qc + WIN_C - 1, 0, 2 * WIN_C - 2)
    pick_c = (rel_c[..., None] == np.arange(2 * WIN_C - 1)).astype(np.float32)
    by_col = jnp.einsum('hab,qkb->haqk', rpb.astype(F32), pick_c, precision=HIGHEST)
    by_col = jnp.where(in_win[None, None], by_col, NEG_INF)
    tab = jnp.stack([by_col[:, WIN_R - 1 - v:2 * WIN_R - 1 - v] for v in range(WIN_R)], axis=0)
    tab = jnp.transpose(tab, (0, 1, 3, 2, 4))
    return tab.reshape(WIN_R, NA_HEADS // 2, 2 * GRID_W, WIN_R * GRID_W)


def _dn_prep_kernel(x_ref, hp_ref, hn_ref, w_ref, rc_ref, rs_ref, ab_ref, alog_ref, dtb_ref,
                    q_ref, k_ref, v_ref, g_ref, beta_ref, xpad, *, s_len, ta):
    i = pl.program_id(1)
    t0 = i * PREP_TILE
    has_prev = jnp.logical_and(t0 != 0, t0 != s_len).astype(F32)
    has_next = jnp.logical_and(t0 + PREP_TILE != s_len, t0 + PREP_TILE != ta).astype(F32)
    half = HALO // 2
    xpad[0:half, :] = hp_ref[0].astype(F32)[half:, :] * has_prev
    xpad[half:half + PREP_TILE, :] = x_ref[0].astype(F32)
    xpad[half + PREP_TILE:, :] = hn_ref[0].astype(F32)[:half, :] * has_next
    xp = xpad[...]
    n_pad = xp.shape[0]
    y = jnp.zeros((PREP_TILE, x_ref.shape[-1]), F32)
    for tap in range(DN_CONV):
        shifted = xp if tap == DN_CONV // 2 else pltpu.roll(xp, (DN_CONV // 2 - tap) % n_pad, 0)
        y = y + w_ref[tap:tap + 1, :] * shifted[half:half + PREP_TILE]
    y = _silu(y)

    lane = lax.broadcasted_iota(jnp.int32, (PREP_TILE, LANES), 1)
    first_half = (lane % (LANES // 2)) < (LANES // 4)
    rc = rc_ref[...]
    rs = rs_ref[...]

    def norm_rope(u, scale):
        u = u * (lax.rsqrt(jnp.sum(u * u, axis=-1, keepdims=True) + EPS) * scale)
        partner = jnp.where(first_half, pltpu.roll(u, LANES - LANES // 4, 1), pltpu.roll(u, LANES // 4, 1))
        return u * rc + partner * rs

    for h in range(DN_HEADS):
        lanes = slice(h * LANES, (h + 1) * LANES)
        qh = norm_rope(y[:, h * LANES:(h + 1) * LANES], DN_HEAD_DIM ** -0.5)
        kh = norm_rope(y[:, DN_WIDTH + h * LANES:DN_WIDTH + (h + 1) * LANES], 1.0)
        q_ref[0, :, lanes] = qh.astype(BF16)
        k_ref[0, :, lanes] = kh.astype(BF16)
    v_ref[0] = y[:, 2 * DN_WIDTH:].astype(BF16)

    ab = ab_ref[0]
    z = ab + dtb_ref[...]
    softplus = jnp.maximum(z, 0.0) + jnp.log1p(jnp.exp(-jnp.abs(z)))
    g = -jnp.exp(alog_ref[...]) * softplus
    g = jnp.where(lane < 2 * DN_HEADS, g, 0.0)
    beta = jax.nn.sigmoid(pltpu.roll(ab, LANES - 2 * DN_HEADS, 1))
    beta_ref[0] = jnp.where(lane < 2 * DN_HEADS, beta, 0.0)

    r = lax.broadcasted_iota(jnp.int32, (PREP_TILE, PREP_TILE), 0)
    c = lax.broadcasted_iota(jnp.int32, (PREP_TILE, PREP_TILE), 1)
    same = (r // DN_CHUNK) == (c // DN_CHUNK)
    tri_f = jnp.where(same & (c <= r), 1.0, 0.0).astype(BF16)
    tri_b = jnp.where(same & (c >= r), 1.0, 0.0).astype(BF16)
    g1 = g.astype(BF16)
    rem = g - g1.astype(F32)
    g2 = rem.astype(BF16)
    g3 = (rem - g2.astype(F32)).astype(BF16)
    cs_f = _dot(tri_f, g1) + _dot(tri_f, g2) + _dot(tri_f, g3)
    cs_b = _dot(tri_b, g1) + _dot(tri_b, g2) + _dot(tri_b, g3)
    g_ref[0] = jnp.where(lane < DN_HEADS, cs_f, cs_b)


def _dn_prep(p, ab, conv_w, rope_c, rope_s, alog_row, dtb_row, *, s_len):
    b, ta, w3 = p.shape
    col_dn = 0
    n_t = ta // PREP_TILE
    hb = PREP_TILE // HALO
    n_hb = ta // HALO
    tok = lambda: pl.BlockSpec((1, PREP_TILE, DN_WIDTH), lambda bb, i: (bb, i, 0))
    sca = lambda: pl.BlockSpec((1, PREP_TILE, LANES), lambda bb, i: (bb, i, 0))
    row = lambda: pl.BlockSpec((1, LANES), lambda bb, i: (0, 0))
    return pl.pallas_call(
        functools.partial(_dn_prep_kernel, s_len=s_len, ta=ta),
        grid=(b, n_t),
        in_specs=[pl.BlockSpec((1, PREP_TILE, w3), lambda bb, i: (bb, i, col_dn)),
                  pl.BlockSpec((1, HALO, w3), lambda bb, i: (bb, jnp.maximum(i * hb - 1, 0), col_dn)),
                  pl.BlockSpec((1, HALO, w3), lambda bb, i: (bb, jnp.minimum((i + 1) * hb, n_hb - 1), col_dn)),
                  pl.BlockSpec((DN_CONV, w3), lambda bb, i: (0, 0)),
                  pl.BlockSpec((PREP_TILE, LANES), lambda bb, i: (i, 0)),
                  pl.BlockSpec((PREP_TILE, LANES), lambda bb, i: (i, 0)),
                  sca(), row(), row()],
        out_specs=[tok(), tok(), tok(), sca(), sca()],
        out_shape=[jax.ShapeDtypeStruct((b, ta, DN_WIDTH), BF16)] * 3
        + [jax.ShapeDtypeStruct((b, ta, LANES), F32)] * 2,
        scratch_shapes=[pltpu.VMEM((PREP_TILE + HALO, w3), F32)],
        compiler_params=_cparams(("parallel", "parallel")),
        name="dn_prep",
    )(p, p, p, conv_w, rope_c, rope_s, ab, alog_row, dtb_row)


CH4 = DN_HEADS * DN_CHUNK


DN_CPS = 4
DN_PAIRS = DN_HEADS // 2
PAIR = 2 * DN_CHUNK


def _stack_pair(x, p):
    return jnp.concatenate([x[:, (2 * p + h) * LANES:(2 * p + h + 1) * LANES] for h in range(2)], axis=0)


def _dn_local_kernel(q_ref, k_ref, v_ref, gc_ref, bc_ref, gr_ref, wq_ref, u_ref, a_ref, kgt_ref):
    r = lax.broadcasted_iota(jnp.int32, (PAIR, PAIR), 0)
    c = lax.broadcasted_iota(jnp.int32, (PAIR, PAIR), 1)
    same = (r // DN_CHUNK) == (c // DN_CHUNK)
    chains = []
    for cc in range(DN_CPS):
        toks = slice(cc * DN_CHUNK, (cc + 1) * DN_CHUNK)
        gc = gc_ref[0, toks, :]
        bc = bc_ref[0, toks, :]
        for p in range(DN_PAIRS):
            rows = slice(p * PAIR, (p + 1) * PAIR)
            kst = _stack_pair(k_ref[0, toks, :], p)
            qst = _stack_pair(q_ref[0, toks, :], p)
            vst = _stack_pair(v_ref[0, toks, :], p)
            kq = _dot_nt(jnp.concatenate([kst, qst], axis=0), kst)
            kk = kq[:PAIR]
            qk = kq[PAIR:]
            kf = kst.astype(F32)
            for d in range(2):
                later = (r >= c) if d == 0 else (r <= c)
                incl = same & later
                strict = incl & (r != c)
                lane0 = d * DN_HEADS + 2 * p
                col = lambda a: jnp.concatenate([a[:, lane0 + h:lane0 + h + 1] for h in range(2)], axis=0)
                g_col = col(gc)
                b_col = col(bc)
                last = DN_CHUNK - 1 if d == 0 else 0
                gl_col = jnp.concatenate(
                    [jnp.broadcast_to(gc[last:last + 1, lane0 + h:lane0 + h + 1], (DN_CHUNK, 1))
                     for h in range(2)], axis=0)
                g_row = gr_ref[0, cc, d:d + 1, rows]
                decay = jnp.exp(jnp.where(incl, g_col - g_row, NEG_INF))
                pw = jnp.where(strict, -(kk * decay * b_col), 0.0)
                z = jnp.concatenate([vst.astype(F32) * b_col, kf * (b_col * jnp.exp(g_col))], axis=1)
                chains.append([pw, z, (cc, d, rows)])
                wq_ref[0, cc, d, CH4 + p * PAIR:CH4 + (p + 1) * PAIR, :] = (
                    qst.astype(F32) * jnp.exp(g_col)).astype(BF16)
                a_ref[0, cc, d, p] = (qk * decay).astype(BF16)
                kgt_ref[0, cc, d, :, rows] = (kf * jnp.exp(gl_col - g_col)).T.astype(BF16)

    for step in range(6):
        if step:
            for chain in chains:
                pwb = chain[0].astype(BF16)
                chain[0] = _dot(pwb, pwb)
        for chain in chains:
            pw, z, _ = chain
            chain[1] = z + _dot(pw.astype(BF16), z.astype(BF16))
    for _, z, (cc, d, rows) in chains:
        wq_ref[0, cc, d, rows, :] = z[:, LANES:].astype(BF16)
        u_ref[0, cc, d, rows, :] = z[:, :LANES]


def _dn_local(qd, kd, vd, gc, bc, g_rows):
    b, ta, _ = qd.shape
    nc = ta // DN_CHUNK
    tok = lambda w: pl.BlockSpec((1, DN_CPS * DN_CHUNK, w), lambda bb, n: (bb, n, 0))
    out = lambda *tail: pl.BlockSpec((1, DN_CPS, 2) + tail, lambda bb, n: (bb, n, 0) + (0,) * len(tail))
    return pl.pallas_call(
        _dn_local_kernel,
        grid=(b, nc // DN_CPS),
        in_specs=[tok(DN_WIDTH), tok(DN_WIDTH), tok(DN_WIDTH), tok(LANES), tok(LANES),
                  pl.BlockSpec((1, DN_CPS, 2, CH4), lambda bb, n: (bb, n, 0, 0))],
        out_specs=[out(2 * CH4, LANES), out(CH4, LANES), out(DN_PAIRS, PAIR, PAIR), out(LANES, CH4)],
        out_shape=[jax.ShapeDtypeStruct((b, nc, 2, 2 * CH4, LANES), BF16),
                   jax.ShapeDtypeStruct((b, nc, 2, CH4, LANES), F32),
                   jax.ShapeDtypeStruct((b, nc, 2, DN_PAIRS, PAIR, PAIR), BF16),
                   jax.ShapeDtypeStruct((b, nc, 2, LANES, CH4), BF16)],
        compiler_params=_cparams(("parallel", "parallel")),
        name="dn_local",
    )(qd, kd, vd, gc, bc, g_rows)


DN_SPS = 4


def _dn_scan_kernel(wq_f, u_f, a_f, kgt_f, gl_f, wq_b, u_b, a_b, kgt_b, gl_b, of_ref, ob_ref, state):
    n = pl.program_id(0)

    @pl.when(n == 0)
    def _():
        state[...] = jnp.zeros_like(state)

    r = lax.broadcasted_iota(jnp.int32, (CH4, DN_WIDTH), 0)
    c = lax.broadcasted_iota(jnp.int32, (CH4, DN_WIDTH), 1)
    own = (r // DN_CHUNK) == (c // LANES)
    pick = lambda m: jnp.concatenate(
        [m[h * DN_CHUNK:(h + 1) * DN_CHUNK, h * LANES:(h + 1) * LANES] for h in range(DN_HEADS)], axis=0)
    dirs = ((wq_f, u_f, a_f, kgt_f, gl_f, of_ref), (wq_b, u_b, a_b, kgt_b, gl_b, ob_ref))
    chains = [(bb, d) + refs for bb in range(state.shape[0]) for d, refs in enumerate(dirs)]
    s_alls = [state[bb, d] for bb, d, *_ in chains]
    for sub in range(DN_SPS):
        at = [sub if d == 0 else DN_SPS - 1 - sub for _, d, *_ in chains]
        wss = [_dot(wq[bb, j, 0], s.astype(BF16)) for (bb, d, wq, *_), s, j in zip(chains, s_alls, at)]
        v_nbs = [(u[bb, j, 0] - pick(ws[:CH4])).astype(BF16)
                 for (bb, d, wq, u, *_), ws, j in zip(chains, wss, at)]
        for (bb, d, wq, u, a, kgt, gl, o_ref), ws, v_nb, j in zip(chains, wss, v_nbs, at):
            o = pick(ws[CH4:]) + jnp.concatenate(
                [_dot(a[bb, j, 0, p], v_nb[p * PAIR:(p + 1) * PAIR]) for p in range(DN_PAIRS)], axis=0)
            for h in range(DN_HEADS):
                o_ref[bb, j * DN_CHUNK:(j + 1) * DN_CHUNK, h * LANES:(h + 1) * LANES] = (
                    o[h * DN_CHUNK:(h + 1) * DN_CHUNK])
        nxt = []
        for (bb, d, wq, u, a, kgt, gl, o_ref), s_all, v_nb, j in zip(chains, s_alls, v_nbs, at):
            v_bd = jnp.where(own, jnp.concatenate([v_nb] * DN_HEADS, axis=1), jnp.zeros((), BF16))
            nxt.append(s_all * jnp.exp(gl[bb, j, 0]) + _dot(kgt[bb, j, 0], v_bd))
        s_alls = nxt
    for (bb, d, *_), s_all in zip(chains, s_alls):
        state[bb, d] = s_all


def _dn_scan(wq, u, a, kgt, gl_wide, *, s_len):
    b, nc = wq.shape[:2]
    n_lat = s_len // DN_CHUNK
    n_ctx = nc - n_lat
    assert n_lat % DN_SPS == 0 and n_ctx % DN_SPS == 0
    nb, nb_lat, nb_ctx = nc // DN_SPS, n_lat // DN_SPS, n_ctx // DN_SPS
    fwd = lambda n: jnp.where(n < nb_ctx, nb_lat + n, n - nb_ctx)
    bwd = lambda n: nb - 1 - n
    specs = []
    for d, order in ((0, fwd), (1, bwd)):
        blk = lambda arr, d=d, order=order: pl.BlockSpec(
            (b, DN_SPS, 1) + arr.shape[3:], lambda n: (0, order(n), d) + (0,) * (arr.ndim - 3))
        specs += [blk(wq), blk(u), blk(a), blk(kgt), blk(gl_wide)]
    o_spec = lambda order: pl.BlockSpec((b, DN_SPS * DN_CHUNK, DN_WIDTH), lambda n: (0, order(n), 0))
    return pl.pallas_call(
        _dn_scan_kernel,
        grid=(nb,),
        in_specs=specs,
        out_specs=[o_spec(fwd), o_spec(bwd)],
        out_shape=[jax.ShapeDtypeStruct((b, nc * DN_CHUNK, DN_WIDTH), F32)] * 2,
        scratch_shapes=[pltpu.VMEM((b, 2, DN_HEAD_DIM, DN_WIDTH), F32)],
        compiler_params=_cparams(("arbitrary",)),
        name="dn_scan",
    )(wq, u, a, kgt, gl_wide, wq, u, a, kgt, gl_wide)


MERGE_PARTS = 2


def _merge_kernel(ona_ref, of_ref, ob_ref, z_ref, sc_ref, schp_ref, schn_ref, gate_ref, x_ref,
                  g1l_ref, g1c_ref, sh2l_ref, sh2c_ref, sc2l_ref, sc2c_ref, ng_ref, og_ref, scw_ref,
                  wna_ref, wdn_ref, wsc_ref, wout_ref, wrh_ref, wrl_ref, rb_ref,
                  x1_ref, h2_ref, comb_ref, combt_ref, cpad, *, tm, s_len, ta):
    i = pl.program_id(1)
    d = x_ref.shape[-1]
    half = HALO // 2

    def cu(ref_val):
        v = ref_val.astype(F32)
        return v[:, 2 * SC_WIDTH:] * v[:, :SC_WIDTH]
    cpad[0:half, :] = cu(schp_ref[0])[half:, :]
    cpad[half:half + tm, :] = cu(sc_ref[0])
    cpad[half + tm:, :] = cu(schn_ref[0])[:half, :]

    nr = tm // MERGE_PARTS
    parts = [slice(k * nr, (k + 1) * nr) for k in range(MERGE_PARTS)]

    def select(k, lat, ctx):
        row = i * tm + k * nr + lax.broadcasted_iota(jnp.int32, (nr, 1), 0)
        return jnp.where(row >= s_len, ctx, lat)

    branch_in = []
    for k, rows in enumerate(parts):
        o = of_ref[0, rows, :] + ob_ref[0, rows, :]
        heads = []
        for h in range(DN_HEADS):
            oh = o[:, h * LANES:(h + 1) * LANES]
            heads.append(oh * lax.rsqrt(jnp.mean(oh * oh, axis=-1, keepdims=True) + EPS))
        o_dn = jnp.concatenate(heads, axis=1) * og_ref[...] * _silu(z_ref[0, rows, :].astype(F32))
        gt = i * tm + k * nr + lax.broadcasted_iota(jnp.int32, (nr, 1), 0)
        has_prev = (gt != 0) & (gt != s_len)
        has_next = (gt != s_len - 1) & (gt != ta - 1)
        conv = (scw_ref[0:1, :] * jnp.where(has_prev, cpad[pl.ds(half - 1 + k * nr, nr), :], 0.0)
                + scw_ref[1:2, :] * cpad[pl.ds(half + k * nr, nr), :]
                + scw_ref[2:3, :] * jnp.where(has_next, cpad[pl.ds(half + 1 + k * nr, nr), :], 0.0))
        o_sc = sc_ref[0, rows, SC_WIDTH:2 * SC_WIDTH].astype(F32) * conv
        branch_in.append((o_dn.astype(BF16), o_sc.astype(BF16)))

    merged = []
    for rows, (o_dn, o_sc) in zip(parts, branch_in):
        merged.append((gate_ref[0, rows, :d].astype(F32) * _dot(ona_ref[0, rows, :], wna_ref[...])
                       + gate_ref[0, rows, d:2 * d].astype(F32) * _dot(o_dn, wdn_ref[...])
                       + gate_ref[0, rows, 2 * d:].astype(F32) * _dot(o_sc, wsc_ref[...])).astype(BF16))

    logit_parts = []
    for k, (rows, m) in enumerate(zip(parts, merged)):
        x1 = x_ref[0, rows, :] + select(k, g1l_ref[0], g1c_ref[...]) * _dot(m, wout_ref[...])
        x1_ref[0, rows, :] = x1
        hn = x1 * lax.rsqrt(jnp.mean(x1 * x1, axis=-1, keepdims=True) + EPS) * ng_ref[...]
        h2 = hn * (1.0 + select(k, sc2l_ref[0], sc2c_ref[...])) + select(k, sh2l_ref[0], sh2c_ref[...])
        h2_ref[0, rows, :] = h2.astype(BF16)
        h2_hi, h2_lo = _split_bf16(h2)
        logit_parts.append(_dot3(h2_hi, h2_lo, wrh_ref[...], wrl_ref[...]))

    logits = jnp.concatenate(logit_parts, axis=0)
    scores = jax.nn.sigmoid(logits.T)
    biased = scores + rb_ref[...]
    rows = [biased[e:e + 1, :] for e in range(N_EXPERTS)]
    gsc = []
    for g in range(N_GROUPS):
        a, b_, c_, d_ = rows[4 * g:4 * g + 4]
        gsc.append(jnp.maximum(jnp.maximum(jnp.maximum(a + b_, a + c_), jnp.maximum(a + d_, b_ + c_)),
                               jnp.maximum(b_ + d_, c_ + d_)))
    best = gsc[0]
    grp = jnp.zeros_like(best, dtype=jnp.int32)
    for g in range(1, N_GROUPS):
        upd = gsc[g] > best
        best = jnp.where(upd, gsc[g], best)
        grp = jnp.where(upd, g, grp)
    eidx = lax.broadcasted_iota(jnp.int32, biased.shape, 0)
    masked = jnp.where((eidx // EXPERTS_PER_GROUP) == grp, biased, -jnp.inf)
    big = jnp.int32(LANES)
    m1 = masked.max(axis=0, keepdims=True)
    i1 = jnp.where(masked == m1, eidx, big).min(axis=0, keepdims=True)
    masked2 = jnp.where(eidx == i1, -jnp.inf, masked)
    m2 = masked2.max(axis=0, keepdims=True)
    i2 = jnp.where(masked2 == m2, eidx, big).min(axis=0, keepdims=True)
    sel1 = eidx == i1
    sel2 = eidx == i2
    s1 = jnp.sum(jnp.where(sel1, scores, 0.0), axis=0, keepdims=True)
    s2 = jnp.sum(jnp.where(sel2, scores, 0.0), axis=0, keepdims=True)
    tot = s1 + s2
    comb_t = jnp.where(sel1, s1 / tot, 0.0) + jnp.where(sel2, s2 / tot, 0.0)
    meta_t = jnp.where(eidx == N_EXPERTS, grp.astype(F32), comb_t)
    combt_ref[0] = meta_t
    comb_ref[0] = meta_t.T


def _merge(o_na, o_f, o_b, z, sc, gates, xs, mods_l, mods_c, norm_g, out_g, sc_w, wna, wdn, wsc, wout, wr, rb,
           *, s_len, tm):
    b, ta, d = xs.shape
    col_z = col_sc = 0
    hb = tm // HALO
    n_hb = ta // HALO
    tok = lambda w, col=0: pl.BlockSpec((1, tm, w), lambda bb, i: (bb, i, col))
    vec = lambda w: pl.BlockSpec((1, w), lambda bb, i: (0, 0))
    bvec = lambda: pl.BlockSpec((1, 1, d), lambda bb, i: (bb, 0, 0))
    full = lambda a: pl.BlockSpec(a.shape, lambda bb, i: (0,) * a.ndim)
    g1l, sh2l, sc2l = mods_l
    g1c, sh2c, sc2c = mods_c
    w3 = 3 * SC_WIDTH
    wr_hi = wr.astype(BF16)
    wr_lo = (wr - wr_hi.astype(F32)).astype(BF16)
    return pl.pallas_call(
        functools.partial(_merge_kernel, tm=tm, s_len=s_len, ta=ta),
        grid=(b, ta // tm),
        in_specs=[tok(NA_WIDTH), tok(DN_WIDTH), tok(DN_WIDTH), tok(DN_WIDTH, col_z), tok(w3, col_sc),
                  pl.BlockSpec((1, HALO, w3), lambda bb, i: (bb, jnp.maximum(i * hb - 1, 0), col_sc)),
                  pl.BlockSpec((1, HALO, w3), lambda bb, i: (bb, jnp.minimum((i + 1) * hb, n_hb - 1), col_sc)),
                  tok(3 * d, 0), tok(d),
                  bvec(), vec(d), bvec(), vec(d), bvec(), vec(d), vec(d), vec(DN_WIDTH),
                  full(sc_w), full(wna), full(wdn), full(wsc), full(wout), full(wr_hi), full(wr_lo), full(rb)],
        out_specs=[tok(d), tok(d), tok(LANES), pl.BlockSpec((1, LANES, tm), lambda bb, i: (bb, 0, i))],
        out_shape=[jax.ShapeDtypeStruct((b, ta, d), F32),
                   jax.ShapeDtypeStruct((b, ta, d), BF16),
                   jax.ShapeDtypeStruct((b, ta, LANES), F32),
                   jax.ShapeDtypeStruct((b, LANES, ta), F32)],
        scratch_shapes=[pltpu.VMEM((tm + HALO, SC_WIDTH), F32)],
        compiler_params=_cparams(("parallel", "parallel")),
        name="merge_router",
    )(o_na, o_f, o_b, z, sc, sc, sc, gates, xs, g1l, g1c, sh2l, sh2c, sc2l, sc2c, norm_g, out_g,
      sc_w, wna, wdn, wsc, wout, wr_hi, wr_lo, rb)


MOE_CAP = 256


def _moe_kernel(h_ref, comb_ref, combt_ref, lt_ref, gt_ref, wg_ref, wu_ref, wd_ref, x_ref, g2l_ref, g2c_ref,
                o_ref, acc, rank_r, rank_c, *, tm, s_len):
    i = pl.program_id(1)
    g = pl.program_id(2)

    comb = comb_ref[0]
    lane = lax.broadcasted_iota(jnp.int32, (tm, LANES), 1)
    grp_row = combt_ref[0, N_EXPERTS:N_EXPERTS + 1, :]
    grp_col = jnp.sum(jnp.where(lane == N_EXPERTS, comb, 0.0), axis=-1, keepdims=True)

    @pl.when(g == 0)
    def _():
        acc[...] = jnp.zeros_like(acc)
        sub = lax.broadcasted_iota(jnp.int32, (8, tm), 0).astype(F32)
        before = _dot(jnp.where(sub == grp_row, 1.0, 0.0).astype(BF16), lt_ref[...])
        rank_r[...] = jnp.broadcast_to(
            jnp.sum(jnp.where(sub == grp_row, before, 0.0), axis=0, keepdims=True), (8, tm))
        lane_f = lane.astype(F32)
        before = _dot(gt_ref[...], jnp.where(lane_f == grp_col, 1.0, 0.0).astype(BF16))
        rank_c[...] = jnp.broadcast_to(
            jnp.sum(jnp.where(lane_f == grp_col, before, 0.0), axis=-1, keepdims=True), (tm, LANES))

    gf = g.astype(F32)
    member_row = grp_row == gf
    member_col = grp_col == gf
    rank_row = rank_r[0:1, :]
    rank_col = rank_c[:, 0:1]
    count = jnp.sum(jnp.where(member_row, 1.0, 0.0))
    comb_hi, comb_lo = _split_bf16(comb)
    d = x_ref.shape[-1]

    for k in range(tm // MOE_CAP):
        @pl.when(count > k * MOE_CAP)
        def _(k=k):
            slot_c = (lax.broadcasted_iota(jnp.int32, (MOE_CAP, 1), 0) + k * MOE_CAP).astype(F32)
            gather = jnp.where(member_row & (rank_row == slot_c), 1.0, 0.0).astype(BF16)
            hs = _dot(gather, h_ref[0]).astype(BF16)
            cws = _dot(gather, comb_hi) + _dot(gather, comb_lo)
            lane_c = lax.broadcasted_iota(jnp.int32, (MOE_CAP, LANES), 1)
            experts = range(EXPERTS_PER_GROUP)
            gates = [_dot(hs, wg_ref[0, e4]) for e4 in experts]
            ups = [_dot(hs, wu_ref[0, e4]) for e4 in experts]
            hiddens = []
            for e4 in experts:
                cw = jnp.sum(jnp.where(lane_c == g * EXPERTS_PER_GROUP + e4, cws, 0.0), axis=-1, keepdims=True)
                hiddens.append((_silu(gates[e4]) * ups[e4] * cw).astype(BF16))
            ys = _dot(hiddens[0], wd_ref[0, 0])
            for e4 in experts[1:]:
                ys = ys + _dot(hiddens[e4], wd_ref[0, e4])
            slot_r = (lax.broadcasted_iota(jnp.int32, (1, MOE_CAP), 1) + k * MOE_CAP).astype(F32)
            scatter = jnp.where(member_col & (rank_col == slot_r), 1.0, 0.0).astype(BF16)
            ys_hi, ys_lo = _split_bf16(ys)
            acc[...] += _dot(scatter, ys_hi) + _dot(scatter, ys_lo)

    @pl.when(g == N_GROUPS - 1)
    def _():
        o_ref[0] = x_ref[0] + _row_select(i, tm, s_len, g2l_ref[0], g2c_ref[...]) * acc[...]


def _moe(h2, comb, comb_t, wg, wu, wd, x1, g2l, g2c, *, layer, s_len, tm):
    b, ta, d = x1.shape
    tok = lambda w: pl.BlockSpec((1, tm, w), lambda bb, i, g: (bb, i, 0))
    grp = lambda a: pl.BlockSpec((1, EXPERTS_PER_GROUP) + a.shape[2:], lambda bb, i, g: (layer, g, 0, 0))
    tri = lambda: pl.BlockSpec((tm, tm), lambda bb, i, g: (0, 0))
    earlier = np.arange(tm)[:, None] < np.arange(tm)[None, :]
    lt = jnp.asarray(earlier, BF16)
    gt = jnp.asarray(earlier.T, BF16)
    return pl.pallas_call(
        functools.partial(_moe_kernel, tm=tm, s_len=s_len),
        grid=(b, ta // tm, N_GROUPS),
        in_specs=[tok(d), tok(LANES),
                  pl.BlockSpec((1, LANES, tm), lambda bb, i, g: (bb, 0, i)),
                  tri(), tri(),
                  grp(wg), grp(wu), grp(wd),
                  tok(d),
                  pl.BlockSpec((1, 1, d), lambda bb, i, g: (bb, 0, 0)),
                  pl.BlockSpec((1, d), lambda bb, i, g: (0, 0))],
        out_specs=tok(d),
        out_shape=jax.ShapeDtypeStruct((b, ta, d), F32),
        scratch_shapes=[pltpu.VMEM((tm, d), F32), pltpu.VMEM((8, tm), F32), pltpu.VMEM((tm, LANES), F32)],
        compiler_params=_cparams(("parallel", "parallel", "arbitrary")),
        name="moe_grouped",
    )(h2, comb, comb_t, lt, gt, wg, wu, wd, x1, g2l, g2c)


def _rope_tables(s_len, ctx_len):
    t = np.arange(s_len)
    n_freq = DN_HEAD_DIM // 4
    inv_freq = (np.float32(ROPE_BASE) ** (-np.arange(n_freq, dtype=np.float32) / n_freq)).astype(np.float32)
    pos = np.stack([t // GRID_W, t % GRID_W], axis=-1).astype(np.float32)
    ang = (pos[..., None] * inv_freq).astype(np.float32)
    cos, sin = np.cos(ang.astype(np.float64)), np.sin(ang.astype(np.float64))
    rc = np.concatenate([cos[:, 0], cos[:, 0], cos[:, 1], cos[:, 1]], axis=-1)
    rs = np.concatenate([-sin[:, 0], sin[:, 0], -sin[:, 1], sin[:, 1]], axis=-1)
    rc = np.concatenate([rc, np.ones((ctx_len, LANES))], axis=0)
    rs = np.concatenate([rs, np.zeros((ctx_len, LANES))], axis=0)
    return jnp.asarray(rc, F32), jnp.asarray(rs, F32)


def _pick_tile(ta, candidates):
    for t in candidates:
        if ta % t == 0:
            return t
    raise ValueError(f"no row tile for {ta} tokens")


def kernel(x, c, ctx, c_ctx, w_ada, b_ada, norm_mix_g, norm_ffn_g, w_in, na_q_norm_g, na_k_norm_g, na_rpb, dn_conv_w, dn_a_log, dn_dt_bias, dn_out_norm_g, sc_conv_w, w_branch_na, w_branch_dn, w_branch_sc, w_out, w_router, router_bias, moe_w_gate, moe_w_up, moe_w_down):
    b, s_len, d = x.shape
    ctx_len = ctx.shape[1]
    depth = w_ada.shape[0]
    ta = s_len + ctx_len
    assert ctx_len == NA_BLK and s_len % (WIN_R * GRID_W) == 0 and s_len // NA_BLK >= 4
    assert ctx_len % PREP_TILE == 0 and b + 1 <= 8
    tm_proj = _pick_tile(ta, (768, 512, 256))
    tm_merge = _pick_tile(ta, (384, 256))
    tm_moe = _pick_tile(ta, (768, 512, 256))

    xs = jnp.concatenate([x, ctx], axis=1)
    c_rows = jnp.zeros((8, d), F32).at[:b].set(c).at[b].set(c_ctx)
    mod = _modulation(c_rows, w_ada, b_ada)

    sizes = (3 * NA_WIDTH, 3 * DN_WIDTH, DN_WIDTH, 4 * DN_HEADS, 3 * SC_WIDTH, 3 * d)
    cuts = [int(v) for v in np.cumsum((0,) + sizes)]
    seg = lambda l, k: lax.slice(w_in, (l, 0, cuts[k]), (l + 1, d, cuts[k + 1]))[0]
    wg_all, wu_all, wd_all = moe_w_gate.astype(BF16), moe_w_up.astype(BF16), moe_w_down.astype(BF16)

    rope_c, rope_s = _rope_tables(s_len, ctx_len)
    bd = jnp.asarray(np.kron(np.eye(NA_HEADS), np.ones((NA_HEAD_DIM, NA_HEAD_DIM))), BF16)
    nc = ta // DN_CHUNK
    wr = jnp.zeros((d, LANES), F32).at[:, :N_EXPERTS].set(w_router)
    rb = jnp.zeros((LANES, 1), F32).at[:N_EXPERTS, 0].set(router_bias)

    for l in range(depth):
        m = mod[l].reshape(8, N_MOD, d)
        lat = lambda k: m[:b, k][:, None, :]
        cx = lambda k: m[b:b + 1, k]
        w_ab = jnp.zeros((d, LANES), F32).at[:, :4 * DN_HEADS].set(seg(l, 3))
        gqk = jnp.concatenate([jnp.tile(na_q_norm_g[l], NA_HEADS) * (NA_HEAD_DIM ** -0.5),
                               jnp.tile(na_k_norm_g[l], NA_HEADS)])[None, :]
        gates, na_q, na_k, na_v, dn_qkv, sc_p, dn_z, ab = _projection(
            xs, (lat(0), lat(1)), (cx(0), cx(1)), norm_mix_g[l][None, :],
            seg(l, 5).astype(BF16), seg(l, 0).astype(BF16), seg(l, 1).astype(BF16),
            seg(l, 4).astype(BF16), seg(l, 2).astype(BF16), w_ab, gqk, bd, s_len=s_len, tm=tm_proj)

        o_na = _neighbourhood_attention(na_q, na_k, na_v, _na_bias_table(na_rpb[l]), s_len=s_len)

        alog_row = jnp.zeros((1, LANES), F32).at[0, :2 * DN_HEADS].set(dn_a_log[l].reshape(-1))
        dtb_row = jnp.zeros((1, LANES), F32).at[0, :2 * DN_HEADS].set(dn_dt_bias[l].reshape(-1))
        qd, kd, vd, gc, bc = _dn_prep(dn_qkv, ab, dn_conv_w[l], rope_c, rope_s, alog_row, dtb_row,
                                      s_len=s_len)
        g8 = gc[..., :2 * DN_HEADS].reshape(b, nc, DN_CHUNK, 2, DN_HEADS)
        g_rows = jnp.transpose(g8, (0, 1, 3, 4, 2)).reshape(b, nc, 2, CH4)
        g_last = jnp.stack([g8[:, :, DN_CHUNK - 1, 0], g8[:, :, 0, 1]], axis=2)
        gl_wide = jnp.repeat(g_last, LANES, axis=-1)[:, :, :, None, :]
        wq, u, a, kgt = _dn_local(qd, kd, vd, gc, bc, g_rows)
        o_f, o_b = _dn_scan(wq, u, a, kgt, gl_wide, s_len=s_len)

        x1, h2, comb, comb_t = _merge(
            o_na, o_f, o_b, dn_z, sc_p, gates, xs, (lat(2), lat(3), lat(4)), (cx(2), cx(3), cx(4)),
            norm_ffn_g[l][None, :], jnp.tile(dn_out_norm_g[l], DN_HEADS)[None, :], sc_conv_w[l],
            w_branch_na[l].astype(BF16), w_branch_dn[l].astype(BF16), w_branch_sc[l].astype(BF16),
            w_out[l].astype(BF16), wr, rb, s_len=s_len, tm=tm_merge)

        xs = _moe(h2, comb, comb_t, wg_all, wu_all, wd_all, x1, lat(5), cx(5), layer=l, s_len=s_len, tm=tm_moe)

    return xs[:, :s_len]
```

```python
import functools
import math

import numpy as np
import jax
import jax.numpy as jnp
from jax import lax
from jax.experimental import pallas as pl
from jax.experimental.pallas import tpu as pltpu

F32 = jnp.float32
BF16 = jnp.bfloat16
HIGHEST = lax.Precision.HIGHEST

GRID_W = 64
NA_HEADS = 8
NA_HEAD_DIM = 64
NA_WIDTH = NA_HEADS * NA_HEAD_DIM
WIN_R = 8
WIN_C = 16
DN_HEADS = 4
DN_HEAD_DIM = 128
DN_WIDTH = DN_HEADS * DN_HEAD_DIM
DN_CONV = 5
DN_CHUNK = 64
ROPE_BASE = 10000.0
SC_WIDTH = 512
SC_CONV = 3
N_EXPERTS = 16
N_GROUPS = 4
EXPERTS_PER_GROUP = N_EXPERTS // N_GROUPS
D_EXPERT = 512
N_MOD = 6
EPS = 1e-6
NEG_INF = -1e30

LANES = 128
HALO = 16
PREP_TILE = 256
VMEM_LIMIT = 56 * 1024 * 1024

N_PROJ_TILE = 512


def _cparams(sem):
    return pltpu.CompilerParams(dimension_semantics=sem, vmem_limit_bytes=VMEM_LIMIT)


def _dot(a, b):
    return jnp.dot(a, b, preferred_element_type=F32)


def _dot_nt(a, b):
    return lax.dot_general(a, b, (((1,), (1,)), ((), ())), preferred_element_type=F32)


def _silu(x):
    return x * jax.nn.sigmoid(x)


def _row_select(i, tm, s_len, lat, ctx):
    row = i * tm + lax.broadcasted_iota(jnp.int32, (tm, 1), 0)
    return jnp.where(row >= s_len, ctx, lat)


def _mod_kernel(c_ref, w_ref, b_ref, o_ref):
    c_hi, c_lo = _split_bf16(_silu(c_ref[...]))
    w_hi, w_lo = _split_bf16(w_ref[0])
    o_ref[0] = _dot3(c_hi, c_lo, w_hi, w_lo) + b_ref[0]


def _modulation(c_rows, w_ada, b_ada):
    depth, d, n = w_ada.shape
    tn = 512
    return pl.pallas_call(
        _mod_kernel,
        grid=(depth, n // tn),
        in_specs=[pl.BlockSpec((8, d), lambda l, j: (0, 0)),
                  pl.BlockSpec((1, d, tn), lambda l, j: (l, 0, j)),
                  pl.BlockSpec((1, 1, tn), lambda l, j: (l, 0, j))],
        out_specs=pl.BlockSpec((1, 8, tn), lambda l, j: (l, 0, j)),
        out_shape=jax.ShapeDtypeStruct((depth, 8, n), F32),
        compiler_params=_cparams(("parallel", "parallel")),
        name="adaln_mod",
    )(c_rows, w_ada, b_ada.reshape(depth, 1, n))


def _split_bf16(a):
    hi = a.astype(BF16)
    return hi, (a - hi.astype(F32)).astype(BF16)


def _dot3(a_hi, a_lo, b_hi, b_lo):
    return _dot(a_hi, b_hi) + _dot(a_lo, b_hi) + _dot(a_hi, b_lo)


def _modulated_norm(x_ref, shl_ref, scl_ref, shc_ref, scc_ref, g_ref, *, tm, s_len):
    i = pl.program_id(1)
    x = x_ref[0]
    y = x * lax.rsqrt(jnp.mean(x * x, axis=-1, keepdims=True) + EPS) * g_ref[...]
    scale = _row_select(i, tm, s_len, scl_ref[0], scc_ref[...])
    shift = _row_select(i, tm, s_len, shl_ref[0], shc_ref[...])
    return y * (1.0 + scale) + shift


def _proj_gate_kernel(x_ref, shl_ref, scl_ref, shc_ref, scc_ref, g_ref, w_ref, gate_ref, *, tm, s_len):
    h = _modulated_norm(x_ref, shl_ref, scl_ref, shc_ref, scc_ref, g_ref, tm=tm, s_len=s_len).astype(BF16)
    for c0 in range(0, w_ref.shape[1], N_PROJ_TILE):
        cols = slice(c0, c0 + N_PROJ_TILE)
        gate_ref[0, :, cols] = jax.nn.sigmoid(_dot(h, w_ref[:, cols])).astype(BF16)


def _proj_mix_kernel(x_ref, shl_ref, scl_ref, shc_ref, scc_ref, g_ref, wna_ref, wdn_ref, wsc_ref, wz_ref,
                     wabh_ref, wabl_ref, gqk_ref, bd_ref,
                     q_ref, k_ref, v_ref, dn_ref, sc_ref, z_ref, ab_ref, *, tm, s_len):
    h = _modulated_norm(x_ref, shl_ref, scl_ref, shc_ref, scc_ref, g_ref, tm=tm, s_len=s_len)
    h_hi, h_lo = _split_bf16(h)
    ab_ref[0] = _dot3(h_hi, h_lo, wabh_ref[...], wabl_ref[...])
    for part, out in enumerate((q_ref, k_ref)):
        cols = slice(part * NA_WIDTH, (part + 1) * NA_WIDTH)
        a = _dot(h_hi, wna_ref[:, cols])
        ss = _dot((a * a).astype(BF16), bd_ref[...])
        out[0] = (a * lax.rsqrt(ss * (1.0 / NA_HEAD_DIM) + EPS) * gqk_ref[:, cols]).astype(BF16)
    v_ref[0] = _dot(h_hi, wna_ref[:, 2 * NA_WIDTH:]).astype(BF16)
    for w_ref, out in ((wdn_ref, dn_ref), (wsc_ref, sc_ref), (wz_ref, z_ref)):
        for c0 in range(0, w_ref.shape[1], N_PROJ_TILE):
            cols = slice(c0, c0 + N_PROJ_TILE)
            out[0, :, cols] = _dot(h_hi, w_ref[:, cols]).astype(BF16)


def _projection(xs, mod_l, mod_c, norm_g, w_gate, w_na, w_dn, w_sc, w_z, w_ab, gqk, bd, *, s_len, tm):
    b, ta, d = xs.shape
    shl, scl = mod_l
    shc, scc = mod_c
    wab_hi, wab_lo = _split_bf16(w_ab)
    tok = lambda w: pl.BlockSpec((1, tm, w), lambda bb, i: (bb, i, 0))
    bvec = lambda: pl.BlockSpec((1, 1, d), lambda bb, i: (bb, 0, 0))
    full = lambda a: pl.BlockSpec(a.shape, lambda bb, i: (0,) * a.ndim)
    head = [tok(d), bvec(), bvec(), full(shc), full(scc), full(norm_g)]
    gates = pl.pallas_call(
        functools.partial(_proj_gate_kernel, tm=tm, s_len=s_len),
        grid=(b, ta // tm),
        in_specs=head + [full(w_gate)],
        out_specs=tok(w_gate.shape[1]),
        out_shape=jax.ShapeDtypeStruct((b, ta, w_gate.shape[1]), BF16),
        compiler_params=_cparams(("parallel", "parallel")),
        name="proj_gates",
    )(xs, shl, scl, shc, scc, norm_g, w_gate)
    consts = (w_na, w_dn, w_sc, w_z, wab_hi, wab_lo, gqk, bd)
    widths = (NA_WIDTH, NA_WIDTH, NA_WIDTH, w_dn.shape[1], w_sc.shape[1], w_z.shape[1])
    outs = pl.pallas_call(
        functools.partial(_proj_mix_kernel, tm=tm, s_len=s_len),
        grid=(b, ta // tm),
        in_specs=head + [full(a) for a in consts],
        out_specs=[tok(w) for w in widths] + [tok(LANES)],
        out_shape=[jax.ShapeDtypeStruct((b, ta, w), BF16) for w in widths]
        + [jax.ShapeDtypeStruct((b, ta, LANES), F32)],
        compiler_params=_cparams(("parallel", "parallel")),
        name="proj_mixers",
    )(xs, shl, scl, shc, scc, norm_g, *consts)
    return (gates,) + tuple(outs)


NA_QROWS = 4
NA_BLK = NA_QROWS * GRID_W
WIN_TOK = WIN_R * GRID_W


def _stack_two_heads(q2):
    lane = lax.broadcasted_iota(jnp.int32, q2.shape, 1)
    zero = jnp.zeros_like(q2)
    return jnp.concatenate([jnp.where(lane < NA_HEAD_DIM, q2, zero),
                            jnp.where(lane >= NA_HEAD_DIM, q2, zero)], axis=0)


def _unstack_two_heads(o2):
    m = o2.shape[0] // 2
    lane = lax.broadcasted_iota(jnp.int32, (m, o2.shape[1]), 1)
    return jnp.where(lane < NA_HEAD_DIM, o2[:m], o2[m:])


def _na_kernel(q_ref, kp_ref, kc_ref, kn_ref, vp_ref, vc_ref, vn_ref, kx_ref, vx_ref, bias_ref, o_ref,
               kbuf, vbuf, *, n_rows):
    i = pl.program_id(1)
    n_lat = n_rows // NA_QROWS

    @pl.when(i < n_lat)
    def _():
        for slot, (kr, vr) in enumerate(((kp_ref, vp_ref), (kc_ref, vc_ref), (kn_ref, vn_ref))):
            kbuf[slot * NA_BLK:(slot + 1) * NA_BLK, :] = kr[0]
            vbuf[slot * NA_BLK:(slot + 1) * NA_BLK, :] = vr[0]
        win = []
        for j in range(NA_QROWS):
            r = i * NA_QROWS + j
            start = jnp.clip(r - WIN_R // 2, 0, n_rows - WIN_R)
            win.append((pl.multiple_of((start - (i - 1) * NA_QROWS) * GRID_W, GRID_W), r - start))

        rows2 = 2 * GRID_W
        pairs = [slice(hp * LANES, (hp + 1) * LANES) for hp in range(NA_HEADS // 2)]
        logits = []
        for hp, lanes in enumerate(pairs):
            qs = jnp.concatenate([_stack_two_heads(q_ref[0, j * GRID_W:(j + 1) * GRID_W, lanes])
                                  for j in range(NA_QROWS)], axis=0)
            s_c = _dot_nt(qs, kx_ref[0, :, lanes])
            s_w = jnp.concatenate(
                [_dot_nt(qs[j * rows2:(j + 1) * rows2], kbuf[pl.ds(win[j][0], WIN_TOK), lanes])
                 + bias_ref[win[j][1], hp] for j in range(NA_QROWS)], axis=0)
            logits.append((s_w, s_c))
        probs = []
        for s_w, s_c in logits:
            m = jnp.maximum(s_w.max(axis=-1, keepdims=True), s_c.max(axis=-1, keepdims=True))
            p_w = jnp.exp(s_w - m)
            p_c = jnp.exp(s_c - m)
            denom = p_w.sum(axis=-1, keepdims=True) + p_c.sum(axis=-1, keepdims=True)
            probs.append((p_w.astype(BF16), p_c.astype(BF16), denom))
        for lanes, (p_wb, p_cb, denom) in zip(pairs, probs):
            o = _dot(p_cb, vx_ref[0, :, lanes]) + jnp.concatenate(
                [_dot(p_wb[j * rows2:(j + 1) * rows2], vbuf[pl.ds(win[j][0], WIN_TOK), lanes])
                 for j in range(NA_QROWS)], axis=0)
            o = o / denom
            for j in range(NA_QROWS):
                o_ref[0, j * GRID_W:(j + 1) * GRID_W, lanes] = _unstack_two_heads(
                    o[j * rows2:(j + 1) * rows2]).astype(BF16)

    @pl.when(i == n_lat)
    def _():
        for hp in range(NA_HEADS // 2):
            lanes = slice(hp * LANES, (hp + 1) * LANES)
            s = _dot_nt(_stack_two_heads(q_ref[0, :, lanes]), kx_ref[0, :, lanes])
            p = jnp.exp(s - s.max(axis=-1, keepdims=True))
            o = _dot(p.astype(BF16), vx_ref[0, :, lanes]) / p.sum(axis=-1, keepdims=True)
            o_ref[0, :, lanes] = _unstack_two_heads(o).astype(BF16)


def _neighbourhood_attention(q, k, v, bias_tab, *, s_len):
    b, ta, _ = q.shape
    n_rows = s_len // GRID_W
    n_lat = n_rows // NA_QROWS
    blk = lambda off: pl.BlockSpec((1, NA_BLK, NA_WIDTH), lambda bb, i: (bb, jnp.clip(i + off, 0, n_lat - 1), 0))
    cx = lambda: pl.BlockSpec((1, NA_BLK, NA_WIDTH), lambda bb, i: (bb, n_lat, 0))
    own = lambda: pl.BlockSpec((1, NA_BLK, NA_WIDTH), lambda bb, i: (bb, i, 0))
    return pl.pallas_call(
        functools.partial(_na_kernel, n_rows=n_rows),
        grid=(b, n_lat + 1),
        in_specs=[own(), blk(-1), blk(0), blk(1), blk(-1), blk(0), blk(1), cx(), cx(),
                  pl.BlockSpec(bias_tab.shape, lambda bb, i: (0, 0, 0, 0))],
        out_specs=own(),
        out_shape=jax.ShapeDtypeStruct((b, ta, NA_WIDTH), BF16),
        scratch_shapes=[pltpu.VMEM((3 * NA_BLK, NA_WIDTH), BF16),
                        pltpu.VMEM((3 * NA_BLK, NA_WIDTH), BF16)],
        compiler_params=_cparams(("parallel", "arbitrary")),
        name="na_attention",
    )(q, k, k, k, v, v, v, k, v, bias_tab)


def _na_bias_table(rpb):
    qc = np.arange(GRID_W)[:, None]
    kc = np.arange(GRID_W)[None, :]
    win_c0 = np.clip(qc - WIN_C // 2, 0, GRID_W - WIN_C)
    in_win = (kc >= win_c0) & (kc < win_c0 + WIN_C)
    rel_c = np.clip(kc - qc + WIN_C - 1, 0, 2 * WIN_C - 2)
    pick_c = (rel_c[..., None] == np.arange(2 * WIN_C - 1)).astype(np.float32)
    by_col = jnp.einsum('hab,qkb->haqk', rpb.astype(F32), pick_c, precision=HIGHEST)
    by_col = jnp.where(in_win[None, None], by_col, NEG_INF)
    tab = jnp.stack([by_col[:, WIN_R - 1 - v:2 * WIN_R - 1 - v] for v in range(WIN_R)], axis=0)
    tab = jnp.transpose(tab, (0, 1, 3, 2, 4))
    return tab.reshape(WIN_R, NA_HEADS // 2, 2 * GRID_W, WIN_R * GRID_W)


def _dn_prep_kernel(x_ref, hp_ref, hn_ref, w_ref, rc_ref, rs_ref, ab_ref, alog_ref, dtb_ref,
                    q_ref, k_ref, v_ref, g_ref, beta_ref, xpad, *, s_len, ta):
    i = pl.program_id(1)
    t0 = i * PREP_TILE
    has_prev = jnp.logical_and(t0 != 0, t0 != s_len).astype(F32)
    has_next = jnp.logical_and(t0 + PREP_TILE != s_len, t0 + PREP_TILE != ta).astype(F32)
    half = HALO // 2
    xpad[0:half, :] = hp_ref[0].astype(F32)[half:, :] * has_prev
    xpad[half:half + PREP_TILE, :] = x_ref[0].astype(F32)
    xpad[half + PREP_TILE:, :] = hn_ref[0].astype(F32)[:half, :] * has_next
    xp = xpad[...]
    n_pad = xp.shape[0]
    y = jnp.zeros((PREP_TILE, x_ref.shape[-1]), F32)
    for tap in range(DN_CONV):
        shifted = xp if tap == DN_CONV // 2 else pltpu.roll(xp, (DN_CONV // 2 - tap) % n_pad, 0)
        y = y + w_ref[tap:tap + 1, :] * shifted[half:half + PREP_TILE]
    y = _silu(y)

    lane = lax.broadcasted_iota(jnp.int32, (PREP_TILE, LANES), 1)
    first_half = (lane % (LANES // 2)) < (LANES // 4)
    rc = rc_ref[...]
    rs = rs_ref[...]

    def norm_rope(u, scale):
        u = u * (lax.rsqrt(jnp.sum(u * u, axis=-1, keepdims=True) + EPS) * scale)
        partner = jnp.where(first_half, pltpu.roll(u, LANES - LANES // 4, 1), pltpu.roll(u, LANES // 4, 1))
        return u * rc + partner * rs

    for h in range(DN_HEADS):
        lanes = slice(h * LANES, (h + 1) * LANES)
        qh = norm_rope(y[:, h * LANES:(h + 1) * LANES], DN_HEAD_DIM ** -0.5)
        kh = norm_rope(y[:, DN_WIDTH + h * LANES:DN_WIDTH + (h + 1) * LANES], 1.0)
        q_ref[0, :, lanes] = qh.astype(BF16)
        k_ref[0, :, lanes] = kh.astype(BF16)
    v_ref[0] = y[:, 2 * DN_WIDTH:].astype(BF16)

    ab = ab_ref[0]
    z = ab + dtb_ref[...]
    softplus = jnp.maximum(z, 0.0) + jnp.log1p(jnp.exp(-jnp.abs(z)))
    g = -jnp.exp(alog_ref[...]) * softplus
    g = jnp.where(lane < 2 * DN_HEADS, g, 0.0)
    beta = jax.nn.sigmoid(pltpu.roll(ab, LANES - 2 * DN_HEADS, 1))
    beta_ref[0] = jnp.where(lane < 2 * DN_HEADS, beta, 0.0)

    r = lax.broadcasted_iota(jnp.int32, (PREP_TILE, PREP_TILE), 0)
    c = lax.broadcasted_iota(jnp.int32, (PREP_TILE, PREP_TILE), 1)
    same = (r // DN_CHUNK) == (c // DN_CHUNK)
    tri_f = jnp.where(same & (c <= r), 1.0, 0.0).astype(BF16)
    tri_b = jnp.where(same & (c >= r), 1.0, 0.0).astype(BF16)
    g1 = g.astype(BF16)
    rem = g - g1.astype(F32)
    g2 = rem.astype(BF16)
    g3 = (rem - g2.astype(F32)).astype(BF16)
    cs_f = _dot(tri_f, g1) + _dot(tri_f, g2) + _dot(tri_f, g3)
    cs_b = _dot(tri_b, g1) + _dot(tri_b, g2) + _dot(tri_b, g3)
    g_ref[0] = jnp.where(lane < DN_HEADS, cs_f, cs_b)


def _dn_prep(p, ab, conv_w, rope_c, rope_s, alog_row, dtb_row, *, s_len):
    b, ta, w3 = p.shape
    col_dn = 0
    n_t = ta // PREP_TILE
    hb = PREP_TILE // HALO
    n_hb = ta // HALO
    tok = lambda: pl.BlockSpec((1, PREP_TILE, DN_WIDTH), lambda bb, i: (bb, i, 0))
    sca = lambda: pl.BlockSpec((1, PREP_TILE, LANES), lambda bb, i: (bb, i, 0))
    row = lambda: pl.BlockSpec((1, LANES), lambda bb, i: (0, 0))
    return pl.pallas_call(
        functools.partial(_dn_prep_kernel, s_len=s_len, ta=ta),
        grid=(b, n_t),
        in_specs=[pl.BlockSpec((1, PREP_TILE, w3), lambda bb, i: (bb, i, col_dn)),
                  pl.BlockSpec((1, HALO, w3), lambda bb, i: (bb, jnp.maximum(i * hb - 1, 0), col_dn)),
                  pl.BlockSpec((1, HALO, w3), lambda bb, i: (bb, jnp.minimum((i + 1) * hb, n_hb - 1), col_dn)),
                  pl.BlockSpec((DN_CONV, w3), lambda bb, i: (0, 0)),
                  pl.BlockSpec((PREP_TILE, LANES), lambda bb, i: (i, 0)),
                  pl.BlockSpec((PREP_TILE, LANES), lambda bb, i: (i, 0)),
                  sca(), row(), row()],
        out_specs=[tok(), tok(), tok(), sca(), sca()],
        out_shape=[jax.ShapeDtypeStruct((b, ta, DN_WIDTH), BF16)] * 3
        + [jax.ShapeDtypeStruct((b, ta, LANES), F32)] * 2,
        scratch_shapes=[pltpu.VMEM((PREP_TILE + HALO, w3), F32)],
        compiler_params=_cparams(("parallel", "parallel")),
        name="dn_prep",
    )(p, p, p, conv_w, rope_c, rope_s, ab, alog_row, dtb_row)


CH4 = DN_HEADS * DN_CHUNK


DN_CPS = 4
DN_PAIRS = DN_HEADS // 2
PAIR = 2 * DN_CHUNK


def _stack_pair(x, p):
    return jnp.concatenate([x[:, (2 * p + h) * LANES:(2 * p + h + 1) * LANES] for h in range(2)], axis=0)


def _dn_local_kernel(q_ref, k_ref, v_ref, gc_ref, bc_ref, gr_ref, wq_ref, u_ref, a_ref, kgt_ref):
    r = lax.broadcasted_iota(jnp.int32, (PAIR, PAIR), 0)
    c = lax.broadcasted_iota(jnp.int32, (PAIR, PAIR), 1)
    same = (r // DN_CHUNK) == (c // DN_CHUNK)
    chains = []
    for cc in range(DN_CPS):
        toks = slice(cc * DN_CHUNK, (cc + 1) * DN_CHUNK)
        gc = gc_ref[0, toks, :]
        bc = bc_ref[0, toks, :]
        for p in range(DN_PAIRS):
            rows = slice(p * PAIR, (p + 1) * PAIR)
            kst = _stack_pair(k_ref[0, toks, :], p)
            qst = _stack_pair(q_ref[0, toks, :], p)
            vst = _stack_pair(v_ref[0, toks, :], p)
            kq = _dot_nt(jnp.concatenate([kst, qst], axis=0), kst)
            kk = kq[:PAIR]
            qk = kq[PAIR:]
            kf = kst.astype(F32)
            for d in range(2):
                later = (r >= c) if d == 0 else (r <= c)
                incl = same & later
                strict = incl & (r != c)
                lane0 = d * DN_HEADS + 2 * p
                col = lambda a: jnp.concatenate([a[:, lane0 + h:lane0 + h + 1] for h in range(2)], axis=0)
                g_col = col(gc)
                b_col = col(bc)
                last = DN_CHUNK - 1 if d == 0 else 0
                gl_col = jnp.concatenate(
                    [jnp.broadcast_to(gc[last:last + 1, lane0 + h:lane0 + h + 1], (DN_CHUNK, 1))
                     for h in range(2)], axis=0)
                g_row = gr_ref[0, cc, d:d + 1, rows]
                decay = jnp.exp(jnp.where(incl, g_col - g_row, NEG_INF))
                pw = jnp.where(strict, -(kk * decay * b_col), 0.0)
                z = jnp.concatenate([vst.astype(F32) * b_col, kf * (b_col * jnp.exp(g_col))], axis=1)
                chains.append([pw, z, (cc, d, rows)])
                wq_ref[0, cc, d, CH4 + p * PAIR:CH4 + (p + 1) * PAIR, :] = (
                    qst.astype(F32) * jnp.exp(g_col)).astype(BF16)
                a_ref[0, cc, d, p] = (qk * decay).astype(BF16)
                kgt_ref[0, cc, d, :, rows] = (kf * jnp.exp(gl_col - g_col)).T.astype(BF16)

    for step in range(6):
        if step:
            for chain in chains:
                pwb = chain[0].astype(BF16)
                chain[0] = _dot(pwb, pwb)
        for chain in chains:
            pw, z, _ = chain
            chain[1] = z + _dot(pw.astype(BF16), z.astype(BF16))
    for _, z, (cc, d, rows) in chains:
        wq_ref[0, cc, d, rows, :] = z[:, LANES:].astype(BF16)
        u_ref[0, cc, d, rows, :] = z[:, :LANES]


def _dn_local(qd, kd, vd, gc, bc, g_rows):
    b, ta, _ = qd.shape
    nc = ta // DN_CHUNK
    tok = lambda w: pl.BlockSpec((1, DN_CPS * DN_CHUNK, w), lambda bb, n: (bb, n, 0))
    out = lambda *tail: pl.BlockSpec((1, DN_CPS, 2) + tail, lambda bb, n: (bb, n, 0) + (0,) * len(tail))
    return pl.pallas_call(
        _dn_local_kernel,
        grid=(b, nc // DN_CPS),
        in_specs=[tok(DN_WIDTH), tok(DN_WIDTH), tok(DN_WIDTH), tok(LANES), tok(LANES),
                  pl.BlockSpec((1, DN_CPS, 2, CH4), lambda bb, n: (bb, n, 0, 0))],
        out_specs=[out(2 * CH4, LANES), out(CH4, LANES), out(DN_PAIRS, PAIR, PAIR), out(LANES, CH4)],
        out_shape=[jax.ShapeDtypeStruct((b, nc, 2, 2 * CH4, LANES), BF16),
                   jax.ShapeDtypeStruct((b, nc, 2, CH4, LANES), F32),
                   jax.ShapeDtypeStruct((b, nc, 2, DN_PAIRS, PAIR, PAIR), BF16),
                   jax.ShapeDtypeStruct((b, nc, 2, LANES, CH4), BF16)],
        compiler_params=_cparams(("parallel", "parallel")),
        name="dn_local",
    )(qd, kd, vd, gc, bc, g_rows)


DN_SPS = 4


def _dn_scan_kernel(wq_f, u_f, a_f, kgt_f, gl_f, wq_b, u_b, a_b, kgt_b, gl_b, of_ref, ob_ref, state):
    n = pl.program_id(0)

    @pl.when(n == 0)
    def _():
        state[...] = jnp.zeros_like(state)

    r = lax.broadcasted_iota(jnp.int32, (CH4, DN_WIDTH), 0)
    c = lax.broadcasted_iota(jnp.int32, (CH4, DN_WIDTH), 1)
    own = (r // DN_CHUNK) == (c // LANES)
    pick = lambda m: jnp.concatenate(
        [m[h * DN_CHUNK:(h + 1) * DN_CHUNK, h * LANES:(h + 1) * LANES] for h in range(DN_HEADS)], axis=0)
    dirs = ((wq_f, u_f, a_f, kgt_f, gl_f, of_ref), (wq_b, u_b, a_b, kgt_b, gl_b, ob_ref))
    chains = [(bb, d) + refs for bb in range(state.shape[0]) for d, refs in enumerate(dirs)]
    s_alls = [state[bb, d] for bb, d, *_ in chains]
    for sub in range(DN_SPS):
        at = [sub if d == 0 else DN_SPS - 1 - sub for _, d, *_ in chains]
        wss = [_dot(wq[bb, j, 0], s.astype(BF16)) for (bb, d, wq, *_), s, j in zip(chains, s_alls, at)]
        v_nbs = [(u[bb, j, 0] - pick(ws[:CH4])).astype(BF16)
                 for (bb, d, wq, u, *_), ws, j in zip(chains, wss, at)]
        for (bb, d, wq, u, a, kgt, gl, o_ref), ws, v_nb, j in zip(chains, wss, v_nbs, at):
            o = pick(ws[CH4:]) + jnp.concatenate(
                [_dot(a[bb, j, 0, p], v_nb[p * PAIR:(p + 1) * PAIR]) for p in range(DN_PAIRS)], axis=0)
            for h in range(DN_HEADS):
                o_ref[bb, j * DN_CHUNK:(j + 1) * DN_CHUNK, h * LANES:(h + 1) * LANES] = (
                    o[h * DN_CHUNK:(h + 1) * DN_CHUNK])
        nxt = []
        for (bb, d, wq, u, a, kgt, gl, o_ref), s_all, v_nb, j in zip(chains, s_alls, v_nbs, at):
            v_bd = jnp.where(own, jnp.concatenate([v_nb] * DN_HEADS, axis=1), jnp.zeros((), BF16))
            nxt.append(s_all * jnp.exp(gl[bb, j, 0]) + _dot(kgt[bb, j, 0], v_bd))
        s_alls = nxt
    for (bb, d, *_), s_all in zip(chains, s_alls):
        state[bb, d] = s_all


def _dn_scan(wq, u, a, kgt, gl_wide, *, s_len):
    b, nc = wq.shape[:2]
    n_lat = s_len // DN_CHUNK
    n_ctx = nc - n_lat
    assert n_lat % DN_SPS == 0 and n_ctx % DN_SPS == 0
    nb, nb_lat, nb_ctx = nc // DN_SPS, n_lat // DN_SPS, n_ctx // DN_SPS
    fwd = lambda n: jnp.where(n < nb_ctx, nb_lat + n, n - nb_ctx)
    bwd = lambda n: nb - 1 - n
    specs = []
    for d, order in ((0, fwd), (1, bwd)):
        blk = lambda arr, d=d, order=order: pl.BlockSpec(
            (b, DN_SPS, 1) + arr.shape[3:], lambda n: (0, order(n), d) + (0,) * (arr.ndim - 3))
        specs += [blk(wq), blk(u), blk(a), blk(kgt), blk(gl_wide)]
    o_spec = lambda order: pl.BlockSpec((b, DN_SPS * DN_CHUNK, DN_WIDTH), lambda n: (0, order(n), 0))
    return pl.pallas_call(
        _dn_scan_kernel,
        grid=(nb,),
        in_specs=specs,
        out_specs=[o_spec(fwd), o_spec(bwd)],
        out_shape=[jax.ShapeDtypeStruct((b, nc * DN_CHUNK, DN_WIDTH), F32)] * 2,
        scratch_shapes=[pltpu.VMEM((b, 2, DN_HEAD_DIM, DN_WIDTH), F32)],
        compiler_params=_cparams(("arbitrary",)),
        name="dn_scan",
    )(wq, u, a, kgt, gl_wide, wq, u, a, kgt, gl_wide)


MERGE_PARTS = 2


def _merge_kernel(ona_ref, of_ref, ob_ref, z_ref, sc_ref, schp_ref, schn_ref, gate_ref, x_ref,
                  g1l_ref, g1c_ref, sh2l_ref, sh2c_ref, sc2l_ref, sc2c_ref, ng_ref, og_ref, scw_ref,
                  wna_ref, wdn_ref, wsc_ref, wout_ref, wrh_ref, wrl_ref, rb_ref,
                  x1_ref, h2_ref, comb_ref, combt_ref, cpad, *, tm, s_len, ta):
    i = pl.program_id(1)
    d = x_ref.shape[-1]
    half = HALO // 2

    def cu(ref_val):
        v = ref_val.astype(F32)
        return v[:, 2 * SC_WIDTH:] * v[:, :SC_WIDTH]
    cpad[0:half, :] = cu(schp_ref[0])[half:, :]
    cpad[half:half + tm, :] = cu(sc_ref[0])
    cpad[half + tm:, :] = cu(schn_ref[0])[:half, :]

    nr = tm // MERGE_PARTS
    parts = [slice(k * nr, (k + 1) * nr) for k in range(MERGE_PARTS)]

    def select(k, lat, ctx):
        row = i * tm + k * nr + lax.broadcasted_iota(jnp.int32, (nr, 1), 0)
        return jnp.where(row >= s_len, ctx, lat)

    branch_in = []
    for k, rows in enumerate(parts):
        o = of_ref[0, rows, :] + ob_ref[0, rows, :]
        heads = []
        for h in range(DN_HEADS):
            oh = o[:, h * LANES:(h + 1) * LANES]
            heads.append(oh * lax.rsqrt(jnp.mean(oh * oh, axis=-1, keepdims=True) + EPS))
        o_dn = jnp.concatenate(heads, axis=1) * og_ref[...] * _silu(z_ref[0, rows, :].astype(F32))
        gt = i * tm + k * nr + lax.broadcasted_iota(jnp.int32, (nr, 1), 0)
        has_prev = (gt != 0) & (gt != s_len)
        has_next = (gt != s_len - 1) & (gt != ta - 1)
        conv = (scw_ref[0:1, :] * jnp.where(has_prev, cpad[pl.ds(half - 1 + k * nr, nr), :], 0.0)
                + scw_ref[1:2, :] * cpad[pl.ds(half + k * nr, nr), :]
                + scw_ref[2:3, :] * jnp.where(has_next, cpad[pl.ds(half + 1 + k * nr, nr), :], 0.0))
        o_sc = sc_ref[0, rows, SC_WIDTH:2 * SC_WIDTH].astype(F32) * conv
        branch_in.append((o_dn.astype(BF16), o_sc.astype(BF16)))

    merged = []
    for rows, (o_dn, o_sc) in zip(parts, branch_in):
        merged.append((gate_ref[0, rows, :d].astype(F32) * _dot(ona_ref[0, rows, :], wna_ref[...])
                       + gate_ref[0, rows, d:2 * d].astype(F32) * _dot(o_dn, wdn_ref[...])
                       + gate_ref[0, rows, 2 * d:].astype(F32) * _dot(o_sc, wsc_ref[...])).astype(BF16))

    logit_parts = []
    for k, (rows, m) in enumerate(zip(parts, merged)):
        x1 = x_ref[0, rows, :] + select(k, g1l_ref[0], g1c_ref[...]) * _dot(m, wout_ref[...])
        x1_ref[0, rows, :] = x1
        hn = x1 * lax.rsqrt(jnp.mean(x1 * x1, axis=-1, keepdims=True) + EPS) * ng_ref[...]
        h2 = hn * (1.0 + select(k, sc2l_ref[0], sc2c_ref[...])) + select(k, sh2l_ref[0], sh2c_ref[...])
        h2_ref[0, rows, :] = h2.astype(BF16)
        h2_hi, h2_lo = _split_bf16(h2)
        logit_parts.append(_dot3(h2_hi, h2_lo, wrh_ref[...], wrl_ref[...]))

    logits = jnp.concatenate(logit_parts, axis=0)
    scores = jax.nn.sigmoid(logits.T)
    biased = scores + rb_ref[...]
    rows = [biased[e:e + 1, :] for e in range(N_EXPERTS)]
    gsc = []
    for g in range(N_GROUPS):
        a, b_, c_, d_ = rows[4 * g:4 * g + 4]
        gsc.append(jnp.maximum(jnp.maximum(jnp.maximum(a + b_, a + c_), jnp.maximum(a + d_, b_ + c_)),
                               jnp.maximum(b_ + d_, c_ + d_)))
    best = gsc[0]
    grp = jnp.zeros_like(best, dtype=jnp.int32)
    for g in range(1, N_GROUPS):
        upd = gsc[g] > best
        best = jnp.where(upd, gsc[g], best)
        grp = jnp.where(upd, g, grp)
    eidx = lax.broadcasted_iota(jnp.int32, biased.shape, 0)
    masked = jnp.where((eidx // EXPERTS_PER_GROUP) == grp, biased, -jnp.inf)
    big = jnp.int32(LANES)
    m1 = masked.max(axis=0, keepdims=True)
    i1 = jnp.where(masked == m1, eidx, big).min(axis=0, keepdims=True)
    masked2 = jnp.where(eidx == i1, -jnp.inf, masked)
    m2 = masked2.max(axis=0, keepdims=True)
    i2 = jnp.where(masked2 == m2, eidx, big).min(axis=0, keepdims=True)
    sel1 = eidx == i1
    sel2 = eidx == i2
    s1 = jnp.sum(jnp.where(sel1, scores, 0.0), axis=0, keepdims=True)
    s2 = jnp.sum(jnp.where(sel2, scores, 0.0), axis=0, keepdims=True)
    tot = s1 + s2
    comb_t = jnp.where(sel1, s1 / tot, 0.0) + jnp.where(sel2, s2 / tot, 0.0)
    meta_t = jnp.where(eidx == N_EXPERTS, grp.astype(F32), comb_t)
    combt_ref[0] = meta_t
    comb_ref[0] = meta_t.T


def _merge(o_na, o_f, o_b, z, sc, gates, xs, mods_l, mods_c, norm_g, out_g, sc_w, wna, wdn, wsc, wout, wr, rb,
           *, s_len, tm):
    b, ta, d = xs.shape
    col_z = col_sc = 0
    hb = tm // HALO
    n_hb = ta // HALO
    tok = lambda w, col=0: pl.BlockSpec((1, tm, w), lambda bb, i: (bb, i, col))
    vec = lambda w: pl.BlockSpec((1, w), lambda bb, i: (0, 0))
    bvec = lambda: pl.BlockSpec((1, 1, d), lambda bb, i: (bb, 0, 0))
    full = lambda a: pl.BlockSpec(a.shape, lambda bb, i: (0,) * a.ndim)
    g1l, sh2l, sc2l = mods_l
    g1c, sh2c, sc2c = mods_c
    w3 = 3 * SC_WIDTH
    wr_hi = wr.astype(BF16)
    wr_lo = (wr - wr_hi.astype(F32)).astype(BF16)
    return pl.pallas_call(
        functools.partial(_merge_kernel, tm=tm, s_len=s_len, ta=ta),
        grid=(b, ta // tm),
        in_specs=[tok(NA_WIDTH), tok(DN_WIDTH), tok(DN_WIDTH), tok(DN_WIDTH, col_z), tok(w3, col_sc),
                  pl.BlockSpec((1, HALO, w3), lambda bb, i: (bb, jnp.maximum(i * hb - 1, 0), col_sc)),
                  pl.BlockSpec((1, HALO, w3), lambda bb, i: (bb, jnp.minimum((i + 1) * hb, n_hb - 1), col_sc)),
                  tok(3 * d, 0), tok(d),
                  bvec(), vec(d), bvec(), vec(d), bvec(), vec(d), vec(d), vec(DN_WIDTH),
                  full(sc_w), full(wna), full(wdn), full(wsc), full(wout), full(wr_hi), full(wr_lo), full(rb)],
        out_specs=[tok(d), tok(d), tok(LANES), pl.BlockSpec((1, LANES, tm), lambda bb, i: (bb, 0, i))],
        out_shape=[jax.ShapeDtypeStruct((b, ta, d), F32),
                   jax.ShapeDtypeStruct((b, ta, d), BF16),
                   jax.ShapeDtypeStruct((b, ta, LANES), F32),
                   jax.ShapeDtypeStruct((b, LANES, ta), F32)],
        scratch_shapes=[pltpu.VMEM((tm + HALO, SC_WIDTH), F32)],
        compiler_params=_cparams(("parallel", "parallel")),
        name="merge_router",
    )(o_na, o_f, o_b, z, sc, sc, sc, gates, xs, g1l, g1c, sh2l, sh2c, sc2l, sc2c, norm_g, out_g,
      sc_w, wna, wdn, wsc, wout, wr_hi, wr_lo, rb)


MOE_CAP = 224


def _moe_kernel(h_ref, comb_ref, combt_ref, lt_ref, gt_ref, wg_ref, wu_ref, wd_ref, x_ref, g2l_ref, g2c_ref,
                o_ref, acc, rank_r, rank_c, *, tm, s_len):
    i = pl.program_id(1)
    g = pl.program_id(2)

    comb = comb_ref[0]
    lane = lax.broadcasted_iota(jnp.int32, (tm, LANES), 1)
    grp_row = combt_ref[0, N_EXPERTS:N_EXPERTS + 1, :]
    grp_col = jnp.sum(jnp.where(lane == N_EXPERTS, comb, 0.0), axis=-1, keepdims=True)

    @pl.when(g == 0)
    def _():
        acc[...] = jnp.zeros_like(acc)
        sub = lax.broadcasted_iota(jnp.int32, (8, tm), 0).astype(F32)
        before = _dot(jnp.where(sub == grp_row, 1.0, 0.0).astype(BF16), lt_ref[...])
        rank_r[...] = jnp.broadcast_to(
            jnp.sum(jnp.where(sub == grp_row, before, 0.0), axis=0, keepdims=True), (8, tm))
        lane_f = lane.astype(F32)
        before = _dot(gt_ref[...], jnp.where(lane_f == grp_col, 1.0, 0.0).astype(BF16))
        rank_c[...] = jnp.broadcast_to(
            jnp.sum(jnp.where(lane_f == grp_col, before, 0.0), axis=-1, keepdims=True), (tm, LANES))

    gf = g.astype(F32)
    member_row = grp_row == gf
    member_col = grp_col == gf
    rank_row = rank_r[0:1, :]
    rank_col = rank_c[:, 0:1]
    count = jnp.sum(jnp.where(member_row, 1.0, 0.0))
    comb_hi, comb_lo = _split_bf16(comb)
    d = x_ref.shape[-1]

    for k in range(-(-tm // MOE_CAP)):
        @pl.when(count > k * MOE_CAP)
        def _(k=k):
            slot_c = (lax.broadcasted_iota(jnp.int32, (MOE_CAP, 1), 0) + k * MOE_CAP).astype(F32)
            gather = jnp.where(member_row & (rank_row == slot_c), 1.0, 0.0).astype(BF16)
            hs = _dot(gather, h_ref[0]).astype(BF16)
            cws = _dot(gather, comb_hi) + _dot(gather, comb_lo)
            lane_c = lax.broadcasted_iota(jnp.int32, (MOE_CAP, LANES), 1)
            experts = range(EXPERTS_PER_GROUP)
            gates = [_dot(hs, wg_ref[0, e4]) for e4 in experts]
            ups = [_dot(hs, wu_ref[0, e4]) for e4 in experts]
            hiddens = []
            for e4 in experts:
                cw = jnp.sum(jnp.where(lane_c == g * EXPERTS_PER_GROUP + e4, cws, 0.0), axis=-1, keepdims=True)
                hiddens.append((_silu(gates[e4]) * ups[e4] * cw).astype(BF16))
            ys = _dot(hiddens[0], wd_ref[0, 0])
            for e4 in experts[1:]:
                ys = ys + _dot(hiddens[e4], wd_ref[0, e4])
            slot_r = (lax.broadcasted_iota(jnp.int32, (1, MOE_CAP), 1) + k * MOE_CAP).astype(F32)
            scatter = jnp.where(member_col & (rank_col == slot_r), 1.0, 0.0).astype(BF16)
            ys_hi, ys_lo = _split_bf16(ys)
            acc[...] += _dot(scatter, ys_hi) + _dot(scatter, ys_lo)

    @pl.when(g == N_GROUPS - 1)
    def _():
        o_ref[0] = x_ref[0] + _row_select(i, tm, s_len, g2l_ref[0], g2c_ref[...]) * acc[...]


def _moe(h2, comb, comb_t, wg, wu, wd, x1, g2l, g2c, *, layer, s_len, tm):
    b, ta, d = x1.shape
    tok = lambda w: pl.BlockSpec((1, tm, w), lambda bb, i, g: (bb, i, 0))
    grp = lambda a: pl.BlockSpec((1, EXPERTS_PER_GROUP) + a.shape[2:], lambda bb, i, g: (layer, g, 0, 0))
    tri = lambda: pl.BlockSpec((tm, tm), lambda bb, i, g: (0, 0))
    earlier = np.arange(tm)[:, None] < np.arange(tm)[None, :]
    lt = jnp.asarray(earlier, BF16)
    gt = jnp.asarray(earlier.T, BF16)
    return pl.pallas_call(
        functools.partial(_moe_kernel, tm=tm, s_len=s_len),
        grid=(b, ta // tm, N_GROUPS),
        in_specs=[tok(d), tok(LANES),
                  pl.BlockSpec((1, LANES, tm), lambda bb, i, g: (bb, 0, i)),
                  tri(), tri(),
                  grp(wg), grp(wu), grp(wd),
                  tok(d),
                  pl.BlockSpec((1, 1, d), lambda bb, i, g: (bb, 0, 0)),
                  pl.BlockSpec((1, d), lambda bb, i, g: (0, 0))],
        out_specs=tok(d),
        out_shape=jax.ShapeDtypeStruct((b, ta, d), F32),
        scratch_shapes=[pltpu.VMEM((tm, d), F32), pltpu.VMEM((8, tm), F32), pltpu.VMEM((tm, LANES), F32)],
        compiler_params=_cparams(("parallel", "parallel", "arbitrary")),
        name="moe_grouped",
    )(h2, comb, comb_t, lt, gt, wg, wu, wd, x1, g2l, g2c)


def _rope_tables(s_len, ctx_len):
    t = np.arange(s_len)
    n_freq = DN_HEAD_DIM // 4
    inv_freq = (np.float32(ROPE_BASE) ** (-np.arange(n_freq, dtype=np.float32) / n_freq)).astype(np.float32)
    pos = np.stack([t // GRID_W, t % GRID_W], axis=-1).astype(np.float32)
    ang = (pos[..., None] * inv_freq).astype(np.float32)
    cos, sin = np.cos(ang.astype(np.float64)), np.sin(ang.astype(np.float64))
    rc = np.concatenate([cos[:, 0], cos[:, 0], cos[:, 1], cos[:, 1]], axis=-1)
    rs = np.concatenate([-sin[:, 0], sin[:, 0], -sin[:, 1], sin[:, 1]], axis=-1)
    rc = np.concatenate([rc, np.ones((ctx_len, LANES))], axis=0)
    rs = np.concatenate([rs, np.zeros((ctx_len, LANES))], axis=0)
    return jnp.asarray(rc, F32), jnp.asarray(rs, F32)


def _pick_tile(ta, candidates):
    for t in candidates:
        if ta % t == 0:
            return t
    raise ValueError(f"no row tile for {ta} tokens")


def kernel(x, c, ctx, c_ctx, w_ada, b_ada, norm_mix_g, norm_ffn_g, w_in, na_q_norm_g, na_k_norm_g, na_rpb, dn_conv_w, dn_a_log, dn_dt_bias, dn_out_norm_g, sc_conv_w, w_branch_na, w_branch_dn, w_branch_sc, w_out, w_router, router_bias, moe_w_gate, moe_w_up, moe_w_down):
    b, s_len, d = x.shape
    ctx_len = ctx.shape[1]
    depth = w_ada.shape[0]
    ta = s_len + ctx_len
    assert ctx_len == NA_BLK and s_len % (WIN_R * GRID_W) == 0 and s_len // NA_BLK >= 4
    assert ctx_len % PREP_TILE == 0 and b + 1 <= 8
    tm_proj = _pick_tile(ta, (768, 512, 256))
    tm_merge = _pick_tile(ta, (384, 256))
    tm_moe = _pick_tile(ta, (768, 512, 256))

    xs = jnp.concatenate([x, ctx], axis=1)
    c_rows = jnp.zeros((8, d), F32).at[:b].set(c).at[b].set(c_ctx)
    mod = _modulation(c_rows, w_ada, b_ada)

    sizes = (3 * NA_WIDTH, 3 * DN_WIDTH, DN_WIDTH, 4 * DN_HEADS, 3 * SC_WIDTH, 3 * d)
    cuts = [int(v) for v in np.cumsum((0,) + sizes)]
    seg = lambda l, k: lax.slice(w_in, (l, 0, cuts[k]), (l + 1, d, cuts[k + 1]))[0]
    wg_all, wu_all, wd_all = moe_w_gate.astype(BF16), moe_w_up.astype(BF16), moe_w_down.astype(BF16)

    rope_c, rope_s = _rope_tables(s_len, ctx_len)
    bd = jnp.asarray(np.kron(np.eye(NA_HEADS), np.ones((NA_HEAD_DIM, NA_HEAD_DIM))), BF16)
    nc = ta // DN_CHUNK
    wr = jnp.zeros((d, LANES), F32).at[:, :N_EXPERTS].set(w_router)
    rb = jnp.zeros((LANES, 1), F32).at[:N_EXPERTS, 0].set(router_bias)

    for l in range(depth):
        m = mod[l].reshape(8, N_MOD, d)
        lat = lambda k: m[:b, k][:, None, :]
        cx = lambda k: m[b:b + 1, k]
        w_ab = jnp.zeros((d, LANES), F32).at[:, :4 * DN_HEADS].set(seg(l, 3))
        gqk = jnp.concatenate([jnp.tile(na_q_norm_g[l], NA_HEADS) * (NA_HEAD_DIM ** -0.5),
                               jnp.tile(na_k_norm_g[l], NA_HEADS)])[None, :]
        gates, na_q, na_k, na_v, dn_qkv, sc_p, dn_z, ab = _projection(
            xs, (lat(0), lat(1)), (cx(0), cx(1)), norm_mix_g[l][None, :],
            seg(l, 5).astype(BF16), seg(l, 0).astype(BF16), seg(l, 1).astype(BF16),
            seg(l, 4).astype(BF16), seg(l, 2).astype(BF16), w_ab, gqk, bd, s_len=s_len, tm=tm_proj)

        o_na = _neighbourhood_attention(na_q, na_k, na_v, _na_bias_table(na_rpb[l]), s_len=s_len)

        alog_row = jnp.zeros((1, LANES), F32).at[0, :2 * DN_HEADS].set(dn_a_log[l].reshape(-1))
        dtb_row = jnp.zeros((1, LANES), F32).at[0, :2 * DN_HEADS].set(dn_dt_bias[l].reshape(-1))
        qd, kd, vd, gc, bc = _dn_prep(dn_qkv, ab, dn_conv_w[l], rope_c, rope_s, alog_row, dtb_row,
                                      s_len=s_len)
        g8 = gc[..., :2 * DN_HEADS].reshape(b, nc, DN_CHUNK, 2, DN_HEADS)
        g_rows = jnp.transpose(g8, (0, 1, 3, 4, 2)).reshape(b, nc, 2, CH4)
        g_last = jnp.stack([g8[:, :, DN_CHUNK - 1, 0], g8[:, :, 0, 1]], axis=2)
        gl_wide = jnp.repeat(g_last, LANES, axis=-1)[:, :, :, None, :]
        wq, u, a, kgt = _dn_local(qd, kd, vd, gc, bc, g_rows)
        o_f, o_b = _dn_scan(wq, u, a, kgt, gl_wide, s_len=s_len)

        x1, h2, comb, comb_t = _merge(
            o_na, o_f, o_b, dn_z, sc_p, gates, xs, (lat(2), lat(3), lat(4)), (cx(2), cx(3), cx(4)),
            norm_ffn_g[l][None, :], jnp.tile(dn_out_norm_g[l], DN_HEADS)[None, :], sc_conv_w[l],
            w_branch_na[l].astype(BF16), w_branch_dn[l].astype(BF16), w_branch_sc[l].astype(BF16),
            w_out[l].astype(BF16), wr, rb, s_len=s_len, tm=tm_merge)

        xs = _moe(h2, comb, comb_t, wg_all, wu_all, wd_all, x1, lat(5), cx(5), layer=l, s_len=s_len, tm=tm_moe)

    return xs[:, :s_len]
```

```python
import functools
import math

import numpy as np
import jax
import jax.numpy as jnp
from jax import lax
from jax.experimental import pallas as pl
from jax.experimental.pallas import tpu as pltpu

F32 = jnp.float32
BF16 = jnp.bfloat16
HIGHEST = lax.Precision.HIGHEST

GRID_W = 64
NA_HEADS = 8
NA_HEAD_DIM = 64
NA_WIDTH = NA_HEADS * NA_HEAD_DIM
WIN_R = 8
WIN_C = 16
DN_HEADS = 4
DN_HEAD_DIM = 128
DN_WIDTH = DN_HEADS * DN_HEAD_DIM
DN_CONV = 5
DN_CHUNK = 64
ROPE_BASE = 10000.0
SC_WIDTH = 512
SC_CONV = 3
N_EXPERTS = 16
N_GROUPS = 4
EXPERTS_PER_GROUP = N_EXPERTS // N_GROUPS
D_EXPERT = 512
N_MOD = 6
EPS = 1e-6
NEG_INF = -1e30

LANES = 128
HALO = 16
PREP_TILE = 256
VMEM_LIMIT = 56 * 1024 * 1024

N_PROJ_TILE = 512


def _cparams(sem):
    return pltpu.CompilerParams(dimension_semantics=sem, vmem_limit_bytes=VMEM_LIMIT)


def _dot(a, b):
    return jnp.dot(a, b, preferred_element_type=F32)


def _dot_nt(a, b):
    return lax.dot_general(a, b, (((1,), (1,)), ((), ())), preferred_element_type=F32)


def _silu(x):
    return x * jax.nn.sigmoid(x)


def _row_select(i, tm, s_len, lat, ctx):
    row = i * tm + lax.broadcasted_iota(jnp.int32, (tm, 1), 0)
    return jnp.where(row >= s_len, ctx, lat)


def _mod_kernel(c_ref, w_ref, b_ref, o_ref):
    c_hi, c_lo = _split_bf16(_silu(c_ref[...]))
    w_hi, w_lo = _split_bf16(w_ref[0])
    o_ref[0] = _dot3(c_hi, c_lo, w_hi, w_lo) + b_ref[0]


def _modulation(c_rows, w_ada, b_ada):
    depth, d, n = w_ada.shape
    tn = 512
    return pl.pallas_call(
        _mod_kernel,
        grid=(depth, n // tn),
        in_specs=[pl.BlockSpec((8, d), lambda l, j: (0, 0)),
                  pl.BlockSpec((1, d, tn), lambda l, j: (l, 0, j)),
                  pl.BlockSpec((1, 1, tn), lambda l, j: (l, 0, j))],
        out_specs=pl.BlockSpec((1, 8, tn), lambda l, j: (l, 0, j)),
        out_shape=jax.ShapeDtypeStruct((depth, 8, n), F32),
        compiler_params=_cparams(("parallel", "parallel")),
        name="adaln_mod",
    )(c_rows, w_ada, b_ada.reshape(depth, 1, n))


def _split_bf16(a):
    hi = a.astype(BF16)
    return hi, (a - hi.astype(F32)).astype(BF16)


def _dot3(a_hi, a_lo, b_hi, b_lo):
    return _dot(a_hi, b_hi) + _dot(a_lo, b_hi) + _dot(a_hi, b_lo)


def _modulated_norm(x_ref, shl_ref, scl_ref, shc_ref, scc_ref, g_ref, *, tm, s_len):
    i = pl.program_id(1)
    x = x_ref[0]
    y = x * lax.rsqrt(jnp.mean(x * x, axis=-1, keepdims=True) + EPS) * g_ref[...]
    scale = _row_select(i, tm, s_len, scl_ref[0], scc_ref[...])
    shift = _row_select(i, tm, s_len, shl_ref[0], shc_ref[...])
    return y * (1.0 + scale) + shift


def _proj_gate_kernel(x_ref, shl_ref, scl_ref, shc_ref, scc_ref, g_ref, w_ref, gate_ref, *, tm, s_len):
    h = _modulated_norm(x_ref, shl_ref, scl_ref, shc_ref, scc_ref, g_ref, tm=tm, s_len=s_len).astype(BF16)
    for c0 in range(0, w_ref.shape[1], N_PROJ_TILE):
        cols = slice(c0, c0 + N_PROJ_TILE)
        gate_ref[0, :, cols] = jax.nn.sigmoid(_dot(h, w_ref[:, cols])).astype(BF16)


def _proj_mix_kernel(x_ref, shl_ref, scl_ref, shc_ref, scc_ref, g_ref, wna_ref, wdn_ref, wsc_ref, wz_ref,
                     wabh_ref, wabl_ref, gqk_ref, bd_ref,
                     q_ref, k_ref, v_ref, dn_ref, sc_ref, z_ref, ab_ref, *, tm, s_len):
    h = _modulated_norm(x_ref, shl_ref, scl_ref, shc_ref, scc_ref, g_ref, tm=tm, s_len=s_len)
    h_hi, h_lo = _split_bf16(h)
    ab_ref[0] = _dot3(h_hi, h_lo, wabh_ref[...], wabl_ref[...])
    for part, out in enumerate((q_ref, k_ref)):
        cols = slice(part * NA_WIDTH, (part + 1) * NA_WIDTH)
        a = _dot(h_hi, wna_ref[:, cols])
        ss = _dot((a * a).astype(BF16), bd_ref[...])
        out[0] = (a * lax.rsqrt(ss * (1.0 / NA_HEAD_DIM) + EPS) * gqk_ref[:, cols]).astype(BF16)
    v_ref[0] = _dot(h_hi, wna_ref[:, 2 * NA_WIDTH:]).astype(BF16)
    for w_ref, out in ((wdn_ref, dn_ref), (wsc_ref, sc_ref), (wz_ref, z_ref)):
        for c0 in range(0, w_ref.shape[1], N_PROJ_TILE):
            cols = slice(c0, c0 + N_PROJ_TILE)
            out[0, :, cols] = _dot(h_hi, w_ref[:, cols]).astype(BF16)


def _projection(xs, mod_l, mod_c, norm_g, w_gate, w_na, w_dn, w_sc, w_z, w_ab, gqk, bd, *, s_len, tm):
    b, ta, d = xs.shape
    shl, scl = mod_l
    shc, scc = mod_c
    wab_hi, wab_lo = _split_bf16(w_ab)
    tok = lambda w: pl.BlockSpec((1, tm, w), lambda bb, i: (bb, i, 0))
    bvec = lambda: pl.BlockSpec((1, 1, d), lambda bb, i: (bb, 0, 0))
    full = lambda a: pl.BlockSpec(a.shape, lambda bb, i: (0,) * a.ndim)
    head = [tok(d), bvec(), bvec(), full(shc), full(scc), full(norm_g)]
    gates = pl.pallas_call(
        functools.partial(_proj_gate_kernel, tm=tm, s_len=s_len),
        grid=(b, ta // tm),
        in_specs=head + [full(w_gate)],
        out_specs=tok(w_gate.shape[1]),
        out_shape=jax.ShapeDtypeStruct((b, ta, w_gate.shape[1]), BF16),
        compiler_params=_cparams(("parallel", "parallel")),
        name="proj_gates",
    )(xs, shl, scl, shc, scc, norm_g, w_gate)
    consts = (w_na, w_dn, w_sc, w_z, wab_hi, wab_lo, gqk, bd)
    widths = (NA_WIDTH, NA_WIDTH, NA_WIDTH, w_dn.shape[1], w_sc.shape[1], w_z.shape[1])
    outs = pl.pallas_call(
        functools.partial(_proj_mix_kernel, tm=tm, s_len=s_len),
        grid=(b, ta // tm),
        in_specs=head + [full(a) for a in consts],
        out_specs=[tok(w) for w in widths] + [tok(LANES)],
        out_shape=[jax.ShapeDtypeStruct((b, ta, w), BF16) for w in widths]
        + [jax.ShapeDtypeStruct((b, ta, LANES), F32)],
        compiler_params=_cparams(("parallel", "parallel")),
        name="proj_mixers",
    )(xs, shl, scl, shc, scc, norm_g, *consts)
    return (gates,) + tuple(outs)


NA_QROWS = 4
NA_BLK = NA_QROWS * GRID_W
WIN_TOK = WIN_R * GRID_W


def _stack_two_heads(q2):
    lane = lax.broadcasted_iota(jnp.int32, q2.shape, 1)
    zero = jnp.zeros_like(q2)
    return jnp.concatenate([jnp.where(lane < NA_HEAD_DIM, q2, zero),
                            jnp.where(lane >= NA_HEAD_DIM, q2, zero)], axis=0)


def _unstack_two_heads(o2):
    m = o2.shape[0] // 2
    lane = lax.broadcasted_iota(jnp.int32, (m, o2.shape[1]), 1)
    return jnp.where(lane < NA_HEAD_DIM, o2[:m], o2[m:])


def _na_kernel(q_ref, kp_ref, kc_ref, kn_ref, vp_ref, vc_ref, vn_ref, kx_ref, vx_ref, bias_ref, o_ref,
               kbuf, vbuf, *, n_rows):
    i = pl.program_id(1)
    n_lat = n_rows // NA_QROWS

    @pl.when(i < n_lat)
    def _():
        for slot, (kr, vr) in enumerate(((kp_ref, vp_ref), (kc_ref, vc_ref), (kn_ref, vn_ref))):
            kbuf[slot * NA_BLK:(slot + 1) * NA_BLK, :] = kr[0]
            vbuf[slot * NA_BLK:(slot + 1) * NA_BLK, :] = vr[0]
        win = []
        for j in range(NA_QROWS):
            r = i * NA_QROWS + j
            start = jnp.clip(r - WIN_R // 2, 0, n_rows - WIN_R)
            win.append((pl.multiple_of((start - (i - 1) * NA_QROWS) * GRID_W, GRID_W), r - start))

        rows2 = 2 * GRID_W
        pairs = [slice(hp * LANES, (hp + 1) * LANES) for hp in range(NA_HEADS // 2)]
        logits = []
        for hp, lanes in enumerate(pairs):
            qs = jnp.concatenate([_stack_two_heads(q_ref[0, j * GRID_W:(j + 1) * GRID_W, lanes])
                                  for j in range(NA_QROWS)], axis=0)
            s_c = _dot_nt(qs, kx_ref[0, :, lanes])
            s_w = jnp.concatenate(
                [_dot_nt(qs[j * rows2:(j + 1) * rows2], kbuf[pl.ds(win[j][0], WIN_TOK), lanes])
                 + bias_ref[win[j][1], hp] for j in range(NA_QROWS)], axis=0)
            logits.append((s_w, s_c))
        probs = []
        for s_w, s_c in logits:
            m = jnp.maximum(s_w.max(axis=-1, keepdims=True), s_c.max(axis=-1, keepdims=True))
            p_w = jnp.exp(s_w - m)
            p_c = jnp.exp(s_c - m)
            denom = p_w.sum(axis=-1, keepdims=True) + p_c.sum(axis=-1, keepdims=True)
            probs.append((p_w.astype(BF16), p_c.astype(BF16), denom))
        for lanes, (p_wb, p_cb, denom) in zip(pairs, probs):
            o = _dot(p_cb, vx_ref[0, :, lanes]) + jnp.concatenate(
                [_dot(p_wb[j * rows2:(j + 1) * rows2], vbuf[pl.ds(win[j][0], WIN_TOK), lanes])
                 for j in range(NA_QROWS)], axis=0)
            o = o / denom
            for j in range(NA_QROWS):
                o_ref[0, j * GRID_W:(j + 1) * GRID_W, lanes] = _unstack_two_heads(
                    o[j * rows2:(j + 1) * rows2]).astype(BF16)

    @pl.when(i == n_lat)
    def _():
        for hp in range(NA_HEADS // 2):
            lanes = slice(hp * LANES, (hp + 1) * LANES)
            s = _dot_nt(_stack_two_heads(q_ref[0, :, lanes]), kx_ref[0, :, lanes])
            p = jnp.exp(s - s.max(axis=-1, keepdims=True))
            o = _dot(p.astype(BF16), vx_ref[0, :, lanes]) / p.sum(axis=-1, keepdims=True)
            o_ref[0, :, lanes] = _unstack_two_heads(o).astype(BF16)


def _neighbourhood_attention(q, k, v, bias_tab, *, s_len):
    b, ta, _ = q.shape
    n_rows = s_len // GRID_W
    n_lat = n_rows // NA_QROWS
    blk = lambda off: pl.BlockSpec((1, NA_BLK, NA_WIDTH), lambda bb, i: (bb, jnp.clip(i + off, 0, n_lat - 1), 0))
    cx = lambda: pl.BlockSpec((1, NA_BLK, NA_WIDTH), lambda bb, i: (bb, n_lat, 0))
    own = lambda: pl.BlockSpec((1, NA_BLK, NA_WIDTH), lambda bb, i: (bb, i, 0))
    return pl.pallas_call(
        functools.partial(_na_kernel, n_rows=n_rows),
        grid=(b, n_lat + 1),
        in_specs=[own(), blk(-1), blk(0), blk(1), blk(-1), blk(0), blk(1), cx(), cx(),
                  pl.BlockSpec(bias_tab.shape, lambda bb, i: (0, 0, 0, 0))],
        out_specs=own(),
        out_shape=jax.ShapeDtypeStruct((b, ta, NA_WIDTH), BF16),
        scratch_shapes=[pltpu.VMEM((3 * NA_BLK, NA_WIDTH), BF16),
                        pltpu.VMEM((3 * NA_BLK, NA_WIDTH), BF16)],
        compiler_params=_cparams(("parallel", "arbitrary")),
        name="na_attention",
    )(q, k, k, k, v, v, v, k, v, bias_tab)


def _na_bias_table(rpb):
    qc = np.arange(GRID_W)[:, None]
    kc = np.arange(GRID_W)[None, :]
    win_c0 = np.clip(qc - WIN_C // 2, 0, GRID_W - WIN_C)
    in_win = (kc >= win_c0) & (kc < win_c0 + WIN_C)
    rel_c = np.clip(kc - qc + WIN_C - 1, 0, 2 * WIN_C - 2)
    pick_c = (rel_c[..., None] == np.arange(2 * WIN_C - 1)).astype(np.float32)
    by_col = jnp.einsum('hab,qkb->haqk', rpb.astype(F32), pick_c, precision=HIGHEST)
    by_col = jnp.where(in_win[None, None], by_col, NEG_INF)
    tab = jnp.stack([by_col[:, WIN_R - 1 - v:2 * WIN_R - 1 - v] for v in range(WIN_R)], axis=0)
    tab = jnp.transpose(tab, (0, 1, 3, 2, 4))
    return tab.reshape(WIN_R, NA_HEADS // 2, 2 * GRID_W, WIN_R * GRID_W)


def _dn_prep_kernel(x_ref, hp_ref, hn_ref, w_ref, rc_ref, rs_ref, ab_ref, alog_ref, dtb_ref,
                    q_ref, k_ref, v_ref, g_ref, beta_ref, xpad, *, s_len, ta):
    i = pl.program_id(1)
    t0 = i * PREP_TILE
    has_prev = jnp.logical_and(t0 != 0, t0 != s_len).astype(F32)
    has_next = jnp.logical_and(t0 + PREP_TILE != s_len, t0 + PREP_TILE != ta).astype(F32)
    half = HALO // 2
    xpad[0:half, :] = hp_ref[0].astype(F32)[half:, :] * has_prev
    xpad[half:half + PREP_TILE, :] = x_ref[0].astype(F32)
    xpad[half + PREP_TILE:, :] = hn_ref[0].astype(F32)[:half, :] * has_next
    xp = xpad[...]
    n_pad = xp.shape[0]
    y = jnp.zeros((PREP_TILE, x_ref.shape[-1]), F32)
    for tap in range(DN_CONV):
        shifted = xp if tap == DN_CONV // 2 else pltpu.roll(xp, (DN_CONV // 2 - tap) % n_pad, 0)
        y = y + w_ref[tap:tap + 1, :] * shifted[half:half + PREP_TILE]
    y = _silu(y)

    lane = lax.broadcasted_iota(jnp.int32, (PREP_TILE, LANES), 1)
    first_half = (lane % (LANES // 2)) < (LANES // 4)
    rc = rc_ref[...]
    rs = rs_ref[...]

    def norm_rope(u, scale):
        u = u * (lax.rsqrt(jnp.sum(u * u, axis=-1, keepdims=True) + EPS) * scale)
        partner = jnp.where(first_half, pltpu.roll(u, LANES - LANES // 4, 1), pltpu.roll(u, LANES // 4, 1))
        return u * rc + partner * rs

    for h in range(DN_HEADS):
        lanes = slice(h * LANES, (h + 1) * LANES)
        qh = norm_rope(y[:, h * LANES:(h + 1) * LANES], DN_HEAD_DIM ** -0.5)
        kh = norm_rope(y[:, DN_WIDTH + h * LANES:DN_WIDTH + (h + 1) * LANES], 1.0)
        q_ref[0, :, lanes] = qh.astype(BF16)
        k_ref[0, :, lanes] = kh.astype(BF16)
    v_ref[0] = y[:, 2 * DN_WIDTH:].astype(BF16)

    ab = ab_ref[0]
    z = ab + dtb_ref[...]
    softplus = jnp.maximum(z, 0.0) + jnp.log1p(jnp.exp(-jnp.abs(z)))
    g = -jnp.exp(alog_ref[...]) * softplus
    g = jnp.where(lane < 2 * DN_HEADS, g, 0.0)
    beta = jax.nn.sigmoid(pltpu.roll(ab, LANES - 2 * DN_HEADS, 1))
    beta_ref[0] = jnp.where(lane < 2 * DN_HEADS, beta, 0.0)

    r = lax.broadcasted_iota(jnp.int32, (PREP_TILE, PREP_TILE), 0)
    c = lax.broadcasted_iota(jnp.int32, (PREP_TILE, PREP_TILE), 1)
    same = (r // DN_CHUNK) == (c // DN_CHUNK)
    tri_f = jnp.where(same & (c <= r), 1.0, 0.0).astype(BF16)
    tri_b = jnp.where(same & (c >= r), 1.0, 0.0).astype(BF16)
    g1 = g.astype(BF16)
    rem = g - g1.astype(F32)
    g2 = rem.astype(BF16)
    g3 = (rem - g2.astype(F32)).astype(BF16)
    cs_f = _dot(tri_f, g1) + _dot(tri_f, g2) + _dot(tri_f, g3)
    cs_b = _dot(tri_b, g1) + _dot(tri_b, g2) + _dot(tri_b, g3)
    g_ref[0] = jnp.where(lane < DN_HEADS, cs_f, cs_b)


def _dn_prep(p, ab, conv_w, rope_c, rope_s, alog_row, dtb_row, *, s_len):
    b, ta, w3 = p.shape
    col_dn = 0
    n_t = ta // PREP_TILE
    hb = PREP_TILE // HALO
    n_hb = ta // HALO
    tok = lambda: pl.BlockSpec((1, PREP_TILE, DN_WIDTH), lambda bb, i: (bb, i, 0))
    sca = lambda: pl.BlockSpec((1, PREP_TILE, LANES), lambda bb, i: (bb, i, 0))
    row = lambda: pl.BlockSpec((1, LANES), lambda bb, i: (0, 0))
    return pl.pallas_call(
        functools.partial(_dn_prep_kernel, s_len=s_len, ta=ta),
        grid=(b, n_t),
        in_specs=[pl.BlockSpec((1, PREP_TILE, w3), lambda bb, i: (bb, i, col_dn)),
                  pl.BlockSpec((1, HALO, w3), lambda bb, i: (bb, jnp.maximum(i * hb - 1, 0), col_dn)),
                  pl.BlockSpec((1, HALO, w3), lambda bb, i: (bb, jnp.minimum((i + 1) * hb, n_hb - 1), col_dn)),
                  pl.BlockSpec((DN_CONV, w3), lambda bb, i: (0, 0)),
                  pl.BlockSpec((PREP_TILE, LANES), lambda bb, i: (i, 0)),
                  pl.BlockSpec((PREP_TILE, LANES), lambda bb, i: (i, 0)),
                  sca(), row(), row()],
        out_specs=[tok(), tok(), tok(), sca(), sca()],
        out_shape=[jax.ShapeDtypeStruct((b, ta, DN_WIDTH), BF16)] * 3
        + [jax.ShapeDtypeStruct((b, ta, LANES), F32)] * 2,
        scratch_shapes=[pltpu.VMEM((PREP_TILE + HALO, w3), F32)],
        compiler_params=_cparams(("parallel", "parallel")),
        name="dn_prep",
    )(p, p, p, conv_w, rope_c, rope_s, ab, alog_row, dtb_row)


CH4 = DN_HEADS * DN_CHUNK


DN_CPS = 4
DN_PAIRS = DN_HEADS // 2
PAIR = 2 * DN_CHUNK


def _stack_pair(x, p):
    return jnp.concatenate([x[:, (2 * p + h) * LANES:(2 * p + h + 1) * LANES] for h in range(2)], axis=0)


def _dn_local_kernel(q_ref, k_ref, v_ref, gc_ref, bc_ref, gr_ref, wq_ref, u_ref, a_ref, kgt_ref):
    r = lax.broadcasted_iota(jnp.int32, (PAIR, PAIR), 0)
    c = lax.broadcasted_iota(jnp.int32, (PAIR, PAIR), 1)
    same = (r // DN_CHUNK) == (c // DN_CHUNK)
    chains = []
    for cc in range(DN_CPS):
        toks = slice(cc * DN_CHUNK, (cc + 1) * DN_CHUNK)
        gc = gc_ref[0, toks, :]
        bc = bc_ref[0, toks, :]
        for p in range(DN_PAIRS):
            rows = slice(p * PAIR, (p + 1) * PAIR)
            kst = _stack_pair(k_ref[0, toks, :], p)
            qst = _stack_pair(q_ref[0, toks, :], p)
            vst = _stack_pair(v_ref[0, toks, :], p)
            kq = _dot_nt(jnp.concatenate([kst, qst], axis=0), kst)
            kk = kq[:PAIR]
            qk = kq[PAIR:]
            kf = kst.astype(F32)
            for d in range(2):
                later = (r >= c) if d == 0 else (r <= c)
                incl = same & later
                strict = incl & (r != c)
                lane0 = d * DN_HEADS + 2 * p
                col = lambda a: jnp.concatenate([a[:, lane0 + h:lane0 + h + 1] for h in range(2)], axis=0)
                g_col = col(gc)
                b_col = col(bc)
                last = DN_CHUNK - 1 if d == 0 else 0
                gl_col = jnp.concatenate(
                    [jnp.broadcast_to(gc[last:last + 1, lane0 + h:lane0 + h + 1], (DN_CHUNK, 1))
                     for h in range(2)], axis=0)
                g_row = gr_ref[0, cc, d:d + 1, rows]
                decay = jnp.exp(jnp.where(incl, g_col - g_row, NEG_INF))
                pw = jnp.where(strict, -(kk * decay * b_col), 0.0)
                z = jnp.concatenate([vst.astype(F32) * b_col, kf * (b_col * jnp.exp(g_col))], axis=1)
                chains.append([pw, z, (cc, d, rows)])
                wq_ref[0, cc, d, CH4 + p * PAIR:CH4 + (p + 1) * PAIR, :] = (
                    qst.astype(F32) * jnp.exp(g_col)).astype(BF16)
                a_ref[0, cc, d, p] = (qk * decay).astype(BF16)
                kgt_ref[0, cc, d, :, rows] = (kf * jnp.exp(gl_col - g_col)).T.astype(BF16)

    for step in range(6):
        if step:
            for chain in chains:
                pwb = chain[0].astype(BF16)
                chain[0] = _dot(pwb, pwb)
        for chain in chains:
            pw, z, _ = chain
            chain[1] = z + _dot(pw.astype(BF16), z.astype(BF16))
    for _, z, (cc, d, rows) in chains:
        wq_ref[0, cc, d, rows, :] = z[:, LANES:].astype(BF16)
        u_ref[0, cc, d, rows, :] = z[:, :LANES]


def _dn_local(qd, kd, vd, gc, bc, g_rows):
    b, ta, _ = qd.shape
    nc = ta // DN_CHUNK
    tok = lambda w: pl.BlockSpec((1, DN_CPS * DN_CHUNK, w), lambda bb, n: (bb, n, 0))
    out = lambda *tail: pl.BlockSpec((1, DN_CPS, 2) + tail, lambda bb, n: (bb, n, 0) + (0,) * len(tail))
    return pl.pallas_call(
        _dn_local_kernel,
        grid=(b, nc // DN_CPS),
        in_specs=[tok(DN_WIDTH), tok(DN_WIDTH), tok(DN_WIDTH), tok(LANES), tok(LANES),
                  pl.BlockSpec((1, DN_CPS, 2, CH4), lambda bb, n: (bb, n, 0, 0))],
        out_specs=[out(2 * CH4, LANES), out(CH4, LANES), out(DN_PAIRS, PAIR, PAIR), out(LANES, CH4)],
        out_shape=[jax.ShapeDtypeStruct((b, nc, 2, 2 * CH4, LANES), BF16),
                   jax.ShapeDtypeStruct((b, nc, 2, CH4, LANES), F32),
                   jax.ShapeDtypeStruct((b, nc, 2, DN_PAIRS, PAIR, PAIR), BF16),
                   jax.ShapeDtypeStruct((b, nc, 2, LANES, CH4), BF16)],
        compiler_params=_cparams(("parallel", "parallel")),
        name="dn_local",
    )(qd, kd, vd, gc, bc, g_rows)


DN_SPS = 4


def _dn_scan_kernel(wq_f, u_f, a_f, kgt_f, gl_f, wq_b, u_b, a_b, kgt_b, gl_b, of_ref, ob_ref, state):
    n = pl.program_id(0)

    @pl.when(n == 0)
    def _():
        state[...] = jnp.zeros_like(state)

    r = lax.broadcasted_iota(jnp.int32, (CH4, DN_WIDTH), 0)
    c = lax.broadcasted_iota(jnp.int32, (CH4, DN_WIDTH), 1)
    own = (r // DN_CHUNK) == (c // LANES)
    pick = lambda m: jnp.concatenate(
        [m[h * DN_CHUNK:(h + 1) * DN_CHUNK, h * LANES:(h + 1) * LANES] for h in range(DN_HEADS)], axis=0)
    dirs = ((wq_f, u_f, a_f, kgt_f, gl_f, of_ref), (wq_b, u_b, a_b, kgt_b, gl_b, ob_ref))
    chains = [(bb, d) + refs for bb in range(state.shape[0]) for d, refs in enumerate(dirs)]
    s_alls = [state[bb, d] for bb, d, *_ in chains]
    for sub in range(DN_SPS):
        at = [sub if d == 0 else DN_SPS - 1 - sub for _, d, *_ in chains]
        wss = [_dot(wq[bb, j, 0], s.astype(BF16)) for (bb, d, wq, *_), s, j in zip(chains, s_alls, at)]
        v_nbs = [(u[bb, j, 0] - pick(ws[:CH4])).astype(BF16)
                 for (bb, d, wq, u, *_), ws, j in zip(chains, wss, at)]
        for (bb, d, wq, u, a, kgt, gl, o_ref), ws, v_nb, j in zip(chains, wss, v_nbs, at):
            o = pick(ws[CH4:]) + jnp.concatenate(
                [_dot(a[bb, j, 0, p], v_nb[p * PAIR:(p + 1) * PAIR]) for p in range(DN_PAIRS)], axis=0)
            for h in range(DN_HEADS):
                o_ref[bb, j * DN_CHUNK:(j + 1) * DN_CHUNK, h * LANES:(h + 1) * LANES] = (
                    o[h * DN_CHUNK:(h + 1) * DN_CHUNK])
        nxt = []
        for (bb, d, wq, u, a, kgt, gl, o_ref), s_all, v_nb, j in zip(chains, s_alls, v_nbs, at):
            v_bd = jnp.where(own, jnp.concatenate([v_nb] * DN_HEADS, axis=1), jnp.zeros((), BF16))
            nxt.append(s_all * jnp.exp(gl[bb, j, 0]) + _dot(kgt[bb, j, 0], v_bd))
        s_alls = nxt
    for (bb, d, *_), s_all in zip(chains, s_alls):
        state[bb, d] = s_all


def _dn_scan(wq, u, a, kgt, gl_wide, *, s_len):
    b, nc = wq.shape[:2]
    n_lat = s_len // DN_CHUNK
    n_ctx = nc - n_lat
    assert n_lat % DN_SPS == 0 and n_ctx % DN_SPS == 0
    nb, nb_lat, nb_ctx = nc // DN_SPS, n_lat // DN_SPS, n_ctx // DN_SPS
    fwd = lambda n: jnp.where(n < nb_ctx, nb_lat + n, n - nb_ctx)
    bwd = lambda n: nb - 1 - n
    specs = []
    for d, order in ((0, fwd), (1, bwd)):
        blk = lambda arr, d=d, order=order: pl.BlockSpec(
            (b, DN_SPS, 1) + arr.shape[3:], lambda n: (0, order(n), d) + (0,) * (arr.ndim - 3))
        specs += [blk(wq), blk(u), blk(a), blk(kgt), blk(gl_wide)]
    o_spec = lambda order: pl.BlockSpec((b, DN_SPS * DN_CHUNK, DN_WIDTH), lambda n: (0, order(n), 0))
    return pl.pallas_call(
        _dn_scan_kernel,
        grid=(nb,),
        in_specs=specs,
        out_specs=[o_spec(fwd), o_spec(bwd)],
        out_shape=[jax.ShapeDtypeStruct((b, nc * DN_CHUNK, DN_WIDTH), F32)] * 2,
        scratch_shapes=[pltpu.VMEM((b, 2, DN_HEAD_DIM, DN_WIDTH), F32)],
        compiler_params=_cparams(("arbitrary",)),
        name="dn_scan",
    )(wq, u, a, kgt, gl_wide, wq, u, a, kgt, gl_wide)


MERGE_PARTS = 2


def _merge_kernel(ona_ref, of_ref, ob_ref, z_ref, sc_ref, schp_ref, schn_ref, gate_ref, x_ref,
                  g1l_ref, g1c_ref, sh2l_ref, sh2c_ref, sc2l_ref, sc2c_ref, ng_ref, og_ref, scw_ref,
                  wna_ref, wdn_ref, wsc_ref, wout_ref, wrh_ref, wrl_ref, rb_ref,
                  x1_ref, h2_ref, comb_ref, combt_ref, cpad, *, tm, s_len, ta):
    i = pl.program_id(1)
    d = x_ref.shape[-1]
    half = HALO // 2

    def cu(ref_val):
        v = ref_val.astype(F32)
        return v[:, 2 * SC_WIDTH:] * v[:, :SC_WIDTH]
    cpad[0:half, :] = cu(schp_ref[0])[half:, :]
    cpad[half:half + tm, :] = cu(sc_ref[0])
    cpad[half + tm:, :] = cu(schn_ref[0])[:half, :]

    nr = tm // MERGE_PARTS
    parts = [slice(k * nr, (k + 1) * nr) for k in range(MERGE_PARTS)]

    def select(k, lat, ctx):
        row = i * tm + k * nr + lax.broadcasted_iota(jnp.int32, (nr, 1), 0)
        return jnp.where(row >= s_len, ctx, lat)

    branch_in = []
    for k, rows in enumerate(parts):
        o = of_ref[0, rows, :] + ob_ref[0, rows, :]
        heads = []
        for h in range(DN_HEADS):
            oh = o[:, h * LANES:(h + 1) * LANES]
            heads.append(oh * lax.rsqrt(jnp.mean(oh * oh, axis=-1, keepdims=True) + EPS))
        o_dn = jnp.concatenate(heads, axis=1) * og_ref[...] * _silu(z_ref[0, rows, :].astype(F32))
        gt = i * tm + k * nr + lax.broadcasted_iota(jnp.int32, (nr, 1), 0)
        has_prev = (gt != 0) & (gt != s_len)
        has_next = (gt != s_len - 1) & (gt != ta - 1)
        conv = (scw_ref[0:1, :] * jnp.where(has_prev, cpad[pl.ds(half - 1 + k * nr, nr), :], 0.0)
                + scw_ref[1:2, :] * cpad[pl.ds(half + k * nr, nr), :]
                + scw_ref[2:3, :] * jnp.where(has_next, cpad[pl.ds(half + 1 + k * nr, nr), :], 0.0))
        o_sc = sc_ref[0, rows, SC_WIDTH:2 * SC_WIDTH].astype(F32) * conv
        branch_in.append((o_dn.astype(BF16), o_sc.astype(BF16)))

    merged = []
    for rows, (o_dn, o_sc) in zip(parts, branch_in):
        merged.append((gate_ref[0, rows, :d].astype(F32) * _dot(ona_ref[0, rows, :], wna_ref[...])
                       + gate_ref[0, rows, d:2 * d].astype(F32) * _dot(o_dn, wdn_ref[...])
                       + gate_ref[0, rows, 2 * d:].astype(F32) * _dot(o_sc, wsc_ref[...])).astype(BF16))

    logit_parts = []
    for k, (rows, m) in enumerate(zip(parts, merged)):
        x1 = x_ref[0, rows, :] + select(k, g1l_ref[0], g1c_ref[...]) * _dot(m, wout_ref[...])
        x1_ref[0, rows, :] = x1
        hn = x1 * lax.rsqrt(jnp.mean(x1 * x1, axis=-1, keepdims=True) + EPS) * ng_ref[...]
        h2 = hn * (1.0 + select(k, sc2l_ref[0], sc2c_ref[...])) + select(k, sh2l_ref[0], sh2c_ref[...])
        h2_ref[0, rows, :] = h2.astype(BF16)
        h2_hi, h2_lo = _split_bf16(h2)
        logit_parts.append(_dot3(h2_hi, h2_lo, wrh_ref[...], wrl_ref[...]))

    logits = jnp.concatenate(logit_parts, axis=0)
    scores = jax.nn.sigmoid(logits.T)
    biased = scores + rb_ref[...]
    rows = [biased[e:e + 1, :] for e in range(N_EXPERTS)]
    gsc = []
    for g in range(N_GROUPS):
        a, b_, c_, d_ = rows[4 * g:4 * g + 4]
        gsc.append(jnp.maximum(jnp.maximum(jnp.maximum(a + b_, a + c_), jnp.maximum(a + d_, b_ + c_)),
                               jnp.maximum(b_ + d_, c_ + d_)))
    best = gsc[0]
    grp = jnp.zeros_like(best, dtype=jnp.int32)
    for g in range(1, N_GROUPS):
        upd = gsc[g] > best
        best = jnp.where(upd, gsc[g], best)
        grp = jnp.where(upd, g, grp)
    eidx = lax.broadcasted_iota(jnp.int32, biased.shape, 0)
    masked = jnp.where((eidx // EXPERTS_PER_GROUP) == grp, biased, -jnp.inf)
    big = jnp.int32(LANES)
    m1 = masked.max(axis=0, keepdims=True)
    i1 = jnp.where(masked == m1, eidx, big).min(axis=0, keepdims=True)
    masked2 = jnp.where(eidx == i1, -jnp.inf, masked)
    m2 = masked2.max(axis=0, keepdims=True)
    i2 = jnp.where(masked2 == m2, eidx, big).min(axis=0, keepdims=True)
    sel1 = eidx == i1
    sel2 = eidx == i2
    s1 = jnp.sum(jnp.where(sel1, scores, 0.0), axis=0, keepdims=True)
    s2 = jnp.sum(jnp.where(sel2, scores, 0.0), axis=0, keepdims=True)
    tot = s1 + s2
    comb_t = jnp.where(sel1, s1 / tot, 0.0) + jnp.where(sel2, s2 / tot, 0.0)
    meta_t = jnp.where(eidx == N_EXPERTS, grp.astype(F32), comb_t)
    combt_ref[0] = meta_t
    comb_ref[0] = meta_t.T


def _merge(o_na, o_f, o_b, z, sc, gates, xs, mods_l, mods_c, norm_g, out_g, sc_w, wna, wdn, wsc, wout, wr, rb,
           *, s_len, tm):
    b, ta, d = xs.shape
    col_z = col_sc = 0
    hb = tm // HALO
    n_hb = ta // HALO
    tok = lambda w, col=0: pl.BlockSpec((1, tm, w), lambda bb, i: (bb, i, col))
    vec = lambda w: pl.BlockSpec((1, w), lambda bb, i: (0, 0))
    bvec = lambda: pl.BlockSpec((1, 1, d), lambda bb, i: (bb, 0, 0))
    full = lambda a: pl.BlockSpec(a.shape, lambda bb, i: (0,) * a.ndim)
    g1l, sh2l, sc2l = mods_l
    g1c, sh2c, sc2c = mods_c
    w3 = 3 * SC_WIDTH
    wr_hi = wr.astype(BF16)
    wr_lo = (wr - wr_hi.astype(F32)).astype(BF16)
    return pl.pallas_call(
        functools.partial(_merge_kernel, tm=tm, s_len=s_len, ta=ta),
        grid=(b, ta // tm),
        in_specs=[tok(NA_WIDTH), tok(DN_WIDTH), tok(DN_WIDTH), tok(DN_WIDTH, col_z), tok(w3, col_sc),
                  pl.BlockSpec((1, HALO, w3), lambda bb, i: (bb, jnp.maximum(i * hb - 1, 0), col_sc)),
                  pl.BlockSpec((1, HALO, w3), lambda bb, i: (bb, jnp.minimum((i + 1) * hb, n_hb - 1), col_sc)),
                  tok(3 * d, 0), tok(d),
                  bvec(), vec(d), bvec(), vec(d), bvec(), vec(d), vec(d), vec(DN_WIDTH),
                  full(sc_w), full(wna), full(wdn), full(wsc), full(wout), full(wr_hi), full(wr_lo), full(rb)],
        out_specs=[tok(d), tok(d), tok(LANES), pl.BlockSpec((1, LANES, tm), lambda bb, i: (bb, 0, i))],
        out_shape=[jax.ShapeDtypeStruct((b, ta, d), F32),
                   jax.ShapeDtypeStruct((b, ta, d), BF16),
                   jax.ShapeDtypeStruct((b, ta, LANES), F32),
                   jax.ShapeDtypeStruct((b, LANES, ta), F32)],
        scratch_shapes=[pltpu.VMEM((tm + HALO, SC_WIDTH), F32)],
        compiler_params=_cparams(("parallel", "parallel")),
        name="merge_router",
    )(o_na, o_f, o_b, z, sc, sc, sc, gates, xs, g1l, g1c, sh2l, sh2c, sc2l, sc2c, norm_g, out_g,
      sc_w, wna, wdn, wsc, wout, wr_hi, wr_lo, rb)


MOE_CAP = 256


def _moe_kernel(h_ref, comb_ref, combt_ref, lt_ref, gt_ref, wg_ref, wu_ref, wd_ref, x_ref, g2l_ref, g2c_ref,
                o_ref, acc, rank_r, rank_c, *, tm, s_len):
    i = pl.program_id(1)
    g = pl.program_id(2)

    comb = comb_ref[0]
    lane = lax.broadcasted_iota(jnp.int32, (tm, LANES), 1)
    grp_row = combt_ref[0, N_EXPERTS:N_EXPERTS + 1, :]
    grp_col = jnp.sum(jnp.where(lane == N_EXPERTS, comb, 0.0), axis=-1, keepdims=True)

    @pl.when(g == 0)
    def _():
        acc[...] = jnp.zeros_like(acc)
        sub = lax.broadcasted_iota(jnp.int32, (8, tm), 0).astype(F32)
        before = _dot(jnp.where(sub == grp_row, 1.0, 0.0).astype(BF16), lt_ref[...])
        rank_r[...] = jnp.broadcast_to(
            jnp.sum(jnp.where(sub == grp_row, before, 0.0), axis=0, keepdims=True), (8, tm))
        lane_f = lane.astype(F32)
        before = _dot(gt_ref[...], jnp.where(lane_f == grp_col, 1.0, 0.0).astype(BF16))
        rank_c[...] = jnp.broadcast_to(
            jnp.sum(jnp.where(lane_f == grp_col, before, 0.0), axis=-1, keepdims=True), (tm, LANES))

    gf = g.astype(F32)
    member_row = grp_row == gf
    member_col = grp_col == gf
    rank_row = rank_r[0:1, :]
    rank_col = rank_c[:, 0:1]
    count = jnp.sum(jnp.where(member_row, 1.0, 0.0))
    comb_hi, comb_lo = _split_bf16(comb)
    d = x_ref.shape[-1]

    for k in range(tm // MOE_CAP):
        @pl.when(count > k * MOE_CAP)
        def _(k=k):
            slot_c = (lax.broadcasted_iota(jnp.int32, (MOE_CAP, 1), 0) + k * MOE_CAP).astype(F32)
            gather = jnp.where(member_row & (rank_row == slot_c), 1.0, 0.0).astype(BF16)
            hs_t = _dot(gather, h_ref[0]).T.astype(BF16)
            cws_t = (_dot(gather, comb_hi) + _dot(gather, comb_lo)).T
            sub_c = lax.broadcasted_iota(jnp.int32, (LANES, MOE_CAP), 0)
            experts = range(EXPERTS_PER_GROUP)
            gates = [_dot(wg_ref[0, e4], hs_t) for e4 in experts]
            ups = [_dot(wu_ref[0, e4], hs_t) for e4 in experts]
            hiddens = []
            for e4 in experts:
                cw = jnp.sum(jnp.where(sub_c == g * EXPERTS_PER_GROUP + e4, cws_t, 0.0), axis=0, keepdims=True)
                hiddens.append((_silu(gates[e4]) * ups[e4] * cw).T.astype(BF16))
            ys = _dot(hiddens[0], wd_ref[0, 0])
            for e4 in experts[1:]:
                ys = ys + _dot(hiddens[e4], wd_ref[0, e4])
            slot_r = (lax.broadcasted_iota(jnp.int32, (1, MOE_CAP), 1) + k * MOE_CAP).astype(F32)
            scatter = jnp.where(member_col & (rank_col == slot_r), 1.0, 0.0).astype(BF16)
            ys_hi, ys_lo = _split_bf16(ys)
            acc[...] += _dot(scatter, ys_hi) + _dot(scatter, ys_lo)

    @pl.when(g == N_GROUPS - 1)
    def _():
        o_ref[0] = x_ref[0] + _row_select(i, tm, s_len, g2l_ref[0], g2c_ref[...]) * acc[...]


def _moe(h2, comb, comb_t, wg, wu, wd, x1, g2l, g2c, *, layer, s_len, tm):
    b, ta, d = x1.shape
    tok = lambda w: pl.BlockSpec((1, tm, w), lambda bb, i, g: (bb, i, 0))
    grp = lambda a: pl.BlockSpec((1, EXPERTS_PER_GROUP) + a.shape[2:], lambda bb, i, g: (layer, g, 0, 0))
    tri = lambda: pl.BlockSpec((tm, tm), lambda bb, i, g: (0, 0))
    earlier = np.arange(tm)[:, None] < np.arange(tm)[None, :]
    lt = jnp.asarray(earlier, BF16)
    gt = jnp.asarray(earlier.T, BF16)
    return pl.pallas_call(
        functools.partial(_moe_kernel, tm=tm, s_len=s_len),
        grid=(b, ta // tm, N_GROUPS),
        in_specs=[tok(d), tok(LANES),
                  pl.BlockSpec((1, LANES, tm), lambda bb, i, g: (bb, 0, i)),
                  tri(), tri(),
                  grp(wg), grp(wu), grp(wd),
                  tok(d),
                  pl.BlockSpec((1, 1, d), lambda bb, i, g: (bb, 0, 0)),
                  pl.BlockSpec((1, d), lambda bb, i, g: (0, 0))],
        out_specs=tok(d),
        out_shape=jax.ShapeDtypeStruct((b, ta, d), F32),
        scratch_shapes=[pltpu.VMEM((tm, d), F32), pltpu.VMEM((8, tm), F32), pltpu.VMEM((tm, LANES), F32)],
        compiler_params=_cparams(("parallel", "parallel", "arbitrary")),
        name="moe_grouped",
    )(h2, comb, comb_t, lt, gt, wg, wu, wd, x1, g2l, g2c)


def _rope_tables(s_len, ctx_len):
    t = np.arange(s_len)
    n_freq = DN_HEAD_DIM // 4
    inv_freq = (np.float32(ROPE_BASE) ** (-np.arange(n_freq, dtype=np.float32) / n_freq)).astype(np.float32)
    pos = np.stack([t // GRID_W, t % GRID_W], axis=-1).astype(np.float32)
    ang = (pos[..., None] * inv_freq).astype(np.float32)
    cos, sin = np.cos(ang.astype(np.float64)), np.sin(ang.astype(np.float64))
    rc = np.concatenate([cos[:, 0], cos[:, 0], cos[:, 1], cos[:, 1]], axis=-1)
    rs = np.concatenate([-sin[:, 0], sin[:, 0], -sin[:, 1], sin[:, 1]], axis=-1)
    rc = np.concatenate([rc, np.ones((ctx_len, LANES))], axis=0)
    rs = np.concatenate([rs, np.zeros((ctx_len, LANES))], axis=0)
    return jnp.asarray(rc, F32), jnp.asarray(rs, F32)


def _pick_tile(ta, candidates):
    for t in candidates:
        if ta % t == 0:
            return t
    raise ValueError(f"no row tile for {ta} tokens")


def kernel(x, c, ctx, c_ctx, w_ada, b_ada, norm_mix_g, norm_ffn_g, w_in, na_q_norm_g, na_k_norm_g, na_rpb, dn_conv_w, dn_a_log, dn_dt_bias, dn_out_norm_g, sc_conv_w, w_branch_na, w_branch_dn, w_branch_sc, w_out, w_router, router_bias, moe_w_gate, moe_w_up, moe_w_down):
    b, s_len, d = x.shape
    ctx_len = ctx.shape[1]
    depth = w_ada.shape[0]
    ta = s_len + ctx_len
    assert ctx_len == NA_BLK and s_len % (WIN_R * GRID_W) == 0 and s_len // NA_BLK >= 4
    assert ctx_len % PREP_TILE == 0 and b + 1 <= 8
    tm_proj = _pick_tile(ta, (768, 512, 256))
    tm_merge = _pick_tile(ta, (384, 256))
    tm_moe = _pick_tile(ta, (768, 512, 256))

    xs = jnp.concatenate([x, ctx], axis=1)
    c_rows = jnp.zeros((8, d), F32).at[:b].set(c).at[b].set(c_ctx)
    mod = _modulation(c_rows, w_ada, b_ada)

    sizes = (3 * NA_WIDTH, 3 * DN_WIDTH, DN_WIDTH, 4 * DN_HEADS, 3 * SC_WIDTH, 3 * d)
    cuts = [int(v) for v in np.cumsum((0,) + sizes)]
    seg = lambda l, k: lax.slice(w_in, (l, 0, cuts[k]), (l + 1, d, cuts[k + 1]))[0]
    wg_all = jnp.swapaxes(moe_w_gate, -1, -2).astype(BF16)
    wu_all = jnp.swapaxes(moe_w_up, -1, -2).astype(BF16)
    wd_all = moe_w_down.astype(BF16)

    rope_c, rope_s = _rope_tables(s_len, ctx_len)
    bd = jnp.asarray(np.kron(np.eye(NA_HEADS), np.ones((NA_HEAD_DIM, NA_HEAD_DIM))), BF16)
    nc = ta // DN_CHUNK
    wr = jnp.zeros((d, LANES), F32).at[:, :N_EXPERTS].set(w_router)
    rb = jnp.zeros((LANES, 1), F32).at[:N_EXPERTS, 0].set(router_bias)

    for l in range(depth):
        m = mod[l].reshape(8, N_MOD, d)
        lat = lambda k: m[:b, k][:, None, :]
        cx = lambda k: m[b:b + 1, k]
        w_ab = jnp.zeros((d, LANES), F32).at[:, :4 * DN_HEADS].set(seg(l, 3))
        gqk = jnp.concatenate([jnp.tile(na_q_norm_g[l], NA_HEADS) * (NA_HEAD_DIM ** -0.5),
                               jnp.tile(na_k_norm_g[l], NA_HEADS)])[None, :]
        gates, na_q, na_k, na_v, dn_qkv, sc_p, dn_z, ab = _projection(
            xs, (lat(0), lat(1)), (cx(0), cx(1)), norm_mix_g[l][None, :],
            seg(l, 5).astype(BF16), seg(l, 0).astype(BF16), seg(l, 1).astype(BF16),
            seg(l, 4).astype(BF16), seg(l, 2).astype(BF16), w_ab, gqk, bd, s_len=s_len, tm=tm_proj)

        o_na = _neighbourhood_attention(na_q, na_k, na_v, _na_bias_table(na_rpb[l]), s_len=s_len)

        alog_row = jnp.zeros((1, LANES), F32).at[0, :2 * DN_HEADS].set(dn_a_log[l].reshape(-1))
        dtb_row = jnp.zeros((1, LANES), F32).at[0, :2 * DN_HEADS].set(dn_dt_bias[l].reshape(-1))
        qd, kd, vd, gc, bc = _dn_prep(dn_qkv, ab, dn_conv_w[l], rope_c, rope_s, alog_row, dtb_row,
                                      s_len=s_len)
        g8 = gc[..., :2 * DN_HEADS].reshape(b, nc, DN_CHUNK, 2, DN_HEADS)
        g_rows = jnp.transpose(g8, (0, 1, 3, 4, 2)).reshape(b, nc, 2, CH4)
        g_last = jnp.stack([g8[:, :, DN_CHUNK - 1, 0], g8[:, :, 0, 1]], axis=2)
        gl_wide = jnp.repeat(g_last, LANES, axis=-1)[:, :, :, None, :]
        wq, u, a, kgt = _dn_local(qd, kd, vd, gc, bc, g_rows)
        o_f, o_b = _dn_scan(wq, u, a, kgt, gl_wide, s_len=s_len)

        x1, h2, comb, comb_t = _merge(
            o_na, o_f, o_b, dn_z, sc_p, gates, xs, (lat(2), lat(3), lat(4)), (cx(2), cx(3), cx(4)),
            norm_ffn_g[l][None, :], jnp.tile(dn_out_norm_g[l], DN_HEADS)[None, :], sc_conv_w[l],
            w_branch_na[l].astype(BF16), w_branch_dn[l].astype(BF16), w_branch_sc[l].astype(BF16),
            w_out[l].astype(BF16), wr, rb, s_len=s_len, tm=tm_merge)

        xs = _moe(h2, comb, comb_t, wg_all, wu_all, wd_all, x1, lat(5), cx(5), layer=l, s_len=s_len, tm=tm_moe)

    return xs[:, :s_len]
```

```python
import functools
import math

import numpy as np
import jax
import jax.numpy as jnp
from jax import lax
from jax.experimental import pallas as pl
from jax.experimental.pallas import tpu as pltpu

F32 = jnp.float32
BF16 = jnp.bfloat16
HIGHEST = lax.Precision.HIGHEST

GRID_W = 64
NA_HEADS = 8
NA_HEAD_DIM = 64
NA_WIDTH = NA_HEADS * NA_HEAD_DIM
WIN_R = 8
WIN_C = 16
DN_HEADS = 4
DN_HEAD_DIM = 128
DN_WIDTH = DN_HEADS * DN_HEAD_DIM
DN_CONV = 5
DN_CHUNK = 64
ROPE_BASE = 10000.0
SC_WIDTH = 512
SC_CONV = 3
N_EXPERTS = 16
N_GROUPS = 4
EXPERTS_PER_GROUP = N_EXPERTS // N_GROUPS
D_EXPERT = 512
N_MOD = 6
EPS = 1e-6
NEG_INF = -1e30

LANES = 128
HALO = 16
PREP_TILE = 256
VMEM_LIMIT = 56 * 1024 * 1024

N_PROJ_TILE = 512


def _cparams(sem):
    return pltpu.CompilerParams(dimension_semantics=sem, vmem_limit_bytes=VMEM_LIMIT)


def _dot(a, b):
    return jnp.dot(a, b, preferred_element_type=F32)


def _dot_nt(a, b):
    return lax.dot_general(a, b, (((1,), (1,)), ((), ())), preferred_element_type=F32)


def _silu(x):
    return x * jax.nn.sigmoid(x)


def _row_select(i, tm, s_len, lat, ctx):
    row = i * tm + lax.broadcasted_iota(jnp.int32, (tm, 1), 0)
    return jnp.where(row >= s_len, ctx, lat)


def _mod_kernel(c_ref, w_ref, b_ref, o_ref):
    c_hi, c_lo = _split_bf16(_silu(c_ref[...]))
    w_hi, w_lo = _split_bf16(w_ref[0])
    o_ref[0] = _dot3(c_hi, c_lo, w_hi, w_lo) + b_ref[0]


def _modulation(c_rows, w_ada, b_ada):
    depth, d, n = w_ada.shape
    tn = 512
    return pl.pallas_call(
        _mod_kernel,
        grid=(depth, n // tn),
        in_specs=[pl.BlockSpec((8, d), lambda l, j: (0, 0)),
                  pl.BlockSpec((1, d, tn), lambda l, j: (l, 0, j)),
                  pl.BlockSpec((1, 1, tn), lambda l, j: (l, 0, j))],
        out_specs=pl.BlockSpec((1, 8, tn), lambda l, j: (l, 0, j)),
        out_shape=jax.ShapeDtypeStruct((depth, 8, n), F32),
        compiler_params=_cparams(("parallel", "parallel")),
        name="adaln_mod",
    )(c_rows, w_ada, b_ada.reshape(depth, 1, n))


def _split_bf16(a):
    hi = a.astype(BF16)
    return hi, (a - hi.astype(F32)).astype(BF16)


def _dot3(a_hi, a_lo, b_hi, b_lo):
    return _dot(a_hi, b_hi) + _dot(a_lo, b_hi) + _dot(a_hi, b_lo)


def _modulated_norm(x_ref, shl_ref, scl_ref, shc_ref, scc_ref, g_ref, *, tm, s_len):
    i = pl.program_id(1)
    x = x_ref[0]
    y = x * lax.rsqrt(jnp.mean(x * x, axis=-1, keepdims=True) + EPS) * g_ref[...]
    scale = _row_select(i, tm, s_len, scl_ref[0], scc_ref[...])
    shift = _row_select(i, tm, s_len, shl_ref[0], shc_ref[...])
    return y * (1.0 + scale) + shift


def _proj_gate_kernel(x_ref, shl_ref, scl_ref, shc_ref, scc_ref, g_ref, w_ref, gate_ref, *, tm, s_len):
    h = _modulated_norm(x_ref, shl_ref, scl_ref, shc_ref, scc_ref, g_ref, tm=tm, s_len=s_len).astype(BF16)
    for c0 in range(0, w_ref.shape[1], N_PROJ_TILE):
        cols = slice(c0, c0 + N_PROJ_TILE)
        gate_ref[0, :, cols] = jax.nn.sigmoid(_dot(h, w_ref[:, cols])).astype(BF16)


def _proj_mix_kernel(x_ref, shl_ref, scl_ref, shc_ref, scc_ref, g_ref, wna_ref, wdn_ref, wsc_ref, wz_ref,
                     wabh_ref, wabl_ref, gqk_ref, bd_ref,
                     q_ref, k_ref, v_ref, dn_ref, sc_ref, z_ref, ab_ref, *, tm, s_len):
    h = _modulated_norm(x_ref, shl_ref, scl_ref, shc_ref, scc_ref, g_ref, tm=tm, s_len=s_len)
    h_hi, h_lo = _split_bf16(h)
    ab_ref[0] = _dot3(h_hi, h_lo, wabh_ref[...], wabl_ref[...])
    for part, out in enumerate((q_ref, k_ref)):
        cols = slice(part * NA_WIDTH, (part + 1) * NA_WIDTH)
        a = _dot(h_hi, wna_ref[:, cols])
        ss = _dot((a * a).astype(BF16), bd_ref[...])
        out[0] = (a * lax.rsqrt(ss * (1.0 / NA_HEAD_DIM) + EPS) * gqk_ref[:, cols]).astype(BF16)
    v_ref[0] = _dot(h_hi, wna_ref[:, 2 * NA_WIDTH:]).astype(BF16)
    for w_ref, out in ((wdn_ref, dn_ref), (wsc_ref, sc_ref), (wz_ref, z_ref)):
        for c0 in range(0, w_ref.shape[1], N_PROJ_TILE):
            cols = slice(c0, c0 + N_PROJ_TILE)
            out[0, :, cols] = _dot(h_hi, w_ref[:, cols]).astype(BF16)


def _projection(xs, mod_l, mod_c, norm_g, w_gate, w_na, w_dn, w_sc, w_z, w_ab, gqk, bd, *, s_len, tm):
    b, ta, d = xs.shape
    shl, scl = mod_l
    shc, scc = mod_c
    wab_hi, wab_lo = _split_bf16(w_ab)
    tok = lambda w: pl.BlockSpec((1, tm, w), lambda bb, i: (bb, i, 0))
    bvec = lambda: pl.BlockSpec((1, 1, d), lambda bb, i: (bb, 0, 0))
    full = lambda a: pl.BlockSpec(a.shape, lambda bb, i: (0,) * a.ndim)
    head = [tok(d), bvec(), bvec(), full(shc), full(scc), full(norm_g)]
    gates = pl.pallas_call(
        functools.partial(_proj_gate_kernel, tm=tm, s_len=s_len),
        grid=(b, ta // tm),
        in_specs=head + [full(w_gate)],
        out_specs=tok(w_gate.shape[1]),
        out_shape=jax.ShapeDtypeStruct((b, ta, w_gate.shape[1]), BF16),
        compiler_params=_cparams(("parallel", "parallel")),
        name="proj_gates",
    )(xs, shl, scl, shc, scc, norm_g, w_gate)
    consts = (w_na, w_dn, w_sc, w_z, wab_hi, wab_lo, gqk, bd)
    widths = (NA_WIDTH, NA_WIDTH, NA_WIDTH, w_dn.shape[1], w_sc.shape[1], w_z.shape[1])
    outs = pl.pallas_call(
        functools.partial(_proj_mix_kernel, tm=tm, s_len=s_len),
        grid=(b, ta // tm),
        in_specs=head + [full(a) for a in consts],
        out_specs=[tok(w) for w in widths] + [tok(LANES)],
        out_shape=[jax.ShapeDtypeStruct((b, ta, w), BF16) for w in widths]
        + [jax.ShapeDtypeStruct((b, ta, LANES), F32)],
        compiler_params=_cparams(("parallel", "parallel")),
        name="proj_mixers",
    )(xs, shl, scl, shc, scc, norm_g, *consts)
    return (gates,) + tuple(outs)


NA_QROWS = 4
NA_BLK = NA_QROWS * GRID_W
WIN_TOK = WIN_R * GRID_W


def _stack_two_heads(q2):
    lane = lax.broadcasted_iota(jnp.int32, q2.shape, 1)
    zero = jnp.zeros_like(q2)
    return jnp.concatenate([jnp.where(lane < NA_HEAD_DIM, q2, zero),
                            jnp.where(lane >= NA_HEAD_DIM, q2, zero)], axis=0)


def _unstack_two_heads(o2):
    m = o2.shape[0] // 2
    lane = lax.broadcasted_iota(jnp.int32, (m, o2.shape[1]), 1)
    return jnp.where(lane < NA_HEAD_DIM, o2[:m], o2[m:])


def _na_kernel(q_ref, kp_ref, kc_ref, kn_ref, vp_ref, vc_ref, vn_ref, kx_ref, vx_ref, bias_ref, o_ref,
               kbuf, vbuf, *, n_rows):
    i = pl.program_id(1)
    n_lat = n_rows // NA_QROWS

    @pl.when(i < n_lat)
    def _():
        for slot, (kr, vr) in enumerate(((kp_ref, vp_ref), (kc_ref, vc_ref), (kn_ref, vn_ref))):
            kbuf[slot * NA_BLK:(slot + 1) * NA_BLK, :] = kr[0]
            vbuf[slot * NA_BLK:(slot + 1) * NA_BLK, :] = vr[0]
        win = []
        for j in range(NA_QROWS):
            r = i * NA_QROWS + j
            start = jnp.clip(r - WIN_R // 2, 0, n_rows - WIN_R)
            win.append((pl.multiple_of((start - (i - 1) * NA_QROWS) * GRID_W, GRID_W), r - start))

        rows2 = 2 * GRID_W
        pairs = [slice(hp * LANES, (hp + 1) * LANES) for hp in range(NA_HEADS // 2)]
        logits = []
        for hp, lanes in enumerate(pairs):
            qs = jnp.concatenate([_stack_two_heads(q_ref[0, j * GRID_W:(j + 1) * GRID_W, lanes])
                                  for j in range(NA_QROWS)], axis=0)
            s_c = _dot_nt(qs, kx_ref[0, :, lanes])
            s_w = jnp.concatenate(
                [_dot_nt(qs[j * rows2:(j + 1) * rows2], kbuf[pl.ds(win[j][0], WIN_TOK), lanes])
                 + bias_ref[win[j][1], hp] for j in range(NA_QROWS)], axis=0)
            logits.append((s_w, s_c))
        probs = []
        for s_w, s_c in logits:
            m = jnp.maximum(s_w.max(axis=-1, keepdims=True), s_c.max(axis=-1, keepdims=True))
            p_w = jnp.exp(s_w - m)
            p_c = jnp.exp(s_c - m)
            denom = p_w.sum(axis=-1, keepdims=True) + p_c.sum(axis=-1, keepdims=True)
            probs.append((p_w.astype(BF16), p_c.astype(BF16), denom))
        for lanes, (p_wb, p_cb, denom) in zip(pairs, probs):
            o = _dot(p_cb, vx_ref[0, :, lanes]) + jnp.concatenate(
                [_dot(p_wb[j * rows2:(j + 1) * rows2], vbuf[pl.ds(win[j][0], WIN_TOK), lanes])
                 for j in range(NA_QROWS)], axis=0)
            o = o / denom
            for j in range(NA_QROWS):
                o_ref[0, j * GRID_W:(j + 1) * GRID_W, lanes] = _unstack_two_heads(
                    o[j * rows2:(j + 1) * rows2]).astype(BF16)

    @pl.when(i == n_lat)
    def _():
        for hp in range(NA_HEADS // 2):
            lanes = slice(hp * LANES, (hp + 1) * LANES)
            s = _dot_nt(_stack_two_heads(q_ref[0, :, lanes]), kx_ref[0, :, lanes])
            p = jnp.exp(s - s.max(axis=-1, keepdims=True))
            o = _dot(p.astype(BF16), vx_ref[0, :, lanes]) / p.sum(axis=-1, keepdims=True)
            o_ref[0, :, lanes] = _unstack_two_heads(o).astype(BF16)


def _neighbourhood_attention(q, k, v, bias_tab, *, s_len):
    b, ta, _ = q.shape
    n_rows = s_len // GRID_W
    n_lat = n_rows // NA_QROWS
    blk = lambda off: pl.BlockSpec((1, NA_BLK, NA_WIDTH), lambda bb, i: (bb, jnp.clip(i + off, 0, n_lat - 1), 0))
    cx = lambda: pl.BlockSpec((1, NA_BLK, NA_WIDTH), lambda bb, i: (bb, n_lat, 0))
    own = lambda: pl.BlockSpec((1, NA_BLK, NA_WIDTH), lambda bb, i: (bb, i, 0))
    return pl.pallas_call(
        functools.partial(_na_kernel, n_rows=n_rows),
        grid=(b, n_lat + 1),
        in_specs=[own(), blk(-1), blk(0), blk(1), blk(-1), blk(0), blk(1), cx(), cx(),
                  pl.BlockSpec(bias_tab.shape, lambda bb, i: (0, 0, 0, 0))],
        out_specs=own(),
        out_shape=jax.ShapeDtypeStruct((b, ta, NA_WIDTH), BF16),
        scratch_shapes=[pltpu.VMEM((3 * NA_BLK, NA_WIDTH), BF16),
                        pltpu.VMEM((3 * NA_BLK, NA_WIDTH), BF16)],
        compiler_params=_cparams(("parallel", "arbitrary")),
        name="na_attention",
    )(q, k, k, k, v, v, v, k, v, bias_tab)


def _na_bias_table(rpb):
    qc = np.arange(GRID_W)[:, None]
    kc = np.arange(GRID_W)[None, :]
    win_c0 = np.clip(qc - WIN_C // 2, 0, GRID_W - WIN_C)
    in_win = (kc >= win_c0) & (kc < win_c0 + WIN_C)
    rel_c = np.clip(kc - qc + WIN_C - 1, 0, 2 * WIN_C - 2)
    pick_c = (rel_c[..., None] == np.arange(2 * WIN_C - 1)).astype(np.float32)
    by_col = jnp.einsum('hab,qkb->haqk', rpb.astype(F32), pick_c, precision=HIGHEST)
    by_col = jnp.where(in_win[None, None], by_col, NEG_INF)
    tab = jnp.stack([by_col[:, WIN_R - 1 - v:2 * WIN_R - 1 - v] for v in range(WIN_R)], axis=0)
    tab = jnp.transpose(tab, (0, 1, 3, 2, 4))
    return tab.reshape(WIN_R, NA_HEADS // 2, 2 * GRID_W, WIN_R * GRID_W)


def _dn_prep_kernel(x_ref, hp_ref, hn_ref, w_ref, rc_ref, rs_ref, ab_ref, alog_ref, dtb_ref,
                    q_ref, k_ref, v_ref, g_ref, beta_ref, xpad, *, s_len, ta):
    i = pl.program_id(1)
    t0 = i * PREP_TILE
    has_prev = jnp.logical_and(t0 != 0, t0 != s_len).astype(F32)
    has_next = jnp.logical_and(t0 + PREP_TILE != s_len, t0 + PREP_TILE != ta).astype(F32)
    half = HALO // 2
    xpad[0:half, :] = hp_ref[0].astype(F32)[half:, :] * has_prev
    xpad[half:half + PREP_TILE, :] = x_ref[0].astype(F32)
    xpad[half + PREP_TILE:, :] = hn_ref[0].astype(F32)[:half, :] * has_next
    xp = xpad[...]
    n_pad = xp.shape[0]
    y = jnp.zeros((PREP_TILE, x_ref.shape[-1]), F32)
    for tap in range(DN_CONV):
        shifted = xp if tap == DN_CONV // 2 else pltpu.roll(xp, (DN_CONV // 2 - tap) % n_pad, 0)
        y = y + w_ref[tap:tap + 1, :] * shifted[half:half + PREP_TILE]
    y = _silu(y)

    lane = lax.broadcasted_iota(jnp.int32, (PREP_TILE, LANES), 1)
    first_half = (lane % (LANES // 2)) < (LANES // 4)
    rc = rc_ref[...]
    rs = rs_ref[...]

    def norm_rope(u, scale):
        u = u * (lax.rsqrt(jnp.sum(u * u, axis=-1, keepdims=True) + EPS) * scale)
        partner = jnp.where(first_half, pltpu.roll(u, LANES - LANES // 4, 1), pltpu.roll(u, LANES // 4, 1))
        return u * rc + partner * rs

    for h in range(DN_HEADS):
        lanes = slice(h * LANES, (h + 1) * LANES)
        qh = norm_rope(y[:, h * LANES:(h + 1) * LANES], DN_HEAD_DIM ** -0.5)
        kh = norm_rope(y[:, DN_WIDTH + h * LANES:DN_WIDTH + (h + 1) * LANES], 1.0)
        q_ref[0, :, lanes] = qh.astype(BF16)
        k_ref[0, :, lanes] = kh.astype(BF16)
    v_ref[0] = y[:, 2 * DN_WIDTH:].astype(BF16)

    ab = ab_ref[0]
    z = ab + dtb_ref[...]
    softplus = jnp.maximum(z, 0.0) + jnp.log1p(jnp.exp(-jnp.abs(z)))
    g = -jnp.exp(alog_ref[...]) * softplus
    g = jnp.where(lane < 2 * DN_HEADS, g, 0.0)
    beta = jax.nn.sigmoid(pltpu.roll(ab, LANES - 2 * DN_HEADS, 1))
    beta_ref[0] = jnp.where(lane < 2 * DN_HEADS, beta, 0.0)

    r = lax.broadcasted_iota(jnp.int32, (PREP_TILE, PREP_TILE), 0)
    c = lax.broadcasted_iota(jnp.int32, (PREP_TILE, PREP_TILE), 1)
    same = (r // DN_CHUNK) == (c // DN_CHUNK)
    tri_f = jnp.where(same & (c <= r), 1.0, 0.0).astype(BF16)
    tri_b = jnp.where(same & (c >= r), 1.0, 0.0).astype(BF16)
    g1 = g.astype(BF16)
    rem = g - g1.astype(F32)
    g2 = rem.astype(BF16)
    g3 = (rem - g2.astype(F32)).astype(BF16)
    cs_f = _dot(tri_f, g1) + _dot(tri_f, g2) + _dot(tri_f, g3)
    cs_b = _dot(tri_b, g1) + _dot(tri_b, g2) + _dot(tri_b, g3)
    g_ref[0] = jnp.where(lane < DN_HEADS, cs_f, cs_b)


def _dn_prep(p, ab, conv_w, rope_c, rope_s, alog_row, dtb_row, *, s_len):
    b, ta, w3 = p.shape
    col_dn = 0
    n_t = ta // PREP_TILE
    hb = PREP_TILE // HALO
    n_hb = ta // HALO
    tok = lambda: pl.BlockSpec((1, PREP_TILE, DN_WIDTH), lambda bb, i: (bb, i, 0))
    sca = lambda: pl.BlockSpec((1, PREP_TILE, LANES), lambda bb, i: (bb, i, 0))
    row = lambda: pl.BlockSpec((1, LANES), lambda bb, i: (0, 0))
    return pl.pallas_call(
        functools.partial(_dn_prep_kernel, s_len=s_len, ta=ta),
        grid=(b, n_t),
        in_specs=[pl.BlockSpec((1, PREP_TILE, w3), lambda bb, i: (bb, i, col_dn)),
                  pl.BlockSpec((1, HALO, w3), lambda bb, i: (bb, jnp.maximum(i * hb - 1, 0), col_dn)),
                  pl.BlockSpec((1, HALO, w3), lambda bb, i: (bb, jnp.minimum((i + 1) * hb, n_hb - 1), col_dn)),
                  pl.BlockSpec((DN_CONV, w3), lambda bb, i: (0, 0)),
                  pl.BlockSpec((PREP_TILE, LANES), lambda bb, i: (i, 0)),
                  pl.BlockSpec((PREP_TILE, LANES), lambda bb, i: (i, 0)),
                  sca(), row(), row()],
        out_specs=[tok(), tok(), tok(), sca(), sca()],
        out_shape=[jax.ShapeDtypeStruct((b, ta, DN_WIDTH), BF16)] * 3
        + [jax.ShapeDtypeStruct((b, ta, LANES), F32)] * 2,
        scratch_shapes=[pltpu.VMEM((PREP_TILE + HALO, w3), F32)],
        compiler_params=_cparams(("parallel", "parallel")),
        name="dn_prep",
    )(p, p, p, conv_w, rope_c, rope_s, ab, alog_row, dtb_row)


CH4 = DN_HEADS * DN_CHUNK


DN_CPS = 2
DN_PAIRS = DN_HEADS // 2
PAIR = 2 * DN_CHUNK


def _stack_pair(x, p):
    return jnp.concatenate([x[:, (2 * p + h) * LANES:(2 * p + h + 1) * LANES] for h in range(2)], axis=0)


def _dn_local_kernel(q_ref, k_ref, v_ref, gc_ref, bc_ref, gr_ref, wq_ref, u_ref, a_ref, kgt_ref):
    r = lax.broadcasted_iota(jnp.int32, (PAIR, PAIR), 0)
    c = lax.broadcasted_iota(jnp.int32, (PAIR, PAIR), 1)
    same = (r // DN_CHUNK) == (c // DN_CHUNK)
    chains = []
    for cc in range(DN_CPS):
        toks = slice(cc * DN_CHUNK, (cc + 1) * DN_CHUNK)
        gc = gc_ref[0, toks, :]
        bc = bc_ref[0, toks, :]
        for p in range(DN_PAIRS):
            rows = slice(p * PAIR, (p + 1) * PAIR)
            kst = _stack_pair(k_ref[0, toks, :], p)
            qst = _stack_pair(q_ref[0, toks, :], p)
            vst = _stack_pair(v_ref[0, toks, :], p)
            kq = _dot_nt(jnp.concatenate([kst, qst], axis=0), kst)
            kk = kq[:PAIR]
            qk = kq[PAIR:]
            kf = kst.astype(F32)
            for d in range(2):
                later = (r >= c) if d == 0 else (r <= c)
                incl = same & later
                strict = incl & (r != c)
                lane0 = d * DN_HEADS + 2 * p
                col = lambda a: jnp.concatenate([a[:, lane0 + h:lane0 + h + 1] for h in range(2)], axis=0)
                g_col = col(gc)
                b_col = col(bc)
                last = DN_CHUNK - 1 if d == 0 else 0
                gl_col = jnp.concatenate(
                    [jnp.broadcast_to(gc[last:last + 1, lane0 + h:lane0 + h + 1], (DN_CHUNK, 1))
                     for h in range(2)], axis=0)
                g_row = gr_ref[0, cc, d:d + 1, rows]
                decay = jnp.exp(jnp.where(incl, g_col - g_row, NEG_INF))
                pw = jnp.where(strict, -(kk * decay * b_col), 0.0)
                z = jnp.concatenate([vst.astype(F32) * b_col, kf * (b_col * jnp.exp(g_col))], axis=1)
                chains.append([pw, z, (cc, d, rows)])
                wq_ref[0, cc, d, CH4 + p * PAIR:CH4 + (p + 1) * PAIR, :] = (
                    qst.astype(F32) * jnp.exp(g_col)).astype(BF16)
                a_ref[0, cc, d, p] = (qk * decay).astype(BF16)
                kgt_ref[0, cc, d, :, rows] = (kf * jnp.exp(gl_col - g_col)).T.astype(BF16)

    for step in range(6):
        if step:
            for chain in chains:
                pwb = chain[0].astype(BF16)
                chain[0] = _dot(pwb, pwb)
        for chain in chains:
            pw, z, _ = chain
            chain[1] = z + _dot(pw.astype(BF16), z.astype(BF16))
    for _, z, (cc, d, rows) in chains:
        wq_ref[0, cc, d, rows, :] = z[:, LANES:].astype(BF16)
        u_ref[0, cc, d, rows, :] = z[:, :LANES]


def _dn_local(qd, kd, vd, gc, bc, g_rows):
    b, ta, _ = qd.shape
    nc = ta // DN_CHUNK
    tok = lambda w: pl.BlockSpec((1, DN_CPS * DN_CHUNK, w), lambda bb, n: (bb, n, 0))
    out = lambda *tail: pl.BlockSpec((1, DN_CPS, 2) + tail, lambda bb, n: (bb, n, 0) + (0,) * len(tail))
    return pl.pallas_call(
        _dn_local_kernel,
        grid=(b, nc // DN_CPS),
        in_specs=[tok(DN_WIDTH), tok(DN_WIDTH), tok(DN_WIDTH), tok(LANES), tok(LANES),
                  pl.BlockSpec((1, DN_CPS, 2, CH4), lambda bb, n: (bb, n, 0, 0))],
        out_specs=[out(2 * CH4, LANES), out(CH4, LANES), out(DN_PAIRS, PAIR, PAIR), out(LANES, CH4)],
        out_shape=[jax.ShapeDtypeStruct((b, nc, 2, 2 * CH4, LANES), BF16),
                   jax.ShapeDtypeStruct((b, nc, 2, CH4, LANES), F32),
                   jax.ShapeDtypeStruct((b, nc, 2, DN_PAIRS, PAIR, PAIR), BF16),
                   jax.ShapeDtypeStruct((b, nc, 2, LANES, CH4), BF16)],
        compiler_params=_cparams(("parallel", "parallel")),
        name="dn_local",
    )(qd, kd, vd, gc, bc, g_rows)


DN_SPS = 4


def _dn_scan_kernel(wq_f, u_f, a_f, kgt_f, gl_f, wq_b, u_b, a_b, kgt_b, gl_b, of_ref, ob_ref, state):
    n = pl.program_id(0)

    @pl.when(n == 0)
    def _():
        state[...] = jnp.zeros_like(state)

    r = lax.broadcasted_iota(jnp.int32, (CH4, DN_WIDTH), 0)
    c = lax.broadcasted_iota(jnp.int32, (CH4, DN_WIDTH), 1)
    own = (r // DN_CHUNK) == (c // LANES)
    pick = lambda m: jnp.concatenate(
        [m[h * DN_CHUNK:(h + 1) * DN_CHUNK, h * LANES:(h + 1) * LANES] for h in range(DN_HEADS)], axis=0)
    dirs = ((wq_f, u_f, a_f, kgt_f, gl_f, of_ref), (wq_b, u_b, a_b, kgt_b, gl_b, ob_ref))
    chains = [(bb, d) + refs for bb in range(state.shape[0]) for d, refs in enumerate(dirs)]
    s_alls = [state[bb, d] for bb, d, *_ in chains]
    for sub in range(DN_SPS):
        at = [sub if d == 0 else DN_SPS - 1 - sub for _, d, *_ in chains]
        wss = [_dot(wq[bb, j, 0], s.astype(BF16)) for (bb, d, wq, *_), s, j in zip(chains, s_alls, at)]
        v_nbs = [(u[bb, j, 0] - pick(ws[:CH4])).astype(BF16)
                 for (bb, d, wq, u, *_), ws, j in zip(chains, wss, at)]
        for (bb, d, wq, u, a, kgt, gl, o_ref), ws, v_nb, j in zip(chains, wss, v_nbs, at):
            o = pick(ws[CH4:]) + jnp.concatenate(
                [_dot(a[bb, j, 0, p], v_nb[p * PAIR:(p + 1) * PAIR]) for p in range(DN_PAIRS)], axis=0)
            for h in range(DN_HEADS):
                o_ref[bb, j * DN_CHUNK:(j + 1) * DN_CHUNK, h * LANES:(h + 1) * LANES] = (
                    o[h * DN_CHUNK:(h + 1) * DN_CHUNK])
        nxt = []
        for (bb, d, wq, u, a, kgt, gl, o_ref), s_all, v_nb, j in zip(chains, s_alls, v_nbs, at):
            v_bd = jnp.where(own, jnp.concatenate([v_nb] * DN_HEADS, axis=1), jnp.zeros((), BF16))
            nxt.append(s_all * jnp.exp(gl[bb, j, 0]) + _dot(kgt[bb, j, 0], v_bd))
        s_alls = nxt
    for (bb, d, *_), s_all in zip(chains, s_alls):
        state[bb, d] = s_all


def _dn_scan(wq, u, a, kgt, gl_wide, *, s_len):
    b, nc = wq.shape[:2]
    n_lat = s_len // DN_CHUNK
    n_ctx = nc - n_lat
    assert n_lat % DN_SPS == 0 and n_ctx % DN_SPS == 0
    nb, nb_lat, nb_ctx = nc // DN_SPS, n_lat // DN_SPS, n_ctx // DN_SPS
    fwd = lambda n: jnp.where(n < nb_ctx, nb_lat + n, n - nb_ctx)
    bwd = lambda n: nb - 1 - n
    specs = []
    for d, order in ((0, fwd), (1, bwd)):
        blk = lambda arr, d=d, order=order: pl.BlockSpec(
            (b, DN_SPS, 1) + arr.shape[3:], lambda n: (0, order(n), d) + (0,) * (arr.ndim - 3))
        specs += [blk(wq), blk(u), blk(a), blk(kgt), blk(gl_wide)]
    o_spec = lambda order: pl.BlockSpec((b, DN_SPS * DN_CHUNK, DN_WIDTH), lambda n: (0, order(n), 0))
    return pl.pallas_call(
        _dn_scan_kernel,
        grid=(nb,),
        in_specs=specs,
        out_specs=[o_spec(fwd), o_spec(bwd)],
        out_shape=[jax.ShapeDtypeStruct((b, nc * DN_CHUNK, DN_WIDTH), F32)] * 2,
        scratch_shapes=[pltpu.VMEM((b, 2, DN_HEAD_DIM, DN_WIDTH), F32)],
        compiler_params=_cparams(("arbitrary",)),
        name="dn_scan",
    )(wq, u, a, kgt, gl_wide, wq, u, a, kgt, gl_wide)


MERGE_PARTS = 2


def _merge_kernel(ona_ref, of_ref, ob_ref, z_ref, sc_ref, schp_ref, schn_ref, gate_ref, x_ref,
                  g1l_ref, g1c_ref, sh2l_ref, sh2c_ref, sc2l_ref, sc2c_ref, ng_ref, og_ref, scw_ref,
                  wna_ref, wdn_ref, wsc_ref, wout_ref, wrh_ref, wrl_ref, rb_ref,
                  x1_ref, h2_ref, comb_ref, combt_ref, cpad, *, tm, s_len, ta):
    i = pl.program_id(1)
    d = x_ref.shape[-1]
    half = HALO // 2

    def cu(ref_val):
        v = ref_val.astype(F32)
        return v[:, 2 * SC_WIDTH:] * v[:, :SC_WIDTH]
    cpad[0:half, :] = cu(schp_ref[0])[half:, :]
    cpad[half:half + tm, :] = cu(sc_ref[0])
    cpad[half + tm:, :] = cu(schn_ref[0])[:half, :]

    nr = tm // MERGE_PARTS
    parts = [slice(k * nr, (k + 1) * nr) for k in range(MERGE_PARTS)]

    def select(k, lat, ctx):
        row = i * tm + k * nr + lax.broadcasted_iota(jnp.int32, (nr, 1), 0)
        return jnp.where(row >= s_len, ctx, lat)

    branch_in = []
    for k, rows in enumerate(parts):
        o = of_ref[0, rows, :] + ob_ref[0, rows, :]
        heads = []
        for h in range(DN_HEADS):
            oh = o[:, h * LANES:(h + 1) * LANES]
            heads.append(oh * lax.rsqrt(jnp.mean(oh * oh, axis=-1, keepdims=True) + EPS))
        o_dn = jnp.concatenate(heads, axis=1) * og_ref[...] * _silu(z_ref[0, rows, :].astype(F32))
        gt = i * tm + k * nr + lax.broadcasted_iota(jnp.int32, (nr, 1), 0)
        has_prev = (gt != 0) & (gt != s_len)
        has_next = (gt != s_len - 1) & (gt != ta - 1)
        conv = (scw_ref[0:1, :] * jnp.where(has_prev, cpad[pl.ds(half - 1 + k * nr, nr), :], 0.0)
                + scw_ref[1:2, :] * cpad[pl.ds(half + k * nr, nr), :]
                + scw_ref[2:3, :] * jnp.where(has_next, cpad[pl.ds(half + 1 + k * nr, nr), :], 0.0))
        o_sc = sc_ref[0, rows, SC_WIDTH:2 * SC_WIDTH].astype(F32) * conv
        branch_in.append((o_dn.astype(BF16), o_sc.astype(BF16)))

    merged = []
    for rows, (o_dn, o_sc) in zip(parts, branch_in):
        merged.append((gate_ref[0, rows, :d].astype(F32) * _dot(ona_ref[0, rows, :], wna_ref[...])
                       + gate_ref[0, rows, d:2 * d].astype(F32) * _dot(o_dn, wdn_ref[...])
                       + gate_ref[0, rows, 2 * d:].astype(F32) * _dot(o_sc, wsc_ref[...])).astype(BF16))

    logit_parts = []
    for k, (rows, m) in enumerate(zip(parts, merged)):
        x1 = x_ref[0, rows, :] + select(k, g1l_ref[0], g1c_ref[...]) * _dot(m, wout_ref[...])
        x1_ref[0, rows, :] = x1
        hn = x1 * lax.rsqrt(jnp.mean(x1 * x1, axis=-1, keepdims=True) + EPS) * ng_ref[...]
        h2 = hn * (1.0 + select(k, sc2l_ref[0], sc2c_ref[...])) + select(k, sh2l_ref[0], sh2c_ref[...])
        h2_ref[0, rows, :] = h2.astype(BF16)
        h2_hi, h2_lo = _split_bf16(h2)
        logit_parts.append(_dot3(h2_hi, h2_lo, wrh_ref[...], wrl_ref[...]))

    logits = jnp.concatenate(logit_parts, axis=0)
    scores = jax.nn.sigmoid(logits.T)
    biased = scores + rb_ref[...]
    rows = [biased[e:e + 1, :] for e in range(N_EXPERTS)]
    gsc = []
    for g in range(N_GROUPS):
        a, b_, c_, d_ = rows[4 * g:4 * g + 4]
        gsc.append(jnp.maximum(jnp.maximum(jnp.maximum(a + b_, a + c_), jnp.maximum(a + d_, b_ + c_)),
                               jnp.maximum(b_ + d_, c_ + d_)))
    best = gsc[0]
    grp = jnp.zeros_like(best, dtype=jnp.int32)
    for g in range(1, N_GROUPS):
        upd = gsc[g] > best
        best = jnp.where(upd, gsc[g], best)
        grp = jnp.where(upd, g, grp)
    eidx = lax.broadcasted_iota(jnp.int32, biased.shape, 0)
    masked = jnp.where((eidx // EXPERTS_PER_GROUP) == grp, biased, -jnp.inf)
    big = jnp.int32(LANES)
    m1 = masked.max(axis=0, keepdims=True)
    i1 = jnp.where(masked == m1, eidx, big).min(axis=0, keepdims=True)
    masked2 = jnp.where(eidx == i1, -jnp.inf, masked)
    m2 = masked2.max(axis=0, keepdims=True)
    i2 = jnp.where(masked2 == m2, eidx, big).min(axis=0, keepdims=True)
    sel1 = eidx == i1
    sel2 = eidx == i2
    s1 = jnp.sum(jnp.where(sel1, scores, 0.0), axis=0, keepdims=True)
    s2 = jnp.sum(jnp.where(sel2, scores, 0.0), axis=0, keepdims=True)
    tot = s1 + s2
    comb_t = jnp.where(sel1, s1 / tot, 0.0) + jnp.where(sel2, s2 / tot, 0.0)
    meta_t = jnp.where(eidx == N_EXPERTS, grp.astype(F32), comb_t)
    combt_ref[0] = meta_t
    comb_ref[0] = meta_t.T


def _merge(o_na, o_f, o_b, z, sc, gates, xs, mods_l, mods_c, norm_g, out_g, sc_w, wna, wdn, wsc, wout, wr, rb,
           *, s_len, tm):
    b, ta, d = xs.shape
    col_z = col_sc = 0
    hb = tm // HALO
    n_hb = ta // HALO
    tok = lambda w, col=0: pl.BlockSpec((1, tm, w), lambda bb, i: (bb, i, col))
    vec = lambda w: pl.BlockSpec((1, w), lambda bb, i: (0, 0))
    bvec = lambda: pl.BlockSpec((1, 1, d), lambda bb, i: (bb, 0, 0))
    full = lambda a: pl.BlockSpec(a.shape, lambda bb, i: (0,) * a.ndim)
    g1l, sh2l, sc2l = mods_l
    g1c, sh2c, sc2c = mods_c
    w3 = 3 * SC_WIDTH
    wr_hi = wr.astype(BF16)
    wr_lo = (wr - wr_hi.astype(F32)).astype(BF16)
    return pl.pallas_call(
        functools.partial(_merge_kernel, tm=tm, s_len=s_len, ta=ta),
        grid=(b, ta // tm),
        in_specs=[tok(NA_WIDTH), tok(DN_WIDTH), tok(DN_WIDTH), tok(DN_WIDTH, col_z), tok(w3, col_sc),
                  pl.BlockSpec((1, HALO, w3), lambda bb, i: (bb, jnp.maximum(i * hb - 1, 0), col_sc)),
                  pl.BlockSpec((1, HALO, w3), lambda bb, i: (bb, jnp.minimum((i + 1) * hb, n_hb - 1), col_sc)),
                  tok(3 * d, 0), tok(d),
                  bvec(), vec(d), bvec(), vec(d), bvec(), vec(d), vec(d), vec(DN_WIDTH),
                  full(sc_w), full(wna), full(wdn), full(wsc), full(wout), full(wr_hi), full(wr_lo), full(rb)],
        out_specs=[tok(d), tok(d), tok(LANES), pl.BlockSpec((1, LANES, tm), lambda bb, i: (bb, 0, i))],
        out_shape=[jax.ShapeDtypeStruct((b, ta, d), F32),
                   jax.ShapeDtypeStruct((b, ta, d), BF16),
                   jax.ShapeDtypeStruct((b, ta, LANES), F32),
                   jax.ShapeDtypeStruct((b, LANES, ta), F32)],
        scratch_shapes=[pltpu.VMEM((tm + HALO, SC_WIDTH), F32)],
        compiler_params=_cparams(("parallel", "parallel")),
        name="merge_router",
    )(o_na, o_f, o_b, z, sc, sc, sc, gates, xs, g1l, g1c, sh2l, sh2c, sc2l, sc2c, norm_g, out_g,
      sc_w, wna, wdn, wsc, wout, wr_hi, wr_lo, rb)


MOE_CAP = 256


def _moe_kernel(h_ref, comb_ref, combt_ref, lt_ref, gt_ref, wg_ref, wu_ref, wd_ref, x_ref, g2l_ref, g2c_ref,
                o_ref, acc, rank_r, rank_c, *, tm, s_len):
    i = pl.program_id(1)
    g = pl.program_id(2)

    comb = comb_ref[0]
    lane = lax.broadcasted_iota(jnp.int32, (tm, LANES), 1)
    grp_row = combt_ref[0, N_EXPERTS:N_EXPERTS + 1, :]
    grp_col = jnp.sum(jnp.where(lane == N_EXPERTS, comb, 0.0), axis=-1, keepdims=True)

    @pl.when(g == 0)
    def _():
        acc[...] = jnp.zeros_like(acc)
        sub = lax.broadcasted_iota(jnp.int32, (8, tm), 0).astype(F32)
        before = _dot(jnp.where(sub == grp_row, 1.0, 0.0).astype(BF16), lt_ref[...])
        rank_r[...] = jnp.broadcast_to(
            jnp.sum(jnp.where(sub == grp_row, before, 0.0), axis=0, keepdims=True), (8, tm))
        lane_f = lane.astype(F32)
        before = _dot(gt_ref[...], jnp.where(lane_f == grp_col, 1.0, 0.0).astype(BF16))
        rank_c[...] = jnp.broadcast_to(
            jnp.sum(jnp.where(lane_f == grp_col, before, 0.0), axis=-1, keepdims=True), (tm, LANES))

    gf = g.astype(F32)
    member_row = grp_row == gf
    member_col = grp_col == gf
    rank_row = rank_r[0:1, :]
    rank_col = rank_c[:, 0:1]
    count = jnp.sum(jnp.where(member_row, 1.0, 0.0))
    comb_hi, comb_lo = _split_bf16(comb)
    d = x_ref.shape[-1]

    for k in range(tm // MOE_CAP):
        @pl.when(count > k * MOE_CAP)
        def _(k=k):
            slot_c = (lax.broadcasted_iota(jnp.int32, (MOE_CAP, 1), 0) + k * MOE_CAP).astype(F32)
            gather = jnp.where(member_row & (rank_row == slot_c), 1.0, 0.0).astype(BF16)
            hs = _dot(gather, h_ref[0]).astype(BF16)
            cws = _dot(gather, comb_hi) + _dot(gather, comb_lo)
            lane_c = lax.broadcasted_iota(jnp.int32, (MOE_CAP, LANES), 1)
            experts = range(EXPERTS_PER_GROUP)
            gates = [_dot(hs, wg_ref[0, e4]) for e4 in experts]
            ups = [_dot(hs, wu_ref[0, e4]) for e4 in experts]
            hiddens = []
            for e4 in experts:
                cw = jnp.sum(jnp.where(lane_c == g * EXPERTS_PER_GROUP + e4, cws, 0.0), axis=-1, keepdims=True)
                hiddens.append((_silu(gates[e4]) * ups[e4] * cw).astype(BF16))
            ys = _dot(hiddens[0], wd_ref[0, 0])
            for e4 in experts[1:]:
                ys = ys + _dot(hiddens[e4], wd_ref[0, e4])
            slot_r = (lax.broadcasted_iota(jnp.int32, (1, MOE_CAP), 1) + k * MOE_CAP).astype(F32)
            scatter = jnp.where(member_col & (rank_col == slot_r), 1.0, 0.0).astype(BF16)
            ys_hi, ys_lo = _split_bf16(ys)
            acc[...] += _dot(scatter, ys_hi) + _dot(scatter, ys_lo)

    @pl.when(g == N_GROUPS - 1)
    def _():
        o_ref[0] = x_ref[0] + _row_select(i, tm, s_len, g2l_ref[0], g2c_ref[...]) * acc[...]


def _moe(h2, comb, comb_t, wg, wu, wd, x1, g2l, g2c, *, layer, s_len, tm):
    b, ta, d = x1.shape
    tok = lambda w: pl.BlockSpec((1, tm, w), lambda bb, i, g: (bb, i, 0))
    grp = lambda a: pl.BlockSpec((1, EXPERTS_PER_GROUP) + a.shape[2:], lambda bb, i, g: (layer, g, 0, 0))
    tri = lambda: pl.BlockSpec((tm, tm), lambda bb, i, g: (0, 0))
    earlier = np.arange(tm)[:, None] < np.arange(tm)[None, :]
    lt = jnp.asarray(earlier, BF16)
    gt = jnp.asarray(earlier.T, BF16)
    return pl.pallas_call(
        functools.partial(_moe_kernel, tm=tm, s_len=s_len),
        grid=(b, ta // tm, N_GROUPS),
        in_specs=[tok(d), tok(LANES),
                  pl.BlockSpec((1, LANES, tm), lambda bb, i, g: (bb, 0, i)),
                  tri(), tri(),
                  grp(wg), grp(wu), grp(wd),
                  tok(d),
                  pl.BlockSpec((1, 1, d), lambda bb, i, g: (bb, 0, 0)),
                  pl.BlockSpec((1, d), lambda bb, i, g: (0, 0))],
        out_specs=tok(d),
        out_shape=jax.ShapeDtypeStruct((b, ta, d), F32),
        scratch_shapes=[pltpu.VMEM((tm, d), F32), pltpu.VMEM((8, tm), F32), pltpu.VMEM((tm, LANES), F32)],
        compiler_params=_cparams(("parallel", "parallel", "arbitrary")),
        name="moe_grouped",
    )(h2, comb, comb_t, lt, gt, wg, wu, wd, x1, g2l, g2c)


def _rope_tables(s_len, ctx_len):
    t = np.arange(s_len)
    n_freq = DN_HEAD_DIM // 4
    inv_freq = (np.float32(ROPE_BASE) ** (-np.arange(n_freq, dtype=np.float32) / n_freq)).astype(np.float32)
    pos = np.stack([t // GRID_W, t % GRID_W], axis=-1).astype(np.float32)
    ang = (pos[..., None] * inv_freq).astype(np.float32)
    cos, sin = np.cos(ang.astype(np.float64)), np.sin(ang.astype(np.float64))
    rc = np.concatenate([cos[:, 0], cos[:, 0], cos[:, 1], cos[:, 1]], axis=-1)
    rs = np.concatenate([-sin[:, 0], sin[:, 0], -sin[:, 1], sin[:, 1]], axis=-1)
    rc = np.concatenate([rc, np.ones((ctx_len, LANES))], axis=0)
    rs = np.concatenate([rs, np.zeros((ctx_len, LANES))], axis=0)
    return jnp.asarray(rc, F32), jnp.asarray(rs, F32)


def _pick_tile(ta, candidates):
    for t in candidates:
        if ta % t == 0:
            return t
    raise ValueError(f"no row tile for {ta} tokens")


def kernel(x, c, ctx, c_ctx, w_ada, b_ada, norm_mix_g, norm_ffn_g, w_in, na_q_norm_g, na_k_norm_g, na_rpb, dn_conv_w, dn_a_log, dn_dt_bias, dn_out_norm_g, sc_conv_w, w_branch_na, w_branch_dn, w_branch_sc, w_out, w_router, router_bias, moe_w_gate, moe_w_up, moe_w_down):
    b, s_len, d = x.shape
    ctx_len = ctx.shape[1]
    depth = w_ada.shape[0]
    ta = s_len + ctx_len
    assert ctx_len == NA_BLK and s_len % (WIN_R * GRID_W) == 0 and s_len // NA_BLK >= 4
    assert ctx_len % PREP_TILE == 0 and b + 1 <= 8
    tm_proj = _pick_tile(ta, (768, 512, 256))
    tm_merge = _pick_tile(ta, (384, 256))
    tm_moe = _pick_tile(ta, (768, 512, 256))

    xs = jnp.concatenate([x, ctx], axis=1)
    c_rows = jnp.zeros((8, d), F32).at[:b].set(c).at[b].set(c_ctx)
    mod = _modulation(c_rows, w_ada, b_ada)

    sizes = (3 * NA_WIDTH, 3 * DN_WIDTH, DN_WIDTH, 4 * DN_HEADS, 3 * SC_WIDTH, 3 * d)
    cuts = [int(v) for v in np.cumsum((0,) + sizes)]
    seg = lambda l, k: lax.slice(w_in, (l, 0, cuts[k]), (l + 1, d, cuts[k + 1]))[0]
    wg_all, wu_all, wd_all = moe_w_gate.astype(BF16), moe_w_up.astype(BF16), moe_w_down.astype(BF16)

    rope_c, rope_s = _rope_tables(s_len, ctx_len)
    bd = jnp.asarray(np.kron(np.eye(NA_HEADS), np.ones((NA_HEAD_DIM, NA_HEAD_DIM))), BF16)
    nc = ta // DN_CHUNK
    wr = jnp.zeros((d, LANES), F32).at[:, :N_EXPERTS].set(w_router)
    rb = jnp.zeros((LANES, 1), F32).at[:N_EXPERTS, 0].set(router_bias)

    for l in range(depth):
        m = mod[l].reshape(8, N_MOD, d)
        lat = lambda k: m[:b, k][:, None, :]
        cx = lambda k: m[b:b + 1, k]
        w_ab = jnp.zeros((d, LANES), F32).at[:, :4 * DN_HEADS].set(seg(l, 3))
        gqk = jnp.concatenate([jnp.tile(na_q_norm_g[l], NA_HEADS) * (NA_HEAD_DIM ** -0.5),
                               jnp.tile(na_k_norm_g[l], NA_HEADS)])[None, :]
        gates, na_q, na_k, na_v, dn_qkv, sc_p, dn_z, ab = _projection(
            xs, (lat(0), lat(1)), (cx(0), cx(1)), norm_mix_g[l][None, :],
            seg(l, 5).astype(BF16), seg(l, 0).astype(BF16), seg(l, 1).astype(BF16),
            seg(l, 4).astype(BF16), seg(l, 2).astype(BF16), w_ab, gqk, bd, s_len=s_len, tm=tm_proj)

        o_na = _neighbourhood_attention(na_q, na_k, na_v, _na_bias_table(na_rpb[l]), s_len=s_len)

        alog_row = jnp.zeros((1, LANES), F32).at[0, :2 * DN_HEADS].set(dn_a_log[l].reshape(-1))
        dtb_row = jnp.zeros((1, LANES), F32).at[0, :2 * DN_HEADS].set(dn_dt_bias[l].reshape(-1))
        qd, kd, vd, gc, bc = _dn_prep(dn_qkv, ab, dn_conv_w[l], rope_c, rope_s, alog_row, dtb_row,
                                      s_len=s_len)
        g8 = gc[..., :2 * DN_HEADS].reshape(b, nc, DN_CHUNK, 2, DN_HEADS)
        g_rows = jnp.transpose(g8, (0, 1, 3, 4, 2)).reshape(b, nc, 2, CH4)
        g_last = jnp.stack([g8[:, :, DN_CHUNK - 1, 0], g8[:, :, 0, 1]], axis=2)
        gl_wide = jnp.repeat(g_last, LANES, axis=-1)[:, :, :, None, :]
        wq, u, a, kgt = _dn_local(qd, kd, vd, gc, bc, g_rows)
        o_f, o_b = _dn_scan(wq, u, a, kgt, gl_wide, s_len=s_len)

        x1, h2, comb, comb_t = _merge(
            o_na, o_f, o_b, dn_z, sc_p, gates, xs, (lat(2), lat(3), lat(4)), (cx(2), cx(3), cx(4)),
            norm_ffn_g[l][None, :], jnp.tile(dn_out_norm_g[l], DN_HEADS)[None, :], sc_conv_w[l],
            w_branch_na[l].astype(BF16), w_branch_dn[l].astype(BF16), w_branch_sc[l].astype(BF16),
            w_out[l].astype(BF16), wr, rb, s_len=s_len, tm=tm_merge)

        xs = _moe(h2, comb, comb_t, wg_all, wu_all, wd_all, x1, lat(5), cx(5), layer=l, s_len=s_len, tm=tm_moe)

    return xs[:, :s_len]
```

```python
import functools
import math

import numpy as np
import jax
import jax.numpy as jnp
from jax import lax
from jax.experimental import pallas as pl
from jax.experimental.pallas import tpu as pltpu

F32 = jnp.float32
BF16 = jnp.bfloat16
HIGHEST = lax.Precision.HIGHEST

GRID_W = 64
NA_HEADS = 8
NA_HEAD_DIM = 64
NA_WIDTH = NA_HEADS * NA_HEAD_DIM
WIN_R = 8
WIN_C = 16
DN_HEADS = 4
DN_HEAD_DIM = 128
DN_WIDTH = DN_HEADS * DN_HEAD_DIM
DN_CONV = 5
DN_CHUNK = 64
ROPE_BASE = 10000.0
SC_WIDTH = 512
SC_CONV = 3
N_EXPERTS = 16
N_GROUPS = 4
EXPERTS_PER_GROUP = N_EXPERTS // N_GROUPS
D_EXPERT = 512
N_MOD = 6
EPS = 1e-6
NEG_INF = -1e30

LANES = 128
HALO = 16
PREP_TILE = 256
VMEM_LIMIT = 56 * 1024 * 1024

N_PROJ_TILE = 512


def _cparams(sem):
    return pltpu.CompilerParams(dimension_semantics=sem, vmem_limit_bytes=VMEM_LIMIT)


def _dot(a, b):
    return jnp.dot(a, b, preferred_element_type=F32)


def _dot_nt(a, b):
    return lax.dot_general(a, b, (((1,), (1,)), ((), ())), preferred_element_type=F32)


def _silu(x):
    return x * jax.nn.sigmoid(x)


def _row_select(i, tm, s_len, lat, ctx):
    row = i * tm + lax.broadcasted_iota(jnp.int32, (tm, 1), 0)
    return jnp.where(row >= s_len, ctx, lat)


def _mod_kernel(c_ref, w_ref, b_ref, o_ref):
    c_hi, c_lo = _split_bf16(_silu(c_ref[...]))
    w_hi, w_lo = _split_bf16(w_ref[0])
    o_ref[0] = _dot3(c_hi, c_lo, w_hi, w_lo) + b_ref[0]


def _modulation(c_rows, w_ada, b_ada):
    depth, d, n = w_ada.shape
    tn = 512
    return pl.pallas_call(
        _mod_kernel,
        grid=(depth, n // tn),
        in_specs=[pl.BlockSpec((8, d), lambda l, j: (0, 0)),
                  pl.BlockSpec((1, d, tn), lambda l, j: (l, 0, j)),
                  pl.BlockSpec((1, 1, tn), lambda l, j: (l, 0, j))],
        out_specs=pl.BlockSpec((1, 8, tn), lambda l, j: (l, 0, j)),
        out_shape=jax.ShapeDtypeStruct((depth, 8, n), F32),
        compiler_params=_cparams(("parallel", "parallel")),
        name="adaln_mod",
    )(c_rows, w_ada, b_ada.reshape(depth, 1, n))


def _split_bf16(a):
    hi = a.astype(BF16)
    return hi, (a - hi.astype(F32)).astype(BF16)


def _dot3(a_hi, a_lo, b_hi, b_lo):
    return _dot(a_hi, b_hi) + _dot(a_lo, b_hi) + _dot(a_hi, b_lo)


def _modulated_norm(x_ref, shl_ref, scl_ref, shc_ref, scc_ref, g_ref, *, tm, s_len):
    i = pl.program_id(1)
    x = x_ref[0]
    y = x * lax.rsqrt(jnp.mean(x * x, axis=-1, keepdims=True) + EPS) * g_ref[...]
    scale = _row_select(i, tm, s_len, scl_ref[0], scc_ref[...])
    shift = _row_select(i, tm, s_len, shl_ref[0], shc_ref[...])
    return y * (1.0 + scale) + shift


def _proj_gate_kernel(x_ref, shl_ref, scl_ref, shc_ref, scc_ref, g_ref, w_ref, gate_ref, *, tm, s_len):
    h = _modulated_norm(x_ref, shl_ref, scl_ref, shc_ref, scc_ref, g_ref, tm=tm, s_len=s_len).astype(BF16)
    for c0 in range(0, w_ref.shape[1], N_PROJ_TILE):
        cols = slice(c0, c0 + N_PROJ_TILE)
        gate_ref[0, :, cols] = jax.nn.sigmoid(_dot(h, w_ref[:, cols])).astype(BF16)


def _proj_mix_kernel(x_ref, shl_ref, scl_ref, shc_ref, scc_ref, g_ref, wna_ref, wdn_ref, wsc_ref, wz_ref,
                     wabh_ref, wabl_ref, gqk_ref, bd_ref,
                     q_ref, k_ref, v_ref, dn_ref, sc_ref, z_ref, ab_ref, *, tm, s_len):
    h = _modulated_norm(x_ref, shl_ref, scl_ref, shc_ref, scc_ref, g_ref, tm=tm, s_len=s_len)
    h_hi, h_lo = _split_bf16(h)
    ab_ref[0] = _dot3(h_hi, h_lo, wabh_ref[...], wabl_ref[...])
    for part, out in enumerate((q_ref, k_ref)):
        cols = slice(part * NA_WIDTH, (part + 1) * NA_WIDTH)
        a = _dot(h_hi, wna_ref[:, cols])
        ss = _dot((a * a).astype(BF16), bd_ref[...])
        out[0] = (a * lax.rsqrt(ss * (1.0 / NA_HEAD_DIM) + EPS) * gqk_ref[:, cols]).astype(BF16)
    v_ref[0] = _dot(h_hi, wna_ref[:, 2 * NA_WIDTH:]).astype(BF16)
    for w_ref, out in ((wdn_ref, dn_ref), (wsc_ref, sc_ref), (wz_ref, z_ref)):
        for c0 in range(0, w_ref.shape[1], N_PROJ_TILE):
            cols = slice(c0, c0 + N_PROJ_TILE)
            out[0, :, cols] = _dot(h_hi, w_ref[:, cols]).astype(BF16)


def _projection(xs, mod_l, mod_c, norm_g, w_gate, w_na, w_dn, w_sc, w_z, w_ab, gqk, bd, *, s_len, tm):
    b, ta, d = xs.shape
    shl, scl = mod_l
    shc, scc = mod_c
    wab_hi, wab_lo = _split_bf16(w_ab)
    tok = lambda w: pl.BlockSpec((1, tm, w), lambda bb, i: (bb, i, 0))
    bvec = lambda: pl.BlockSpec((1, 1, d), lambda bb, i: (bb, 0, 0))
    full = lambda a: pl.BlockSpec(a.shape, lambda bb, i: (0,) * a.ndim)
    head = [tok(d), bvec(), bvec(), full(shc), full(scc), full(norm_g)]
    gates = pl.pallas_call(
        functools.partial(_proj_gate_kernel, tm=tm, s_len=s_len),
        grid=(b, ta // tm),
        in_specs=head + [full(w_gate)],
        out_specs=tok(w_gate.shape[1]),
        out_shape=jax.ShapeDtypeStruct((b, ta, w_gate.shape[1]), BF16),
        compiler_params=_cparams(("parallel", "parallel")),
        name="proj_gates",
    )(xs, shl, scl, shc, scc, norm_g, w_gate)
    consts = (w_na, w_dn, w_sc, w_z, wab_hi, wab_lo, gqk, bd)
    widths = (NA_WIDTH, NA_WIDTH, NA_WIDTH, w_dn.shape[1], w_sc.shape[1], w_z.shape[1])
    outs = pl.pallas_call(
        functools.partial(_proj_mix_kernel, tm=tm, s_len=s_len),
        grid=(b, ta // tm),
        in_specs=head + [full(a) for a in consts],
        out_specs=[tok(w) for w in widths] + [tok(LANES)],
        out_shape=[jax.ShapeDtypeStruct((b, ta, w), BF16) for w in widths]
        + [jax.ShapeDtypeStruct((b, ta, LANES), F32)],
        compiler_params=_cparams(("parallel", "parallel")),
        name="proj_mixers",
    )(xs, shl, scl, shc, scc, norm_g, *consts)
    return (gates,) + tuple(outs)


NA_QROWS = 4
NA_BLK = NA_QROWS * GRID_W
WIN_TOK = WIN_R * GRID_W


def _stack_two_heads(q2):
    lane = lax.broadcasted_iota(jnp.int32, q2.shape, 1)
    zero = jnp.zeros_like(q2)
    return jnp.concatenate([jnp.where(lane < NA_HEAD_DIM, q2, zero),
                            jnp.where(lane >= NA_HEAD_DIM, q2, zero)], axis=0)


def _unstack_two_heads(o2):
    m = o2.shape[0] // 2
    lane = lax.broadcasted_iota(jnp.int32, (m, o2.shape[1]), 1)
    return jnp.where(lane < NA_HEAD_DIM, o2[:m], o2[m:])


def _na_kernel(q_ref, kp_ref, kc_ref, kn_ref, vp_ref, vc_ref, vn_ref, kx_ref, vx_ref, bias_ref, o_ref,
               kbuf, vbuf, *, n_rows):
    i = pl.program_id(1)
    n_lat = n_rows // NA_QROWS

    @pl.when(i < n_lat)
    def _():
        for slot, (kr, vr) in enumerate(((kp_ref, vp_ref), (kc_ref, vc_ref), (kn_ref, vn_ref))):
            kbuf[slot * NA_BLK:(slot + 1) * NA_BLK, :] = kr[0]
            vbuf[slot * NA_BLK:(slot + 1) * NA_BLK, :] = vr[0]
        win = []
        for j in range(NA_QROWS):
            r = i * NA_QROWS + j
            start = jnp.clip(r - WIN_R // 2, 0, n_rows - WIN_R)
            win.append((pl.multiple_of((start - (i - 1) * NA_QROWS) * GRID_W, GRID_W), r - start))

        rows2 = 2 * GRID_W
        pairs = [slice(hp * LANES, (hp + 1) * LANES) for hp in range(NA_HEADS // 2)]
        logits = []
        for hp, lanes in enumerate(pairs):
            qs = jnp.concatenate([_stack_two_heads(q_ref[0, j * GRID_W:(j + 1) * GRID_W, lanes])
                                  for j in range(NA_QROWS)], axis=0)
            s_c = _dot_nt(qs, kx_ref[0, :, lanes])
            s_w = jnp.concatenate(
                [_dot_nt(qs[j * rows2:(j + 1) * rows2], kbuf[pl.ds(win[j][0], WIN_TOK), lanes])
                 + bias_ref[win[j][1], hp] for j in range(NA_QROWS)], axis=0)
            logits.append((s_w, s_c))
        probs = []
        for s_w, s_c in logits:
            m = jnp.maximum(s_w.max(axis=-1, keepdims=True), s_c.max(axis=-1, keepdims=True))
            p_w = jnp.exp(s_w - m)
            p_c = jnp.exp(s_c - m)
            denom = p_w.sum(axis=-1, keepdims=True) + p_c.sum(axis=-1, keepdims=True)
            probs.append((p_w.astype(BF16), p_c.astype(BF16), denom))
        for lanes, (p_wb, p_cb, denom) in zip(pairs, probs):
            o = _dot(p_cb, vx_ref[0, :, lanes]) + jnp.concatenate(
                [_dot(p_wb[j * rows2:(j + 1) * rows2], vbuf[pl.ds(win[j][0], WIN_TOK), lanes])
                 for j in range(NA_QROWS)], axis=0)
            o = o / denom
            for j in range(NA_QROWS):
                o_ref[0, j * GRID_W:(j + 1) * GRID_W, lanes] = _unstack_two_heads(
                    o[j * rows2:(j + 1) * rows2]).astype(BF16)

    @pl.when(i == n_lat)
    def _():
        for hp in range(NA_HEADS // 2):
            lanes = slice(hp * LANES, (hp + 1) * LANES)
            s = _dot_nt(_stack_two_heads(q_ref[0, :, lanes]), kx_ref[0, :, lanes])
            p = jnp.exp(s - s.max(axis=-1, keepdims=True))
            o = _dot(p.astype(BF16), vx_ref[0, :, lanes]) / p.sum(axis=-1, keepdims=True)
            o_ref[0, :, lanes] = _unstack_two_heads(o).astype(BF16)


def _neighbourhood_attention(q, k, v, bias_tab, *, s_len):
    b, ta, _ = q.shape
    n_rows = s_len // GRID_W
    n_lat = n_rows // NA_QROWS
    blk = lambda off: pl.BlockSpec((1, NA_BLK, NA_WIDTH), lambda bb, i: (bb, jnp.clip(i + off, 0, n_lat - 1), 0))
    cx = lambda: pl.BlockSpec((1, NA_BLK, NA_WIDTH), lambda bb, i: (bb, n_lat, 0))
    own = lambda: pl.BlockSpec((1, NA_BLK, NA_WIDTH), lambda bb, i: (bb, i, 0))
    return pl.pallas_call(
        functools.partial(_na_kernel, n_rows=n_rows),
        grid=(b, n_lat + 1),
        in_specs=[own(), blk(-1), blk(0), blk(1), blk(-1), blk(0), blk(1), cx(), cx(),
                  pl.BlockSpec(bias_tab.shape, lambda bb, i: (0, 0, 0, 0))],
        out_specs=own(),
        out_shape=jax.ShapeDtypeStruct((b, ta, NA_WIDTH), BF16),
        scratch_shapes=[pltpu.VMEM((3 * NA_BLK, NA_WIDTH), BF16),
                        pltpu.VMEM((3 * NA_BLK, NA_WIDTH), BF16)],
        compiler_params=_cparams(("parallel", "arbitrary")),
        name="na_attention",
    )(q, k, k, k, v, v, v, k, v, bias_tab)


def _na_bias_table(rpb):
    qc = np.arange(GRID_W)[:, None]
    kc = np.arange(GRID_W)[None, :]
    win_c0 = np.clip(qc - WIN_C // 2, 0, GRID_W - WIN_C)
    in_win = (kc >= win_c0) & (kc < win_c0 + WIN_C)
    rel_c = np.clip(kc - qc + WIN_C - 1, 0, 2 * WIN_C - 2)
    pick_c = (rel_c[..., None] == np.arange(2 * WIN_C - 1)).astype(np.float32)
    by_col = jnp.einsum('hab,qkb->haqk', rpb.astype(F32), pick_c, precision=HIGHEST)
    by_col = jnp.where(in_win[None, None], by_col, NEG_INF)
    tab = jnp.stack([by_col[:, WIN_R - 1 - v:2 * WIN_R - 1 - v] for v in range(WIN_R)], axis=0)
    tab = jnp.transpose(tab, (0, 1, 3, 2, 4))
    return tab.reshape(WIN_R, NA_HEADS // 2, 2 * GRID_W, WIN_R * GRID_W)


def _dn_prep_kernel(x_ref, hp_ref, hn_ref, w_ref, rc_ref, rs_ref, ab_ref, alog_ref, dtb_ref,
                    q_ref, k_ref, v_ref, g_ref, beta_ref, xpad, *, s_len, ta):
    i = pl.program_id(1)
    t0 = i * PREP_TILE
    has_prev = jnp.logical_and(t0 != 0, t0 != s_len).astype(F32)
    has_next = jnp.logical_and(t0 + PREP_TILE != s_len, t0 + PREP_TILE != ta).astype(F32)
    half = HALO // 2
    xpad[0:half, :] = hp_ref[0].astype(F32)[half:, :] * has_prev
    xpad[half:half + PREP_TILE, :] = x_ref[0].astype(F32)
    xpad[half + PREP_TILE:, :] = hn_ref[0].astype(F32)[:half, :] * has_next
    xp = xpad[...]
    n_pad = xp.shape[0]
    y = jnp.zeros((PREP_TILE, x_ref.shape[-1]), F32)
    for tap in range(DN_CONV):
        shifted = xp if tap == DN_CONV // 2 else pltpu.roll(xp, (DN_CONV // 2 - tap) % n_pad, 0)
        y = y + w_ref[tap:tap + 1, :] * shifted[half:half + PREP_TILE]
    y = _silu(y)

    lane = lax.broadcasted_iota(jnp.int32, (PREP_TILE, LANES), 1)
    first_half = (lane % (LANES // 2)) < (LANES // 4)
    rc = rc_ref[...]
    rs = rs_ref[...]

    def norm_rope(u, scale):
        u = u * (lax.rsqrt(jnp.sum(u * u, axis=-1, keepdims=True) + EPS) * scale)
        partner = jnp.where(first_half, pltpu.roll(u, LANES - LANES // 4, 1), pltpu.roll(u, LANES // 4, 1))
        return u * rc + partner * rs

    for h in range(DN_HEADS):
        lanes = slice(h * LANES, (h + 1) * LANES)
        qh = norm_rope(y[:, h * LANES:(h + 1) * LANES], DN_HEAD_DIM ** -0.5)
        kh = norm_rope(y[:, DN_WIDTH + h * LANES:DN_WIDTH + (h + 1) * LANES], 1.0)
        q_ref[0, :, lanes] = qh.astype(BF16)
        k_ref[0, :, lanes] = kh.astype(BF16)
    v_ref[0] = y[:, 2 * DN_WIDTH:].astype(BF16)

    ab = ab_ref[0]
    z = ab + dtb_ref[...]
    softplus = jnp.maximum(z, 0.0) + jnp.log1p(jnp.exp(-jnp.abs(z)))
    g = -jnp.exp(alog_ref[...]) * softplus
    g = jnp.where(lane < 2 * DN_HEADS, g, 0.0)
    beta = jax.nn.sigmoid(pltpu.roll(ab, LANES - 2 * DN_HEADS, 1))
    beta_ref[0] = jnp.where(lane < 2 * DN_HEADS, beta, 0.0)

    r = lax.broadcasted_iota(jnp.int32, (PREP_TILE, PREP_TILE), 0)
    c = lax.broadcasted_iota(jnp.int32, (PREP_TILE, PREP_TILE), 1)
    same = (r // DN_CHUNK) == (c // DN_CHUNK)
    tri_f = jnp.where(same & (c <= r), 1.0, 0.0).astype(BF16)
    tri_b = jnp.where(same & (c >= r), 1.0, 0.0).astype(BF16)
    g1 = g.astype(BF16)
    rem = g - g1.astype(F32)
    g2 = rem.astype(BF16)
    g3 = (rem - g2.astype(F32)).astype(BF16)
    cs_f = _dot(tri_f, g1) + _dot(tri_f, g2) + _dot(tri_f, g3)
    cs_b = _dot(tri_b, g1) + _dot(tri_b, g2) + _dot(tri_b, g3)
    g_ref[0] = jnp.where(lane < DN_HEADS, cs_f, cs_b)


def _dn_prep(p, ab, conv_w, rope_c, rope_s, alog_row, dtb_row, *, s_len):
    b, ta, w3 = p.shape
    col_dn = 0
    n_t = ta // PREP_TILE
    hb = PREP_TILE // HALO
    n_hb = ta // HALO
    tok = lambda: pl.BlockSpec((1, PREP_TILE, DN_WIDTH), lambda bb, i: (bb, i, 0))
    sca = lambda: pl.BlockSpec((1, PREP_TILE, LANES), lambda bb, i: (bb, i, 0))
    row = lambda: pl.BlockSpec((1, LANES), lambda bb, i: (0, 0))
    return pl.pallas_call(
        functools.partial(_dn_prep_kernel, s_len=s_len, ta=ta),
        grid=(b, n_t),
        in_specs=[pl.BlockSpec((1, PREP_TILE, w3), lambda bb, i: (bb, i, col_dn)),
                  pl.BlockSpec((1, HALO, w3), lambda bb, i: (bb, jnp.maximum(i * hb - 1, 0), col_dn)),
                  pl.BlockSpec((1, HALO, w3), lambda bb, i: (bb, jnp.minimum((i + 1) * hb, n_hb - 1), col_dn)),
                  pl.BlockSpec((DN_CONV, w3), lambda bb, i: (0, 0)),
                  pl.BlockSpec((PREP_TILE, LANES), lambda bb, i: (i, 0)),
                  pl.BlockSpec((PREP_TILE, LANES), lambda bb, i: (i, 0)),
                  sca(), row(), row()],
        out_specs=[tok(), tok(), tok(), sca(), sca()],
        out_shape=[jax.ShapeDtypeStruct((b, ta, DN_WIDTH), BF16)] * 3
        + [jax.ShapeDtypeStruct((b, ta, LANES), F32)] * 2,
        scratch_shapes=[pltpu.VMEM((PREP_TILE + HALO, w3), F32)],
        compiler_params=_cparams(("parallel", "parallel")),
        name="dn_prep",
    )(p, p, p, conv_w, rope_c, rope_s, ab, alog_row, dtb_row)


CH4 = DN_HEADS * DN_CHUNK


DN_CPS = 4
DN_PAIRS = DN_HEADS // 2
PAIR = 2 * DN_CHUNK


def _stack_pair(x, p):
    return jnp.concatenate([x[:, (2 * p + h) * LANES:(2 * p + h + 1) * LANES] for h in range(2)], axis=0)


def _dn_local_kernel(q_ref, k_ref, v_ref, gc_ref, bc_ref, gr_ref, wq_ref, u_ref, a_ref, kgt_ref):
    r = lax.broadcasted_iota(jnp.int32, (PAIR, PAIR), 0)
    c = lax.broadcasted_iota(jnp.int32, (PAIR, PAIR), 1)
    same = (r // DN_CHUNK) == (c // DN_CHUNK)
    chains = []
    for cc in range(DN_CPS):
        toks = slice(cc * DN_CHUNK, (cc + 1) * DN_CHUNK)
        gc = gc_ref[0, toks, :]
        bc = bc_ref[0, toks, :]
        for p in range(DN_PAIRS):
            rows = slice(p * PAIR, (p + 1) * PAIR)
            kst = _stack_pair(k_ref[0, toks, :], p)
            qst = _stack_pair(q_ref[0, toks, :], p)
            vst = _stack_pair(v_ref[0, toks, :], p)
            kq = _dot_nt(jnp.concatenate([kst, qst], axis=0), kst)
            kk = kq[:PAIR]
            qk = kq[PAIR:]
            kf = kst.astype(F32)
            for d in range(2):
                later = (r >= c) if d == 0 else (r <= c)
                incl = same & later
                strict = incl & (r != c)
                lane0 = d * DN_HEADS + 2 * p
                col = lambda a: jnp.concatenate([a[:, lane0 + h:lane0 + h + 1] for h in range(2)], axis=0)
                g_col = col(gc)
                b_col = col(bc)
                last = DN_CHUNK - 1 if d == 0 else 0
                gl_col = jnp.concatenate(
                    [jnp.broadcast_to(gc[last:last + 1, lane0 + h:lane0 + h + 1], (DN_CHUNK, 1))
                     for h in range(2)], axis=0)
                g_row = gr_ref[0, cc, d:d + 1, rows]
                decay = jnp.exp(jnp.where(incl, g_col - g_row, NEG_INF))
                pw = jnp.where(strict, -(kk * decay * b_col), 0.0)
                z = jnp.concatenate([vst.astype(F32) * b_col, kf * (b_col * jnp.exp(g_col))], axis=1)
                chains.append([pw, z, (cc, d, rows)])
                wq_ref[0, cc, d, CH4 + p * PAIR:CH4 + (p + 1) * PAIR, :] = (
                    qst.astype(F32) * jnp.exp(g_col)).astype(BF16)
                a_ref[0, cc, d, p] = (qk * decay).astype(BF16)
                kgt_ref[0, cc, d, :, rows] = (kf * jnp.exp(gl_col - g_col)).T.astype(BF16)

    for step in range(6):
        if step:
            for chain in chains:
                pwb = chain[0].astype(BF16)
                chain[0] = _dot(pwb, pwb)
        for chain in chains:
            pw, z, _ = chain
            chain[1] = z + _dot(pw.astype(BF16), z.astype(BF16))
    for _, z, (cc, d, rows) in chains:
        wq_ref[0, cc, d, rows, :] = z[:, LANES:].astype(BF16)
        u_ref[0, cc, d, rows, :] = z[:, :LANES]


def _dn_local(qd, kd, vd, gc, bc, g_rows):
    b, ta, _ = qd.shape
    nc = ta // DN_CHUNK
    tok = lambda w: pl.BlockSpec((1, DN_CPS * DN_CHUNK, w), lambda bb, n: (bb, n, 0))
    out = lambda *tail: pl.BlockSpec((1, DN_CPS, 2) + tail, lambda bb, n: (bb, n, 0) + (0,) * len(tail))
    return pl.pallas_call(
        _dn_local_kernel,
        grid=(b, nc // DN_CPS),
        in_specs=[tok(DN_WIDTH), tok(DN_WIDTH), tok(DN_WIDTH), tok(LANES), tok(LANES),
                  pl.BlockSpec((1, DN_CPS, 2, CH4), lambda bb, n: (bb, n, 0, 0))],
        out_specs=[out(2 * CH4, LANES), out(CH4, LANES), out(DN_PAIRS, PAIR, PAIR), out(LANES, CH4)],
        out_shape=[jax.ShapeDtypeStruct((b, nc, 2, 2 * CH4, LANES), BF16),
                   jax.ShapeDtypeStruct((b, nc, 2, CH4, LANES), F32),
                   jax.ShapeDtypeStruct((b, nc, 2, DN_PAIRS, PAIR, PAIR), BF16),
                   jax.ShapeDtypeStruct((b, nc, 2, LANES, CH4), BF16)],
        compiler_params=_cparams(("parallel", "parallel")),
        name="dn_local",
    )(qd, kd, vd, gc, bc, g_rows)


DN_SPS = 4


def _dn_scan_kernel(wq_f, u_f, a_f, kgt_f, gl_f, wq_b, u_b, a_b, kgt_b, gl_b, of_ref, ob_ref, state):
    n = pl.program_id(0)

    @pl.when(n == 0)
    def _():
        state[...] = jnp.zeros_like(state)

    r = lax.broadcasted_iota(jnp.int32, (CH4, DN_WIDTH), 0)
    c = lax.broadcasted_iota(jnp.int32, (CH4, DN_WIDTH), 1)
    own = (r // DN_CHUNK) == (c // LANES)
    pick = lambda m: jnp.concatenate(
        [m[h * DN_CHUNK:(h + 1) * DN_CHUNK, h * LANES:(h + 1) * LANES] for h in range(DN_HEADS)], axis=0)
    dirs = ((wq_f, u_f, a_f, kgt_f, gl_f, of_ref), (wq_b, u_b, a_b, kgt_b, gl_b, ob_ref))
    chains = [(bb, d) + refs for bb in range(state.shape[0]) for d, refs in enumerate(dirs)]
    s_alls = [state[bb, d] for bb, d, *_ in chains]
    for sub in range(DN_SPS):
        at = [sub if d == 0 else DN_SPS - 1 - sub for _, d, *_ in chains]
        wss = [_dot(wq[bb, j, 0], s.astype(BF16)) for (bb, d, wq, *_), s, j in zip(chains, s_alls, at)]
        v_nbs = [(u[bb, j, 0] - pick(ws[:CH4])).astype(BF16)
                 for (bb, d, wq, u, *_), ws, j in zip(chains, wss, at)]
        for (bb, d, wq, u, a, kgt, gl, o_ref), ws, v_nb, j in zip(chains, wss, v_nbs, at):
            o = pick(ws[CH4:]) + jnp.concatenate(
                [_dot(a[bb, j, 0, p], v_nb[p * PAIR:(p + 1) * PAIR]) for p in range(DN_PAIRS)], axis=0)
            for h in range(DN_HEADS):
                o_ref[bb, j * DN_CHUNK:(j + 1) * DN_CHUNK, h * LANES:(h + 1) * LANES] = (
                    o[h * DN_CHUNK:(h + 1) * DN_CHUNK])
        nxt = []
        for (bb, d, wq, u, a, kgt, gl, o_ref), s_all, v_nb, j in zip(chains, s_alls, v_nbs, at):
            v_bd = jnp.where(own, jnp.concatenate([v_nb] * DN_HEADS, axis=1), jnp.zeros((), BF16))
            nxt.append(s_all * jnp.exp(gl[bb, j, 0]) + _dot(kgt[bb, j, 0], v_bd))
        s_alls = nxt
    for (bb, d, *_), s_all in zip(chains, s_alls):
        state[bb, d] = s_all


def _dn_scan(wq, u, a, kgt, gl_wide, *, s_len):
    b, nc = wq.shape[:2]
    n_lat = s_len // DN_CHUNK
    n_ctx = nc - n_lat
    assert n_lat % DN_SPS == 0 and n_ctx % DN_SPS == 0
    nb, nb_lat, nb_ctx = nc // DN_SPS, n_lat // DN_SPS, n_ctx // DN_SPS
    fwd = lambda n: jnp.where(n < nb_ctx, nb_lat + n, n - nb_ctx)
    bwd = lambda n: nb - 1 - n
    specs = []
    for d, order in ((0, fwd), (1, bwd)):
        blk = lambda arr, d=d, order=order: pl.BlockSpec(
            (b, DN_SPS, 1) + arr.shape[3:], lambda n: (0, order(n), d) + (0,) * (arr.ndim - 3))
        specs += [blk(wq), blk(u), blk(a), blk(kgt), blk(gl_wide)]
    o_spec = lambda order: pl.BlockSpec((b, DN_SPS * DN_CHUNK, DN_WIDTH), lambda n: (0, order(n), 0))
    return pl.pallas_call(
        _dn_scan_kernel,
        grid=(nb,),
        in_specs=specs,
        out_specs=[o_spec(fwd), o_spec(bwd)],
        out_shape=[jax.ShapeDtypeStruct((b, nc * DN_CHUNK, DN_WIDTH), F32)] * 2,
        scratch_shapes=[pltpu.VMEM((b, 2, DN_HEAD_DIM, DN_WIDTH), F32)],
        compiler_params=_cparams(("arbitrary",)),
        name="dn_scan",
    )(wq, u, a, kgt, gl_wide, wq, u, a, kgt, gl_wide)


MERGE_PARTS = 2


def _merge_kernel(ona_ref, of_ref, ob_ref, z_ref, sc_ref, schp_ref, schn_ref, gate_ref, x_ref,
                  g1l_ref, g1c_ref, sh2l_ref, sh2c_ref, sc2l_ref, sc2c_ref, ng_ref, og_ref, scw_ref,
                  wna_ref, wdn_ref, wsc_ref, wout_ref, wrh_ref, wrl_ref, rb_ref,
                  x1_ref, h2_ref, comb_ref, combt_ref, cpad, *, tm, s_len, ta):
    i = pl.program_id(1)
    d = x_ref.shape[-1]
    half = HALO // 2

    def cu(ref_val):
        v = ref_val.astype(F32)
        return v[:, 2 * SC_WIDTH:] * v[:, :SC_WIDTH]
    cpad[0:half, :] = cu(schp_ref[0])[half:, :]
    cpad[half:half + tm, :] = cu(sc_ref[0])
    cpad[half + tm:, :] = cu(schn_ref[0])[:half, :]

    nr = tm // MERGE_PARTS
    parts = [slice(k * nr, (k + 1) * nr) for k in range(MERGE_PARTS)]

    def select(k, lat, ctx):
        row = i * tm + k * nr + lax.broadcasted_iota(jnp.int32, (nr, 1), 0)
        return jnp.where(row >= s_len, ctx, lat)

    branch_in = []
    for k, rows in enumerate(parts):
        o = of_ref[0, rows, :] + ob_ref[0, rows, :]
        heads = []
        for h in range(DN_HEADS):
            oh = o[:, h * LANES:(h + 1) * LANES]
            heads.append(oh * lax.rsqrt(jnp.mean(oh * oh, axis=-1, keepdims=True) + EPS))
        o_dn = jnp.concatenate(heads, axis=1) * og_ref[...] * _silu(z_ref[0, rows, :].astype(F32))
        gt = i * tm + k * nr + lax.broadcasted_iota(jnp.int32, (nr, 1), 0)
        has_prev = (gt != 0) & (gt != s_len)
        has_next = (gt != s_len - 1) & (gt != ta - 1)
        conv = (scw_ref[0:1, :] * jnp.where(has_prev, cpad[pl.ds(half - 1 + k * nr, nr), :], 0.0)
                + scw_ref[1:2, :] * cpad[pl.ds(half + k * nr, nr), :]
                + scw_ref[2:3, :] * jnp.where(has_next, cpad[pl.ds(half + 1 + k * nr, nr), :], 0.0))
        o_sc = sc_ref[0, rows, SC_WIDTH:2 * SC_WIDTH].astype(F32) * conv
        branch_in.append((o_dn.astype(BF16), o_sc.astype(BF16)))

    merged = []
    for rows, (o_dn, o_sc) in zip(parts, branch_in):
        merged.append((gate_ref[0, rows, :d].astype(F32) * _dot(ona_ref[0, rows, :], wna_ref[...])
                       + gate_ref[0, rows, d:2 * d].astype(F32) * _dot(o_dn, wdn_ref[...])
                       + gate_ref[0, rows, 2 * d:].astype(F32) * _dot(o_sc, wsc_ref[...])).astype(BF16))

    logit_parts = []
    for k, (rows, m) in enumerate(zip(parts, merged)):
        x1 = x_ref[0, rows, :] + select(k, g1l_ref[0], g1c_ref[...]) * _dot(m, wout_ref[...])
        x1_ref[0, rows, :] = x1
        hn = x1 * lax.rsqrt(jnp.mean(x1 * x1, axis=-1, keepdims=True) + EPS) * ng_ref[...]
        h2 = hn * (1.0 + select(k, sc2l_ref[0], sc2c_ref[...])) + select(k, sh2l_ref[0], sh2c_ref[...])
        h2_ref[0, rows, :] = h2.astype(BF16)
        h2_hi, h2_lo = _split_bf16(h2)
        logit_parts.append(_dot3(h2_hi, h2_lo, wrh_ref[...], wrl_ref[...]))

    logits = jnp.concatenate(logit_parts, axis=0)
    scores = jax.nn.sigmoid(logits.T)
    biased = scores + rb_ref[...]
    rows = [biased[e:e + 1, :] for e in range(N_EXPERTS)]
    gsc = []
    for g in range(N_GROUPS):
        a, b_, c_, d_ = rows[4 * g:4 * g + 4]
        gsc.append(jnp.maximum(jnp.maximum(jnp.maximum(a + b_, a + c_), jnp.maximum(a + d_, b_ + c_)),
                               jnp.maximum(b_ + d_, c_ + d_)))
    best = gsc[0]
    grp = jnp.zeros_like(best, dtype=jnp.int32)
    for g in range(1, N_GROUPS):
        upd = gsc[g] > best
        best = jnp.where(upd, gsc[g], best)
        grp = jnp.where(upd, g, grp)
    eidx = lax.broadcasted_iota(jnp.int32, biased.shape, 0)
    masked = jnp.where((eidx // EXPERTS_PER_GROUP) == grp, biased, -jnp.inf)
    big = jnp.int32(LANES)
    m1 = masked.max(axis=0, keepdims=True)
    i1 = jnp.where(masked == m1, eidx, big).min(axis=0, keepdims=True)
    masked2 = jnp.where(eidx == i1, -jnp.inf, masked)
    m2 = masked2.max(axis=0, keepdims=True)
    i2 = jnp.where(masked2 == m2, eidx, big).min(axis=0, keepdims=True)
    sel1 = eidx == i1
    sel2 = eidx == i2
    s1 = jnp.sum(jnp.where(sel1, scores, 0.0), axis=0, keepdims=True)
    s2 = jnp.sum(jnp.where(sel2, scores, 0.0), axis=0, keepdims=True)
    tot = s1 + s2
    comb_t = jnp.where(sel1, s1 / tot, 0.0) + jnp.where(sel2, s2 / tot, 0.0)
    meta_t = jnp.where(eidx == N_EXPERTS, grp.astype(F32), comb_t)
    combt_ref[0] = meta_t
    comb_ref[0] = meta_t.T


def _merge(o_na, o_f, o_b, z, sc, gates, xs, mods_l, mods_c, norm_g, out_g, sc_w, wna, wdn, wsc, wout, wr, rb,
           *, s_len, tm):
    b, ta, d = xs.shape
    col_z = col_sc = 0
    hb = tm // HALO
    n_hb = ta // HALO
    tok = lambda w, col=0: pl.BlockSpec((1, tm, w), lambda bb, i: (bb, i, col))
    vec = lambda w: pl.BlockSpec((1, w), lambda bb, i: (0, 0))
    bvec = lambda: pl.BlockSpec((1, 1, d), lambda bb, i: (bb, 0, 0))
    full = lambda a: pl.BlockSpec(a.shape, lambda bb, i: (0,) * a.ndim)
    g1l, sh2l, sc2l = mods_l
    g1c, sh2c, sc2c = mods_c
    w3 = 3 * SC_WIDTH
    wr_hi = wr.astype(BF16)
    wr_lo = (wr - wr_hi.astype(F32)).astype(BF16)
    return pl.pallas_call(
        functools.partial(_merge_kernel, tm=tm, s_len=s_len, ta=ta),
        grid=(b, ta // tm),
        in_specs=[tok(NA_WIDTH), tok(DN_WIDTH), tok(DN_WIDTH), tok(DN_WIDTH, col_z), tok(w3, col_sc),
                  pl.BlockSpec((1, HALO, w3), lambda bb, i: (bb, jnp.maximum(i * hb - 1, 0), col_sc)),
                  pl.BlockSpec((1, HALO, w3), lambda bb, i: (bb, jnp.minimum((i + 1) * hb, n_hb - 1), col_sc)),
                  tok(3 * d, 0), tok(d),
                  bvec(), vec(d), bvec(), vec(d), bvec(), vec(d), vec(d), vec(DN_WIDTH),
                  full(sc_w), full(wna), full(wdn), full(wsc), full(wout), full(wr_hi), full(wr_lo), full(rb)],
        out_specs=[tok(d), tok(d), tok(LANES), pl.BlockSpec((1, LANES, tm), lambda bb, i: (bb, 0, i))],
        out_shape=[jax.ShapeDtypeStruct((b, ta, d), F32),
                   jax.ShapeDtypeStruct((b, ta, d), BF16),
                   jax.ShapeDtypeStruct((b, ta, LANES), F32),
                   jax.ShapeDtypeStruct((b, LANES, ta), F32)],
        scratch_shapes=[pltpu.VMEM((tm + HALO, SC_WIDTH), F32)],
        compiler_params=_cparams(("parallel", "parallel")),
        name="merge_router",
    )(o_na, o_f, o_b, z, sc, sc, sc, gates, xs, g1l, g1c, sh2l, sh2c, sc2l, sc2c, norm_g, out_g,
      sc_w, wna, wdn, wsc, wout, wr_hi, wr_lo, rb)


MOE_CAP = 256


def _moe_kernel(h_ref, comb_ref, combt_ref, lt_ref, gt_ref, wg_ref, wu_ref, wd_ref, x_ref, g2l_ref, g2c_ref,
                o_ref, acc, rank_r, rank_c, *, tm, s_len):
    i = pl.program_id(1)
    g = pl.program_id(2)

    comb = comb_ref[0]
    lane = lax.broadcasted_iota(jnp.int32, (tm, LANES), 1)
    grp_row = combt_ref[0, N_EXPERTS:N_EXPERTS + 1, :]
    grp_col = jnp.sum(jnp.where(lane == N_EXPERTS, comb, 0.0), axis=-1, keepdims=True)

    @pl.when(g == 0)
    def _():
        acc[...] = jnp.zeros_like(acc)
        sub = lax.broadcasted_iota(jnp.int32, (8, tm), 0).astype(F32)
        before = _dot(jnp.where(sub == grp_row, 1.0, 0.0).astype(BF16), lt_ref[...])
        rank_r[...] = jnp.broadcast_to(
            jnp.sum(jnp.where(sub == grp_row, before, 0.0), axis=0, keepdims=True), (8, tm))
        lane_f = lane.astype(F32)
        before = _dot(gt_ref[...], jnp.where(lane_f == grp_col, 1.0, 0.0).astype(BF16))
        rank_c[...] = jnp.broadcast_to(
            jnp.sum(jnp.where(lane_f == grp_col, before, 0.0), axis=-1, keepdims=True), (tm, LANES))

    gf = g.astype(F32)
    member_row = grp_row == gf
    member_col = grp_col == gf
    rank_row = rank_r[0:1, :]
    rank_col = rank_c[:, 0:1]
    count = jnp.sum(jnp.where(member_row, 1.0, 0.0))
    comb_hi, comb_lo = _split_bf16(comb)
    d = x_ref.shape[-1]

    for k in range(tm // MOE_CAP):
        @pl.when(count > k * MOE_CAP)
        def _(k=k):
            slot_c = (lax.broadcasted_iota(jnp.int32, (MOE_CAP, 1), 0) + k * MOE_CAP).astype(F32)
            gather = jnp.where(member_row & (rank_row == slot_c), 1.0, 0.0).astype(BF16)
            hs = _dot(gather, h_ref[0]).astype(BF16)
            cws = _dot(gather, comb_hi) + _dot(gather, comb_lo)
            lane_c = lax.broadcasted_iota(jnp.int32, (MOE_CAP, LANES), 1)
            experts = range(EXPERTS_PER_GROUP)
            gates = [_dot(hs, wg_ref[0, e4]) for e4 in experts]
            ups = [_dot(hs, wu_ref[0, e4]) for e4 in experts]
            hiddens = []
            for e4 in experts:
                cw = jnp.sum(jnp.where(lane_c == g * EXPERTS_PER_GROUP + e4, cws, 0.0), axis=-1, keepdims=True)
                hiddens.append((_silu(gates[e4]) * ups[e4] * cw).astype(BF16))
            ys = _dot(hiddens[0], wd_ref[0, 0])
            for e4 in experts[1:]:
                ys = ys + _dot(hiddens[e4], wd_ref[0, e4])
            slot_r = (lax.broadcasted_iota(jnp.int32, (1, MOE_CAP), 1) + k * MOE_CAP).astype(F32)
            scatter = jnp.where(member_col & (rank_col == slot_r), 1.0, 0.0).astype(BF16)
            ys_hi, ys_lo = _split_bf16(ys)
            acc[...] += _dot(scatter, ys_hi) + _dot(scatter, ys_lo)

    @pl.when(g == N_GROUPS - 1)
    def _():
        o_ref[0] = x_ref[0] + _row_select(i, tm, s_len, g2l_ref[0], g2c_ref[...]) * acc[...]


def _moe(h2, comb, comb_t, wg, wu, wd, x1, g2l, g2c, *, layer, s_len, tm, out_rows):
    b, ta, d = x1.shape
    tok = lambda w: pl.BlockSpec((1, tm, w), lambda bb, i, g: (bb, i, 0))
    grp = lambda a: pl.BlockSpec((1, EXPERTS_PER_GROUP) + a.shape[2:], lambda bb, i, g: (layer, g, 0, 0))
    tri = lambda: pl.BlockSpec((tm, tm), lambda bb, i, g: (0, 0))
    earlier = np.arange(tm)[:, None] < np.arange(tm)[None, :]
    lt = jnp.asarray(earlier, BF16)
    gt = jnp.asarray(earlier.T, BF16)
    return pl.pallas_call(
        functools.partial(_moe_kernel, tm=tm, s_len=s_len),
        grid=(b, ta // tm, N_GROUPS),
        in_specs=[tok(d), tok(LANES),
                  pl.BlockSpec((1, LANES, tm), lambda bb, i, g: (bb, 0, i)),
                  tri(), tri(),
                  grp(wg), grp(wu), grp(wd),
                  tok(d),
                  pl.BlockSpec((1, 1, d), lambda bb, i, g: (bb, 0, 0)),
                  pl.BlockSpec((1, d), lambda bb, i, g: (0, 0))],
        out_specs=tok(d),
        out_shape=jax.ShapeDtypeStruct((b, out_rows, d), F32),
        scratch_shapes=[pltpu.VMEM((tm, d), F32), pltpu.VMEM((8, tm), F32), pltpu.VMEM((tm, LANES), F32)],
        compiler_params=_cparams(("parallel", "parallel", "arbitrary")),
        name="moe_grouped",
    )(h2, comb, comb_t, lt, gt, wg, wu, wd, x1, g2l, g2c)


def _rope_tables(s_len, ctx_len):
    t = np.arange(s_len)
    n_freq = DN_HEAD_DIM // 4
    inv_freq = (np.float32(ROPE_BASE) ** (-np.arange(n_freq, dtype=np.float32) / n_freq)).astype(np.float32)
    pos = np.stack([t // GRID_W, t % GRID_W], axis=-1).astype(np.float32)
    ang = (pos[..., None] * inv_freq).astype(np.float32)
    cos, sin = np.cos(ang.astype(np.float64)), np.sin(ang.astype(np.float64))
    rc = np.concatenate([cos[:, 0], cos[:, 0], cos[:, 1], cos[:, 1]], axis=-1)
    rs = np.concatenate([-sin[:, 0], sin[:, 0], -sin[:, 1], sin[:, 1]], axis=-1)
    rc = np.concatenate([rc, np.ones((ctx_len, LANES))], axis=0)
    rs = np.concatenate([rs, np.zeros((ctx_len, LANES))], axis=0)
    return jnp.asarray(rc, F32), jnp.asarray(rs, F32)


def _pick_tile(ta, candidates):
    for t in candidates:
        if ta % t == 0:
            return t
    raise ValueError(f"no row tile for {ta} tokens")


def kernel(x, c, ctx, c_ctx, w_ada, b_ada, norm_mix_g, norm_ffn_g, w_in, na_q_norm_g, na_k_norm_g, na_rpb, dn_conv_w, dn_a_log, dn_dt_bias, dn_out_norm_g, sc_conv_w, w_branch_na, w_branch_dn, w_branch_sc, w_out, w_router, router_bias, moe_w_gate, moe_w_up, moe_w_down):
    b, s_len, d = x.shape
    ctx_len = ctx.shape[1]
    depth = w_ada.shape[0]
    ta = s_len + ctx_len
    assert ctx_len == NA_BLK and s_len % (WIN_R * GRID_W) == 0 and s_len // NA_BLK >= 4
    assert ctx_len % PREP_TILE == 0 and b + 1 <= 8
    tm_proj = _pick_tile(ta, (768, 512, 256))
    tm_merge = _pick_tile(ta, (384, 256))
    tm_moe = _pick_tile(ta, (768, 512, 256))

    xs = jnp.concatenate([x, ctx], axis=1)
    c_rows = jnp.zeros((8, d), F32).at[:b].set(c).at[b].set(c_ctx)
    mod = _modulation(c_rows, w_ada, b_ada)

    sizes = (3 * NA_WIDTH, 3 * DN_WIDTH, DN_WIDTH, 4 * DN_HEADS, 3 * SC_WIDTH, 3 * d)
    cuts = [int(v) for v in np.cumsum((0,) + sizes)]
    seg = lambda l, k: lax.slice(w_in, (l, 0, cuts[k]), (l + 1, d, cuts[k + 1]))[0]
    wg_all, wu_all, wd_all = moe_w_gate.astype(BF16), moe_w_up.astype(BF16), moe_w_down.astype(BF16)

    rope_c, rope_s = _rope_tables(s_len, ctx_len)
    bd = jnp.asarray(np.kron(np.eye(NA_HEADS), np.ones((NA_HEAD_DIM, NA_HEAD_DIM))), BF16)
    nc = ta // DN_CHUNK
    wr = jnp.zeros((d, LANES), F32).at[:, :N_EXPERTS].set(w_router)
    rb = jnp.zeros((LANES, 1), F32).at[:N_EXPERTS, 0].set(router_bias)

    for l in range(depth):
        m = mod[l].reshape(8, N_MOD, d)
        lat = lambda k: m[:b, k][:, None, :]
        cx = lambda k: m[b:b + 1, k]
        w_ab = jnp.zeros((d, LANES), F32).at[:, :4 * DN_HEADS].set(seg(l, 3))
        gqk = jnp.concatenate([jnp.tile(na_q_norm_g[l], NA_HEADS) * (NA_HEAD_DIM ** -0.5),
                               jnp.tile(na_k_norm_g[l], NA_HEADS)])[None, :]
        gates, na_q, na_k, na_v, dn_qkv, sc_p, dn_z, ab = _projection(
            xs, (lat(0), lat(1)), (cx(0), cx(1)), norm_mix_g[l][None, :],
            seg(l, 5).astype(BF16), seg(l, 0).astype(BF16), seg(l, 1).astype(BF16),
            seg(l, 4).astype(BF16), seg(l, 2).astype(BF16), w_ab, gqk, bd, s_len=s_len, tm=tm_proj)

        o_na = _neighbourhood_attention(na_q, na_k, na_v, _na_bias_table(na_rpb[l]), s_len=s_len)

        alog_row = jnp.zeros((1, LANES), F32).at[0, :2 * DN_HEADS].set(dn_a_log[l].reshape(-1))
        dtb_row = jnp.zeros((1, LANES), F32).at[0, :2 * DN_HEADS].set(dn_dt_bias[l].reshape(-1))
        qd, kd, vd, gc, bc = _dn_prep(dn_qkv, ab, dn_conv_w[l], rope_c, rope_s, alog_row, dtb_row,
                                      s_len=s_len)
        g8 = gc[..., :2 * DN_HEADS].reshape(b, nc, DN_CHUNK, 2, DN_HEADS)
        g_rows = jnp.transpose(g8, (0, 1, 3, 4, 2)).reshape(b, nc, 2, CH4)
        g_last = jnp.stack([g8[:, :, DN_CHUNK - 1, 0], g8[:, :, 0, 1]], axis=2)
        gl_wide = jnp.repeat(g_last, LANES, axis=-1)[:, :, :, None, :]
        wq, u, a, kgt = _dn_local(qd, kd, vd, gc, bc, g_rows)
        o_f, o_b = _dn_scan(wq, u, a, kgt, gl_wide, s_len=s_len)

        x1, h2, comb, comb_t = _merge(
            o_na, o_f, o_b, dn_z, sc_p, gates, xs, (lat(2), lat(3), lat(4)), (cx(2), cx(3), cx(4)),
            norm_ffn_g[l][None, :], jnp.tile(dn_out_norm_g[l], DN_HEADS)[None, :], sc_conv_w[l],
            w_branch_na[l].astype(BF16), w_branch_dn[l].astype(BF16), w_branch_sc[l].astype(BF16),
            w_out[l].astype(BF16), wr, rb, s_len=s_len, tm=tm_merge)

        xs = _moe(h2, comb, comb_t, wg_all, wu_all, wd_all, x1, lat(5), cx(5), layer=l, s_len=s_len, tm=tm_moe,
                  out_rows=s_len if l == depth - 1 else ta)

    return xs
```
